```python
import jax, jax.numpy as jnp
from jax import lax
import numpy as np

D_MODEL = 1024
BATCH = 16
SEQ = 256
DEPTH = 4
DEC_BATCH = 2
DEC_SEQ = 2048
PAST_LEN = 256

GRID_W = 64
HEAD_DIM = 64
NA_HEADS = 4
NA_ROWS = 8
NA_COLS = 16
GQ_HEADS = 4
GQ_KV_HEADS = 2
WIN_HEADS = 4
WIN_KV_HEADS = 2
WINDOW = 128
BLOCK = 128
MLA_HEADS = 4
MLA_NOPE = 64
MLA_ROPE = 32
MLA_V = 64
MLA_KV_LORA = 128
MLA_QK = MLA_NOPE + MLA_ROPE
N_BRANCH = 4
BRANCH_W = 256
D_FF = 2816
CONV_W = 3
ROPE_THETA = 10000.0
EPS = 1e-6
NEG_INF = -1e30

IN_SIZES = (NA_HEADS * HEAD_DIM, NA_HEADS * HEAD_DIM, NA_HEADS * HEAD_DIM,
            GQ_HEADS * HEAD_DIM, GQ_KV_HEADS * HEAD_DIM, GQ_KV_HEADS * HEAD_DIM,
            WIN_HEADS * HEAD_DIM, WIN_KV_HEADS * HEAD_DIM, WIN_KV_HEADS * HEAD_DIM,
            MLA_HEADS * MLA_QK, MLA_KV_LORA, MLA_ROPE,
            N_BRANCH * D_MODEL)
IN_SPLITS = tuple(sum(IN_SIZES[:i + 1]) for i in range(len(IN_SIZES) - 1))
IN_COLS = sum(IN_SIZES)

kernel_name = 'hybrid_diffusion_prefix_step'


def _rmsnorm(x, g):
    xf = x.astype(jnp.float32)
    y = xf * lax.rsqrt(jnp.mean(xf * xf, axis=-1, keepdims=True) + EPS)
    return (y * g.astype(jnp.float32)).astype(x.dtype)


def _axial_cos_sin(T, dim):
    half = dim // 2
    inv = ROPE_THETA ** (-jnp.arange(0, half, 2, dtype=jnp.float32) / half)
    t = jnp.arange(T, dtype=jnp.int32)
    row = (t // GRID_W).astype(jnp.float32)[:, None] * inv[None, :]
    col = (t % GRID_W).astype(jnp.float32)[:, None] * inv[None, :]
    ang = jnp.concatenate([row, row, col, col], axis=-1)
    return jnp.cos(ang), jnp.sin(ang)


def _rotate_half(v):
    a, b = jnp.split(v, 2, axis=-1)
    return jnp.concatenate([-b, a], axis=-1)


def _apply_axial_rope(x, cos, sin):
    half = x.shape[-1] // 2
    xf = x.astype(jnp.float32)
    rot = jnp.concatenate([_rotate_half(xf[..., :half]), _rotate_half(xf[..., half:])], axis=-1)
    return (xf * cos[:, None, :] + rot * sin[:, None, :]).astype(x.dtype)


def _rope_tail(x, cos, sin):
    return jnp.concatenate([x[..., :MLA_NOPE], _apply_axial_rope(x[..., MLA_NOPE:], cos, sin)], axis=-1)


def _softmax_with_sink(s, sink):
    m = jnp.maximum(jnp.max(s, axis=-1, keepdims=True), sink)
    e = jnp.exp(s - m)
    return e / (jnp.sum(e, axis=-1, keepdims=True) + jnp.exp(sink - m))


def _dense_attn(q, k, v, sink=None):
    B, Tq, Hq, dk = q.shape
    Hkv = k.shape[2]
    g = Hq // Hkv
    qg = q.reshape(B, Tq, Hkv, g, dk)
    s = jnp.einsum('bqhgd,bkhd->bhgqk', qg, k, preferred_element_type=jnp.float32) * (dk ** -0.5)
    if sink is None:
        p = jax.nn.softmax(s, axis=-1)
    else:
        p = _softmax_with_sink(s, sink.astype(jnp.float32).reshape(1, Hkv, g, 1, 1))
    o = jnp.einsum('bhgqk,bkhd->bqhgd', p.astype(v.dtype), v)
    return o.reshape(B, Tq, Hq * v.shape[-1])


def _blocked_global_attn(q, k, v, k_ctx, v_ctx):
    B, T, Hq, dk = q.shape
    Hkv = k.shape[2]
    g = Hq // Hkv
    dv = v.shape[-1]
    k_all = jnp.concatenate([k_ctx, k], axis=1)
    v_all = jnp.concatenate([v_ctx, v], axis=1)
    qb = q.reshape(B, T // BLOCK, BLOCK, Hkv, g, dk).transpose(1, 0, 2, 3, 4, 5)

    def one_block(qblk):
        s = jnp.einsum('bqhgd,bkhd->bhgqk', qblk, k_all, preferred_element_type=jnp.float32) * (dk ** -0.5)
        p = jax.nn.softmax(s, axis=-1).astype(v_all.dtype)
        return jnp.einsum('bhgqk,bkhd->bqhgd', p, v_all)

    o = lax.map(one_block, qb)
    return o.transpose(1, 0, 2, 3, 4, 5).reshape(B, T, Hq * dv)


def _window_attn(q, k, v, k_ctx, v_ctx, sink):
    B, T, Hq, d = q.shape
    Hkv = k.shape[2]
    g = Hq // Hkv
    nb = T // BLOCK
    pad = ((0, 0), (BLOCK, BLOCK), (0, 0), (0, 0))
    kp = jnp.pad(k, pad)
    vp = jnp.pad(v, pad)

    def band(xp):
        return jnp.concatenate(
            [xp[:, i * BLOCK: i * BLOCK + T].reshape(B, nb, BLOCK, Hkv, xp.shape[-1]) for i in range(3)], axis=2)

    kb = band(kp)
    vb = band(vp)
    qb = q.reshape(B, nb, BLOCK, Hkv, g, d)
    scale = d ** -0.5
    s_loc = jnp.einsum('bnqhgd,bnkhd->bhgnqk', qb, kb, preferred_element_type=jnp.float32) * scale
    qpos = jnp.arange(T).reshape(nb, BLOCK)[:, :, None]
    kpos = (jnp.arange(nb)[:, None] * BLOCK - BLOCK + jnp.arange(3 * BLOCK)[None, :])[:, None, :]
    ok = (jnp.abs(qpos - kpos) <= WINDOW) & (kpos >= 0) & (kpos < T)
    s_loc = jnp.where(ok, s_loc, NEG_INF)
    s_ctx = jnp.einsum('bnqhgd,blhd->bhgnql', qb, k_ctx, preferred_element_type=jnp.float32) * scale
    L = k_ctx.shape[1]
    sk = sink.astype(jnp.float32).reshape(1, Hkv, g, 1, 1, 1)
    p = _softmax_with_sink(jnp.concatenate([s_ctx, s_loc], axis=-1), sk).astype(v.dtype)
    o = (jnp.einsum('bhgnql,blhd->bnqhgd', p[..., :L], v_ctx)
         + jnp.einsum('bhgnqk,bnkhd->bnqhgd', p[..., L:], vb))
    return o.reshape(B, T, Hq * d)


def _neighbourhood_attn(q, k, v, k_ctx, v_ctx, rpb):
    B, T, H, d = q.shape
    rows = T // GRID_W
    wr = min(NA_ROWS, rows)
    n_keys = wr * GRID_W
    r_idx = jnp.arange(rows)
    c_idx = jnp.arange(GRID_W)
    r0 = jnp.clip(r_idx - wr // 2, 0, rows - wr)
    c0 = jnp.clip(c_idx - NA_COLS // 2, 0, GRID_W - NA_COLS)
    key_rows = r0[:, None] + jnp.arange(wr)[None, :]
    kg = k.reshape(B, rows, GRID_W, H, d)[:, key_rows].reshape(B, rows, n_keys, H, d)
    vg = v.reshape(B, rows, GRID_W, H, d)[:, key_rows].reshape(B, rows, n_keys, H, d)
    qg = q.reshape(B, rows, GRID_W, H, d)
    key_r = jnp.repeat(key_rows, GRID_W, axis=1)
    key_c = jnp.broadcast_to(c_idx[None, :], (wr, GRID_W)).reshape(n_keys)
    col_ok = (key_c[None, :] >= c0[:, None]) & (key_c[None, :] < c0[:, None] + NA_COLS)
    dr = key_r - r_idx[:, None] + (NA_ROWS - 1)
    dc = jnp.clip(key_c[None, :] - c_idx[:, None], 1 - NA_COLS, NA_COLS - 1) + (NA_COLS - 1)
    bias = rpb[:, dr[:, None, :], dc[None, :, :]].astype(jnp.float32)
    scale = d ** -0.5
    s_loc = jnp.einsum('brqhd,brkhd->bhrqk', qg, kg, preferred_element_type=jnp.float32) * scale + bias[None]
    s_loc = jnp.where(col_ok[None, None, None], s_loc, NEG_INF)
    s_ctx = jnp.einsum('brqhd,blhd->bhrql', qg, k_ctx, preferred_element_type=jnp.float32) * scale
    L = k_ctx.shape[1]
    p = jax.nn.softmax(jnp.concatenate([s_ctx, s_loc], axis=-1), axis=-1).astype(v.dtype)
    o = (jnp.einsum('bhrql,blhd->brqhd', p[..., :L], v_ctx)
         + jnp.einsum('bhrqk,brkhd->brqhd', p[..., L:], vg))
    return o.reshape(B, T, H * d)


def _adaln(cond, lp):
    m = jax.nn.silu(cond) @ lp['w_ada'] + lp['b_ada']
    return jnp.split(m, 6, axis=-1)


def _mixer_inputs(h, lp):
    B, T = h.shape[0], h.shape[1]
    (qa, ka, va, qb, kb, vb, qc, kc, vc, qd, ckv, kpe, gates) = jnp.split(h @ lp['w_in'], IN_SPLITS, axis=-1)

    def heads(t, n, dd):
        return t.reshape(B, T, n, dd)

    return {
        'qa': _rmsnorm(heads(qa, NA_HEADS, HEAD_DIM), lp['qn_a']),
        'ka': _rmsnorm(heads(ka, NA_HEADS, HEAD_DIM), lp['kn_a']),
        'va': heads(va, NA_HEADS, HEAD_DIM),
        'qb': _rmsnorm(heads(qb, GQ_HEADS, HEAD_DIM), lp['qn_b']),
        'kb': _rmsnorm(heads(kb, GQ_KV_HEADS, HEAD_DIM), lp['kn_b']),
        'vb': heads(vb, GQ_KV_HEADS, HEAD_DIM),
        'qc': _rmsnorm(heads(qc, WIN_HEADS, HEAD_DIM), lp['qn_c']),
        'kc': _rmsnorm(heads(kc, WIN_KV_HEADS, HEAD_DIM), lp['kn_c']),
        'vc': heads(vc, WIN_KV_HEADS, HEAD_DIM),
        'qd': _rmsnorm(heads(qd, MLA_HEADS, MLA_QK), lp['qn_d']),
        'ckv': _rmsnorm(ckv, lp['kvn_d']),
        'kpe': kpe,
        'gates': jax.nn.sigmoid(gates.astype(jnp.float32)).astype(h.dtype).reshape(B, T, N_BRANCH, D_MODEL),
    }


def _mla_kv(ckv, kpe, w_ukv, kn):
    B, T = ckv.shape[0], ckv.shape[1]
    kv = (ckv @ w_ukv).reshape(B, T, MLA_HEADS, MLA_NOPE + MLA_V)
    k_nope, v = kv[..., :MLA_NOPE], kv[..., MLA_NOPE:]
    kpe_h = jnp.broadcast_to(kpe[:, :, None, :], (B, T, MLA_HEADS, MLA_ROPE))
    k = _rmsnorm(jnp.concatenate([k_nope, kpe_h], axis=-1), kn)
    return k, v


def _merge(branches, gates, lp):
    br = jnp.stack(branches, axis=2)
    proj = jnp.einsum('btnc,ncd->btnd', br, lp['w_branch'])
    return jnp.sum(gates * proj, axis=2) @ lp['w_out']


def _conv_ffn(h, lp):
    u = h @ lp['w_up']
    T = u.shape[1]
    half = CONV_W // 2
    up = jnp.pad(u, ((0, 0), (half, half), (0, 0)))
    acc = lp['conv_b'] + up[:, 0:T] * lp['conv_w'][0]
    for j in range(1, CONV_W):
        acc = acc + up[:, j:j + T] * lp['conv_w'][j]
    a, g = jnp.split(acc, 2, axis=-1)
    return (jax.nn.silu(g) * a) @ lp['w_down']


def _context_layer(x, c_ctx, lp):
    sh1, sc1, g1, sh2, sc2, g2 = _adaln(c_ctx[None, None, :], lp)
    h = _rmsnorm(x, lp['norm1']) * (1 + sc1) + sh1
    m = _mixer_inputs(h, lp)
    kd, vd = _mla_kv(m['ckv'], m['kpe'], lp['w_ukv'], lp['kn_d'])
    oa = _dense_attn(m['qa'], m['ka'], m['va'])
    ob = _dense_attn(m['qb'], m['kb'], m['vb'])
    oc = _dense_attn(m['qc'], m['kc'], m['vc'], lp['sink_c'])
    od = _dense_attn(m['qd'], kd, vd)
    x = x + g1 * _merge((oa, ob, oc, od), m['gates'], lp)
    h2 = _rmsnorm(x, lp['norm2']) * (1 + sc2) + sh2
    x = x + g2 * _conv_ffn(h2, lp)
    ctx = (m['ka'], m['va'], m['kb'], m['vb'], m['kc'], m['vc'], m['ckv'], m['kpe'])
    return x, ctx


def _latent_layer(x, c, ctx, lp, cos64, sin64, cos32, sin32):
    cka, cva, ckb, cvb, ckc, cvc, cckv, ckpe = ctx
    sh1, sc1, g1, sh2, sc2, g2 = _adaln(c[:, None, :], lp)
    h = _rmsnorm(x, lp['norm1']) * (1 + sc1) + sh1
    m = _mixer_inputs(h, lp)
    oa = _neighbourhood_attn(m['qa'], m['ka'], m['va'], cka, cva, lp['rpb_a'])
    ob = _blocked_global_attn(_apply_axial_rope(m['qb'], cos64, sin64),
                              _apply_axial_rope(m['kb'], cos64, sin64), m['vb'], ckb, cvb)
    oc = _window_attn(_apply_axial_rope(m['qc'], cos64, sin64),
                      _apply_axial_rope(m['kc'], cos64, sin64), m['vc'], ckc, cvc, lp['sink_c'])
    kd, vd = _mla_kv(m['ckv'], m['kpe'], lp['w_ukv'], lp['kn_d'])
    kd_ctx, vd_ctx = _mla_kv(cckv, ckpe, lp['w_ukv'], lp['kn_d'])
    od = _blocked_global_attn(_rope_tail(m['qd'], cos32, sin32), _rope_tail(kd, cos32, sin32),
                              vd, kd_ctx, vd_ctx)
    x = x + g1 * _merge((oa, ob, oc, od), m['gates'], lp)
    h2 = _rmsnorm(x, lp['norm2']) * (1 + sc2) + sh2
    return x + g2 * _conv_ffn(h2, lp)


def setup_inputs(seed: int = 0) -> dict:
    key = jax.random.key(seed)
    keys = list(jax.random.split(key, 48))
    f32 = jnp.float32

    def nrm(shape, scale=1.0):
        return jax.random.normal(keys.pop(), shape, f32) * scale

    def gain(shape):
        return 1.0 + 0.05 * jax.random.normal(keys.pop(), shape, f32)

    L = PAST_LEN
    return {
        'x_prompt': nrm((BATCH, SEQ, D_MODEL)),
        'x_sample': nrm((DEC_BATCH, DEC_SEQ, D_MODEL)),
        'cache_nat_k': nrm((DEC_BATCH, DEPTH, L, NA_HEADS, HEAD_DIM)),
        'cache_nat_v': nrm((DEC_BATCH, DEPTH, L, NA_HEADS, HEAD_DIM)),
        'cache_gqa_k': nrm((DEC_BATCH, DEPTH, L, GQ_KV_HEADS, HEAD_DIM)),
        'cache_gqa_v': nrm((DEC_BATCH, DEPTH, L, GQ_KV_HEADS, HEAD_DIM)),
        'cache_win_k': nrm((DEC_BATCH, DEPTH, L, WIN_KV_HEADS, HEAD_DIM)),
        'cache_win_v': nrm((DEC_BATCH, DEPTH, L, WIN_KV_HEADS, HEAD_DIM)),
        'cache_mla_ckv': nrm((DEC_BATCH, DEPTH, L, MLA_KV_LORA)),
        'cache_mla_kpe': nrm((DEC_BATCH, DEPTH, L, MLA_ROPE)),
        'c': nrm((DEC_BATCH, D_MODEL)),
        'c_ctx': nrm((D_MODEL,)),
        'w_ada': nrm((DEPTH, D_MODEL, 6 * D_MODEL), 0.5 * D_MODEL ** -0.5),
        'b_ada': nrm((DEPTH, 6 * D_MODEL), 0.1),
        'norm1': gain((DEPTH, D_MODEL)),
        'norm2': gain((DEPTH, D_MODEL)),
        'w_in': nrm((DEPTH, D_MODEL, IN_COLS), D_MODEL ** -0.5),
        'qn_a': gain((DEPTH, HEAD_DIM)),
        'kn_a': gain((DEPTH, HEAD_DIM)),
        'rpb_a': nrm((DEPTH, NA_HEADS, 2 * NA_ROWS - 1, 2 * NA_COLS - 1), 0.1),
        'qn_b': gain((DEPTH, HEAD_DIM)),
        'kn_b': gain((DEPTH, HEAD_DIM)),
        'qn_c': gain((DEPTH, HEAD_DIM)),
        'kn_c': gain((DEPTH, HEAD_DIM)),
        'sink_c': nrm((DEPTH, WIN_HEADS), 0.5),
        'qn_d': gain((DEPTH, MLA_QK)),
        'kn_d': gain((DEPTH, MLA_QK)),
        'kvn_d': gain((DEPTH, MLA_KV_LORA)),
        'w_ukv': nrm((DEPTH, MLA_KV_LORA, MLA_HEADS * (MLA_NOPE + MLA_V)), MLA_KV_LORA ** -0.5),
        'w_branch': nrm((DEPTH, N_BRANCH, BRANCH_W, D_MODEL), BRANCH_W ** -0.5),
        'w_out': nrm((DEPTH, D_MODEL, D_MODEL), D_MODEL ** -0.5),
        'w_up': nrm((DEPTH, D_MODEL, 2 * D_FF), D_MODEL ** -0.5),
        'conv_w': nrm((DEPTH, CONV_W, 2 * D_FF), CONV_W ** -0.5),
        'conv_b': nrm((DEPTH, 2 * D_FF), 0.02),
        'w_down': nrm((DEPTH, D_FF, D_MODEL), D_FF ** -0.5),
    }


def reference(x_prompt, x_sample, cache_nat_k, cache_nat_v, cache_gqa_k, cache_gqa_v,
              cache_win_k, cache_win_v, cache_mla_ckv, cache_mla_kpe, c, c_ctx,
              w_ada, b_ada, norm1, norm2, w_in, qn_a, kn_a, rpb_a, qn_b, kn_b, qn_c, kn_c,
              sink_c, qn_d, kn_d, kvn_d, w_ukv, w_branch, w_out, w_up, conv_w, conv_b, w_down):
    T = x_sample.shape[1]
    cos64, sin64 = _axial_cos_sin(T, HEAD_DIM)
    cos32, sin32 = _axial_cos_sin(T, MLA_ROPE)
    xp = x_prompt
    xs = x_sample
    new = [[] for _ in range(8)]
    for l in range(DEPTH):
        lp = {
            'w_ada': w_ada[l], 'b_ada': b_ada[l], 'norm1': norm1[l], 'norm2': norm2[l],
            'w_in': w_in[l], 'qn_a': qn_a[l], 'kn_a': kn_a[l], 'rpb_a': rpb_a[l],
            'qn_b': qn_b[l], 'kn_b': kn_b[l], 'qn_c': qn_c[l], 'kn_c': kn_c[l],
            'sink_c': sink_c[l], 'qn_d': qn_d[l], 'kn_d': kn_d[l], 'kvn_d': kvn_d[l],
            'w_ukv': w_ukv[l], 'w_branch': w_branch[l], 'w_out': w_out[l],
            'w_up': w_up[l], 'conv_w': conv_w[l], 'conv_b': conv_b[l], 'w_down': w_down[l],
        }
        xp, ctx_new = _context_layer(xp, c_ctx, lp)
        for lst, t in zip(new, ctx_new):
            lst.append(t)
        ctx_cached = (cache_nat_k[:, l], cache_nat_v[:, l], cache_gqa_k[:, l], cache_gqa_v[:, l],
                      cache_win_k[:, l], cache_win_v[:, l], cache_mla_ckv[:, l], cache_mla_kpe[:, l])
        xs = _latent_layer(xs, c, ctx_cached, lp, cos64, sin64, cos32, sin32)
    nat_k = jnp.stack(new[0], axis=1)
    nat_v = jnp.stack(new[1], axis=1)
    gqa_k = jnp.stack(new[2], axis=1)
    gqa_v = jnp.stack(new[3], axis=1)
    win_k = jnp.stack(new[4], axis=1)
    win_v = jnp.stack(new[5], axis=1)
    mla_ckv = jnp.stack(new[6], axis=1)
    mla_kpe = jnp.stack(new[7], axis=1)
    return (xp, xs, nat_k, nat_v, gqa_k, gqa_v, win_k, win_v, mla_ckv, mla_kpe)
```

```python
import functools

import jax
import jax.numpy as jnp
import numpy as np
from jax import lax
from jax.experimental import pallas as pl
from jax.experimental.pallas import tpu as pltpu

F32 = jnp.float32
BF16 = jnp.bfloat16

D_MODEL = 1024
BATCH = 16
SEQ = 256
DEPTH = 4
DEC_BATCH = 2
DEC_SEQ = 2048
PAST_LEN = 256
GRID_W = 64
HEAD_DIM = 64
NA_ROWS = 8
NA_COLS = 16
WINDOW = 128
MLA_NOPE = 64
MLA_ROPE = 32
MLA_QK = MLA_NOPE + MLA_ROPE
MLA_KV_LORA = 128
D_FF = 2816
ROPE_THETA = 10000.0
EPS = 1e-6
NEG_INF = -1e30

LANES = 128
TM = 256
N_CTX = BATCH * SEQ
N_LAT = DEC_BATCH * DEC_SEQ
N_TOK = N_CTX + N_LAT
CTX_TILES = N_CTX // TM
LAT_TILES_PER_BATCH = DEC_SEQ // TM
N_TILES = N_TOK // TM
GRID_ROWS = DEC_SEQ // GRID_W
FF_CHUNK = 256
N_FF_CHUNKS = D_FF // FF_CHUNK
VMEM_LIMIT = 56 * 1024 * 1024

C_QA, C_KA, C_VA = 0, 256, 512
C_QB, C_KB, C_VB = 768, 1024, 1152
C_QC, C_KC, C_VC = 1280, 1536, 1664
C_QD, C_CKV, C_KPE, C_END = 1792, 2304, 2432, 2560
N_GATE = 4 * D_MODEL
IN_QKV = 2336


def _cparams(sem):
    return pltpu.CompilerParams(dimension_semantics=sem, vmem_limit_bytes=VMEM_LIMIT)


def _tile_group(i):
    return jnp.where(i < CTX_TILES, 0, 1 + (i - CTX_TILES) // LAT_TILES_PER_BATCH)


def _tile_pos_block(i):
    return jnp.where(i < CTX_TILES, 0, 1 + (i - CTX_TILES) % LAT_TILES_PER_BATCH)


def _resident(shape, index_map):
    return pl.BlockSpec(shape, index_map, pipeline_mode=pl.Buffered(1))


def _dot(a, b):
    return jnp.dot(a, b, preferred_element_type=F32)


def _dot_t(a, b):
    return lax.dot_general(a, b, (((1,), (1,)), ((), ())), preferred_element_type=F32)


def _rms_mod(x, gain, scale, shift):
    ms = jnp.mean(x * x, axis=-1, keepdims=True)
    return (x * lax.rsqrt(ms + EPS)) * gain * (1.0 + scale) + shift


def _group_norm(t, ones_bd, inv_n, gain):
    ss = _dot((t * t).astype(BF16), ones_bd)
    return t * lax.rsqrt(ss * inv_n + EPS) * gain


def _rope(t, cos, sin_lo, sin_hi, half):
    return (t * cos + pltpu.roll(t, LANES - half, 1) * sin_lo + pltpu.roll(t, half, 1) * sin_hi)


def _lane_iota(shape):
    return lax.broadcasted_iota(jnp.int32, shape, len(shape) - 1)


def _adaln_kernel(cond_ref, w_ref, b_ref, out_ref):
    cnd = cond_ref[...]
    s = cnd * (1.0 / (1.0 + jnp.exp(-cnd)))
    out_ref[...] = jnp.dot(s, w_ref[...], preferred_element_type=F32,
                           precision=lax.Precision.HIGHEST) + b_ref[...]


def _adaln(cond8, w_ada, b_ada):
    n_col = 6 * D_MODEL
    tn = 1024
    return pl.pallas_call(
        _adaln_kernel,
        grid=(DEPTH, n_col // tn),
        in_specs=[
            pl.BlockSpec((8, D_MODEL), lambda l, n: (0, 0)),
            pl.BlockSpec((None, D_MODEL, tn), lambda l, n: (l, 0, n)),
            pl.BlockSpec((None, 1, tn), lambda l, n: (l, 0, n)),
        ],
        out_specs=pl.BlockSpec((None, 8, tn), lambda l, n: (l, 0, n)),
        out_shape=jax.ShapeDtypeStruct((DEPTH, 8, n_col), F32),
        compiler_params=_cparams(("parallel", "parallel")),
        name="adaln",
    )(cond8, w_ada, b_ada.reshape(DEPTH, 1, n_col))


_INPROJ_WIDTHS = (256, 256, 256, 256, 128, 128, 256, 128, 128, 512, 128, 128, 512, 256)


def _inproj_kernel(x_ref, mod_ref, n1_ref, w_ref, wukv_ref, g_ref, bd_ref, ones_ref, rope_ref,
                   qa_o, ka_o, va_o, qb_o, kb_o, vb_o, qc_o, kc_o, vc_o, qd_o, ckv_o, kpe_o, kd_o, vd_o):
    x = x_ref[...]
    mod = mod_ref[...]
    h = _rms_mod(x, n1_ref[...], mod[:, D_MODEL:2 * D_MODEL], mod[:, 0:D_MODEL])
    hb = h.astype(BF16)

    def proj(c0, c1):
        return _dot(hb, w_ref[:, c0:c1])

    bd = bd_ref[...]
    ones = ones_ref[...]
    cos64, slo64, shi64 = rope_ref[0], rope_ref[1], rope_ref[2]
    cosm, slom, shim = rope_ref[3], rope_ref[4], rope_ref[5]

    def rope64(t):
        parts = [_rope(t[:, c:c + LANES], cos64, slo64, shi64, 16) for c in range(0, t.shape[1], LANES)]
        return parts[0] if len(parts) == 1 else jnp.concatenate(parts, axis=1)

    inv64 = 1.0 / HEAD_DIM
    qa_o[...] = _group_norm(proj(C_QA, C_KA), bd, inv64, g_ref[0:1, :])
    ka_o[...] = _group_norm(proj(C_KA, C_VA), bd, inv64, g_ref[1:2, :])
    va_o[...] = proj(C_VA, C_QB)
    qb_o[...] = rope64(_group_norm(proj(C_QB, C_KB), bd, inv64, g_ref[2:3, :]))
    kb_o[...] = rope64(_group_norm(proj(C_KB, C_VB), bd[:LANES, :LANES], inv64, g_ref[3:4, :LANES]))
    vb_o[...] = proj(C_VB, C_QC)
    qc_o[...] = rope64(_group_norm(proj(C_QC, C_KC), bd, inv64, g_ref[4:5, :]))
    kc_o[...] = rope64(_group_norm(proj(C_KC, C_VC), bd[:LANES, :LANES], inv64, g_ref[5:6, :LANES]))
    vc_o[...] = proj(C_VC, C_QD)

    inv96 = 1.0 / MLA_QK
    qd = proj(C_QD, C_CKV)
    qn_d = g_ref[6:7, :LANES]
    qd_o[...] = jnp.concatenate(
        [_rope(_group_norm(qd[:, c:c + LANES], ones, inv96, qn_d), cosm, slom, shim, 8)
         for c in range(0, 4 * LANES, LANES)], axis=1)

    ckv = _group_norm(proj(C_CKV, C_KPE), ones, 1.0 / MLA_KV_LORA, g_ref[8:9, :LANES])
    ckv_o[...] = ckv
    kpe = proj(C_KPE, C_END)
    kpe_o[...] = kpe
    kv = _dot(ckv.astype(BF16), wukv_ref[...])
    kn_d = g_ref[7:8, :LANES]
    kd_o[...] = jnp.concatenate(
        [_rope(_group_norm(kv[:, c:c + LANES] + kpe, ones, inv96, kn_d), cosm, slom, shim, 8)
         for c in range(0, 4 * LANES, LANES)], axis=1)
    vd_o[...] = kv[:, 4 * LANES:]


def _inproj(l, x_all, mods, norm1, w_qkv, w_ukv_p, gains, bd, ones, rope_tab):
    tok = lambda w: pl.BlockSpec((TM, w), lambda i: (i, 0))
    return pl.pallas_call(
        _inproj_kernel,
        grid=(N_TILES,),
        in_specs=[
            tok(D_MODEL),
            pl.BlockSpec((None, None, 1, 6 * D_MODEL), lambda i: (l, _tile_group(i), 0, 0)),
            _resident((None, 1, D_MODEL), lambda i: (l, 0, 0)),
            _resident((None, D_MODEL, C_END), lambda i: (l, 0, 0)),
            _resident((None, MLA_KV_LORA, 768), lambda i: (l, 0, 0)),
            _resident((None, 16, 256), lambda i: (l, 0, 0)),
            _resident((256, 256), lambda i: (0, 0)),
            _resident((LANES, LANES), lambda i: (0, 0)),
            pl.BlockSpec((6, TM, LANES), lambda i: (0, _tile_pos_block(i), 0)),
        ],
        out_specs=[tok(w) for w in _INPROJ_WIDTHS],
        out_shape=[jax.ShapeDtypeStruct((N_TOK, w), F32) for w in _INPROJ_WIDTHS],
        compiler_params=_cparams(("parallel",)),
        name=f"inproj_l{l}",
    )(x_all, mods, norm1, w_qkv, w_ukv_p, gains, bd, ones, rope_tab)


def _attend(q, pieces, scale, sink=None):
    scores = []
    for k, _, bias in pieces:
        s = _dot_t(q, k) * scale
        if bias is not None:
            s = s + bias
        scores.append(s)
    m = functools.reduce(jnp.maximum, [jnp.max(s, axis=-1, keepdims=True) for s in scores])
    if sink is not None:
        m = jnp.maximum(m, sink)
    denom = None
    out = None
    for s, (_, v, _) in zip(scores, pieces):
        e = jnp.exp(s - m)
        d = jnp.sum(e, axis=-1, keepdims=True)
        o = _dot(e.astype(BF16), v)
        denom = d if denom is None else denom + d
        out = o if out is None else out + o
    if sink is not None:
        denom = denom + jnp.exp(sink - m)
    return out / denom


def _half_mask(shape, half):
    lane = _lane_iota(shape)
    return (lane < HEAD_DIM) if half == 0 else (lane >= HEAD_DIM)


def _place(t, src_half, dst_half):
    if src_half != dst_half:
        t = pltpu.roll(t, HEAD_DIM, 1)
    return jnp.where(_half_mask(t.shape, dst_half), t, 0.0)


def _pair_attention(q_chunk, kv_half, pieces, scale, sinks=None):
    m_rows = q_chunk.shape[0]
    halves = (0, 1) if kv_half is None else (kv_half, kv_half)
    q2 = jnp.concatenate([_place(q_chunk, 0, halves[0]), _place(q_chunk, 1, halves[1])], axis=0).astype(BF16)
    pieces2 = []
    for k, v, bias in pieces:
        pieces2.append((k, v, bias))
    sink = None
    if sinks is not None:
        row = lax.broadcasted_iota(jnp.int32, (2 * m_rows, 1), 0)
        sink = jnp.where(row < m_rows, sinks[0], sinks[1])
    o2 = _attend(q2, pieces2, scale, sink)
    lo = _place(o2[:m_rows], halves[0], 0)
    hi = _place(o2[m_rows:], halves[1], 1)
    return lo + hi


def _ctx_attn_kernel(sink_ref, qa, ka, va, qb, kb, vb, qc, kc, vc, qd, kd, vd, o_ref):
    scale = HEAD_DIM ** -0.5
    for j in range(2):
        c = slice(j * LANES, (j + 1) * LANES)
        k = ka[:, c].astype(BF16)
        v = va[:, c].astype(BF16)
        o_ref[:, j * LANES:(j + 1) * LANES] = _pair_attention(qa[:, c], None, [(k, v, None)], scale)
    for idx, (q, k_ref, v_ref) in enumerate(((qb, kb, vb), (qc, kc, vc))):
        k = k_ref[...].astype(BF16)
        v = v_ref[...].astype(BF16)
        for j in range(2):
            c = slice(j * LANES, (j + 1) * LANES)
            sinks = (sink_ref[2 * j], sink_ref[2 * j + 1]) if idx == 1 else None
            base = (1 + idx) * 256 + j * LANES
            o_ref[:, base:base + LANES] = _pair_attention(q[:, c], j, [(k, v, None)], scale, sinks)
    scale_d = MLA_QK ** -0.5
    for j in range(2):
        v = vd[:, j * LANES:(j + 1) * LANES].astype(BF16)
        outs = []
        for hh in range(2):
            hd = 2 * j + hh
            c = slice(hd * LANES, (hd + 1) * LANES)
            o = _attend(qd[:, c].astype(BF16), [(kd[:, c].astype(BF16), v, None)], scale_d)
            outs.append(jnp.where(_half_mask(o.shape, hh), o, 0.0))
        base = 3 * 256 + j * LANES
        o_ref[:, base:base + LANES] = outs[0] + outs[1]


def _ctx_attention(l, sink_c, qa, ka, va, qb, kb, vb, qc, kc, vc, qd, kd, vd):
    blk = lambda w: pl.BlockSpec((SEQ, w), lambda b: (b, 0))
    widths = (256, 256, 256, 256, 128, 128, 256, 128, 128, 512, 512, 256)
    return pl.pallas_call(
        _ctx_attn_kernel,
        grid=(BATCH,),
        in_specs=[pl.BlockSpec(memory_space=pltpu.SMEM)] + [blk(w) for w in widths],
        out_specs=pl.BlockSpec((SEQ, D_MODEL), lambda b: (b, 0)),
        out_shape=jax.ShapeDtypeStruct((N_TOK, D_MODEL), F32),
        compiler_params=_cparams(("parallel",)),
        name=f"ctx_attn_l{l}",
    )(sink_c, qa, ka, va, qb, kb, vb, qc, kc, vc, qd, kd, vd)


LAT_TQ = 128
LAT_QBLOCKS = DEC_SEQ // LAT_TQ
LAT_ROW0 = N_CTX // LAT_TQ
LAT_KV0 = N_CTX // DEC_SEQ


def _lat_qspec(width):
    return pl.BlockSpec((LAT_TQ, width), lambda b, i: (LAT_ROW0 + b * LAT_QBLOCKS + i, 0))


def _lat_kvspec(width):
    return pl.BlockSpec((DEC_SEQ, width), lambda b, i: (LAT_KV0 + b, 0))


def _lat_ospec(mixer):
    return pl.BlockSpec((LAT_TQ, 256), lambda b, i: (LAT_ROW0 + b * LAT_QBLOCKS + i, mixer))


def _lat_call(kernel, l, name, in_specs, args, o_buf, mixer, grid=None, ospec=None):
    return pl.pallas_call(
        kernel,
        grid=grid or (DEC_BATCH, LAT_QBLOCKS),
        in_specs=in_specs + [pl.BlockSpec(memory_space=pl.ANY)],
        out_specs=ospec or _lat_ospec(mixer),
        out_shape=jax.ShapeDtypeStruct((N_TOK, D_MODEL), F32),
        input_output_aliases={len(in_specs): 0},
        compiler_params=_cparams(("parallel", "parallel")),
        name=f"{name}_l{l}",
    )(*args, o_buf)


def _gqa_kernel(q_ref, k_ref, v_ref, kc_ref, vc_ref, _, o_ref):
    scale = HEAD_DIM ** -0.5
    pieces = [(kc_ref[...].astype(BF16), vc_ref[...].astype(BF16), None),
              (k_ref[...].astype(BF16), v_ref[...].astype(BF16), None)]
    for j in range(2):
        c = slice(j * LANES, (j + 1) * LANES)
        o_ref[:, c] = _pair_attention(q_ref[:, c], j, pieces, scale)


def _lat_gqa(l, qb, kb, vb, cache_k, cache_v, o_buf):
    cspec = pl.BlockSpec((None, None, PAST_LEN, LANES), lambda b, i: (b, l, 0, 0))
    return _lat_call(_gqa_kernel, l, "lat_gqa",
                     [_lat_qspec(256), _lat_kvspec(LANES), _lat_kvspec(LANES), cspec, cspec],
                     (qb, kb, vb, cache_k, cache_v), o_buf, 1)


WIN_KEYS = 3 * LAT_TQ


def _win_kernel(sink_ref, q_ref, k_ref, v_ref, kc_ref, vc_ref, _, o_ref):
    scale = HEAD_DIM ** -0.5
    i = pl.program_id(1)
    start = pl.multiple_of(jnp.clip((i - 1) * LAT_TQ, 0, DEC_SEQ - WIN_KEYS), LAT_TQ)
    k = k_ref[pl.ds(start, WIN_KEYS), :].astype(BF16)
    v = v_ref[pl.ds(start, WIN_KEYS), :].astype(BF16)
    row = lax.broadcasted_iota(jnp.int32, (2 * LAT_TQ, WIN_KEYS), 0)
    qpos = i * LAT_TQ + jnp.where(row < LAT_TQ, row, row - LAT_TQ)
    kpos = start + lax.broadcasted_iota(jnp.int32, (2 * LAT_TQ, WIN_KEYS), 1)
    bias = jnp.where(jnp.abs(qpos - kpos) <= WINDOW, 0.0, NEG_INF)
    pieces = [(kc_ref[...].astype(BF16), vc_ref[...].astype(BF16), None), (k, v, bias)]
    for j in range(2):
        c = slice(j * LANES, (j + 1) * LANES)
        sinks = (sink_ref[2 * j], sink_ref[2 * j + 1])
        o_ref[:, c] = _pair_attention(q_ref[:, c], j, pieces, scale, sinks)


def _lat_window(l, sink_c, qc, kc, vc, cache_k, cache_v, o_buf):
    cspec = pl.BlockSpec((None, None, PAST_LEN, LANES), lambda b, i: (b, l, 0, 0))
    return _lat_call(_win_kernel, l, "lat_win",
                     [pl.BlockSpec(memory_space=pltpu.SMEM), _lat_qspec(256), _lat_kvspec(LANES),
                      _lat_kvspec(LANES), cspec, cspec],
                     (sink_c, qc, kc, vc, cache_k, cache_v), o_buf, 2)


def _mla_kernel(q_ref, k_ref, v_ref, kc_ref, vc_ref, _, o_ref):
    scale = MLA_QK ** -0.5
    for j in range(2):
        vs = slice(j * LANES, (j + 1) * LANES)
        v = v_ref[:, vs].astype(BF16)
        vc = vc_ref[:, vs].astype(BF16)
        outs = []
        for hh in range(2):
            hd = 2 * j + hh
            c = slice(hd * LANES, (hd + 1) * LANES)
            pieces = [(kc_ref[:, c].astype(BF16), vc, None), (k_ref[:, c].astype(BF16), v, None)]
            o = _attend(q_ref[:, c].astype(BF16), pieces, scale)
            outs.append(jnp.where(_half_mask(o.shape, hh), o, 0.0))
        o_ref[:, vs] = outs[0] + outs[1]


def _lat_mla(l, qd, kd, vd, kd_ctx, vd_ctx, o_buf):
    kcs = pl.BlockSpec((None, None, PAST_LEN, 512), lambda b, i: (l, b, 0, 0))
    vcs = pl.BlockSpec((None, None, PAST_LEN, 256), lambda b, i: (l, b, 0, 0))
    return _lat_call(_mla_kernel, l, "lat_mla",
                     [_lat_qspec(512), _lat_kvspec(512), _lat_kvspec(256), kcs, vcs],
                     (qd, kd, vd, kd_ctx, vd_ctx), o_buf, 3)


NA_KEYS = NA_ROWS * GRID_W


def _na_kernel(q_ref, k_ref, v_ref, kc_ref, vc_ref, bias_ref, _, o_ref):
    scale = HEAD_DIM ** -0.5
    r = pl.program_id(1)
    r0 = jnp.clip(r - NA_ROWS // 2, 0, GRID_ROWS - NA_ROWS)
    start = pl.multiple_of(r0 * GRID_W, GRID_W)
    for j in range(2):
        c = slice(j * LANES, (j + 1) * LANES)
        k = k_ref[pl.ds(start, NA_KEYS), c].astype(BF16)
        v = v_ref[pl.ds(start, NA_KEYS), c].astype(BF16)
        bias = jnp.concatenate([bias_ref[2 * j], bias_ref[2 * j + 1]], axis=0)
        pieces = [(kc_ref[:, c].astype(BF16), vc_ref[:, c].astype(BF16), None), (k, v, bias)]
        o_ref[:, c] = _pair_attention(q_ref[:, c], None, pieces, scale)


def _na_shift(r):
    return jnp.where(r < NA_ROWS // 2, r, jnp.where(r > GRID_ROWS - NA_ROWS // 2, r - (GRID_ROWS - NA_ROWS), NA_ROWS // 2))


def _lat_na(l, qa, ka, va, cache_k, cache_v, na_bias, o_buf):
    row0 = N_CTX // GRID_W
    qspec = pl.BlockSpec((GRID_W, 256), lambda b, r: (row0 + b * GRID_ROWS + r, 0))
    ospec = pl.BlockSpec((GRID_W, 256), lambda b, r: (row0 + b * GRID_ROWS + r, 0))
    cspec = pl.BlockSpec((None, None, PAST_LEN, 256), lambda b, r: (b, l, 0, 0))
    bspec = pl.BlockSpec((None, 4, None, GRID_W, NA_KEYS), lambda b, r: (l, 0, _na_shift(r), 0, 0))
    return _lat_call(_na_kernel, l, "lat_na",
                     [qspec, _lat_kvspec(256), _lat_kvspec(256), cspec, cspec, bspec],
                     (qa, ka, va, cache_k, cache_v, na_bias), o_buf, 0,
                     grid=(DEC_BATCH, GRID_ROWS), ospec=ospec)


def _na_bias_kernel(rpb_ref, out_ref, t_ref):
    l = pl.program_id(0)
    n_dr, n_dc = 2 * NA_ROWS - 1, 2 * NA_COLS - 1
    shape = (GRID_W, LANES)
    qc = lax.broadcasted_iota(jnp.int32, shape, 0)
    lane = _lane_iota(shape)
    kc = lane & (GRID_W - 1)
    dc = jnp.clip(kc - qc, 1 - NA_COLS, NA_COLS - 1) + (NA_COLS - 1)
    c0 = jnp.clip(qc - NA_COLS // 2, 0, GRID_W - NA_COLS)
    ok = (kc >= c0) & (kc < c0 + NA_COLS)
    for hd in range(4):
        def fill(a, carry):
            base = ((l * 4 + hd) * n_dr + a) * n_dc
            t = jnp.zeros(shape, F32)
            for b in range(n_dc):
                t = jnp.where(dc == b, rpb_ref[base + b], t)
            t_ref[a] = jnp.where(ok, t, NEG_INF)
            return carry
        lax.fori_loop(0, n_dr, fill, 0)
        for d in range(NA_ROWS):
            for m in range(NA_ROWS // 2):
                lo = t_ref[2 * m - d + NA_ROWS - 1]
                hi = t_ref[2 * m + 1 - d + NA_ROWS - 1]
                out_ref[hd, d, :, m * LANES:(m + 1) * LANES] = jnp.where(lane < GRID_W, lo, hi)


def _na_bias(rpb_flat):
    return pl.pallas_call(
        _na_bias_kernel,
        grid=(DEPTH,),
        in_specs=[pl.BlockSpec(memory_space=pltpu.SMEM)],
        out_specs=pl.BlockSpec((None, 4, NA_ROWS, GRID_W, NA_KEYS), lambda l: (l, 0, 0, 0, 0)),
        out_shape=jax.ShapeDtypeStruct((DEPTH, 4, NA_ROWS, GRID_W, NA_KEYS), F32),
        scratch_shapes=[pltpu.VMEM((2 * NA_ROWS - 1, GRID_W, LANES), F32)],
        compiler_params=_cparams(("parallel",)),
        name="na_bias",
    )(rpb_flat)


def _mla_ctx_kernel(ckv_ref, kpe_ref, wukv_ref, g_ref, ones_ref, kd_o, vd_o):
    kv = _dot(ckv_ref[...].astype(BF16), wukv_ref[...])
    kpe = kpe_ref[...]
    kn_d = g_ref[7:8, :LANES]
    ones = ones_ref[...]
    kd_o[...] = jnp.concatenate(
        [_group_norm(kv[:, c:c + LANES] + kpe, ones, 1.0 / MLA_QK, kn_d) for c in range(0, 4 * LANES, LANES)],
        axis=1)
    vd_o[...] = kv[:, 4 * LANES:]


def _mla_ctx(cache_ckv, cache_kpe_p, w_ukv_p, gains, ones):
    return pl.pallas_call(
        _mla_ctx_kernel,
        grid=(DEPTH, DEC_BATCH),
        in_specs=[
            pl.BlockSpec((None, None, PAST_LEN, LANES), lambda l, b: (b, l, 0, 0)),
            pl.BlockSpec((None, None, PAST_LEN, LANES), lambda l, b: (b, l, 0, 0)),
            pl.BlockSpec((None, MLA_KV_LORA, 768), lambda l, b: (l, 0, 0)),
            pl.BlockSpec((None, 16, 256), lambda l, b: (l, 0, 0)),
            pl.BlockSpec((LANES, LANES), lambda l, b: (0, 0)),
        ],
        out_specs=[pl.BlockSpec((None, None, PAST_LEN, 512), lambda l, b: (l, b, 0, 0)),
                   pl.BlockSpec((None, None, PAST_LEN, 256), lambda l, b: (l, b, 0, 0))],
        out_shape=[jax.ShapeDtypeStruct((DEPTH, DEC_BATCH, PAST_LEN, 512), F32),
                   jax.ShapeDtypeStruct((DEPTH, DEC_BATCH, PAST_LEN, 256), F32)],
        compiler_params=_cparams(("parallel", "parallel")),
        name="mla_ctx",
    )(cache_ckv, cache_kpe_p, w_ukv_p, gains, ones)


def _merge_kernel(x_ref, o_ref, mod_ref, n1_ref, wg_ref, wb_ref, wo_ref, x1_ref):
    x = x_ref[...]
    mod = mod_ref[...]
    h = _rms_mod(x, n1_ref[...], mod[:, D_MODEL:2 * D_MODEL], mod[:, 0:D_MODEL])
    hb = h.astype(BF16)
    merged = None
    for n in range(4):
        gate = _dot(hb, wg_ref[:, n * D_MODEL:(n + 1) * D_MODEL])
        gate = 1.0 / (1.0 + jnp.exp(-gate))
        br = _dot(o_ref[:, n * 256:(n + 1) * 256].astype(BF16), wb_ref[n])
        merged = gate * br if merged is None else merged + gate * br
    out = _dot(merged.astype(BF16), wo_ref[...])
    x1_ref[...] = x + mod[:, 2 * D_MODEL:3 * D_MODEL] * out


def _merge(l, x_all, o_buf, mods, norm1, w_gate, w_branch, w_out):
    tok = pl.BlockSpec((TM, D_MODEL), lambda i: (i, 0))
    return pl.pallas_call(
        _merge_kernel,
        grid=(N_TILES,),
        in_specs=[
            tok, tok,
            pl.BlockSpec((None, None, 1, 6 * D_MODEL), lambda i: (l, _tile_group(i), 0, 0)),
            _resident((None, 1, D_MODEL), lambda i: (l, 0, 0)),
            _resident((None, D_MODEL, N_GATE), lambda i: (l, 0, 0)),
            _resident((None, 4, 256, D_MODEL), lambda i: (l, 0, 0, 0)),
            _resident((None, D_MODEL, D_MODEL), lambda i: (l, 0, 0)),
        ],
        out_specs=tok,
        out_shape=jax.ShapeDtypeStruct((N_TOK, D_MODEL), F32),
        compiler_params=_cparams(("parallel",)),
        name=f"merge_l{l}",
    )(x_all, o_buf, mods, norm1, w_gate, w_branch, w_out)


HALO = 8


def _ffn_kernel(x_ref, xp_ref, xn_ref, mod_ref, n2_ref, wup_ref, cw_ref, cb_ref, wdn_ref, y_ref, acc_ref):
    i = pl.program_id(0)
    p = (i - CTX_TILES) % LAT_TILES_PER_BATCH
    has_prev = jnp.where((i >= CTX_TILES) & (p != 0), 1.0, 0.0)
    has_next = jnp.where((i >= CTX_TILES) & (p != LAT_TILES_PER_BATCH - 1), 1.0, 0.0)
    x = x_ref[...]
    mod = mod_ref[...]
    n2 = n2_ref[...]
    sh2, sc2 = mod[:, 3 * D_MODEL:4 * D_MODEL], mod[:, 4 * D_MODEL:5 * D_MODEL]
    hb = _rms_mod(x, n2, sc2, sh2).astype(BF16)
    halo = jnp.concatenate([xp_ref[...], xn_ref[...]], axis=0)
    hhb = _rms_mod(halo, n2, sc2, sh2).astype(BF16)
    row = lax.broadcasted_iota(jnp.int32, (TM, 2 * FF_CHUNK), 0)
    acc_ref[...] = jnp.zeros_like(acc_ref)

    def chunk(c, carry):
        w = wup_ref[c]
        u = _dot(hb, w)
        uh = _dot(hhb, w)
        u_prev = jnp.where(row == 0, uh[HALO - 1:HALO] * has_prev, pltpu.roll(u, 1, 0))
        u_next = jnp.where(row == TM - 1, uh[HALO:HALO + 1] * has_next, pltpu.roll(u, TM - 1, 0))
        cw = cw_ref[c]
        t = cb_ref[c] + u_prev * cw[0:1] + u * cw[1:2] + u_next * cw[2:3]
        a, g = t[:, :FF_CHUNK], t[:, FF_CHUNK:]
        act = (g * (1.0 / (1.0 + jnp.exp(-g)))) * a
        acc_ref[...] += _dot(act.astype(BF16), wdn_ref[c])
        return carry

    lax.fori_loop(0, N_FF_CHUNKS, chunk, 0)
    y_ref[...] = x + mod[:, 5 * D_MODEL:6 * D_MODEL] * acc_ref[...]


def _ffn(l, x1, mods, norm2, w_up_r, cw_r, cb_r, w_dn_r):
    tok = pl.BlockSpec((TM, D_MODEL), lambda i: (i, 0))
    per = TM // HALO
    last = N_TOK // HALO - 1
    return pl.pallas_call(
        _ffn_kernel,
        grid=(N_TILES,),
        in_specs=[
            tok,
            pl.BlockSpec((HALO, D_MODEL), lambda i: (jnp.maximum(i * per - 1, 0), 0)),
            pl.BlockSpec((HALO, D_MODEL), lambda i: (jnp.minimum((i + 1) * per, last), 0)),
            pl.BlockSpec((None, None, 1, 6 * D_MODEL), lambda i: (l, _tile_group(i), 0, 0)),
            _resident((None, 1, D_MODEL), lambda i: (l, 0, 0)),
            _resident((None, N_FF_CHUNKS, D_MODEL, 2 * FF_CHUNK), lambda i: (l, 0, 0, 0)),
            _resident((None, N_FF_CHUNKS, 3, 2 * FF_CHUNK), lambda i: (l, 0, 0, 0)),
            _resident((None, N_FF_CHUNKS, 1, 2 * FF_CHUNK), lambda i: (l, 0, 0, 0)),
            _resident((None, N_FF_CHUNKS, FF_CHUNK, D_MODEL), lambda i: (l, 0, 0, 0)),
        ],
        out_specs=tok,
        out_shape=jax.ShapeDtypeStruct((N_TOK, D_MODEL), F32),
        scratch_shapes=[pltpu.VMEM((TM, D_MODEL), F32)],
        compiler_params=_cparams(("parallel",)),
        name=f"ffn_l{l}",
    )(x1, x1, x1, mods, norm2, w_up_r, cw_r, cb_r, w_dn_r)


def _rope_tables():
    t = np.arange(DEC_SEQ)

    def angles(dim):
        half = dim // 2
        inv = ROPE_THETA ** (-np.arange(0, half, 2, dtype=np.float32) / half)
        row = (t // GRID_W).astype(np.float32)[:, None] * inv[None, :]
        col = (t % GRID_W).astype(np.float32)[:, None] * inv[None, :]
        return np.concatenate([row, row, col, col], axis=-1).astype(np.float32)

    def signed(ang, quarter):
        cos, sin = np.cos(ang), np.sin(ang)
        first = (np.arange(ang.shape[1]) % (2 * quarter)) < quarter
        return cos, np.where(first, -sin, 0.0), np.where(first, 0.0, sin)

    c64, lo64, hi64 = signed(angles(HEAD_DIM), 16)
    c64, lo64, hi64 = (np.tile(a, (1, 2)) for a in (c64, lo64, hi64))
    cm, lom, him = signed(angles(MLA_ROPE), 8)

    def mla_pad(a, fill):
        out = np.full((DEC_SEQ, LANES), fill, np.float32)
        out[:, MLA_NOPE:MLA_QK] = a
        return out

    tabs = [c64, lo64, hi64, mla_pad(cm, 1.0), mla_pad(lom, 0.0), mla_pad(him, 0.0)]
    ident = [np.ones, np.zeros, np.zeros, np.ones, np.zeros, np.zeros]
    full = [np.concatenate([f((TM, LANES), np.float32), a.astype(np.float32)], axis=0) for f, a in zip(ident, tabs)]
    return jnp.asarray(np.stack(full, axis=0))


def _block_diag_ones():
    idx = np.arange(256) // HEAD_DIM
    return jnp.asarray((idx[:, None] == idx[None, :]).astype(np.float32), dtype=BF16)


def _prep_weights(w_in, w_ukv, w_branch, w_out, w_up, conv_w, conv_b, w_down,
                  qn_a, kn_a, qn_b, kn_b, qn_c, kn_c, qn_d, kn_d, kvn_d):
    z = lambda n: jnp.zeros((DEPTH, D_MODEL, n), w_in.dtype)
    qd = w_in[:, :, 1792:2176].reshape(DEPTH, D_MODEL, 4, MLA_QK)
    qd = jnp.pad(qd, ((0, 0), (0, 0), (0, 0), (0, LANES - MLA_QK))).reshape(DEPTH, D_MODEL, 4 * LANES)
    kpe = jnp.concatenate([z(MLA_NOPE), w_in[:, :, 2304:2336], z(LANES - MLA_QK)], axis=-1)
    w_qkv = jnp.concatenate([w_in[:, :, :1792], qd, w_in[:, :, 2176:2304], kpe], axis=-1).astype(BF16)
    w_gate = w_in[:, :, IN_QKV:].astype(BF16)
    ukv = w_ukv.reshape(DEPTH, MLA_KV_LORA, 4, 2, HEAD_DIM)
    k_part = jnp.pad(ukv[:, :, :, 0], ((0, 0), (0, 0), (0, 0), (0, HEAD_DIM))).reshape(DEPTH, MLA_KV_LORA, 512)
    v_part = ukv[:, :, :, 1].reshape(DEPTH, MLA_KV_LORA, 256)
    w_ukv_p = jnp.concatenate([k_part, v_part], axis=-1).astype(BF16)

    def chunked(a):
        r = a.shape[1]
        a = a.reshape(DEPTH, r, 2, N_FF_CHUNKS, FF_CHUNK)
        return a.transpose(0, 3, 1, 2, 4).reshape(DEPTH, N_FF_CHUNKS, r, 2 * FF_CHUNK)

    w_up_r = chunked(w_up).astype(BF16)
    cw_r = chunked(conv_w)
    cb_r = chunked(conv_b[:, None, :])
    w_dn_r = w_down.reshape(DEPTH, N_FF_CHUNKS, FF_CHUNK, D_MODEL).astype(BF16)

    def row(g, reps, width=256):
        t = jnp.tile(g, (1, reps))
        return jnp.pad(t, ((0, 0), (0, width - t.shape[1])))

    pad96 = lambda g: jnp.pad(g, ((0, 0), (0, LANES - MLA_QK)))
    rows = [row(qn_a, 4), row(kn_a, 4), row(qn_b, 4), row(kn_b, 2), row(qn_c, 4), row(kn_c, 2),
            row(pad96(qn_d), 1), row(pad96(kn_d), 1), row(kvn_d, 1)]
    gains = jnp.stack(rows + [jnp.zeros_like(rows[0])] * (16 - len(rows)), axis=1)
    return (w_qkv, w_gate, w_ukv_p, w_branch.astype(BF16), w_out.astype(BF16), w_up_r, cw_r, cb_r, w_dn_r, gains)


def kernel(x_prompt, x_sample, cache_nat_k, cache_nat_v, cache_gqa_k, cache_gqa_v, cache_win_k, cache_win_v,
           cache_mla_ckv, cache_mla_kpe, c, c_ctx, w_ada, b_ada, norm1, norm2, w_in, qn_a, kn_a, rpb_a,
           qn_b, kn_b, qn_c, kn_c, sink_c, qn_d, kn_d, kvn_d, w_ukv, w_branch, w_out, w_up, conv_w, conv_b,
           w_down):
    (w_qkv, w_gate, w_ukv_p, w_branch_b, w_out_b, w_up_r, cw_r, cb_r, w_dn_r, gains) = _prep_weights(
        w_in, w_ukv, w_branch, w_out, w_up, conv_w, conv_b, w_down,
        qn_a, kn_a, qn_b, kn_b, qn_c, kn_c, qn_d, kn_d, kvn_d)
    rope_tab = _rope_tables()
    bd = _block_diag_ones()
    ones = jnp.ones((LANES, LANES), BF16)
    norm1_r = norm1.reshape(DEPTH, 1, D_MODEL)
    norm2_r = norm2.reshape(DEPTH, 1, D_MODEL)

    cond8 = jnp.zeros((8, D_MODEL), F32).at[0].set(c_ctx).at[1:1 + DEC_BATCH].set(c)
    mods = _adaln(cond8, w_ada, b_ada).reshape(DEPTH, 8, 1, 6 * D_MODEL)

    rpb_flat = jnp.pad(rpb_a.reshape(-1), (0, 8192 - rpb_a.size))
    na_bias = _na_bias(rpb_flat)
    kpe_p = jnp.pad(cache_mla_kpe, ((0, 0), (0, 0), (0, 0), (MLA_NOPE, LANES - MLA_QK)))
    kd_ctx, vd_ctx = _mla_ctx(cache_mla_ckv, kpe_p, w_ukv_p, gains, ones)
    c_nat_k = cache_nat_k.reshape(DEC_BATCH, DEPTH, PAST_LEN, 256)
    c_nat_v = cache_nat_v.reshape(DEC_BATCH, DEPTH, PAST_LEN, 256)
    c_gqa_k = cache_gqa_k.reshape(DEC_BATCH, DEPTH, PAST_LEN, LANES)
    c_gqa_v = cache_gqa_v.reshape(DEC_BATCH, DEPTH, PAST_LEN, LANES)
    c_win_k = cache_win_k.reshape(DEC_BATCH, DEPTH, PAST_LEN, LANES)
    c_win_v = cache_win_v.reshape(DEC_BATCH, DEPTH, PAST_LEN, LANES)

    x = jnp.concatenate([x_prompt.reshape(N_CTX, D_MODEL), x_sample.reshape(N_LAT, D_MODEL)], axis=0)
    new = [[] for _ in range(8)]
    for l in range(DEPTH):
        (qa, ka, va, qb, kb, vb, qc, kc, vc, qd, ckv, kpe, kd, vd) = _inproj(
            l, x, mods, norm1_r, w_qkv, w_ukv_p, gains, bd, ones, rope_tab)
        sink_l = sink_c[l]
        o = _ctx_attention(l, sink_l, qa, ka, va, qb, kb, vb, qc, kc, vc, qd, kd, vd)
        o = _lat_na(l, qa, ka, va, c_nat_k, c_nat_v, na_bias, o)
        o = _lat_gqa(l, qb, kb, vb, c_gqa_k, c_gqa_v, o)
        o = _lat_window(l, sink_l, qc, kc, vc, c_win_k, c_win_v, o)
        o = _lat_mla(l, qd, kd, vd, kd_ctx, vd_ctx, o)
        x1 = _merge(l, x, o, mods, norm1_r, w_gate, w_branch_b, w_out_b)
        x = _ffn(l, x1, mods, norm2_r, w_up_r, cw_r, cb_r, w_dn_r)
        ctx = lambda a, h: a[:N_CTX].reshape(BATCH, SEQ, h, HEAD_DIM)
        for lst, t in zip(new, (ctx(ka, 4), ctx(va, 4), ctx(kb, 2), ctx(vb, 2), ctx(kc, 2), ctx(vc, 2),
                                ckv[:N_CTX].reshape(BATCH, SEQ, MLA_KV_LORA),
                                kpe[:N_CTX, MLA_NOPE:MLA_QK].reshape(BATCH, SEQ, MLA_ROPE))):
            lst.append(t)
    outs = [jnp.stack(lst, axis=1) for lst in new]
    y_prompt = x[:N_CTX].reshape(BATCH, SEQ, D_MODEL)
    y_sample = x[N_CTX:].reshape(DEC_BATCH, DEC_SEQ, D_MODEL)
    return (y_prompt, y_sample, *outs)
```

```python
import functools

import jax
import jax.numpy as jnp
import numpy as np
from jax import lax
from jax.experimental import pallas as pl
from jax.experimental.pallas import tpu as pltpu

F32 = jnp.float32
BF16 = jnp.bfloat16

D_MODEL = 1024
BATCH = 16
SEQ = 256
DEPTH = 4
DEC_BATCH = 2
DEC_SEQ = 2048
PAST_LEN = 256
GRID_W = 64
HEAD_DIM = 64
NA_ROWS = 8
NA_COLS = 16
WINDOW = 128
MLA_NOPE = 64
MLA_ROPE = 32
MLA_QK = MLA_NOPE + MLA_ROPE
MLA_KV_LORA = 128
D_FF = 2816
ROPE_THETA = 10000.0
EPS = 1e-6
NEG_INF = -1e30

LANES = 128
TM = 256
N_CTX = BATCH * SEQ
N_LAT = DEC_BATCH * DEC_SEQ
N_TOK = N_CTX + N_LAT
CTX_TILES = N_CTX // TM
LAT_TILES_PER_BATCH = DEC_SEQ // TM
N_TILES = N_TOK // TM
GRID_ROWS = DEC_SEQ // GRID_W
FF_CHUNK = 256
N_FF_CHUNKS = D_FF // FF_CHUNK
VMEM_LIMIT = 56 * 1024 * 1024

C_QA, C_KA, C_VA = 0, 256, 512
C_QB, C_KB, C_VB = 768, 1024, 1152
C_QC, C_KC, C_VC = 1280, 1536, 1664
C_QD, C_CKV, C_KPE, C_END = 1792, 2304, 2432, 2560
N_GATE = 4 * D_MODEL
IN_QKV = 2336


def _cparams(sem):
    return pltpu.CompilerParams(dimension_semantics=sem, vmem_limit_bytes=VMEM_LIMIT)


def _tile_group(i):
    return jnp.where(i < CTX_TILES, 0, 1 + (i - CTX_TILES) // LAT_TILES_PER_BATCH)


def _tile_pos_block(i):
    return jnp.where(i < CTX_TILES, 0, 1 + (i - CTX_TILES) % LAT_TILES_PER_BATCH)


def _resident(shape, index_map):
    return pl.BlockSpec(shape, index_map, pipeline_mode=pl.Buffered(1))


def _dot(a, b):
    return jnp.dot(a, b, preferred_element_type=F32)


def _dot_t(a, b):
    return lax.dot_general(a, b, (((1,), (1,)), ((), ())), preferred_element_type=F32)


def _rms_mod(x, gain, scale, shift):
    ms = jnp.mean(x * x, axis=-1, keepdims=True)
    return (x * lax.rsqrt(ms + EPS)) * gain * (1.0 + scale) + shift


def _group_norm(t, ones_bd, inv_n, gain):
    ss = _dot((t * t).astype(BF16), ones_bd)
    return t * lax.rsqrt(ss * inv_n + EPS) * gain


def _rope(t, cos, sin_lo, sin_hi, half):
    return (t * cos + pltpu.roll(t, LANES - half, 1) * sin_lo + pltpu.roll(t, half, 1) * sin_hi)


def _lane_iota(shape):
    return lax.broadcasted_iota(jnp.int32, shape, len(shape) - 1)


def _adaln_kernel(cond_ref, w_ref, b_ref, out_ref):
    cnd = cond_ref[...]
    s = cnd * (1.0 / (1.0 + jnp.exp(-cnd)))
    out_ref[...] = jnp.dot(s, w_ref[...], preferred_element_type=F32,
                           precision=lax.Precision.HIGHEST) + b_ref[...]


def _adaln(cond8, w_ada, b_ada):
    n_col = 6 * D_MODEL
    tn = 1024
    return pl.pallas_call(
        _adaln_kernel,
        grid=(DEPTH, n_col // tn),
        in_specs=[
            pl.BlockSpec((8, D_MODEL), lambda l, n: (0, 0)),
            pl.BlockSpec((None, D_MODEL, tn), lambda l, n: (l, 0, n)),
            pl.BlockSpec((None, 1, tn), lambda l, n: (l, 0, n)),
        ],
        out_specs=pl.BlockSpec((None, 8, tn), lambda l, n: (l, 0, n)),
        out_shape=jax.ShapeDtypeStruct((DEPTH, 8, n_col), F32),
        compiler_params=_cparams(("parallel", "parallel")),
        name="adaln",
    )(cond8, w_ada, b_ada.reshape(DEPTH, 1, n_col))


_INPROJ_WIDTHS = (256, 256, 256, 256, 128, 128, 256, 128, 128, 512, 128, 128, 512, 256)


def _inproj_kernel(x_ref, mod_ref, n1_ref, w_ref, wukv_ref, g_ref, bd_ref, ones_ref, rope_ref,
                   qa_o, ka_o, va_o, qb_o, kb_o, vb_o, qc_o, kc_o, vc_o, qd_o, ckv_o, kpe_o, kd_o, vd_o):
    x = x_ref[...]
    mod = mod_ref[...]
    h = _rms_mod(x, n1_ref[...], mod[:, D_MODEL:2 * D_MODEL], mod[:, 0:D_MODEL])
    hb = h.astype(BF16)

    def proj(c0, c1):
        return _dot(hb, w_ref[:, c0:c1])

    bd = bd_ref[...]
    ones = ones_ref[...]
    cos64, slo64, shi64 = rope_ref[0], rope_ref[1], rope_ref[2]
    cosm, slom, shim = rope_ref[3], rope_ref[4], rope_ref[5]

    def rope64(t):
        parts = [_rope(t[:, c:c + LANES], cos64, slo64, shi64, 16) for c in range(0, t.shape[1], LANES)]
        return parts[0] if len(parts) == 1 else jnp.concatenate(parts, axis=1)

    inv64 = 1.0 / HEAD_DIM
    qa_o[...] = _group_norm(proj(C_QA, C_KA), bd, inv64, g_ref[0:1, :])
    ka_o[...] = _group_norm(proj(C_KA, C_VA), bd, inv64, g_ref[1:2, :])
    va_o[...] = proj(C_VA, C_QB)
    qb_o[...] = rope64(_group_norm(proj(C_QB, C_KB), bd, inv64, g_ref[2:3, :]))
    kb_o[...] = rope64(_group_norm(proj(C_KB, C_VB), bd[:LANES, :LANES], inv64, g_ref[3:4, :LANES]))
    vb_o[...] = proj(C_VB, C_QC)
    qc_o[...] = rope64(_group_norm(proj(C_QC, C_KC), bd, inv64, g_ref[4:5, :]))
    kc_o[...] = rope64(_group_norm(proj(C_KC, C_VC), bd[:LANES, :LANES], inv64, g_ref[5:6, :LANES]))
    vc_o[...] = proj(C_VC, C_QD)

    inv96 = 1.0 / MLA_QK
    qd = proj(C_QD, C_CKV)
    qn_d = g_ref[6:7, :LANES]
    qd_o[...] = jnp.concatenate(
        [_rope(_group_norm(qd[:, c:c + LANES], ones, inv96, qn_d), cosm, slom, shim, 8)
         for c in range(0, 4 * LANES, LANES)], axis=1)

    ckv = _group_norm(proj(C_CKV, C_KPE), ones, 1.0 / MLA_KV_LORA, g_ref[8:9, :LANES])
    ckv_o[...] = ckv
    kpe = proj(C_KPE, C_END)
    kpe_o[...] = kpe
    kv = _dot(ckv.astype(BF16), wukv_ref[...])
    kn_d = g_ref[7:8, :LANES]
    kd_o[...] = jnp.concatenate(
        [_rope(_group_norm(kv[:, c:c + LANES] + kpe, ones, inv96, kn_d), cosm, slom, shim, 8)
         for c in range(0, 4 * LANES, LANES)], axis=1)
    vd_o[...] = kv[:, 4 * LANES:]


def _inproj(l, x_all, mods, norm1, w_qkv, w_ukv_p, gains, bd, ones, rope_tab):
    tok = lambda w: pl.BlockSpec((TM, w), lambda i: (i, 0))
    return pl.pallas_call(
        _inproj_kernel,
        grid=(N_TILES,),
        in_specs=[
            tok(D_MODEL),
            pl.BlockSpec((None, None, 1, 6 * D_MODEL), lambda i: (l, _tile_group(i), 0, 0)),
            _resident((None, 1, D_MODEL), lambda i: (l, 0, 0)),
            _resident((None, D_MODEL, C_END), lambda i: (l, 0, 0)),
            _resident((None, MLA_KV_LORA, 768), lambda i: (l, 0, 0)),
            _resident((None, 16, 256), lambda i: (l, 0, 0)),
            _resident((256, 256), lambda i: (0, 0)),
            _resident((LANES, LANES), lambda i: (0, 0)),
            pl.BlockSpec((6, TM, LANES), lambda i: (0, _tile_pos_block(i), 0)),
        ],
        out_specs=[tok(w) for w in _INPROJ_WIDTHS],
        out_shape=[jax.ShapeDtypeStruct((N_TOK, w), F32) for w in _INPROJ_WIDTHS],
        compiler_params=_cparams(("parallel",)),
        name=f"inproj_l{l}",
    )(x_all, mods, norm1, w_qkv, w_ukv_p, gains, bd, ones, rope_tab)


def _attend(q, pieces, scale, sink=None):
    scores = []
    for k, _, bias in pieces:
        s = _dot_t(q, k) * scale
        if bias is not None:
            s = s + bias
        scores.append(s)
    m = functools.reduce(jnp.maximum, [jnp.max(s, axis=-1, keepdims=True) for s in scores])
    if sink is not None:
        m = jnp.maximum(m, sink)
    denom = None
    out = None
    for s, (_, v, _) in zip(scores, pieces):
        e = jnp.exp(s - m)
        d = jnp.sum(e, axis=-1, keepdims=True)
        o = _dot(e.astype(BF16), v)
        denom = d if denom is None else denom + d
        out = o if out is None else out + o
    if sink is not None:
        denom = denom + jnp.exp(sink - m)
    return out / denom


def _half_mask(shape, half):
    lane = _lane_iota(shape)
    return (lane < HEAD_DIM) if half == 0 else (lane >= HEAD_DIM)


def _place(t, src_half, dst_half):
    if src_half != dst_half:
        t = pltpu.roll(t, HEAD_DIM, 1)
    return jnp.where(_half_mask(t.shape, dst_half), t, 0.0)


def _pair_attention(q_chunk, kv_half, pieces, scale, sinks=None):
    m_rows = q_chunk.shape[0]
    halves = (0, 1) if kv_half is None else (kv_half, kv_half)
    q2 = jnp.concatenate([_place(q_chunk, 0, halves[0]), _place(q_chunk, 1, halves[1])], axis=0).astype(BF16)
    pieces2 = []
    for k, v, bias in pieces:
        pieces2.append((k, v, bias))
    sink = None
    if sinks is not None:
        row = lax.broadcasted_iota(jnp.int32, (2 * m_rows, 1), 0)
        sink = jnp.where(row < m_rows, sinks[0], sinks[1])
    o2 = _attend(q2, pieces2, scale, sink)
    lo = _place(o2[:m_rows], halves[0], 0)
    hi = _place(o2[m_rows:], halves[1], 1)
    return lo + hi


_CACHE_WIDTHS = (256, 256, 128, 128, 128, 128, MLA_KV_LORA, MLA_ROPE)


def _ctx_attn_kernel(sink_ref, qa, ka, va, qb, kb, vb, qc, kc, vc, qd, kd, vd, ckv, kpe, *rest):
    o_ref = rest[-1 - len(_CACHE_WIDTHS)]
    cache_out = rest[-len(_CACHE_WIDTHS):]
    for src, dst in zip((ka, va, kb, vb, kc, vc, ckv), cache_out[:-1]):
        dst[...] = src[...]
    cache_out[-1][...] = kpe[:, MLA_NOPE:MLA_QK]
    scale = HEAD_DIM ** -0.5
    for j in range(2):
        c = slice(j * LANES, (j + 1) * LANES)
        k = ka[:, c].astype(BF16)
        v = va[:, c].astype(BF16)
        o_ref[:, j * LANES:(j + 1) * LANES] = _pair_attention(qa[:, c], None, [(k, v, None)], scale)
    for idx, (q, k_ref, v_ref) in enumerate(((qb, kb, vb), (qc, kc, vc))):
        k = k_ref[...].astype(BF16)
        v = v_ref[...].astype(BF16)
        for j in range(2):
            c = slice(j * LANES, (j + 1) * LANES)
            sinks = (sink_ref[2 * j], sink_ref[2 * j + 1]) if idx == 1 else None
            base = (1 + idx) * 256 + j * LANES
            o_ref[:, base:base + LANES] = _pair_attention(q[:, c], j, [(k, v, None)], scale, sinks)
    scale_d = MLA_QK ** -0.5
    for j in range(2):
        v = vd[:, j * LANES:(j + 1) * LANES].astype(BF16)
        outs = []
        for hh in range(2):
            hd = 2 * j + hh
            c = slice(hd * LANES, (hd + 1) * LANES)
            o = _attend(qd[:, c].astype(BF16), [(kd[:, c].astype(BF16), v, None)], scale_d)
            outs.append(jnp.where(_half_mask(o.shape, hh), o, 0.0))
        base = 3 * 256 + j * LANES
        o_ref[:, base:base + LANES] = outs[0] + outs[1]


def _ctx_attention(l, sink_c, qa, ka, va, qb, kb, vb, qc, kc, vc, qd, kd, vd, ckv, kpe, caches):
    blk = lambda w: pl.BlockSpec((SEQ, w), lambda b: (b, 0))
    widths = (256, 256, 256, 256, 128, 128, 256, 128, 128, 512, 512, 256, 128, 128)
    in_specs = [pl.BlockSpec(memory_space=pltpu.SMEM)] + [blk(w) for w in widths]
    args = [sink_c, qa, ka, va, qb, kb, vb, qc, kc, vc, qd, kd, vd, ckv, kpe]
    aliases = {}
    if caches is not None:
        aliases = {len(in_specs) + n: 1 + n for n in range(len(caches))}
        in_specs = in_specs + [pl.BlockSpec(memory_space=pl.ANY)] * len(caches)
        args = args + list(caches)
    outs = pl.pallas_call(
        _ctx_attn_kernel,
        grid=(BATCH,),
        in_specs=in_specs,
        out_specs=[pl.BlockSpec((SEQ, D_MODEL), lambda b: (b, 0))]
        + [pl.BlockSpec((None, None, SEQ, w), lambda b: (b, l, 0, 0)) for w in _CACHE_WIDTHS],
        out_shape=[jax.ShapeDtypeStruct((N_TOK, D_MODEL), F32)]
        + [jax.ShapeDtypeStruct((BATCH, DEPTH, SEQ, w), F32) for w in _CACHE_WIDTHS],
        input_output_aliases=aliases,
        compiler_params=_cparams(("parallel",)),
        name=f"ctx_attn_l{l}",
    )(*args)
    return outs[0], outs[1:]


LAT_TQ = 128
LAT_QBLOCKS = DEC_SEQ // LAT_TQ
LAT_ROW0 = N_CTX // LAT_TQ
LAT_KV0 = N_CTX // DEC_SEQ


def _lat_qspec(width):
    return pl.BlockSpec((LAT_TQ, width), lambda b, i: (LAT_ROW0 + b * LAT_QBLOCKS + i, 0))


def _lat_kvspec(width):
    return pl.BlockSpec((DEC_SEQ, width), lambda b, i: (LAT_KV0 + b, 0))


def _lat_ospec(mixer):
    return pl.BlockSpec((LAT_TQ, 256), lambda b, i: (LAT_ROW0 + b * LAT_QBLOCKS + i, mixer))


def _lat_call(kernel, l, name, in_specs, args, o_buf, mixer, grid=None, ospec=None):
    return pl.pallas_call(
        kernel,
        grid=grid or (DEC_BATCH, LAT_QBLOCKS),
        in_specs=in_specs + [pl.BlockSpec(memory_space=pl.ANY)],
        out_specs=ospec or _lat_ospec(mixer),
        out_shape=jax.ShapeDtypeStruct((N_TOK, D_MODEL), F32),
        input_output_aliases={len(in_specs): 0},
        compiler_params=_cparams(("parallel", "parallel")),
        name=f"{name}_l{l}",
    )(*args, o_buf)


def _gqa_kernel(q_ref, k_ref, v_ref, kc_ref, vc_ref, _, o_ref):
    scale = HEAD_DIM ** -0.5
    pieces = [(kc_ref[...].astype(BF16), vc_ref[...].astype(BF16), None),
              (k_ref[...].astype(BF16), v_ref[...].astype(BF16), None)]
    for j in range(2):
        c = slice(j * LANES, (j + 1) * LANES)
        o_ref[:, c] = _pair_attention(q_ref[:, c], j, pieces, scale)


def _lat_gqa(l, qb, kb, vb, cache_k, cache_v, o_buf):
    cspec = pl.BlockSpec((None, None, PAST_LEN, LANES), lambda b, i: (b, l, 0, 0))
    return _lat_call(_gqa_kernel, l, "lat_gqa",
                     [_lat_qspec(256), _lat_kvspec(LANES), _lat_kvspec(LANES), cspec, cspec],
                     (qb, kb, vb, cache_k, cache_v), o_buf, 1)


WIN_KEYS = 3 * LAT_TQ


def _win_kernel(sink_ref, q_ref, k_ref, v_ref, kc_ref, vc_ref, _, o_ref):
    scale = HEAD_DIM ** -0.5
    i = pl.program_id(1)
    start = pl.multiple_of(jnp.clip((i - 1) * LAT_TQ, 0, DEC_SEQ - WIN_KEYS), LAT_TQ)
    k = k_ref[pl.ds(start, WIN_KEYS), :].astype(BF16)
    v = v_ref[pl.ds(start, WIN_KEYS), :].astype(BF16)
    row = lax.broadcasted_iota(jnp.int32, (2 * LAT_TQ, WIN_KEYS), 0)
    qpos = i * LAT_TQ + jnp.where(row < LAT_TQ, row, row - LAT_TQ)
    kpos = start + lax.broadcasted_iota(jnp.int32, (2 * LAT_TQ, WIN_KEYS), 1)
    bias = jnp.where(jnp.abs(qpos - kpos) <= WINDOW, 0.0, NEG_INF)
    pieces = [(kc_ref[...].astype(BF16), vc_ref[...].astype(BF16), None), (k, v, bias)]
    for j in range(2):
        c = slice(j * LANES, (j + 1) * LANES)
        sinks = (sink_ref[2 * j], sink_ref[2 * j + 1])
        o_ref[:, c] = _pair_attention(q_ref[:, c], j, pieces, scale, sinks)


def _lat_window(l, sink_c, qc, kc, vc, cache_k, cache_v, o_buf):
    cspec = pl.BlockSpec((None, None, PAST_LEN, LANES), lambda b, i: (b, l, 0, 0))
    return _lat_call(_win_kernel, l, "lat_win",
                     [pl.BlockSpec(memory_space=pltpu.SMEM), _lat_qspec(256), _lat_kvspec(LANES),
                      _lat_kvspec(LANES), cspec, cspec],
                     (sink_c, qc, kc, vc, cache_k, cache_v), o_buf, 2)


def _mla_kernel(q_ref, k_ref, v_ref, kc_ref, vc_ref, _, o_ref):
    scale = MLA_QK ** -0.5
    for j in range(2):
        vs = slice(j * LANES, (j + 1) * LANES)
        v = v_ref[:, vs].astype(BF16)
        vc = vc_ref[:, vs].astype(BF16)
        outs = []
        for hh in range(2):
            hd = 2 * j + hh
            c = slice(hd * LANES, (hd + 1) * LANES)
            pieces = [(kc_ref[:, c].astype(BF16), vc, None), (k_ref[:, c].astype(BF16), v, None)]
            o = _attend(q_ref[:, c].astype(BF16), pieces, scale)
            outs.append(jnp.where(_half_mask(o.shape, hh), o, 0.0))
        o_ref[:, vs] = outs[0] + outs[1]


def _lat_mla(l, qd, kd, vd, kd_ctx, vd_ctx, o_buf):
    kcs = pl.BlockSpec((None, None, PAST_LEN, 512), lambda b, i: (l, b, 0, 0))
    vcs = pl.BlockSpec((None, None, PAST_LEN, 256), lambda b, i: (l, b, 0, 0))
    return _lat_call(_mla_kernel, l, "lat_mla",
                     [_lat_qspec(512), _lat_kvspec(512), _lat_kvspec(256), kcs, vcs],
                     (qd, kd, vd, kd_ctx, vd_ctx), o_buf, 3)


NA_KEYS = NA_ROWS * GRID_W


def _na_kernel(q_ref, k_ref, v_ref, kc_ref, vc_ref, bias_ref, _, o_ref):
    scale = HEAD_DIM ** -0.5
    r = pl.program_id(1)
    r0 = jnp.clip(r - NA_ROWS // 2, 0, GRID_ROWS - NA_ROWS)
    start = pl.multiple_of(r0 * GRID_W, GRID_W)
    for j in range(2):
        c = slice(j * LANES, (j + 1) * LANES)
        k = k_ref[pl.ds(start, NA_KEYS), c].astype(BF16)
        v = v_ref[pl.ds(start, NA_KEYS), c].astype(BF16)
        bias = jnp.concatenate([bias_ref[2 * j], bias_ref[2 * j + 1]], axis=0)
        pieces = [(kc_ref[:, c].astype(BF16), vc_ref[:, c].astype(BF16), None), (k, v, bias)]
        o_ref[:, c] = _pair_attention(q_ref[:, c], None, pieces, scale)


def _na_shift(r):
    return jnp.where(r < NA_ROWS // 2, r, jnp.where(r > GRID_ROWS - NA_ROWS // 2, r - (GRID_ROWS - NA_ROWS), NA_ROWS // 2))


def _lat_na(l, qa, ka, va, cache_k, cache_v, na_bias, o_buf):
    row0 = N_CTX // GRID_W
    qspec = pl.BlockSpec((GRID_W, 256), lambda b, r: (row0 + b * GRID_ROWS + r, 0))
    ospec = pl.BlockSpec((GRID_W, 256), lambda b, r: (row0 + b * GRID_ROWS + r, 0))
    cspec = pl.BlockSpec((None, None, PAST_LEN, 256), lambda b, r: (b, l, 0, 0))
    bspec = pl.BlockSpec((None, 4, None, GRID_W, NA_KEYS), lambda b, r: (l, 0, _na_shift(r), 0, 0))
    return _lat_call(_na_kernel, l, "lat_na",
                     [qspec, _lat_kvspec(256), _lat_kvspec(256), cspec, cspec, bspec],
                     (qa, ka, va, cache_k, cache_v, na_bias), o_buf, 0,
                     grid=(DEC_BATCH, GRID_ROWS), ospec=ospec)


def _na_bias_kernel(rpb_ref, out_ref, t_ref):
    l = pl.program_id(0)
    n_dr, n_dc = 2 * NA_ROWS - 1, 2 * NA_COLS - 1
    shape = (GRID_W, LANES)
    qc = lax.broadcasted_iota(jnp.int32, shape, 0)
    lane = _lane_iota(shape)
    kc = lane & (GRID_W - 1)
    dc = jnp.clip(kc - qc, 1 - NA_COLS, NA_COLS - 1) + (NA_COLS - 1)
    c0 = jnp.clip(qc - NA_COLS // 2, 0, GRID_W - NA_COLS)
    ok = (kc >= c0) & (kc < c0 + NA_COLS)
    for hd in range(4):
        def fill(a, carry):
            base = ((l * 4 + hd) * n_dr + a) * n_dc
            t = jnp.zeros(shape, F32)
            for b in range(n_dc):
                t = jnp.where(dc == b, rpb_ref[base + b], t)
            t_ref[a] = jnp.where(ok, t, NEG_INF)
            return carry
        lax.fori_loop(0, n_dr, fill, 0)
        for d in range(NA_ROWS):
            for m in range(NA_ROWS // 2):
                lo = t_ref[2 * m - d + NA_ROWS - 1]
                hi = t_ref[2 * m + 1 - d + NA_ROWS - 1]
                out_ref[hd, d, :, m * LANES:(m + 1) * LANES] = jnp.where(lane < GRID_W, lo, hi)


def _na_bias(rpb_flat):
    return pl.pallas_call(
        _na_bias_kernel,
        grid=(DEPTH,),
        in_specs=[pl.BlockSpec(memory_space=pltpu.SMEM)],
        out_specs=pl.BlockSpec((None, 4, NA_ROWS, GRID_W, NA_KEYS), lambda l: (l, 0, 0, 0, 0)),
        out_shape=jax.ShapeDtypeStruct((DEPTH, 4, NA_ROWS, GRID_W, NA_KEYS), F32),
        scratch_shapes=[pltpu.VMEM((2 * NA_ROWS - 1, GRID_W, LANES), F32)],
        compiler_params=_cparams(("parallel",)),
        name="na_bias",
    )(rpb_flat)


def _mla_ctx_kernel(ckv_ref, kpe_ref, wukv_ref, g_ref, ones_ref, kd_o, vd_o):
    kv = _dot(ckv_ref[...].astype(BF16), wukv_ref[...])
    kpe = kpe_ref[...]
    kn_d = g_ref[7:8, :LANES]
    ones = ones_ref[...]
    kd_o[...] = jnp.concatenate(
        [_group_norm(kv[:, c:c + LANES] + kpe, ones, 1.0 / MLA_QK, kn_d) for c in range(0, 4 * LANES, LANES)],
        axis=1)
    vd_o[...] = kv[:, 4 * LANES:]


def _mla_ctx(cache_ckv, cache_kpe_p, w_ukv_p, gains, ones):
    return pl.pallas_call(
        _mla_ctx_kernel,
        grid=(DEPTH, DEC_BATCH),
        in_specs=[
            pl.BlockSpec((None, None, PAST_LEN, LANES), lambda l, b: (b, l, 0, 0)),
            pl.BlockSpec((None, None, PAST_LEN, LANES), lambda l, b: (b, l, 0, 0)),
            pl.BlockSpec((None, MLA_KV_LORA, 768), lambda l, b: (l, 0, 0)),
            pl.BlockSpec((None, 16, 256), lambda l, b: (l, 0, 0)),
            pl.BlockSpec((LANES, LANES), lambda l, b: (0, 0)),
        ],
        out_specs=[pl.BlockSpec((None, None, PAST_LEN, 512), lambda l, b: (l, b, 0, 0)),
                   pl.BlockSpec((None, None, PAST_LEN, 256), lambda l, b: (l, b, 0, 0))],
        out_shape=[jax.ShapeDtypeStruct((DEPTH, DEC_BATCH, PAST_LEN, 512), F32),
                   jax.ShapeDtypeStruct((DEPTH, DEC_BATCH, PAST_LEN, 256), F32)],
        compiler_params=_cparams(("parallel", "parallel")),
        name="mla_ctx",
    )(cache_ckv, cache_kpe_p, w_ukv_p, gains, ones)


def _merge_kernel(x_ref, o_ref, mod_ref, n1_ref, wg_ref, wb_ref, wo_ref, x1_ref):
    x = x_ref[...]
    mod = mod_ref[...]
    h = _rms_mod(x, n1_ref[...], mod[:, D_MODEL:2 * D_MODEL], mod[:, 0:D_MODEL])
    hb = h.astype(BF16)
    merged = None
    for n in range(4):
        gate = _dot(hb, wg_ref[:, n * D_MODEL:(n + 1) * D_MODEL])
        gate = 1.0 / (1.0 + jnp.exp(-gate))
        br = _dot(o_ref[:, n * 256:(n + 1) * 256].astype(BF16), wb_ref[n])
        merged = gate * br if merged is None else merged + gate * br
    out = _dot(merged.astype(BF16), wo_ref[...])
    x1_ref[...] = x + mod[:, 2 * D_MODEL:3 * D_MODEL] * out


def _merge(l, x_all, o_buf, mods, norm1, w_gate, w_branch, w_out):
    tok = pl.BlockSpec((TM, D_MODEL), lambda i: (i, 0))
    return pl.pallas_call(
        _merge_kernel,
        grid=(N_TILES,),
        in_specs=[
            tok, tok,
            pl.BlockSpec((None, None, 1, 6 * D_MODEL), lambda i: (l, _tile_group(i), 0, 0)),
            _resident((None, 1, D_MODEL), lambda i: (l, 0, 0)),
            _resident((None, D_MODEL, N_GATE), lambda i: (l, 0, 0)),
            _resident((None, 4, 256, D_MODEL), lambda i: (l, 0, 0, 0)),
            _resident((None, D_MODEL, D_MODEL), lambda i: (l, 0, 0)),
        ],
        out_specs=tok,
        out_shape=jax.ShapeDtypeStruct((N_TOK, D_MODEL), F32),
        compiler_params=_cparams(("parallel",)),
        name=f"merge_l{l}",
    )(x_all, o_buf, mods, norm1, w_gate, w_branch, w_out)


HALO = 8


def _ffn_kernel(x_ref, xp_ref, xn_ref, mod_ref, n2_ref, wup_ref, cw_ref, cb_ref, wdn_ref, *rest, split_out):
    u_ref = rest[-1]
    i = pl.program_id(0)
    p = (i - CTX_TILES) % LAT_TILES_PER_BATCH
    has_prev = (i >= CTX_TILES) & (p != 0)
    has_next = (i >= CTX_TILES) & (p != LAT_TILES_PER_BATCH - 1)
    x = x_ref[...]
    mod = mod_ref[...]
    n2 = n2_ref[...]
    sh2, sc2 = mod[:, 3 * D_MODEL:4 * D_MODEL], mod[:, 4 * D_MODEL:5 * D_MODEL]
    h_prev = jnp.where(has_prev, _rms_mod(xp_ref[...], n2, sc2, sh2), 0.0)
    h_next = jnp.where(has_next, _rms_mod(xn_ref[...], n2, sc2, sh2), 0.0)
    hb = jnp.concatenate([h_prev, _rms_mod(x, n2, sc2, sh2), h_next], axis=0).astype(BF16)
    def cols(c):
        return (slice(c * FF_CHUNK, (c + 1) * FF_CHUNK), slice(D_FF + c * FF_CHUNK, D_FF + (c + 1) * FF_CHUNK))

    def up(c):
        va, vg = cols(c)
        u_ref[c, :, :FF_CHUNK] = _dot(hb, wup_ref[:, va])
        u_ref[c, :, FF_CHUNK:] = _dot(hb, wup_ref[:, vg])

    def both(ref, c):
        va, vg = cols(c)
        return jnp.concatenate([ref[:, va], ref[:, vg]], axis=1)

    acc = None
    up(0)
    for c in range(N_FF_CHUNKS):
        if c + 1 < N_FF_CHUNKS:
            up(c + 1)
        cw = both(cw_ref, c)
        t = (both(cb_ref, c) + u_ref[c, HALO - 1:HALO - 1 + TM, :] * cw[0:1] + u_ref[c, HALO:HALO + TM, :] * cw[1:2]
             + u_ref[c, HALO + 1:HALO + 1 + TM, :] * cw[2:3])
        a, g = t[:, :FF_CHUNK], t[:, FF_CHUNK:]
        act = (g * (1.0 / (1.0 + jnp.exp(-g)))) * a
        d = _dot(act.astype(BF16), wdn_ref[c * FF_CHUNK:(c + 1) * FF_CHUNK, :])
        acc = d if acc is None else acc + d
    y = x + mod[:, 5 * D_MODEL:6 * D_MODEL] * acc
    if split_out:
        @pl.when(i < CTX_TILES)
        def _():
            rest[0][...] = y

        @pl.when(i >= CTX_TILES)
        def _():
            rest[1][...] = y
    else:
        rest[0][...] = y


def _ffn(l, x1, mods, norm2, w_up, conv_w, conv_b, w_down, split_out):
    tok = pl.BlockSpec((TM, D_MODEL), lambda i: (i, 0))
    per = TM // HALO
    last = N_TOK // HALO - 1
    if split_out:
        out_specs = [pl.BlockSpec((TM, D_MODEL), lambda i: (jnp.minimum(i, CTX_TILES - 1), 0)),
                     pl.BlockSpec((TM, D_MODEL), lambda i: (jnp.maximum(i - CTX_TILES, 0), 0))]
        out_shape = [jax.ShapeDtypeStruct((N_CTX, D_MODEL), F32), jax.ShapeDtypeStruct((N_LAT, D_MODEL), F32)]
    else:
        out_specs, out_shape = tok, jax.ShapeDtypeStruct((N_TOK, D_MODEL), F32)
    return pl.pallas_call(
        functools.partial(_ffn_kernel, split_out=split_out),
        grid=(N_TILES,),
        in_specs=[
            tok,
            pl.BlockSpec((HALO, D_MODEL), lambda i: (jnp.maximum(i * per - 1, 0), 0)),
            pl.BlockSpec((HALO, D_MODEL), lambda i: (jnp.minimum((i + 1) * per, last), 0)),
            pl.BlockSpec((None, None, 1, 6 * D_MODEL), lambda i: (l, _tile_group(i), 0, 0)),
            _resident((None, 1, D_MODEL), lambda i: (l, 0, 0)),
            _resident((None, D_MODEL, 2 * D_FF), lambda i: (l, 0, 0)),
            _resident((None, 3, 2 * D_FF), lambda i: (l, 0, 0)),
            _resident((None, 1, 2 * D_FF), lambda i: (l, 0, 0)),
            _resident((None, D_FF, D_MODEL), lambda i: (l, 0, 0)),
        ],
        out_specs=out_specs,
        out_shape=out_shape,
        scratch_shapes=[pltpu.VMEM((N_FF_CHUNKS, TM + 2 * HALO, 2 * FF_CHUNK), F32)],
        compiler_params=_cparams(("arbitrary",)),
        name=f"ffn_l{l}",
    )(x1, x1, x1, mods, norm2, w_up, conv_w, conv_b, w_down)


def _cast_kernel(a_ref, o_ref):
    o_ref[...] = a_ref[...].astype(BF16)


def _cast_bf16(a, rows, name):
    _, r, c = a.shape
    spec = pl.BlockSpec((None, rows, c), lambda l, i: (l, i, 0))
    return pl.pallas_call(
        _cast_kernel, grid=(DEPTH, r // rows), in_specs=[spec], out_specs=spec,
        out_shape=jax.ShapeDtypeStruct(a.shape, BF16),
        compiler_params=_cparams(("parallel", "parallel")), name=name,
    )(a)


def _win_prep_kernel(w_ref, q_ref, g_ref):
    bf = lambda c0, c1: w_ref[:, c0:c1].astype(BF16)
    rows = w_ref.shape[0]
    g_ref[...] = bf(IN_QKV, IN_QKV + N_GATE)
    q_ref[:, 0:C_QD] = bf(0, C_QD)
    for hd in range(4):
        dst = C_QD + hd * LANES
        q_ref[:, dst:dst + MLA_QK] = bf(C_QD + hd * MLA_QK, C_QD + (hd + 1) * MLA_QK)
        q_ref[:, dst + MLA_QK:dst + LANES] = jnp.zeros((rows, LANES - MLA_QK), BF16)
    src_ckv = C_QD + 4 * MLA_QK
    q_ref[:, C_CKV:C_KPE] = bf(src_ckv, src_ckv + MLA_KV_LORA)
    q_ref[:, C_KPE:C_KPE + MLA_NOPE] = jnp.zeros((rows, MLA_NOPE), BF16)
    q_ref[:, C_KPE + MLA_NOPE:C_KPE + MLA_QK] = bf(src_ckv + MLA_KV_LORA, IN_QKV)
    q_ref[:, C_KPE + MLA_QK:C_END] = jnp.zeros((rows, LANES - MLA_QK), BF16)


def _win_prep(w_in):
    rows = 256
    return pl.pallas_call(
        _win_prep_kernel,
        grid=(DEPTH, D_MODEL // rows),
        in_specs=[pl.BlockSpec((None, rows, w_in.shape[2]), lambda l, i: (l, i, 0))],
        out_specs=[pl.BlockSpec((None, rows, C_END), lambda l, i: (l, i, 0)),
                   pl.BlockSpec((None, rows, N_GATE), lambda l, i: (l, i, 0))],
        out_shape=[jax.ShapeDtypeStruct((DEPTH, D_MODEL, C_END), BF16),
                   jax.ShapeDtypeStruct((DEPTH, D_MODEL, N_GATE), BF16)],
        compiler_params=_cparams(("parallel", "parallel")),
        name="w_in_prep",
    )(w_in)


def _rope_tables():
    t = np.arange(DEC_SEQ)

    def angles(dim):
        half = dim // 2
        inv = ROPE_THETA ** (-np.arange(0, half, 2, dtype=np.float32) / half)
        row = (t // GRID_W).astype(np.float32)[:, None] * inv[None, :]
        col = (t % GRID_W).astype(np.float32)[:, None] * inv[None, :]
        return np.concatenate([row, row, col, col], axis=-1).astype(np.float32)

    def signed(ang, quarter):
        cos, sin = np.cos(ang), np.sin(ang)
        first = (np.arange(ang.shape[1]) % (2 * quarter)) < quarter
        return cos, np.where(first, -sin, 0.0), np.where(first, 0.0, sin)

    c64, lo64, hi64 = signed(angles(HEAD_DIM), 16)
    c64, lo64, hi64 = (np.tile(a, (1, 2)) for a in (c64, lo64, hi64))
    cm, lom, him = signed(angles(MLA_ROPE), 8)

    def mla_pad(a, fill):
        out = np.full((DEC_SEQ, LANES), fill, np.float32)
        out[:, MLA_NOPE:MLA_QK] = a
        return out

    tabs = [c64, lo64, hi64, mla_pad(cm, 1.0), mla_pad(lom, 0.0), mla_pad(him, 0.0)]
    ident = [np.ones, np.zeros, np.zeros, np.ones, np.zeros, np.zeros]
    full = [np.concatenate([f((TM, LANES), np.float32), a.astype(np.float32)], axis=0) for f, a in zip(ident, tabs)]
    return jnp.asarray(np.stack(full, axis=0))


def _block_diag_ones():
    idx = np.arange(256) // HEAD_DIM
    return jnp.asarray((idx[:, None] == idx[None, :]).astype(np.float32), dtype=BF16)


def _prep_small(w_ukv, qn_a, kn_a, qn_b, kn_b, qn_c, kn_c, qn_d, kn_d, kvn_d):
    ukv = w_ukv.reshape(DEPTH, MLA_KV_LORA, 4, 2, HEAD_DIM)
    k_part = jnp.pad(ukv[:, :, :, 0], ((0, 0), (0, 0), (0, 0), (0, HEAD_DIM))).reshape(DEPTH, MLA_KV_LORA, 512)
    v_part = ukv[:, :, :, 1].reshape(DEPTH, MLA_KV_LORA, 256)
    w_ukv_p = jnp.concatenate([k_part, v_part], axis=-1).astype(BF16)

    def row(g, reps, width=256):
        t = jnp.tile(g, (1, reps))
        return jnp.pad(t, ((0, 0), (0, width - t.shape[1])))

    pad96 = lambda g: jnp.pad(g, ((0, 0), (0, LANES - MLA_QK)))
    rows = [row(qn_a, 4), row(kn_a, 4), row(qn_b, 4), row(kn_b, 2), row(qn_c, 4), row(kn_c, 2),
            row(pad96(qn_d), 1), row(pad96(kn_d), 1), row(kvn_d, 1)]
    gains = jnp.stack(rows + [jnp.zeros_like(rows[0])] * (16 - len(rows)), axis=1)
    return w_ukv_p, gains


def kernel(x_prompt, x_sample, cache_nat_k, cache_nat_v, cache_gqa_k, cache_gqa_v, cache_win_k, cache_win_v,
           cache_mla_ckv, cache_mla_kpe, c, c_ctx, w_ada, b_ada, norm1, norm2, w_in, qn_a, kn_a, rpb_a,
           qn_b, kn_b, qn_c, kn_c, sink_c, qn_d, kn_d, kvn_d, w_ukv, w_branch, w_out, w_up, conv_w, conv_b,
           w_down):
    w_ukv_p, gains = _prep_small(w_ukv, qn_a, kn_a, qn_b, kn_b, qn_c, kn_c, qn_d, kn_d, kvn_d)
    w_qkv, w_gate = _win_prep(w_in)
    w_branch_b = _cast_bf16(w_branch.reshape(DEPTH, 4 * 256, D_MODEL), 1024, "cast_w_branch").reshape(
        DEPTH, 4, 256, D_MODEL)
    w_out_b = _cast_bf16(w_out, 1024, "cast_w_out")
    w_up_b = _cast_bf16(w_up, 256, "cast_w_up")
    w_down_b = _cast_bf16(w_down, D_FF // 4, "cast_w_down")
    conv_b_r = conv_b.reshape(DEPTH, 1, 2 * D_FF)
    rope_tab = _rope_tables()
    bd = _block_diag_ones()
    ones = jnp.ones((LANES, LANES), BF16)
    norm1_r = norm1.reshape(DEPTH, 1, D_MODEL)
    norm2_r = norm2.reshape(DEPTH, 1, D_MODEL)

    cond8 = jnp.zeros((8, D_MODEL), F32).at[0].set(c_ctx).at[1:1 + DEC_BATCH].set(c)
    mods = _adaln(cond8, w_ada, b_ada).reshape(DEPTH, 8, 1, 6 * D_MODEL)

    rpb_flat = jnp.pad(rpb_a.reshape(-1), (0, 8192 - rpb_a.size))
    na_bias = _na_bias(rpb_flat)
    kpe_p = jnp.pad(cache_mla_kpe, ((0, 0), (0, 0), (0, 0), (MLA_NOPE, LANES - MLA_QK)))
    kd_ctx, vd_ctx = _mla_ctx(cache_mla_ckv, kpe_p, w_ukv_p, gains, ones)
    c_nat_k = cache_nat_k.reshape(DEC_BATCH, DEPTH, PAST_LEN, 256)
    c_nat_v = cache_nat_v.reshape(DEC_BATCH, DEPTH, PAST_LEN, 256)
    c_gqa_k = cache_gqa_k.reshape(DEC_BATCH, DEPTH, PAST_LEN, LANES)
    c_gqa_v = cache_gqa_v.reshape(DEC_BATCH, DEPTH, PAST_LEN, LANES)
    c_win_k = cache_win_k.reshape(DEC_BATCH, DEPTH, PAST_LEN, LANES)
    c_win_v = cache_win_v.reshape(DEC_BATCH, DEPTH, PAST_LEN, LANES)

    x = jnp.concatenate([x_prompt.reshape(N_CTX, D_MODEL), x_sample.reshape(N_LAT, D_MODEL)], axis=0)
    caches = None
    for l in range(DEPTH):
        (qa, ka, va, qb, kb, vb, qc, kc, vc, qd, ckv, kpe, kd, vd) = _inproj(
            l, x, mods, norm1_r, w_qkv, w_ukv_p, gains, bd, ones, rope_tab)
        sink_l = sink_c[l]
        o, caches = _ctx_attention(l, sink_l, qa, ka, va, qb, kb, vb, qc, kc, vc, qd, kd, vd, ckv, kpe, caches)
        o = _lat_na(l, qa, ka, va, c_nat_k, c_nat_v, na_bias, o)
        o = _lat_gqa(l, qb, kb, vb, c_gqa_k, c_gqa_v, o)
        o = _lat_window(l, sink_l, qc, kc, vc, c_win_k, c_win_v, o)
        o = _lat_mla(l, qd, kd, vd, kd_ctx, vd_ctx, o)
        x1 = _merge(l, x, o, mods, norm1_r, w_gate, w_branch_b, w_out_b)
        x = _ffn(l, x1, mods, norm2_r, w_up_b, conv_w, conv_b_r, w_down_b, split_out=(l == DEPTH - 1))
    y_prompt = x[0].reshape(BATCH, SEQ, D_MODEL)
    y_sample = x[1].reshape(DEC_BATCH, DEC_SEQ, D_MODEL)
    heads = (4, 4, 2, 2, 2, 2)
    outs = [a.reshape(BATCH, DEPTH, SEQ, h, HEAD_DIM) for a, h in zip(caches[:6], heads)] + list(caches[6:])
    return (y_prompt, y_sample, *outs)
```

```python
import functools

import jax
import jax.numpy as jnp
import numpy as np
from jax import lax
from jax.experimental import pallas as pl
from jax.experimental.pallas import tpu as pltpu

F32 = jnp.float32
BF16 = jnp.bfloat16

D_MODEL = 1024
BATCH = 16
SEQ = 256
DEPTH = 4
DEC_BATCH = 2
DEC_SEQ = 2048
PAST_LEN = 256
GRID_W = 64
HEAD_DIM = 64
NA_ROWS = 8
NA_COLS = 16
WINDOW = 128
MLA_NOPE = 64
MLA_ROPE = 32
MLA_QK = MLA_NOPE + MLA_ROPE
MLA_KV_LORA = 128
D_FF = 2816
ROPE_THETA = 10000.0
EPS = 1e-6
NEG_INF = -1e30

LANES = 128
TM = 256
N_CTX = BATCH * SEQ
N_LAT = DEC_BATCH * DEC_SEQ
N_TOK = N_CTX + N_LAT
CTX_TILES = N_CTX // TM
LAT_TILES_PER_BATCH = DEC_SEQ // TM
N_TILES = N_TOK // TM
GRID_ROWS = DEC_SEQ // GRID_W
FF_CHUNK = 256
N_FF_CHUNKS = D_FF // FF_CHUNK
VMEM_LIMIT = 56 * 1024 * 1024

C_QA, C_KA, C_VA = 0, 256, 512
C_QB, C_KB, C_VB = 768, 1024, 1152
C_QC, C_KC, C_VC = 1280, 1536, 1664
C_QD, C_CKV, C_KPE, C_END = 1792, 2304, 2432, 2560
N_GATE = 4 * D_MODEL
IN_QKV = 2336
Q_SCALE = HEAD_DIM ** -0.5


def _cparams(sem):
    return pltpu.CompilerParams(dimension_semantics=sem, vmem_limit_bytes=VMEM_LIMIT)


def _tile_group(i):
    return jnp.where(i < CTX_TILES, 0, 1 + (i - CTX_TILES) // LAT_TILES_PER_BATCH)


def _tile_pos_block(i):
    return jnp.where(i < CTX_TILES, 0, 1 + (i - CTX_TILES) % LAT_TILES_PER_BATCH)


def _resident(shape, index_map):
    return pl.BlockSpec(shape, index_map, pipeline_mode=pl.Buffered(1))


def _dot(a, b):
    return jnp.dot(a, b, preferred_element_type=F32)


def _dot_t(a, b):
    return lax.dot_general(a, b, (((1,), (1,)), ((), ())), preferred_element_type=F32)


def _rms_mod(x, gain, scale, shift):
    ms = jnp.mean(x * x, axis=-1, keepdims=True)
    return (x * lax.rsqrt(ms + EPS)) * gain * (1.0 + scale) + shift


def _group_norm(t, ones_bd, inv_n, gain):
    ss = _dot((t * t).astype(BF16), ones_bd)
    return t * lax.rsqrt(ss * inv_n + EPS) * gain


def _rope(t, cos, sin_lo, sin_hi, half):
    return (t * cos + pltpu.roll(t, LANES - half, 1) * sin_lo + pltpu.roll(t, half, 1) * sin_hi)


def _lane_iota(shape):
    return lax.broadcasted_iota(jnp.int32, shape, len(shape) - 1)


def _low_half(shape):
    return _lane_iota(shape) < HEAD_DIM


def _run_units(units, depth=2):
    n = len(units)
    pending = {}
    outs = []
    for k in range(min(depth, n)):
        pending[k] = units[k][0]()
    for k in range(n):
        outs.append(units[k][1](pending.pop(k)))
        if k + depth < n:
            pending[k + depth] = units[k + depth][0]()
    return outs


def _adaln_kernel(cond_ref, w_ref, b_ref, out_ref):
    cnd = cond_ref[...]
    s = cnd * (1.0 / (1.0 + jnp.exp(-cnd)))
    out_ref[...] = jnp.dot(s, w_ref[...], preferred_element_type=F32,
                           precision=lax.Precision.HIGHEST) + b_ref[...]


def _adaln(cond8, w_ada, b_ada):
    n_col = 6 * D_MODEL
    tn = 1024
    return pl.pallas_call(
        _adaln_kernel,
        grid=(DEPTH, n_col // tn),
        in_specs=[
            pl.BlockSpec((8, D_MODEL), lambda l, n: (0, 0)),
            pl.BlockSpec((None, D_MODEL, tn), lambda l, n: (l, 0, n)),
            pl.BlockSpec((None, 1, tn), lambda l, n: (l, 0, n)),
        ],
        out_specs=pl.BlockSpec((None, 8, tn), lambda l, n: (l, 0, n)),
        out_shape=jax.ShapeDtypeStruct((DEPTH, 8, n_col), F32),
        compiler_params=_cparams(("parallel", "parallel")),
        name="adaln",
    )(cond8, w_ada, b_ada.reshape(DEPTH, 1, n_col))


_QKV_WIDTHS = (512, 256, 256, 512, 128, 128, 512, 128, 128, 512, 512, 256)
_CACHE_WIDTHS = (256, 256, 128, 128, 128, 128, MLA_KV_LORA, MLA_ROPE)


def _inproj_kernel(x_ref, mod_ref, n1_ref, w_ref, wukv_ref, g_ref, bd_ref, rope_ref, *rest):
    outs = rest[-(len(_QKV_WIDTHS) + len(_CACHE_WIDTHS)):]
    qa_o, ka_o, va_o, qb_o, kb_o, vb_o, qc_o, kc_o, vc_o, qd_o, kd_o, vd_o = outs[:len(_QKV_WIDTHS)]
    natk_o, natv_o, gqak_o, gqav_o, wink_o, winv_o, ckv_o, kpe_o = outs[len(_QKV_WIDTHS):]
    x = x_ref[...]
    mod = mod_ref[...]
    hb = _rms_mod(x, n1_ref[...], mod[:, D_MODEL:2 * D_MODEL], mod[:, 0:D_MODEL]).astype(BF16)
    bd64, bd128 = bd_ref[0], bd_ref[1]
    cos64, slo64, shi64 = rope_ref[0], rope_ref[1], rope_ref[2]
    cosm, slom, shim = rope_ref[3], rope_ref[4], rope_ref[5]
    inv64, inv96 = 1.0 / HEAD_DIM, 1.0 / MLA_QK
    low = _low_half((TM, LANES))
    held = {}

    def chunks(t):
        return [t[:, c:c + LANES] for c in range(0, t.shape[1], LANES)]

    def rope64(t):
        parts = [_rope(p, cos64, slo64, shi64, 16) for p in chunks(t)]
        return parts[0] if len(parts) == 1 else jnp.concatenate(parts, axis=1)

    def ropem(t):
        return jnp.concatenate([_rope(p, cosm, slom, shim, 8) for p in chunks(t)], axis=1)

    def store_q(ref, q, grouped):
        q = q * Q_SCALE
        slots = []
        for hd in range(4):
            part, src = q[:, (hd // 2) * LANES:(hd // 2 + 1) * LANES], hd % 2
            dst = hd // 2 if grouped else src
            if src != dst:
                part = pltpu.roll(part, HEAD_DIM, 1)
            slots.append(jnp.where(low if dst == 0 else ~low, part, 0.0))
        ref[...] = jnp.concatenate(slots, axis=1).astype(BF16)

    def post_latent(t):
        ckv = _group_norm(t[:, :LANES], bd128[:LANES, :LANES], 1.0 / MLA_KV_LORA, g_ref[8:9, :LANES])
        kpe = t[:, LANES:]
        ckv_o[...] = ckv
        kpe_o[...] = kpe[:, MLA_NOPE:MLA_QK]
        held["kpe"] = kpe
        held["kv"] = _dot(ckv.astype(BF16), wukv_ref[...])

    def post_kd(_):
        kv, kpe = held["kv"], held["kpe"]
        for j in range(2):
            kf = jnp.concatenate([kv[:, (2 * j) * LANES:(2 * j + 1) * LANES] + kpe,
                                  kv[:, (2 * j + 1) * LANES:(2 * j + 2) * LANES] + kpe], axis=1)
            kd_o[:, 2 * j * LANES:(2 * j + 2) * LANES] = ropem(
                _group_norm(kf, bd128, inv96, g_ref[7:8, :])).astype(BF16)
        vd_o[...] = kv[:, 4 * LANES:].astype(BF16)

    def post_qa(t):
        store_q(qa_o, _group_norm(t, bd64, inv64, g_ref[0:1, :]), grouped=False)

    def post_ka(t):
        k = _group_norm(t, bd64, inv64, g_ref[1:2, :])
        natk_o[...] = k
        ka_o[...] = k.astype(BF16)

    def post_va(t):
        natv_o[...] = t
        va_o[...] = t.astype(BF16)

    def post_q_rope(ref, row):
        def post(t):
            store_q(ref, rope64(_group_norm(t, bd64, inv64, g_ref[row:row + 1, :])), grouped=True)
        return post

    def post_kv(k_ref, v_ref, kc_ref, vc_ref, row):
        def post(t):
            k = _group_norm(t[:, :LANES], bd64[:LANES, :LANES], inv64, g_ref[row:row + 1, :LANES])
            kc_ref[...] = k
            k_ref[...] = rope64(k).astype(BF16)
            v = t[:, LANES:]
            vc_ref[...] = v
            v_ref[...] = v.astype(BF16)
        return post

    def post_qd(j):
        def post(t):
            qd_o[:, 2 * j * LANES:(2 * j + 2) * LANES] = ropem(
                _group_norm(t, bd128, inv96, g_ref[6:7, :])).astype(BF16)
        return post

    stages = [
        ((C_CKV, C_END), post_latent),
        ((C_QA, C_KA), post_qa),
        (None, post_kd),
        ((C_KA, C_VA), post_ka),
        ((C_VA, C_QB), post_va),
        ((C_QB, C_KB), post_q_rope(qb_o, 2)),
        ((C_KB, C_QC), post_kv(kb_o, vb_o, gqak_o, gqav_o, 3)),
        ((C_QC, C_KC), post_q_rope(qc_o, 4)),
        ((C_KC, C_QD), post_kv(kc_o, vc_o, wink_o, winv_o, 5)),
        ((C_QD, C_QD + 2 * LANES), post_qd(0)),
        ((C_QD + 2 * LANES, C_CKV), post_qd(1)),
    ]

    def issue(cols):
        return None if cols is None else _dot(hb, w_ref[:, cols[0]:cols[1]])

    _run_units([(functools.partial(issue, cols), post) for cols, post in stages], depth=2)


def _inproj(l, x_all, mods, norm1, w_qkv, w_ukv_p, gains, bd, rope_tab, caches):
    tile = lambda s: (s + CTX_TILES) % N_TILES
    tok = lambda w: pl.BlockSpec((TM, w), lambda s: (tile(s), 0))
    in_specs = [
        tok(D_MODEL),
        pl.BlockSpec((None, None, 1, 6 * D_MODEL), lambda s: (l, _tile_group(tile(s)), 0, 0)),
        _resident((None, 1, D_MODEL), lambda s: (l, 0, 0)),
        _resident((None, D_MODEL, C_END), lambda s: (l, 0, 0)),
        _resident((None, MLA_KV_LORA, 768), lambda s: (l, 0, 0)),
        _resident((None, 16, 256), lambda s: (l, 0, 0)),
        _resident((2, 256, 256), lambda s: (0, 0, 0)),
        pl.BlockSpec((6, TM, LANES), lambda s: (0, _tile_pos_block(tile(s)), 0)),
    ]
    args = [x_all, mods, norm1, w_qkv, w_ukv_p, gains, bd, rope_tab]
    aliases = {}
    if caches is not None:
        aliases = {len(in_specs) + n: len(_QKV_WIDTHS) + n for n in range(len(caches))}
        in_specs = in_specs + [pl.BlockSpec(memory_space=pl.ANY)] * len(caches)
        args = args + list(caches)
    cache_spec = lambda w: pl.BlockSpec((None, None, SEQ, w), lambda s: (jnp.maximum(s - CTX_TILES, 0), l, 0, 0))
    outs = pl.pallas_call(
        _inproj_kernel,
        grid=(N_TILES,),
        in_specs=in_specs,
        out_specs=[tok(w) for w in _QKV_WIDTHS] + [cache_spec(w) for w in _CACHE_WIDTHS],
        out_shape=[jax.ShapeDtypeStruct((N_TOK, w), BF16) for w in _QKV_WIDTHS]
        + [jax.ShapeDtypeStruct((BATCH, DEPTH, SEQ, w), F32) for w in _CACHE_WIDTHS],
        input_output_aliases=aliases,
        compiler_params=_cparams(("arbitrary",)),
        name=f"inproj_l{l}",
    )(*args)
    return outs[:len(_QKV_WIDTHS)], outs[len(_QKV_WIDTHS):]


def _scores(q, keys, scale=None):
    out = []
    for k, bias in keys:
        s = _dot_t(q, k)
        if scale is not None:
            s = s * scale
        if bias is not None:
            s = s + bias
        out.append(s)
    return out


def _softmax_pv(scores, values, sink=None):
    m = functools.reduce(jnp.maximum, [jnp.max(s, axis=-1, keepdims=True) for s in scores])
    if sink is not None:
        m = jnp.maximum(m, sink)
    denom = None
    out = None
    for s, v in zip(scores, values):
        e = jnp.exp(s - m)
        d = jnp.sum(e, axis=-1, keepdims=True)
        o = _dot(e.astype(BF16), v)
        denom = d if denom is None else denom + d
        out = o if out is None else out + o
    if sink is not None:
        denom = denom + jnp.exp(sink - m)
    return out / denom


def _pair_unit(q_ref, rows, j, keys, values, kv_half, sinks=None):
    m_rows = rows.stop - rows.start

    def issue():
        q2 = jnp.concatenate([q_ref[rows, (2 * j) * LANES:(2 * j + 1) * LANES],
                              q_ref[rows, (2 * j + 1) * LANES:(2 * j + 2) * LANES]], axis=0)
        return _scores(q2, keys())

    def finish(scores):
        sink = None
        if sinks is not None:
            row = lax.broadcasted_iota(jnp.int32, (2 * m_rows, 1), 0)
            sink = jnp.where(row < m_rows, sinks[0], sinks[1])
        o2 = _softmax_pv(scores, values(), sink)
        lo, hi = o2[:m_rows], o2[m_rows:]
        if kv_half == 0:
            hi = pltpu.roll(hi, HEAD_DIM, 1)
        elif kv_half == 1:
            lo = pltpu.roll(lo, HEAD_DIM, 1)
        return jnp.where(_low_half(lo.shape), lo, hi)

    return issue, finish


def _single_unit(q_ref, rows, hd, keys, values, scale):
    def issue():
        return _scores(q_ref[rows, hd * LANES:(hd + 1) * LANES], keys(), scale)

    def finish(scores):
        return _softmax_pv(scores, values())

    return issue, finish


def _merge_halves(o_even, o_odd):
    return jnp.where(_low_half(o_even.shape), o_even, o_odd)


def _ctx_attn_kernel(sink_ref, qa, ka, va, qb, kb, vb, qc, kc, vc, qd, kd, vd, o_ref):
    rows = slice(0, SEQ)
    units = []
    for j in range(2):
        c = slice(j * LANES, (j + 1) * LANES)
        units.append(_pair_unit(qa, rows, j, lambda c=c: [(ka[:, c], None)], lambda c=c: [va[:, c]], None))
    for q, k, v, with_sink in ((qb, kb, vb, False), (qc, kc, vc, True)):
        for j in range(2):
            sinks = (sink_ref[2 * j], sink_ref[2 * j + 1]) if with_sink else None
            units.append(_pair_unit(q, rows, j, lambda k=k: [(k[...], None)], lambda v=v: [v[...]], j, sinks))
    for hd in range(4):
        c = slice(hd * LANES, (hd + 1) * LANES)
        vs = slice((hd // 2) * LANES, (hd // 2 + 1) * LANES)
        units.append(_single_unit(qd, rows, hd, lambda c=c: [(kd[:, c], None)], lambda vs=vs: [vd[:, vs]],
                                  MLA_QK ** -0.5))
    outs = _run_units(units)
    for n in range(6):
        o_ref[:, n * LANES:(n + 1) * LANES] = outs[n].astype(BF16)
    for j in range(2):
        o_ref[:, (6 + j) * LANES:(7 + j) * LANES] = _merge_halves(outs[6 + 2 * j], outs[7 + 2 * j]).astype(BF16)


def _ctx_attention(l, sink_c, qkv):
    blk = lambda w: pl.BlockSpec((SEQ, w), lambda b: (b, 0))
    return pl.pallas_call(
        _ctx_attn_kernel,
        grid=(BATCH,),
        in_specs=[pl.BlockSpec(memory_space=pltpu.SMEM)] + [blk(w) for w in _QKV_WIDTHS],
        out_specs=pl.BlockSpec((SEQ, D_MODEL), lambda b: (b, 0)),
        out_shape=jax.ShapeDtypeStruct((N_TOK, D_MODEL), BF16),
        compiler_params=_cparams(("parallel",)),
        name=f"ctx_attn_l{l}",
    )(sink_c, *qkv)


LAT_TQ = 256
LAT_SUB = 128
LAT_QBLOCKS = DEC_SEQ // LAT_TQ
LAT_ROW0 = N_CTX // LAT_TQ
LAT_KV0 = N_CTX // DEC_SEQ


def _lat_qspec(width):
    return pl.BlockSpec((LAT_TQ, width), lambda b, i: (LAT_ROW0 + b * LAT_QBLOCKS + i, 0))


def _lat_kvspec(width):
    return pl.BlockSpec((DEC_SEQ, width), lambda b, i: (LAT_KV0 + b, 0))


def _lat_call(kernel, l, name, in_specs, args, o_buf, mixer):
    return pl.pallas_call(
        kernel,
        grid=(DEC_BATCH, LAT_QBLOCKS),
        in_specs=in_specs + [pl.BlockSpec(memory_space=pl.ANY)],
        out_specs=pl.BlockSpec((LAT_TQ, 256), lambda b, i: (LAT_ROW0 + b * LAT_QBLOCKS + i, mixer)),
        out_shape=jax.ShapeDtypeStruct((N_TOK, D_MODEL), BF16),
        input_output_aliases={len(in_specs): 0},
        compiler_params=_cparams(("parallel", "parallel")),
        name=f"{name}_l{l}",
    )(*args, o_buf)


def _sub_rows(u):
    return slice(u * LAT_SUB, (u + 1) * LAT_SUB)


def _store_units(o_ref, outs):
    for n, o in enumerate(outs):
        u, j = divmod(n, 2)
        o_ref[_sub_rows(u), j * LANES:(j + 1) * LANES] = o.astype(BF16)


def _gqa_kernel(q_ref, k_ref, v_ref, kc_ref, vc_ref, _, o_ref):
    keys = lambda: [(kc_ref[...].astype(BF16), None), (k_ref[...], None)]
    values = lambda: [vc_ref[...].astype(BF16), v_ref[...]]
    units = [_pair_unit(q_ref, _sub_rows(u), j, keys, values, j) for u in range(LAT_TQ // LAT_SUB) for j in range(2)]
    _store_units(o_ref, _run_units(units))


def _lat_gqa(l, qb, kb, vb, cache_k, cache_v, o_buf):
    cspec = pl.BlockSpec((None, None, PAST_LEN, LANES), lambda b, i: (b, l, 0, 0))
    return _lat_call(_gqa_kernel, l, "lat_gqa",
                     [_lat_qspec(512), _lat_kvspec(LANES), _lat_kvspec(LANES), cspec, cspec],
                     (qb, kb, vb, cache_k, cache_v), o_buf, 1)


WIN_KEYS = 3 * LAT_SUB


def _win_kernel(sink_ref, q_ref, k_ref, v_ref, kc_ref, vc_ref, _, o_ref):
    units = []
    for u in range(LAT_TQ // LAT_SUB):
        blk = pl.program_id(1) * (LAT_TQ // LAT_SUB) + u
        start = pl.multiple_of(jnp.clip((blk - 1) * LAT_SUB, 0, DEC_SEQ - WIN_KEYS), LAT_SUB)

        def keys(blk=blk, start=start):
            row = lax.broadcasted_iota(jnp.int32, (2 * LAT_SUB, WIN_KEYS), 0)
            qpos = blk * LAT_SUB + jnp.where(row < LAT_SUB, row, row - LAT_SUB)
            kpos = start + lax.broadcasted_iota(jnp.int32, (2 * LAT_SUB, WIN_KEYS), 1)
            bias = jnp.where(jnp.abs(qpos - kpos) <= WINDOW, 0.0, NEG_INF)
            return [(kc_ref[...].astype(BF16), None), (k_ref[pl.ds(start, WIN_KEYS), :], bias)]

        def values(start=start):
            return [vc_ref[...].astype(BF16), v_ref[pl.ds(start, WIN_KEYS), :]]

        for j in range(2):
            units.append(_pair_unit(q_ref, _sub_rows(u), j, keys, values, j, (sink_ref[2 * j], sink_ref[2 * j + 1])))
    _store_units(o_ref, _run_units(units))


def _lat_window(l, sink_c, qc, kc, vc, cache_k, cache_v, o_buf):
    cspec = pl.BlockSpec((None, None, PAST_LEN, LANES), lambda b, i: (b, l, 0, 0))
    return _lat_call(_win_kernel, l, "lat_win",
                     [pl.BlockSpec(memory_space=pltpu.SMEM), _lat_qspec(512), _lat_kvspec(LANES),
                      _lat_kvspec(LANES), cspec, cspec],
                     (sink_c, qc, kc, vc, cache_k, cache_v), o_buf, 2)


def _mla_kernel(q_ref, k_ref, v_ref, kc_ref, vc_ref, _, o_ref):
    units = []
    for u in range(LAT_TQ // LAT_SUB):
        for hd in range(4):
            c = slice(hd * LANES, (hd + 1) * LANES)
            vs = slice((hd // 2) * LANES, (hd // 2 + 1) * LANES)
            units.append(_single_unit(q_ref, _sub_rows(u), hd,
                                      lambda c=c: [(kc_ref[:, c], None), (k_ref[:, c], None)],
                                      lambda vs=vs: [vc_ref[:, vs], v_ref[:, vs]], MLA_QK ** -0.5))
    outs = _run_units(units)
    _store_units(o_ref, [_merge_halves(outs[2 * n], outs[2 * n + 1]) for n in range(len(outs) // 2)])


def _lat_mla(l, qd, kd, vd, kd_ctx, vd_ctx, o_buf):
    kcs = pl.BlockSpec((None, None, PAST_LEN, 512), lambda b, i: (l, b, 0, 0))
    vcs = pl.BlockSpec((None, None, PAST_LEN, 256), lambda b, i: (l, b, 0, 0))
    return _lat_call(_mla_kernel, l, "lat_mla",
                     [_lat_qspec(512), _lat_kvspec(512), _lat_kvspec(256), kcs, vcs],
                     (qd, kd, vd, kd_ctx, vd_ctx), o_buf, 3)


NA_KEYS = NA_ROWS * GRID_W
NA_ROWS_PER_STEP = LAT_TQ // GRID_W


def _na_shift(r):
    return jnp.where(r < NA_ROWS // 2, r,
                     jnp.where(r > GRID_ROWS - NA_ROWS // 2, r - (GRID_ROWS - NA_ROWS), NA_ROWS // 2))


def _na_kernel(q_ref, k_ref, v_ref, kc_ref, vc_ref, bias_ref, _, o_ref):
    units = []
    for rr in range(NA_ROWS_PER_STEP):
        r = pl.program_id(1) * NA_ROWS_PER_STEP + rr
        start = pl.multiple_of(jnp.clip(r - NA_ROWS // 2, 0, GRID_ROWS - NA_ROWS) * GRID_W, GRID_W)
        shift = _na_shift(r)
        rows = slice(rr * GRID_W, (rr + 1) * GRID_W)
        for j in range(2):
            c = slice(j * LANES, (j + 1) * LANES)

            def keys(j=j, c=c, start=start, shift=shift):
                bias = jnp.concatenate([bias_ref[2 * j, shift], bias_ref[2 * j + 1, shift]], axis=0)
                return [(kc_ref[:, c].astype(BF16), None), (k_ref[pl.ds(start, NA_KEYS), c], bias)]

            def values(c=c, start=start):
                return [vc_ref[:, c].astype(BF16), v_ref[pl.ds(start, NA_KEYS), c]]

            units.append(_pair_unit(q_ref, rows, j, keys, values, None))
    for n, o in enumerate(_run_units(units)):
        rr, j = divmod(n, 2)
        o_ref[rr * GRID_W:(rr + 1) * GRID_W, j * LANES:(j + 1) * LANES] = o.astype(BF16)


def _lat_na(l, qa, ka, va, cache_k, cache_v, na_bias, o_buf):
    cspec = pl.BlockSpec((None, None, PAST_LEN, 256), lambda b, i: (b, l, 0, 0))
    bspec = _resident((None, 4, NA_ROWS, GRID_W, NA_KEYS), lambda b, i: (l, 0, 0, 0, 0))
    return _lat_call(_na_kernel, l, "lat_na",
                     [_lat_qspec(512), _lat_kvspec(256), _lat_kvspec(256), cspec, cspec, bspec],
                     (qa, ka, va, cache_k, cache_v, na_bias), o_buf, 0)


def _na_bias_kernel(rpb_ref, out_ref, t_ref):
    l = pl.program_id(0)
    n_dr, n_dc = 2 * NA_ROWS - 1, 2 * NA_COLS - 1
    shape = (GRID_W, LANES)
    qc = lax.broadcasted_iota(jnp.int32, shape, 0)
    lane = _lane_iota(shape)
    kc = lane & (GRID_W - 1)
    dc = jnp.clip(kc - qc, 1 - NA_COLS, NA_COLS - 1) + (NA_COLS - 1)
    c0 = jnp.clip(qc - NA_COLS // 2, 0, GRID_W - NA_COLS)
    ok = (kc >= c0) & (kc < c0 + NA_COLS)
    for hd in range(4):
        def fill(a, carry):
            base = ((l * 4 + hd) * n_dr + a) * n_dc
            t = jnp.zeros(shape, F32)
            for b in range(n_dc):
                t = jnp.where(dc == b, rpb_ref[base + b], t)
            t_ref[a] = jnp.where(ok, t, NEG_INF)
            return carry
        lax.fori_loop(0, n_dr, fill, 0)
        for d in range(NA_ROWS):
            for m in range(NA_ROWS // 2):
                lo = t_ref[2 * m - d + NA_ROWS - 1]
                hi = t_ref[2 * m + 1 - d + NA_ROWS - 1]
                out_ref[hd, d, :, m * LANES:(m + 1) * LANES] = jnp.where(lane < GRID_W, lo, hi)


def _na_bias(rpb_flat):
    return pl.pallas_call(
        _na_bias_kernel,
        grid=(DEPTH,),
        in_specs=[pl.BlockSpec(memory_space=pltpu.SMEM)],
        out_specs=pl.BlockSpec((None, 4, NA_ROWS, GRID_W, NA_KEYS), lambda l: (l, 0, 0, 0, 0)),
        out_shape=jax.ShapeDtypeStruct((DEPTH, 4, NA_ROWS, GRID_W, NA_KEYS), F32),
        scratch_shapes=[pltpu.VMEM((2 * NA_ROWS - 1, GRID_W, LANES), F32)],
        compiler_params=_cparams(("parallel",)),
        name="na_bias",
    )(rpb_flat)


def _mla_ctx_kernel(ckv_ref, kpe_ref, wukv_ref, g_ref, bd_ref, kd_o, vd_o):
    kv = _dot(ckv_ref[...].astype(BF16), wukv_ref[...])
    kpe = kpe_ref[...]
    for j in range(2):
        kf = jnp.concatenate([kv[:, (2 * j) * LANES:(2 * j + 1) * LANES] + kpe,
                              kv[:, (2 * j + 1) * LANES:(2 * j + 2) * LANES] + kpe], axis=1)
        kd_o[:, 2 * j * LANES:(2 * j + 2) * LANES] = _group_norm(
            kf, bd_ref[1], 1.0 / MLA_QK, g_ref[7:8, :]).astype(BF16)
    vd_o[...] = kv[:, 4 * LANES:].astype(BF16)


def _mla_ctx(cache_ckv, cache_kpe_p, w_ukv_p, gains, bd):
    return pl.pallas_call(
        _mla_ctx_kernel,
        grid=(DEPTH, DEC_BATCH),
        in_specs=[
            pl.BlockSpec((None, None, PAST_LEN, LANES), lambda l, b: (b, l, 0, 0)),
            pl.BlockSpec((None, None, PAST_LEN, LANES), lambda l, b: (b, l, 0, 0)),
            pl.BlockSpec((None, MLA_KV_LORA, 768), lambda l, b: (l, 0, 0)),
            pl.BlockSpec((None, 16, 256), lambda l, b: (l, 0, 0)),
            pl.BlockSpec((2, 256, 256), lambda l, b: (0, 0, 0)),
        ],
        out_specs=[pl.BlockSpec((None, None, PAST_LEN, 512), lambda l, b: (l, b, 0, 0)),
                   pl.BlockSpec((None, None, PAST_LEN, 256), lambda l, b: (l, b, 0, 0))],
        out_shape=[jax.ShapeDtypeStruct((DEPTH, DEC_BATCH, PAST_LEN, 512), BF16),
                   jax.ShapeDtypeStruct((DEPTH, DEC_BATCH, PAST_LEN, 256), BF16)],
        compiler_params=_cparams(("parallel", "parallel")),
        name="mla_ctx",
    )(cache_ckv, cache_kpe_p, w_ukv_p, gains, bd)


def _merge_kernel(x_ref, o_ref, mod_ref, n1_ref, wg_ref, wb_ref, wo_ref, x1_ref):
    x = x_ref[...]
    mod = mod_ref[...]
    h = _rms_mod(x, n1_ref[...], mod[:, D_MODEL:2 * D_MODEL], mod[:, 0:D_MODEL])
    hb = h.astype(BF16)
    merged = None
    for n in range(4):
        gate = _dot(hb, wg_ref[:, n * D_MODEL:(n + 1) * D_MODEL])
        gate = 1.0 / (1.0 + jnp.exp(-gate))
        br = _dot(o_ref[:, n * 256:(n + 1) * 256], wb_ref[n])
        merged = gate * br if merged is None else merged + gate * br
    out = _dot(merged.astype(BF16), wo_ref[...])
    x1_ref[...] = x + mod[:, 2 * D_MODEL:3 * D_MODEL] * out


def _merge(l, x_all, o_buf, mods, norm1, w_gate, w_branch, w_out):
    tok = pl.BlockSpec((TM, D_MODEL), lambda i: (i, 0))
    return pl.pallas_call(
        _merge_kernel,
        grid=(N_TILES,),
        in_specs=[
            tok, tok,
            pl.BlockSpec((None, None, 1, 6 * D_MODEL), lambda i: (l, _tile_group(i), 0, 0)),
            _resident((None, 1, D_MODEL), lambda i: (l, 0, 0)),
            _resident((None, D_MODEL, N_GATE), lambda i: (l, 0, 0)),
            _resident((None, 4, 256, D_MODEL), lambda i: (l, 0, 0, 0)),
            _resident((None, D_MODEL, D_MODEL), lambda i: (l, 0, 0)),
        ],
        out_specs=tok,
        out_shape=jax.ShapeDtypeStruct((N_TOK, D_MODEL), F32),
        compiler_params=_cparams(("parallel",)),
        name=f"merge_l{l}",
    )(x_all, o_buf, mods, norm1, w_gate, w_branch, w_out)


HALO = 8


def _ffn_kernel(x_ref, xp_ref, xn_ref, mod_ref, n2_ref, wup_ref, cw_ref, cb_ref, wdn_ref, *rest, split_out):
    u_ref = rest[-1]
    i = pl.program_id(0)
    p = (i - CTX_TILES) % LAT_TILES_PER_BATCH
    has_prev = (i >= CTX_TILES) & (p != 0)
    has_next = (i >= CTX_TILES) & (p != LAT_TILES_PER_BATCH - 1)
    x = x_ref[...]
    mod = mod_ref[...]
    n2 = n2_ref[...]
    sh2, sc2 = mod[:, 3 * D_MODEL:4 * D_MODEL], mod[:, 4 * D_MODEL:5 * D_MODEL]
    h_prev = jnp.where(has_prev, _rms_mod(xp_ref[...], n2, sc2, sh2), 0.0)
    h_next = jnp.where(has_next, _rms_mod(xn_ref[...], n2, sc2, sh2), 0.0)
    hb = jnp.concatenate([h_prev, _rms_mod(x, n2, sc2, sh2), h_next], axis=0).astype(BF16)

    def cols(c):
        return (slice(c * FF_CHUNK, (c + 1) * FF_CHUNK), slice(D_FF + c * FF_CHUNK, D_FF + (c + 1) * FF_CHUNK))

    def up(c):
        va, vg = cols(c)
        u_ref[c, :, :FF_CHUNK] = _dot(hb, wup_ref[:, va])
        u_ref[c, :, FF_CHUNK:] = _dot(hb, wup_ref[:, vg])

    def both(ref, c):
        va, vg = cols(c)
        return jnp.concatenate([ref[:, va], ref[:, vg]], axis=1)

    acc = None
    up(0)
    for c in range(N_FF_CHUNKS):
        if c + 1 < N_FF_CHUNKS:
            up(c + 1)
        cw = both(cw_ref, c)
        t = (both(cb_ref, c) + u_ref[c, HALO - 1:HALO - 1 + TM, :] * cw[0:1] + u_ref[c, HALO:HALO + TM, :] * cw[1:2]
             + u_ref[c, HALO + 1:HALO + 1 + TM, :] * cw[2:3])
        a, g = t[:, :FF_CHUNK], t[:, FF_CHUNK:]
        act = (g * (1.0 / (1.0 + jnp.exp(-g)))) * a
        d = _dot(act.astype(BF16), wdn_ref[c * FF_CHUNK:(c + 1) * FF_CHUNK, :])
        acc = d if acc is None else acc + d
    y = x + mod[:, 5 * D_MODEL:6 * D_MODEL] * acc
    if split_out:
        @pl.when(i < CTX_TILES)
        def _():
            rest[0][...] = y

        @pl.when(i >= CTX_TILES)
        def _():
            rest[1][...] = y
    else:
        rest[0][...] = y


def _ffn(l, x1, mods, norm2, w_up, conv_w, conv_b, w_down, split_out):
    tok = pl.BlockSpec((TM, D_MODEL), lambda i: (i, 0))
    per = TM // HALO
    last = N_TOK // HALO - 1
    if split_out:
        out_specs = [pl.BlockSpec((TM, D_MODEL), lambda i: (jnp.minimum(i, CTX_TILES - 1), 0)),
                     pl.BlockSpec((TM, D_MODEL), lambda i: (jnp.maximum(i - CTX_TILES, 0), 0))]
        out_shape = [jax.ShapeDtypeStruct((N_CTX, D_MODEL), F32), jax.ShapeDtypeStruct((N_LAT, D_MODEL), F32)]
    else:
        out_specs, out_shape = tok, jax.ShapeDtypeStruct((N_TOK, D_MODEL), F32)
    return pl.pallas_call(
        functools.partial(_ffn_kernel, split_out=split_out),
        grid=(N_TILES,),
        in_specs=[
            tok,
            pl.BlockSpec((HALO, D_MODEL), lambda i: (jnp.maximum(i * per - 1, 0), 0)),
            pl.BlockSpec((HALO, D_MODEL), lambda i: (jnp.minimum((i + 1) * per, last), 0)),
            pl.BlockSpec((None, None, 1, 6 * D_MODEL), lambda i: (l, _tile_group(i), 0, 0)),
            _resident((None, 1, D_MODEL), lambda i: (l, 0, 0)),
            _resident((None, D_MODEL, 2 * D_FF), lambda i: (l, 0, 0)),
            _resident((None, 3, 2 * D_FF), lambda i: (l, 0, 0)),
            _resident((None, 1, 2 * D_FF), lambda i: (l, 0, 0)),
            _resident((None, D_FF, D_MODEL), lambda i: (l, 0, 0)),
        ],
        out_specs=out_specs,
        out_shape=out_shape,
        scratch_shapes=[pltpu.VMEM((N_FF_CHUNKS, TM + 2 * HALO, 2 * FF_CHUNK), F32)],
        compiler_params=_cparams(("arbitrary",)),
        name=f"ffn_l{l}",
    )(x1, x1, x1, mods, norm2, w_up, conv_w, conv_b, w_down)


def _cast_kernel(a_ref, o_ref):
    o_ref[...] = a_ref[...].astype(BF16)


def _cast_bf16(a, rows, name):
    _, r, c = a.shape
    spec = pl.BlockSpec((None, rows, c), lambda l, i: (l, i, 0))
    return pl.pallas_call(
        _cast_kernel, grid=(DEPTH, r // rows), in_specs=[spec], out_specs=spec,
        out_shape=jax.ShapeDtypeStruct(a.shape, BF16),
        compiler_params=_cparams(("parallel", "parallel")), name=name,
    )(a)


def _win_prep_kernel(w_ref, q_ref, g_ref):
    bf = lambda c0, c1: w_ref[:, c0:c1].astype(BF16)
    rows = w_ref.shape[0]
    g_ref[...] = bf(IN_QKV, IN_QKV + N_GATE)
    q_ref[:, 0:C_QD] = bf(0, C_QD)
    for hd in range(4):
        dst = C_QD + hd * LANES
        q_ref[:, dst:dst + MLA_QK] = bf(C_QD + hd * MLA_QK, C_QD + (hd + 1) * MLA_QK)
        q_ref[:, dst + MLA_QK:dst + LANES] = jnp.zeros((rows, LANES - MLA_QK), BF16)
    src_ckv = C_QD + 4 * MLA_QK
    q_ref[:, C_CKV:C_KPE] = bf(src_ckv, src_ckv + MLA_KV_LORA)
    q_ref[:, C_KPE:C_KPE + MLA_NOPE] = jnp.zeros((rows, MLA_NOPE), BF16)
    q_ref[:, C_KPE + MLA_NOPE:C_KPE + MLA_QK] = bf(src_ckv + MLA_KV_LORA, IN_QKV)
    q_ref[:, C_KPE + MLA_QK:C_END] = jnp.zeros((rows, LANES - MLA_QK), BF16)


def _win_prep(w_in):
    rows = 256
    return pl.pallas_call(
        _win_prep_kernel,
        grid=(DEPTH, D_MODEL // rows),
        in_specs=[pl.BlockSpec((None, rows, w_in.shape[2]), lambda l, i: (l, i, 0))],
        out_specs=[pl.BlockSpec((None, rows, C_END), lambda l, i: (l, i, 0)),
                   pl.BlockSpec((None, rows, N_GATE), lambda l, i: (l, i, 0))],
        out_shape=[jax.ShapeDtypeStruct((DEPTH, D_MODEL, C_END), BF16),
                   jax.ShapeDtypeStruct((DEPTH, D_MODEL, N_GATE), BF16)],
        compiler_params=_cparams(("parallel", "parallel")),
        name="w_in_prep",
    )(w_in)


def _rope_tables():
    t = np.arange(DEC_SEQ)

    def angles(dim):
        half = dim // 2
        inv = ROPE_THETA ** (-np.arange(0, half, 2, dtype=np.float32) / half)
        row = (t // GRID_W).astype(np.float32)[:, None] * inv[None, :]
        col = (t % GRID_W).astype(np.float32)[:, None] * inv[None, :]
        return np.concatenate([row, row, col, col], axis=-1).astype(np.float32)

    def signed(ang, quarter):
        cos, sin = np.cos(ang), np.sin(ang)
        first = (np.arange(ang.shape[1]) % (2 * quarter)) < quarter
        return cos, np.where(first, -sin, 0.0), np.where(first, 0.0, sin)

    c64, lo64, hi64 = signed(angles(HEAD_DIM), 16)
    c64, lo64, hi64 = (np.tile(a, (1, 2)) for a in (c64, lo64, hi64))
    cm, lom, him = signed(angles(MLA_ROPE), 8)

    def mla_pad(a, fill):
        out = np.full((DEC_SEQ, LANES), fill, np.float32)
        out[:, MLA_NOPE:MLA_QK] = a
        return out

    tabs = [c64, lo64, hi64, mla_pad(cm, 1.0), mla_pad(lom, 0.0), mla_pad(him, 0.0)]
    ident = [np.ones, np.zeros, np.zeros, np.ones, np.zeros, np.zeros]
    full = [np.concatenate([f((TM, LANES), np.float32), a.astype(np.float32)], axis=0) for f, a in zip(ident, tabs)]
    return jnp.asarray(np.stack(full, axis=0))


def _block_diag_ones():
    idx = np.arange(256)
    mats = [(idx[:, None] // w == idx[None, :] // w).astype(np.float32) for w in (HEAD_DIM, LANES)]
    return jnp.asarray(np.stack(mats, axis=0), dtype=BF16)


def _prep_small(w_ukv, qn_a, kn_a, qn_b, kn_b, qn_c, kn_c, qn_d, kn_d, kvn_d):
    ukv = w_ukv.reshape(DEPTH, MLA_KV_LORA, 4, 2, HEAD_DIM)
    k_part = jnp.pad(ukv[:, :, :, 0], ((0, 0), (0, 0), (0, 0), (0, HEAD_DIM))).reshape(DEPTH, MLA_KV_LORA, 512)
    v_part = ukv[:, :, :, 1].reshape(DEPTH, MLA_KV_LORA, 256)
    w_ukv_p = jnp.concatenate([k_part, v_part], axis=-1).astype(BF16)

    def row(g, reps, width=256):
        t = jnp.tile(g, (1, reps))
        return jnp.pad(t, ((0, 0), (0, width - t.shape[1])))

    pad96 = lambda g: jnp.pad(g, ((0, 0), (0, LANES - MLA_QK)))
    rows = [row(qn_a, 4), row(kn_a, 4), row(qn_b, 4), row(kn_b, 2), row(qn_c, 4), row(kn_c, 2),
            row(pad96(qn_d), 2), row(pad96(kn_d), 2), row(kvn_d, 1)]
    gains = jnp.stack(rows + [jnp.zeros_like(rows[0])] * (16 - len(rows)), axis=1)
    return w_ukv_p, gains


def kernel(x_prompt, x_sample, cache_nat_k, cache_nat_v, cache_gqa_k, cache_gqa_v, cache_win_k, cache_win_v,
           cache_mla_ckv, cache_mla_kpe, c, c_ctx, w_ada, b_ada, norm1, norm2, w_in, qn_a, kn_a, rpb_a,
           qn_b, kn_b, qn_c, kn_c, sink_c, qn_d, kn_d, kvn_d, w_ukv, w_branch, w_out, w_up, conv_w, conv_b,
           w_down):
    w_ukv_p, gains = _prep_small(w_ukv, qn_a, kn_a, qn_b, kn_b, qn_c, kn_c, qn_d, kn_d, kvn_d)
    w_qkv, w_gate = _win_prep(w_in)
    w_branch_b = _cast_bf16(w_branch.reshape(DEPTH, 4 * 256, D_MODEL), 1024, "cast_w_branch").reshape(
        DEPTH, 4, 256, D_MODEL)
    w_out_b = _cast_bf16(w_out, 1024, "cast_w_out")
    w_up_b = _cast_bf16(w_up, 256, "cast_w_up")
    w_down_b = _cast_bf16(w_down, D_FF // 4, "cast_w_down")
    conv_b_r = conv_b.reshape(DEPTH, 1, 2 * D_FF)
    rope_tab = _rope_tables()
    bd = _block_diag_ones()
    norm1_r = norm1.reshape(DEPTH, 1, D_MODEL)
    norm2_r = norm2.reshape(DEPTH, 1, D_MODEL)

    cond8 = jnp.zeros((8, D_MODEL), F32).at[0].set(c_ctx).at[1:1 + DEC_BATCH].set(c)
    mods = _adaln(cond8, w_ada, b_ada).reshape(DEPTH, 8, 1, 6 * D_MODEL)

    rpb_flat = jnp.pad(rpb_a.reshape(-1), (0, 8192 - rpb_a.size))
    na_bias = _na_bias(rpb_flat)
    kpe_p = jnp.pad(cache_mla_kpe, ((0, 0), (0, 0), (0, 0), (MLA_NOPE, LANES - MLA_QK)))
    kd_ctx, vd_ctx = _mla_ctx(cache_mla_ckv, kpe_p, w_ukv_p, gains, bd)
    c_nat_k = cache_nat_k.reshape(DEC_BATCH, DEPTH, PAST_LEN, 256)
    c_nat_v = cache_nat_v.reshape(DEC_BATCH, DEPTH, PAST_LEN, 256)
    c_gqa_k = cache_gqa_k.reshape(DEC_BATCH, DEPTH, PAST_LEN, LANES)
    c_gqa_v = cache_gqa_v.reshape(DEC_BATCH, DEPTH, PAST_LEN, LANES)
    c_win_k = cache_win_k.reshape(DEC_BATCH, DEPTH, PAST_LEN, LANES)
    c_win_v = cache_win_v.reshape(DEC_BATCH, DEPTH, PAST_LEN, LANES)

    x = jnp.concatenate([x_prompt.reshape(N_CTX, D_MODEL), x_sample.reshape(N_LAT, D_MODEL)], axis=0)
    caches = None
    for l in range(DEPTH):
        qkv, caches = _inproj(l, x, mods, norm1_r, w_qkv, w_ukv_p, gains, bd, rope_tab, caches)
        qa, ka, va, qb, kb, vb, qc, kc, vc, qd, kd, vd = qkv
        sink_l = sink_c[l]
        o = _ctx_attention(l, sink_l, qkv)
        o = _lat_na(l, qa, ka, va, c_nat_k, c_nat_v, na_bias, o)
        o = _lat_gqa(l, qb, kb, vb, c_gqa_k, c_gqa_v, o)
        o = _lat_window(l, sink_l, qc, kc, vc, c_win_k, c_win_v, o)
        o = _lat_mla(l, qd, kd, vd, kd_ctx, vd_ctx, o)
        x1 = _merge(l, x, o, mods, norm1_r, w_gate, w_branch_b, w_out_b)
        x = _ffn(l, x1, mods, norm2_r, w_up_b, conv_w, conv_b_r, w_down_b, split_out=(l == DEPTH - 1))
    y_prompt = x[0].reshape(BATCH, SEQ, D_MODEL)
    y_sample = x[1].reshape(DEC_BATCH, DEC_SEQ, D_MODEL)
    heads = (4, 4, 2, 2, 2, 2)
    outs = [a.reshape(BATCH, DEPTH, SEQ, h, HEAD_DIM) for a, h in zip(caches[:6], heads)] + list(caches[6:])
    return (y_prompt, y_sample, *outs)
```

```python
import functools

import jax
import jax.numpy as jnp
import numpy as np
from jax import lax
from jax.experimental import pallas as pl
from jax.experimental.pallas import tpu as pltpu

F32 = jnp.float32
BF16 = jnp.bfloat16

D_MODEL = 1024
BATCH = 16
SEQ = 256
DEPTH = 4
DEC_BATCH = 2
DEC_SEQ = 2048
PAST_LEN = 256
GRID_W = 64
HEAD_DIM = 64
NA_ROWS = 8
NA_COLS = 16
WINDOW = 128
MLA_NOPE = 64
MLA_ROPE = 32
MLA_QK = MLA_NOPE + MLA_ROPE
MLA_KV_LORA = 128
D_FF = 2816
ROPE_THETA = 10000.0
EPS = 1e-6
NEG_INF = -1e30

LANES = 128
TM = 256
N_CTX = BATCH * SEQ
N_LAT = DEC_BATCH * DEC_SEQ
N_TOK = N_CTX + N_LAT
CTX_TILES = N_CTX // TM
LAT_TILES_PER_BATCH = DEC_SEQ // TM
N_TILES = N_TOK // TM
GRID_ROWS = DEC_SEQ // GRID_W
FF_CHUNK = 256
N_FF_CHUNKS = D_FF // FF_CHUNK
VMEM_LIMIT = 56 * 1024 * 1024

C_QA, C_KA, C_VA = 0, 256, 512
C_QB, C_KB, C_VB = 768, 1024, 1152
C_QC, C_KC, C_VC = 1280, 1536, 1664
C_QD, C_CKV, C_KPE, C_END = 1792, 2304, 2432, 2560
N_GATE = 4 * D_MODEL
IN_QKV = 2336
Q_SCALE = HEAD_DIM ** -0.5


def _cparams(sem):
    return pltpu.CompilerParams(dimension_semantics=sem, vmem_limit_bytes=VMEM_LIMIT)


def _tile_group(i):
    return jnp.where(i < CTX_TILES, 0, 1 + (i - CTX_TILES) // LAT_TILES_PER_BATCH)


def _tile_pos_block(i):
    return jnp.where(i < CTX_TILES, 0, 1 + (i - CTX_TILES) % LAT_TILES_PER_BATCH)


def _resident(shape, index_map):
    return pl.BlockSpec(shape, index_map, pipeline_mode=pl.Buffered(1))


def _dot(a, b):
    return jnp.dot(a, b, preferred_element_type=F32)


def _dot_t(a, b):
    return lax.dot_general(a, b, (((1,), (1,)), ((), ())), preferred_element_type=F32)


def _rms_mod(x, gain, scale, shift):
    ms = jnp.mean(x * x, axis=-1, keepdims=True)
    return (x * lax.rsqrt(ms + EPS)) * gain * (1.0 + scale) + shift


def _group_norm(t, ones_bd, inv_n, gain):
    ss = _dot((t * t).astype(BF16), ones_bd)
    return t * lax.rsqrt(ss * inv_n + EPS) * gain


def _rope(t, cos, sin_lo, sin_hi, half):
    return (t * cos + pltpu.roll(t, LANES - half, 1) * sin_lo + pltpu.roll(t, half, 1) * sin_hi)


def _lane_iota(shape):
    return lax.broadcasted_iota(jnp.int32, shape, len(shape) - 1)


def _low_half(shape):
    return _lane_iota(shape) < HEAD_DIM


def _run_units(units, depth=2):
    n = len(units)
    pending = {}
    outs = []
    for k in range(min(depth, n)):
        pending[k] = units[k][0]()
    for k in range(n):
        outs.append(units[k][1](pending.pop(k)))
        if k + depth < n:
            pending[k + depth] = units[k + depth][0]()
    return outs


def _adaln_kernel(cond_ref, w_ref, b_ref, out_ref):
    cnd = cond_ref[...]
    s = cnd * (1.0 / (1.0 + jnp.exp(-cnd)))
    out_ref[...] = jnp.dot(s, w_ref[...], preferred_element_type=F32,
                           precision=lax.Precision.HIGHEST) + b_ref[...]


def _adaln(cond8, w_ada, b_ada):
    n_col = 6 * D_MODEL
    tn = 1024
    return pl.pallas_call(
        _adaln_kernel,
        grid=(DEPTH, n_col // tn),
        in_specs=[
            pl.BlockSpec((8, D_MODEL), lambda l, n: (0, 0)),
            pl.BlockSpec((None, D_MODEL, tn), lambda l, n: (l, 0, n)),
            pl.BlockSpec((None, 1, tn), lambda l, n: (l, 0, n)),
        ],
        out_specs=pl.BlockSpec((None, 8, tn), lambda l, n: (l, 0, n)),
        out_shape=jax.ShapeDtypeStruct((DEPTH, 8, n_col), F32),
        compiler_params=_cparams(("parallel", "parallel")),
        name="adaln",
    )(cond8, w_ada, b_ada.reshape(DEPTH, 1, n_col))


_QKV_WIDTHS = (512, 256, 256, 512, 128, 128, 512, 128, 128, 512, 512, 256)
_CACHE_WIDTHS = (256, 256, 128, 128, 128, 128, MLA_KV_LORA, MLA_ROPE)
_CACHE_BLOCKS = tuple((w, SEQ) for w in _CACHE_WIDTHS)


def _inproj_kernel(x_ref, mod_ref, n1_ref, w_ref, wukv_ref, g_ref, bd_ref, rope_ref, *rest):
    outs = rest[-(len(_QKV_WIDTHS) + len(_CACHE_WIDTHS)):]
    qa_o, ka_o, va_o, qb_o, kb_o, vb_o, qc_o, kc_o, vc_o, qd_o, kd_o, vd_o = outs[:len(_QKV_WIDTHS)]
    natk_o, natv_o, gqak_o, gqav_o, wink_o, winv_o, ckv_o, kpe_o = outs[len(_QKV_WIDTHS):]
    x = x_ref[...]
    mod = mod_ref[...]
    hb = _rms_mod(x, n1_ref[...], mod[:, D_MODEL:2 * D_MODEL], mod[:, 0:D_MODEL]).astype(BF16)
    bd64, bd128 = bd_ref[0], bd_ref[1]
    cos64, slo64, shi64 = rope_ref[0], rope_ref[1], rope_ref[2]
    cosm, slom, shim = rope_ref[3], rope_ref[4], rope_ref[5]
    inv64, inv96 = 1.0 / HEAD_DIM, 1.0 / MLA_QK
    low = _low_half((TM, LANES))
    held = {}

    def chunks(t):
        return [t[:, c:c + LANES] for c in range(0, t.shape[1], LANES)]

    def rope64(t):
        parts = [_rope(p, cos64, slo64, shi64, 16) for p in chunks(t)]
        return parts[0] if len(parts) == 1 else jnp.concatenate(parts, axis=1)

    def ropem(t):
        return jnp.concatenate([_rope(p, cosm, slom, shim, 8) for p in chunks(t)], axis=1)

    def store_q(ref, q, grouped):
        q = q * Q_SCALE
        slots = []
        for hd in range(4):
            part, src = q[:, (hd // 2) * LANES:(hd // 2 + 1) * LANES], hd % 2
            dst = hd // 2 if grouped else src
            if src != dst:
                part = pltpu.roll(part, HEAD_DIM, 1)
            slots.append(jnp.where(low if dst == 0 else ~low, part, 0.0))
        ref[...] = jnp.concatenate(slots, axis=1).astype(BF16)

    def post_latent(t):
        ckv = _group_norm(t[:, :LANES], bd128[:LANES, :LANES], 1.0 / MLA_KV_LORA, g_ref[8:9, :LANES])
        kpe = t[:, LANES:]
        ckv_o[...] = ckv.T
        kpe_o[...] = kpe.T[MLA_NOPE:MLA_QK, :]
        held["kpe"] = kpe
        held["kv"] = _dot(ckv.astype(BF16), wukv_ref[...])

    def post_kd(_):
        kv, kpe = held["kv"], held["kpe"]
        for j in range(2):
            kf = jnp.concatenate([kv[:, (2 * j) * LANES:(2 * j + 1) * LANES] + kpe,
                                  kv[:, (2 * j + 1) * LANES:(2 * j + 2) * LANES] + kpe], axis=1)
            kd_o[:, 2 * j * LANES:(2 * j + 2) * LANES] = ropem(
                _group_norm(kf, bd128, inv96, g_ref[7:8, :])).astype(BF16)
        vd_o[...] = kv[:, 4 * LANES:].astype(BF16)

    def post_qa(t):
        store_q(qa_o, _group_norm(t, bd64, inv64, g_ref[0:1, :]), grouped=False)

    def post_ka(t):
        k = _group_norm(t, bd64, inv64, g_ref[1:2, :])
        natk_o[...] = k.T
        ka_o[...] = k.astype(BF16)

    def post_va(t):
        natv_o[...] = t.T
        va_o[...] = t.astype(BF16)

    def post_q_rope(ref, row):
        def post(t):
            store_q(ref, rope64(_group_norm(t, bd64, inv64, g_ref[row:row + 1, :])), grouped=True)
        return post

    def post_kv(k_ref, v_ref, kc_ref, vc_ref, row):
        def post(t):
            k = _group_norm(t[:, :LANES], bd64[:LANES, :LANES], inv64, g_ref[row:row + 1, :LANES])
            kc_ref[...] = k.T
            k_ref[...] = rope64(k).astype(BF16)
            v = t[:, LANES:]
            vc_ref[...] = v.T
            v_ref[...] = v.astype(BF16)
        return post

    def post_qd(j):
        def post(t):
            qd_o[:, 2 * j * LANES:(2 * j + 2) * LANES] = ropem(
                _group_norm(t, bd128, inv96, g_ref[6:7, :])).astype(BF16)
        return post

    stages = [
        ((C_CKV, C_END), post_latent),
        ((C_QA, C_KA), post_qa),
        (None, post_kd),
        ((C_KA, C_VA), post_ka),
        ((C_VA, C_QB), post_va),
        ((C_QB, C_KB), post_q_rope(qb_o, 2)),
        ((C_KB, C_QC), post_kv(kb_o, vb_o, gqak_o, gqav_o, 3)),
        ((C_QC, C_KC), post_q_rope(qc_o, 4)),
        ((C_KC, C_QD), post_kv(kc_o, vc_o, wink_o, winv_o, 5)),
        ((C_QD, C_QD + 2 * LANES), post_qd(0)),
        ((C_QD + 2 * LANES, C_CKV), post_qd(1)),
    ]

    def issue(cols):
        return None if cols is None else _dot(hb, w_ref[:, cols[0]:cols[1]])

    _run_units([(functools.partial(issue, cols), post) for cols, post in stages], depth=2)


def _inproj(l, x_all, mods, norm1, w_qkv, w_ukv_p, gains, bd, rope_tab, caches):
    tile = lambda s: (s + CTX_TILES) % N_TILES
    tok = lambda w: pl.BlockSpec((TM, w), lambda s: (tile(s), 0))
    in_specs = [
        tok(D_MODEL),
        pl.BlockSpec((None, None, 1, 6 * D_MODEL), lambda s: (l, _tile_group(tile(s)), 0, 0)),
        _resident((None, 1, D_MODEL), lambda s: (l, 0, 0)),
        _resident((None, D_MODEL, C_END), lambda s: (l, 0, 0)),
        _resident((None, MLA_KV_LORA, 768), lambda s: (l, 0, 0)),
        _resident((None, 16, 256), lambda s: (l, 0, 0)),
        _resident((2, 256, 256), lambda s: (0, 0, 0)),
        pl.BlockSpec((6, TM, LANES), lambda s: (0, _tile_pos_block(tile(s)), 0)),
    ]
    args = [x_all, mods, norm1, w_qkv, w_ukv_p, gains, bd, rope_tab]
    aliases = {}
    if caches is not None:
        aliases = {len(in_specs) + n: len(_QKV_WIDTHS) + n for n in range(len(caches))}
        in_specs = in_specs + [pl.BlockSpec(memory_space=pl.ANY)] * len(caches)
        args = args + list(caches)
    cache_spec = lambda blk: pl.BlockSpec((None, None) + blk, lambda s: (jnp.maximum(s - CTX_TILES, 0), l, 0, 0))
    outs = pl.pallas_call(
        _inproj_kernel,
        grid=(N_TILES,),
        in_specs=in_specs,
        out_specs=[tok(w) for w in _QKV_WIDTHS] + [cache_spec(blk) for blk in _CACHE_BLOCKS],
        out_shape=[jax.ShapeDtypeStruct((N_TOK, w), BF16) for w in _QKV_WIDTHS]
        + [jax.ShapeDtypeStruct((BATCH, DEPTH) + blk, F32) for blk in _CACHE_BLOCKS],
        input_output_aliases=aliases,
        compiler_params=_cparams(("arbitrary",)),
        name=f"inproj_l{l}",
    )(*args)
    return outs[:len(_QKV_WIDTHS)], outs[len(_QKV_WIDTHS):]


def _scores(q, keys, scale=None):
    out = []
    for k, bias in keys:
        s = _dot_t(q, k)
        if scale is not None:
            s = s * scale
        if bias is not None:
            s = s + bias
        out.append(s)
    return out


def _softmax_pv(scores, values, sink=None):
    m = functools.reduce(jnp.maximum, [jnp.max(s, axis=-1, keepdims=True) for s in scores])
    if sink is not None:
        m = jnp.maximum(m, sink)
    denom = None
    out = None
    for s, v in zip(scores, values):
        e = jnp.exp(s - m)
        d = jnp.sum(e, axis=-1, keepdims=True)
        o = _dot(e.astype(BF16), v)
        denom = d if denom is None else denom + d
        out = o if out is None else out + o
    if sink is not None:
        denom = denom + jnp.exp(sink - m)
    return out / denom


def _pair_unit(q_ref, rows, j, keys, values, kv_half, sinks=None):
    m_rows = rows.stop - rows.start

    def issue():
        q2 = jnp.concatenate([q_ref[rows, (2 * j) * LANES:(2 * j + 1) * LANES],
                              q_ref[rows, (2 * j + 1) * LANES:(2 * j + 2) * LANES]], axis=0)
        return _scores(q2, keys())

    def finish(scores):
        sink = None
        if sinks is not None:
            row = lax.broadcasted_iota(jnp.int32, (2 * m_rows, 1), 0)
            sink = jnp.where(row < m_rows, sinks[0], sinks[1])
        o2 = _softmax_pv(scores, values(), sink)
        lo, hi = o2[:m_rows], o2[m_rows:]
        if kv_half == 0:
            hi = pltpu.roll(hi, HEAD_DIM, 1)
        elif kv_half == 1:
            lo = pltpu.roll(lo, HEAD_DIM, 1)
        return jnp.where(_low_half(lo.shape), lo, hi)

    return issue, finish


def _single_unit(q_ref, rows, hd, keys, values, scale):
    def issue():
        return _scores(q_ref[rows, hd * LANES:(hd + 1) * LANES], keys(), scale)

    def finish(scores):
        return _softmax_pv(scores, values())

    return issue, finish


def _merge_halves(o_even, o_odd):
    return jnp.where(_low_half(o_even.shape), o_even, o_odd)


def _ctx_attn_kernel(sink_ref, qa, ka, va, qb, kb, vb, qc, kc, vc, qd, kd, vd, o_ref):
    rows = slice(0, SEQ)
    units = []
    for j in range(2):
        c = slice(j * LANES, (j + 1) * LANES)
        units.append(_pair_unit(qa, rows, j, lambda c=c: [(ka[:, c], None)], lambda c=c: [va[:, c]], None))
    for q, k, v, with_sink in ((qb, kb, vb, False), (qc, kc, vc, True)):
        for j in range(2):
            sinks = (sink_ref[2 * j], sink_ref[2 * j + 1]) if with_sink else None
            units.append(_pair_unit(q, rows, j, lambda k=k: [(k[...], None)], lambda v=v: [v[...]], j, sinks))
    for hd in range(4):
        c = slice(hd * LANES, (hd + 1) * LANES)
        vs = slice((hd // 2) * LANES, (hd // 2 + 1) * LANES)
        units.append(_single_unit(qd, rows, hd, lambda c=c: [(kd[:, c], None)], lambda vs=vs: [vd[:, vs]],
                                  MLA_QK ** -0.5))
    outs = _run_units(units)
    for n in range(6):
        o_ref[:, n * LANES:(n + 1) * LANES] = outs[n].astype(BF16)
    for j in range(2):
        o_ref[:, (6 + j) * LANES:(7 + j) * LANES] = _merge_halves(outs[6 + 2 * j], outs[7 + 2 * j]).astype(BF16)


def _ctx_attention(l, sink_c, qkv):
    blk = lambda w: pl.BlockSpec((SEQ, w), lambda b: (b, 0))
    return pl.pallas_call(
        _ctx_attn_kernel,
        grid=(BATCH,),
        in_specs=[pl.BlockSpec(memory_space=pltpu.SMEM)] + [blk(w) for w in _QKV_WIDTHS],
        out_specs=pl.BlockSpec((SEQ, D_MODEL), lambda b: (b, 0)),
        out_shape=jax.ShapeDtypeStruct((N_TOK, D_MODEL), BF16),
        compiler_params=_cparams(("parallel",)),
        name=f"ctx_attn_l{l}",
    )(sink_c, *qkv)


LAT_TQ = 256
LAT_SUB = 128
LAT_QBLOCKS = DEC_SEQ // LAT_TQ
LAT_ROW0 = N_CTX // LAT_TQ
LAT_KV0 = N_CTX // DEC_SEQ


def _lat_qspec(width):
    return pl.BlockSpec((LAT_TQ, width), lambda b, i: (LAT_ROW0 + b * LAT_QBLOCKS + i, 0))


def _lat_kvspec(width):
    return pl.BlockSpec((DEC_SEQ, width), lambda b, i: (LAT_KV0 + b, 0))


def _lat_call(kernel, l, name, in_specs, args, o_buf, mixer):
    return pl.pallas_call(
        kernel,
        grid=(DEC_BATCH, LAT_QBLOCKS),
        in_specs=in_specs + [pl.BlockSpec(memory_space=pl.ANY)],
        out_specs=pl.BlockSpec((LAT_TQ, 256), lambda b, i: (LAT_ROW0 + b * LAT_QBLOCKS + i, mixer)),
        out_shape=jax.ShapeDtypeStruct((N_TOK, D_MODEL), BF16),
        input_output_aliases={len(in_specs): 0},
        compiler_params=_cparams(("parallel", "parallel")),
        name=f"{name}_l{l}",
    )(*args, o_buf)


def _sub_rows(u):
    return slice(u * LAT_SUB, (u + 1) * LAT_SUB)


def _store_units(o_ref, outs):
    for n, o in enumerate(outs):
        u, j = divmod(n, 2)
        o_ref[_sub_rows(u), j * LANES:(j + 1) * LANES] = o.astype(BF16)


def _gqa_kernel(q_ref, k_ref, v_ref, kc_ref, vc_ref, _, o_ref):
    keys = lambda: [(kc_ref[...].astype(BF16), None), (k_ref[...], None)]
    values = lambda: [vc_ref[...].astype(BF16), v_ref[...]]
    units = [_pair_unit(q_ref, _sub_rows(u), j, keys, values, j) for u in range(LAT_TQ // LAT_SUB) for j in range(2)]
    _store_units(o_ref, _run_units(units))


def _lat_gqa(l, qb, kb, vb, cache_k, cache_v, o_buf):
    cspec = pl.BlockSpec((None, None, PAST_LEN, LANES), lambda b, i: (b, l, 0, 0))
    return _lat_call(_gqa_kernel, l, "lat_gqa",
                     [_lat_qspec(512), _lat_kvspec(LANES), _lat_kvspec(LANES), cspec, cspec],
                     (qb, kb, vb, cache_k, cache_v), o_buf, 1)


WIN_KEYS = 3 * LAT_SUB


def _win_kernel(sink_ref, q_ref, k_ref, v_ref, kc_ref, vc_ref, _, o_ref):
    units = []
    for u in range(LAT_TQ // LAT_SUB):
        blk = pl.program_id(1) * (LAT_TQ // LAT_SUB) + u
        start = pl.multiple_of(jnp.clip((blk - 1) * LAT_SUB, 0, DEC_SEQ - WIN_KEYS), LAT_SUB)

        def keys(blk=blk, start=start):
            row = lax.broadcasted_iota(jnp.int32, (2 * LAT_SUB, WIN_KEYS), 0)
            qpos = blk * LAT_SUB + jnp.where(row < LAT_SUB, row, row - LAT_SUB)
            kpos = start + lax.broadcasted_iota(jnp.int32, (2 * LAT_SUB, WIN_KEYS), 1)
            bias = jnp.where(jnp.abs(qpos - kpos) <= WINDOW, 0.0, NEG_INF)
            return [(kc_ref[...].astype(BF16), None), (k_ref[pl.ds(start, WIN_KEYS), :], bias)]

        def values(start=start):
            return [vc_ref[...].astype(BF16), v_ref[pl.ds(start, WIN_KEYS), :]]

        for j in range(2):
            units.append(_pair_unit(q_ref, _sub_rows(u), j, keys, values, j, (sink_ref[2 * j], sink_ref[2 * j + 1])))
    _store_units(o_ref, _run_units(units))


def _lat_window(l, sink_c, qc, kc, vc, cache_k, cache_v, o_buf):
    cspec = pl.BlockSpec((None, None, PAST_LEN, LANES), lambda b, i: (b, l, 0, 0))
    return _lat_call(_win_kernel, l, "lat_win",
                     [pl.BlockSpec(memory_space=pltpu.SMEM), _lat_qspec(512), _lat_kvspec(LANES),
                      _lat_kvspec(LANES), cspec, cspec],
                     (sink_c, qc, kc, vc, cache_k, cache_v), o_buf, 2)


def _mla_kernel(q_ref, k_ref, v_ref, kc_ref, vc_ref, _, o_ref):
    units = []
    for u in range(LAT_TQ // LAT_SUB):
        for hd in range(4):
            c = slice(hd * LANES, (hd + 1) * LANES)
            vs = slice((hd // 2) * LANES, (hd // 2 + 1) * LANES)
            units.append(_single_unit(q_ref, _sub_rows(u), hd,
                                      lambda c=c: [(kc_ref[:, c], None), (k_ref[:, c], None)],
                                      lambda vs=vs: [vc_ref[:, vs], v_ref[:, vs]], MLA_QK ** -0.5))
    outs = _run_units(units)
    _store_units(o_ref, [_merge_halves(outs[2 * n], outs[2 * n + 1]) for n in range(len(outs) // 2)])


def _lat_mla(l, qd, kd, vd, kd_ctx, vd_ctx, o_buf):
    kcs = pl.BlockSpec((None, None, PAST_LEN, 512), lambda b, i: (l, b, 0, 0))
    vcs = pl.BlockSpec((None, None, PAST_LEN, 256), lambda b, i: (l, b, 0, 0))
    return _lat_call(_mla_kernel, l, "lat_mla",
                     [_lat_qspec(512), _lat_kvspec(512), _lat_kvspec(256), kcs, vcs],
                     (qd, kd, vd, kd_ctx, vd_ctx), o_buf, 3)


NA_KEYS = NA_ROWS * GRID_W
NA_ROWS_PER_STEP = LAT_TQ // GRID_W


def _na_shift(r):
    return jnp.where(r < NA_ROWS // 2, r,
                     jnp.where(r > GRID_ROWS - NA_ROWS // 2, r - (GRID_ROWS - NA_ROWS), NA_ROWS // 2))


def _na_kernel(q_ref, k_ref, v_ref, kc_ref, vc_ref, bias_ref, _, o_ref):
    units = []
    for rr in range(NA_ROWS_PER_STEP):
        r = pl.program_id(1) * NA_ROWS_PER_STEP + rr
        start = pl.multiple_of(jnp.clip(r - NA_ROWS // 2, 0, GRID_ROWS - NA_ROWS) * GRID_W, GRID_W)
        shift = _na_shift(r)
        rows = slice(rr * GRID_W, (rr + 1) * GRID_W)
        for j in range(2):
            c = slice(j * LANES, (j + 1) * LANES)

            def keys(j=j, c=c, start=start, shift=shift):
                bias = jnp.concatenate([bias_ref[2 * j, shift], bias_ref[2 * j + 1, shift]], axis=0)
                return [(kc_ref[:, c].astype(BF16), None), (k_ref[pl.ds(start, NA_KEYS), c], bias)]

            def values(c=c, start=start):
                return [vc_ref[:, c].astype(BF16), v_ref[pl.ds(start, NA_KEYS), c]]

            units.append(_pair_unit(q_ref, rows, j, keys, values, None))
    for n, o in enumerate(_run_units(units)):
        rr, j = divmod(n, 2)
        o_ref[rr * GRID_W:(rr + 1) * GRID_W, j * LANES:(j + 1) * LANES] = o.astype(BF16)


def _lat_na(l, qa, ka, va, cache_k, cache_v, na_bias, o_buf):
    cspec = pl.BlockSpec((None, None, PAST_LEN, 256), lambda b, i: (b, l, 0, 0))
    bspec = _resident((None, 4, NA_ROWS, GRID_W, NA_KEYS), lambda b, i: (l, 0, 0, 0, 0))
    return _lat_call(_na_kernel, l, "lat_na",
                     [_lat_qspec(512), _lat_kvspec(256), _lat_kvspec(256), cspec, cspec, bspec],
                     (qa, ka, va, cache_k, cache_v, na_bias), o_buf, 0)


def _na_bias_kernel(rpb_ref, out_ref, t_ref):
    l = pl.program_id(0)
    n_dr, n_dc = 2 * NA_ROWS - 1, 2 * NA_COLS - 1
    shape = (GRID_W, LANES)
    qc = lax.broadcasted_iota(jnp.int32, shape, 0)
    lane = _lane_iota(shape)
    kc = lane & (GRID_W - 1)
    dc = jnp.clip(kc - qc, 1 - NA_COLS, NA_COLS - 1) + (NA_COLS - 1)
    c0 = jnp.clip(qc - NA_COLS // 2, 0, GRID_W - NA_COLS)
    ok = (kc >= c0) & (kc < c0 + NA_COLS)
    for hd in range(4):
        def fill(a, carry):
            base = ((l * 4 + hd) * n_dr + a) * n_dc
            t = jnp.zeros(shape, F32)
            for b in range(n_dc):
                t = jnp.where(dc == b, rpb_ref[base + b], t)
            t_ref[a] = jnp.where(ok, t, NEG_INF)
            return carry
        lax.fori_loop(0, n_dr, fill, 0)
        for d in range(NA_ROWS):
            for m in range(NA_ROWS // 2):
                lo = t_ref[2 * m - d + NA_ROWS - 1]
                hi = t_ref[2 * m + 1 - d + NA_ROWS - 1]
                out_ref[hd, d, :, m * LANES:(m + 1) * LANES] = jnp.where(lane < GRID_W, lo, hi)


def _na_bias(rpb_flat):
    return pl.pallas_call(
        _na_bias_kernel,
        grid=(DEPTH,),
        in_specs=[pl.BlockSpec(memory_space=pltpu.SMEM)],
        out_specs=pl.BlockSpec((None, 4, NA_ROWS, GRID_W, NA_KEYS), lambda l: (l, 0, 0, 0, 0)),
        out_shape=jax.ShapeDtypeStruct((DEPTH, 4, NA_ROWS, GRID_W, NA_KEYS), F32),
        scratch_shapes=[pltpu.VMEM((2 * NA_ROWS - 1, GRID_W, LANES), F32)],
        compiler_params=_cparams(("parallel",)),
        name="na_bias",
    )(rpb_flat)


def _mla_ctx_kernel(ckv_ref, kpe_ref, wukv_ref, g_ref, bd_ref, kd_o, vd_o):
    kv = _dot(ckv_ref[...].astype(BF16), wukv_ref[...])
    kpe = kpe_ref[...]
    for j in range(2):
        kf = jnp.concatenate([kv[:, (2 * j) * LANES:(2 * j + 1) * LANES] + kpe,
                              kv[:, (2 * j + 1) * LANES:(2 * j + 2) * LANES] + kpe], axis=1)
        kd_o[:, 2 * j * LANES:(2 * j + 2) * LANES] = _group_norm(
            kf, bd_ref[1], 1.0 / MLA_QK, g_ref[7:8, :]).astype(BF16)
    vd_o[...] = kv[:, 4 * LANES:].astype(BF16)


def _mla_ctx(cache_ckv, cache_kpe_p, w_ukv_p, gains, bd):
    return pl.pallas_call(
        _mla_ctx_kernel,
        grid=(DEPTH, DEC_BATCH),
        in_specs=[
            pl.BlockSpec((None, None, PAST_LEN, LANES), lambda l, b: (b, l, 0, 0)),
            pl.BlockSpec((None, None, PAST_LEN, LANES), lambda l, b: (b, l, 0, 0)),
            pl.BlockSpec((None, MLA_KV_LORA, 768), lambda l, b: (l, 0, 0)),
            pl.BlockSpec((None, 16, 256), lambda l, b: (l, 0, 0)),
            pl.BlockSpec((2, 256, 256), lambda l, b: (0, 0, 0)),
        ],
        out_specs=[pl.BlockSpec((None, None, PAST_LEN, 512), lambda l, b: (l, b, 0, 0)),
                   pl.BlockSpec((None, None, PAST_LEN, 256), lambda l, b: (l, b, 0, 0))],
        out_shape=[jax.ShapeDtypeStruct((DEPTH, DEC_BATCH, PAST_LEN, 512), BF16),
                   jax.ShapeDtypeStruct((DEPTH, DEC_BATCH, PAST_LEN, 256), BF16)],
        compiler_params=_cparams(("parallel", "parallel")),
        name="mla_ctx",
    )(cache_ckv, cache_kpe_p, w_ukv_p, gains, bd)


def _merge_kernel(x_ref, o_ref, mod_ref, n1_ref, wg_ref, wb_ref, wo_ref, x1_ref):
    x = x_ref[...]
    mod = mod_ref[...]
    h = _rms_mod(x, n1_ref[...], mod[:, D_MODEL:2 * D_MODEL], mod[:, 0:D_MODEL])
    hb = h.astype(BF16)
    merged = None
    for n in range(4):
        gate = _dot(hb, wg_ref[:, n * D_MODEL:(n + 1) * D_MODEL])
        gate = 1.0 / (1.0 + jnp.exp(-gate))
        br = _dot(o_ref[:, n * 256:(n + 1) * 256], wb_ref[n])
        merged = gate * br if merged is None else merged + gate * br
    out = _dot(merged.astype(BF16), wo_ref[...])
    x1_ref[...] = x + mod[:, 2 * D_MODEL:3 * D_MODEL] * out


def _merge(l, x_all, o_buf, mods, norm1, w_gate, w_branch, w_out):
    tok = pl.BlockSpec((TM, D_MODEL), lambda i: (i, 0))
    return pl.pallas_call(
        _merge_kernel,
        grid=(N_TILES,),
        in_specs=[
            tok, tok,
            pl.BlockSpec((None, None, 1, 6 * D_MODEL), lambda i: (l, _tile_group(i), 0, 0)),
            _resident((None, 1, D_MODEL), lambda i: (l, 0, 0)),
            _resident((None, D_MODEL, N_GATE), lambda i: (l, 0, 0)),
            _resident((None, 4, 256, D_MODEL), lambda i: (l, 0, 0, 0)),
            _resident((None, D_MODEL, D_MODEL), lambda i: (l, 0, 0)),
        ],
        out_specs=tok,
        out_shape=jax.ShapeDtypeStruct((N_TOK, D_MODEL), F32),
        compiler_params=_cparams(("parallel",)),
        name=f"merge_l{l}",
    )(x_all, o_buf, mods, norm1, w_gate, w_branch, w_out)


HALO = 8


def _ffn_kernel(x_ref, xp_ref, xn_ref, mod_ref, n2_ref, perm_ref, wup_ref, cw_ref, cb_ref, wdn_ref, *rest, split_out):
    u_ref = rest[-1]
    i = pl.program_id(0)
    p = (i - CTX_TILES) % LAT_TILES_PER_BATCH
    has_prev = (i >= CTX_TILES) & (p != 0)
    has_next = (i >= CTX_TILES) & (p != LAT_TILES_PER_BATCH - 1)
    x = x_ref[...]
    mod = mod_ref[...]
    n2 = n2_ref[...]
    sh2, sc2 = mod[:, 3 * D_MODEL:4 * D_MODEL], mod[:, 4 * D_MODEL:5 * D_MODEL]
    h_prev = jnp.where(has_prev, _rms_mod(xp_ref[...], n2, sc2, sh2), 0.0)
    h_next = jnp.where(has_next, _rms_mod(xn_ref[...], n2, sc2, sh2), 0.0)
    hb = _rms_mod(x, n2, sc2, sh2).astype(BF16)
    halo = jnp.concatenate([h_prev, h_next], axis=0).astype(BF16)
    lhs = jnp.concatenate([_dot(perm_ref[0], hb).astype(BF16), halo], axis=0)
    sub = lax.broadcasted_iota(jnp.int32, (HALO, FF_CHUNK), 0)

    def cols(c):
        return (slice(c * FF_CHUNK, (c + 1) * FF_CHUNK), slice(D_FF + c * FF_CHUNK, D_FF + (c + 1) * FF_CHUNK))

    def up(c):
        for half, cs in enumerate(cols(c)):
            lanes = slice(half * FF_CHUNK, (half + 1) * FF_CHUNK)
            u = _dot(lhs, wup_ref[:, cs])
            u_ref[c, HALO:HALO + TM, lanes] = u[:TM]
            prev_row, next_row = u[TM + HALO - 1:TM + HALO], u[TM + HALO:TM + HALO + 1]
            u_ref[c, 0:HALO, lanes] = jnp.where(sub == 0, prev_row, pltpu.roll(u[TM - HALO:TM], 1, 0))
            u_ref[c, HALO + TM:, lanes] = jnp.where(sub == HALO - 1, next_row, pltpu.roll(u[0:HALO], HALO - 1, 0))

    def both(ref, c):
        va, vg = cols(c)
        return jnp.concatenate([ref[:, va], ref[:, vg]], axis=1)

    acc = None
    up(0)
    up(1)
    for c in range(N_FF_CHUNKS):
        cw = both(cw_ref, c)
        t = (both(cb_ref, c) + u_ref[c, 0:TM, :] * cw[0:1] + u_ref[c, HALO:HALO + TM, :] * cw[1:2]
             + u_ref[c, 2 * HALO:2 * HALO + TM, :] * cw[2:3])
        a, g = t[:, :FF_CHUNK], t[:, FF_CHUNK:]
        act = ((g * (1.0 / (1.0 + jnp.exp(-g)))) * a).astype(BF16)
        act = _dot(perm_ref[1], act).astype(BF16)
        if c + 2 < N_FF_CHUNKS:
            up(c + 2)
        d = _dot(act, wdn_ref[c * FF_CHUNK:(c + 1) * FF_CHUNK, :])
        acc = d if acc is None else acc + d
    y = x + mod[:, 5 * D_MODEL:6 * D_MODEL] * acc
    if split_out:
        @pl.when(i < CTX_TILES)
        def _():
            rest[0][...] = y

        @pl.when(i >= CTX_TILES)
        def _():
            rest[1][...] = y
    else:
        rest[0][...] = y


def _row_permutations():
    r = np.arange(TM)
    token = r // HALO + (TM // HALO) * (r % HALO)
    p = (token[:, None] == np.arange(TM)[None, :]).astype(np.float32)
    return jnp.asarray(np.stack([p, p.T], axis=0), dtype=BF16)


def _ffn(l, x1, mods, norm2, perm, w_up, conv_w, conv_b, w_down, split_out):
    tok = pl.BlockSpec((TM, D_MODEL), lambda i: (i, 0))
    per = TM // HALO
    last = N_TOK // HALO - 1
    if split_out:
        out_specs = [pl.BlockSpec((TM, D_MODEL), lambda i: (jnp.minimum(i, CTX_TILES - 1), 0)),
                     pl.BlockSpec((TM, D_MODEL), lambda i: (jnp.maximum(i - CTX_TILES, 0), 0))]
        out_shape = [jax.ShapeDtypeStruct((N_CTX, D_MODEL), F32), jax.ShapeDtypeStruct((N_LAT, D_MODEL), F32)]
    else:
        out_specs, out_shape = tok, jax.ShapeDtypeStruct((N_TOK, D_MODEL), F32)
    return pl.pallas_call(
        functools.partial(_ffn_kernel, split_out=split_out),
        grid=(N_TILES,),
        in_specs=[
            tok,
            pl.BlockSpec((HALO, D_MODEL), lambda i: (jnp.maximum(i * per - 1, 0), 0)),
            pl.BlockSpec((HALO, D_MODEL), lambda i: (jnp.minimum((i + 1) * per, last), 0)),
            pl.BlockSpec((None, None, 1, 6 * D_MODEL), lambda i: (l, _tile_group(i), 0, 0)),
            _resident((None, 1, D_MODEL), lambda i: (l, 0, 0)),
            _resident((2, TM, TM), lambda i: (0, 0, 0)),
            _resident((None, D_MODEL, 2 * D_FF), lambda i: (l, 0, 0)),
            _resident((None, 3, 2 * D_FF), lambda i: (l, 0, 0)),
            _resident((None, 1, 2 * D_FF), lambda i: (l, 0, 0)),
            _resident((None, D_FF, D_MODEL), lambda i: (l, 0, 0)),
        ],
        out_specs=out_specs,
        out_shape=out_shape,
        scratch_shapes=[pltpu.VMEM((N_FF_CHUNKS, TM + 2 * HALO, 2 * FF_CHUNK), F32)],
        compiler_params=_cparams(("arbitrary",)),
        name=f"ffn_l{l}",
    )(x1, x1, x1, mods, norm2, perm, w_up, conv_w, conv_b, w_down)


def _cast_kernel(a_ref, o_ref):
    o_ref[...] = a_ref[...].astype(BF16)


def _cast_bf16(a, rows, name):
    _, r, c = a.shape
    spec = pl.BlockSpec((None, rows, c), lambda l, i: (l, i, 0))
    return pl.pallas_call(
        _cast_kernel, grid=(DEPTH, r // rows), in_specs=[spec], out_specs=spec,
        out_shape=jax.ShapeDtypeStruct(a.shape, BF16),
        compiler_params=_cparams(("parallel", "parallel")), name=name,
    )(a)


WPREP_COLS = 256
WPREP_QKV_BLOCKS = C_END // WPREP_COLS
WPREP_BLOCKS = WPREP_QKV_BLOCKS + N_GATE // WPREP_COLS
_SRC_QD = C_QD
_SRC_CKV = C_QD + 4 * MLA_QK


def _win_prep_start(j):
    return jnp.where(j < 7, WPREP_COLS * j,
                     jnp.where(j == 7, _SRC_QD,
                               jnp.where(j == 8, _SRC_QD + 2 * MLA_QK,
                                         jnp.where(j == 9, _SRC_CKV, IN_QKV + WPREP_COLS * (j - WPREP_QKV_BLOCKS)))))


def _win_prep_kernel(w_ref, q_ref, g_ref):
    j = pl.program_id(1)
    w = w_ref[0]
    zeros = lambda n: jnp.zeros((n, D_MODEL), F32)

    @pl.when(j >= WPREP_QKV_BLOCKS)
    def _():
        g_ref[...] = w.T.astype(BF16)

    @pl.when(j < 7)
    def _():
        q_ref[...] = w.T.astype(BF16)

    @pl.when((j == 7) | (j == 8))
    def _():
        pad = zeros(LANES - MLA_QK)
        sel = jnp.concatenate([w[0:MLA_QK], pad, w[MLA_QK:2 * MLA_QK], pad], axis=0)
        q_ref[...] = sel.T.astype(BF16)

    @pl.when(j == 9)
    def _():
        sel = jnp.concatenate([w[0:MLA_KV_LORA], zeros(MLA_NOPE), w[MLA_KV_LORA:MLA_KV_LORA + MLA_ROPE],
                               zeros(LANES - MLA_QK)], axis=0)
        q_ref[...] = sel.T.astype(BF16)


def _win_prep(w_in_t):
    window = pl.BlockSpec((pl.Element(1), pl.Element(WPREP_COLS), pl.Element(D_MODEL)),
                          lambda l, j: (l, pl.multiple_of(_win_prep_start(j), 32), 0))
    return pl.pallas_call(
        _win_prep_kernel,
        grid=(DEPTH, WPREP_BLOCKS),
        in_specs=[window],
        out_specs=[pl.BlockSpec((None, D_MODEL, WPREP_COLS), lambda l, j: (l, 0, jnp.minimum(j, WPREP_QKV_BLOCKS - 1))),
                   pl.BlockSpec((None, D_MODEL, WPREP_COLS), lambda l, j: (l, 0, jnp.maximum(j - WPREP_QKV_BLOCKS, 0)))],
        out_shape=[jax.ShapeDtypeStruct((DEPTH, D_MODEL, C_END), BF16),
                   jax.ShapeDtypeStruct((DEPTH, D_MODEL, N_GATE), BF16)],
        compiler_params=_cparams(("arbitrary", "arbitrary")),
        name="w_in_prep",
    )(w_in_t)


def _rope_tables():
    t = np.arange(DEC_SEQ)

    def angles(dim):
        half = dim // 2
        inv = ROPE_THETA ** (-np.arange(0, half, 2, dtype=np.float32) / half)
        row = (t // GRID_W).astype(np.float32)[:, None] * inv[None, :]
        col = (t % GRID_W).astype(np.float32)[:, None] * inv[None, :]
        return np.concatenate([row, row, col, col], axis=-1).astype(np.float32)

    def signed(ang, quarter):
        cos, sin = np.cos(ang), np.sin(ang)
        first = (np.arange(ang.shape[1]) % (2 * quarter)) < quarter
        return cos, np.where(first, -sin, 0.0), np.where(first, 0.0, sin)

    c64, lo64, hi64 = signed(angles(HEAD_DIM), 16)
    c64, lo64, hi64 = (np.tile(a, (1, 2)) for a in (c64, lo64, hi64))
    cm, lom, him = signed(angles(MLA_ROPE), 8)

    def mla_pad(a, fill):
        out = np.full((DEC_SEQ, LANES), fill, np.float32)
        out[:, MLA_NOPE:MLA_QK] = a
        return out

    tabs = [c64, lo64, hi64, mla_pad(cm, 1.0), mla_pad(lom, 0.0), mla_pad(him, 0.0)]
    ident = [np.ones, np.zeros, np.zeros, np.ones, np.zeros, np.zeros]
    full = [np.concatenate([f((TM, LANES), np.float32), a.astype(np.float32)], axis=0) for f, a in zip(ident, tabs)]
    return jnp.asarray(np.stack(full, axis=0))


def _block_diag_ones():
    idx = np.arange(256)
    mats = [(idx[:, None] // w == idx[None, :] // w).astype(np.float32) for w in (HEAD_DIM, LANES)]
    return jnp.asarray(np.stack(mats, axis=0), dtype=BF16)


def _prep_small(w_ukv, qn_a, kn_a, qn_b, kn_b, qn_c, kn_c, qn_d, kn_d, kvn_d):
    ukv = w_ukv.reshape(DEPTH, MLA_KV_LORA, 4, 2, HEAD_DIM)
    k_part = jnp.pad(ukv[:, :, :, 0], ((0, 0), (0, 0), (0, 0), (0, HEAD_DIM))).reshape(DEPTH, MLA_KV_LORA, 512)
    v_part = ukv[:, :, :, 1].reshape(DEPTH, MLA_KV_LORA, 256)
    w_ukv_p = jnp.concatenate([k_part, v_part], axis=-1).astype(BF16)

    def row(g, reps, width=256):
        t = jnp.tile(g, (1, reps))
        return jnp.pad(t, ((0, 0), (0, width - t.shape[1])))

    pad96 = lambda g: jnp.pad(g, ((0, 0), (0, LANES - MLA_QK)))
    rows = [row(qn_a, 4), row(kn_a, 4), row(qn_b, 4), row(kn_b, 2), row(qn_c, 4), row(kn_c, 2),
            row(pad96(qn_d), 2), row(pad96(kn_d), 2), row(kvn_d, 1)]
    gains = jnp.stack(rows + [jnp.zeros_like(rows[0])] * (16 - len(rows)), axis=1)
    return w_ukv_p, gains


def kernel(x_prompt, x_sample, cache_nat_k, cache_nat_v, cache_gqa_k, cache_gqa_v, cache_win_k, cache_win_v,
           cache_mla_ckv, cache_mla_kpe, c, c_ctx, w_ada, b_ada, norm1, norm2, w_in, qn_a, kn_a, rpb_a,
           qn_b, kn_b, qn_c, kn_c, sink_c, qn_d, kn_d, kvn_d, w_ukv, w_branch, w_out, w_up, conv_w, conv_b,
           w_down):
    w_ukv_p, gains = _prep_small(w_ukv, qn_a, kn_a, qn_b, kn_b, qn_c, kn_c, qn_d, kn_d, kvn_d)
    w_qkv, w_gate = _win_prep(jnp.swapaxes(w_in, 1, 2))
    perm = _row_permutations()
    w_branch_b = _cast_bf16(w_branch.reshape(DEPTH, 4 * 256, D_MODEL), 1024, "cast_w_branch").reshape(
        DEPTH, 4, 256, D_MODEL)
    w_out_b = _cast_bf16(w_out, 1024, "cast_w_out")
    w_up_b = _cast_bf16(w_up, 256, "cast_w_up")
    w_down_b = _cast_bf16(w_down, D_FF // 4, "cast_w_down")
    conv_b_r = conv_b.reshape(DEPTH, 1, 2 * D_FF)
    rope_tab = _rope_tables()
    bd = _block_diag_ones()
    norm1_r = norm1.reshape(DEPTH, 1, D_MODEL)
    norm2_r = norm2.reshape(DEPTH, 1, D_MODEL)

    cond8 = jnp.zeros((8, D_MODEL), F32).at[0].set(c_ctx).at[1:1 + DEC_BATCH].set(c)
    mods = _adaln(cond8, w_ada, b_ada).reshape(DEPTH, 8, 1, 6 * D_MODEL)

    rpb_flat = jnp.pad(rpb_a.reshape(-1), (0, 8192 - rpb_a.size))
    na_bias = _na_bias(rpb_flat)
    kpe_p = jnp.pad(cache_mla_kpe, ((0, 0), (0, 0), (0, 0), (MLA_NOPE, LANES - MLA_QK)))
    kd_ctx, vd_ctx = _mla_ctx(cache_mla_ckv, kpe_p, w_ukv_p, gains, bd)
    c_nat_k = cache_nat_k.reshape(DEC_BATCH, DEPTH, PAST_LEN, 256)
    c_nat_v = cache_nat_v.reshape(DEC_BATCH, DEPTH, PAST_LEN, 256)
    c_gqa_k = cache_gqa_k.reshape(DEC_BATCH, DEPTH, PAST_LEN, LANES)
    c_gqa_v = cache_gqa_v.reshape(DEC_BATCH, DEPTH, PAST_LEN, LANES)
    c_win_k = cache_win_k.reshape(DEC_BATCH, DEPTH, PAST_LEN, LANES)
    c_win_v = cache_win_v.reshape(DEC_BATCH, DEPTH, PAST_LEN, LANES)

    x = jnp.concatenate([x_prompt.reshape(N_CTX, D_MODEL), x_sample.reshape(N_LAT, D_MODEL)], axis=0)
    caches = None
    for l in range(DEPTH):
        qkv, caches = _inproj(l, x, mods, norm1_r, w_qkv, w_ukv_p, gains, bd, rope_tab, caches)
        qa, ka, va, qb, kb, vb, qc, kc, vc, qd, kd, vd = qkv
        sink_l = sink_c[l]
        o = _ctx_attention(l, sink_l, qkv)
        o = _lat_na(l, qa, ka, va, c_nat_k, c_nat_v, na_bias, o)
        o = _lat_gqa(l, qb, kb, vb, c_gqa_k, c_gqa_v, o)
        o = _lat_window(l, sink_l, qc, kc, vc, c_win_k, c_win_v, o)
        o = _lat_mla(l, qd, kd, vd, kd_ctx, vd_ctx, o)
        x1 = _merge(l, x, o, mods, norm1_r, w_gate, w_branch_b, w_out_b)
        x = _ffn(l, x1, mods, norm2_r, perm, w_up_b, conv_w, conv_b_r, w_down_b, split_out=(l == DEPTH - 1))
    y_prompt = x[0].reshape(BATCH, SEQ, D_MODEL)
    y_sample = x[1].reshape(DEC_BATCH, DEC_SEQ, D_MODEL)
    heads = (4, 4, 2, 2, 2, 2)
    outs = [a.reshape(BATCH, DEPTH, h, HEAD_DIM, SEQ).transpose(0, 1, 4, 2, 3) for a, h in zip(caches[:6], heads)]
    return (y_prompt, y_sample, *outs, jnp.swapaxes(caches[6], 2, 3), jnp.swapaxes(caches[7], 2, 3))
```

```python
import functools

import jax
import jax.numpy as jnp
import numpy as np
from jax import lax
from jax.experimental import pallas as pl
from jax.experimental.pallas import tpu as pltpu

F32 = jnp.float32
BF16 = jnp.bfloat16

D_MODEL = 1024
BATCH = 16
SEQ = 256
DEPTH = 4
DEC_BATCH = 2
DEC_SEQ = 2048
PAST_LEN = 256
GRID_W = 64
HEAD_DIM = 64
NA_ROWS = 8
NA_COLS = 16
WINDOW = 128
MLA_NOPE = 64
MLA_ROPE = 32
MLA_QK = MLA_NOPE + MLA_ROPE
MLA_KV_LORA = 128
D_FF = 2816
ROPE_THETA = 10000.0
EPS = 1e-6
NEG_INF = -1e30

LANES = 128
TM = 256
N_CTX = BATCH * SEQ
N_LAT = DEC_BATCH * DEC_SEQ
N_TOK = N_CTX + N_LAT
CTX_TILES = N_CTX // TM
LAT_TILES_PER_BATCH = DEC_SEQ // TM
N_TILES = N_TOK // TM
GRID_ROWS = DEC_SEQ // GRID_W
FF_CHUNK = 256
N_FF_CHUNKS = D_FF // FF_CHUNK
VMEM_LIMIT = 56 * 1024 * 1024

C_QA, C_KA, C_VA = 0, 256, 512
C_QB, C_KB, C_VB = 768, 1024, 1152
C_QC, C_KC, C_VC = 1280, 1536, 1664
C_QD, C_CKV, C_KPE, C_END = 1792, 2304, 2432, 2560
N_GATE = 4 * D_MODEL
IN_QKV = 2336
Q_SCALE = HEAD_DIM ** -0.5


def _cparams(sem):
    return pltpu.CompilerParams(dimension_semantics=sem, vmem_limit_bytes=VMEM_LIMIT)


def _tile_group(i):
    return jnp.where(i < CTX_TILES, 0, 1 + (i - CTX_TILES) // LAT_TILES_PER_BATCH)


def _tile_pos_block(i):
    return jnp.where(i < CTX_TILES, 0, 1 + (i - CTX_TILES) % LAT_TILES_PER_BATCH)


def _resident(shape, index_map):
    return pl.BlockSpec(shape, index_map, pipeline_mode=pl.Buffered(1))


def _dot(a, b):
    return jnp.dot(a, b, preferred_element_type=F32)


def _dot_t(a, b):
    return lax.dot_general(a, b, (((1,), (1,)), ((), ())), preferred_element_type=F32)


def _rms_mod(x, gain, scale, shift):
    ms = jnp.mean(x * x, axis=-1, keepdims=True)
    return (x * lax.rsqrt(ms + EPS)) * gain * (1.0 + scale) + shift


def _group_norm(t, ones_bd, inv_n, gain):
    ss = _dot((t * t).astype(BF16), ones_bd)
    return t * lax.rsqrt(ss * inv_n + EPS) * gain


def _rope(t, cos, sin_lo, sin_hi, half):
    return (t * cos + pltpu.roll(t, LANES - half, 1) * sin_lo + pltpu.roll(t, half, 1) * sin_hi)


def _lane_iota(shape):
    return lax.broadcasted_iota(jnp.int32, shape, len(shape) - 1)


def _low_half(shape):
    return _lane_iota(shape) < HEAD_DIM


def _run_units(units, depth=2):
    n = len(units)
    pending = {}
    outs = []
    for k in range(min(depth, n)):
        pending[k] = units[k][0]()
    for k in range(n):
        outs.append(units[k][1](pending.pop(k)))
        if k + depth < n:
            pending[k + depth] = units[k + depth][0]()
    return outs


N_COND = 1 + DEC_BATCH


def _adaln_kernel(cond_ref, w_ref, b_ref, out_ref):
    cnd = cond_ref[...]
    s_t = (cnd * (1.0 / (1.0 + jnp.exp(-cnd)))).T
    w = w_ref[...]
    out_ref[...] = jnp.zeros(out_ref.shape, F32)
    for r in range(N_COND):
        out_ref[r:r + 1, :] = jnp.sum(s_t[:, r:r + 1] * w, axis=0, keepdims=True) + b_ref[...]


def _adaln(cond8, w_ada, b_ada):
    n_col = 6 * D_MODEL
    tn = 1024
    return pl.pallas_call(
        _adaln_kernel,
        grid=(DEPTH, n_col // tn),
        in_specs=[
            pl.BlockSpec((8, D_MODEL), lambda l, n: (0, 0)),
            pl.BlockSpec((None, D_MODEL, tn), lambda l, n: (l, 0, n)),
            pl.BlockSpec((None, 1, tn), lambda l, n: (l, 0, n)),
        ],
        out_specs=pl.BlockSpec((None, 8, tn), lambda l, n: (l, 0, n)),
        out_shape=jax.ShapeDtypeStruct((DEPTH, 8, n_col), F32),
        compiler_params=_cparams(("parallel", "parallel")),
        name="adaln",
    )(cond8, w_ada, b_ada.reshape(DEPTH, 1, n_col))


_QKV_WIDTHS = (512, 256, 256, 512, 128, 128, 512, 128, 128, 512, 512, 256)
_CACHE_WIDTHS = (256, 256, 128, 128, 128, 128, MLA_KV_LORA, MLA_ROPE)
_CACHE_BLOCKS = tuple((w, SEQ) for w in _CACHE_WIDTHS)


def _inproj_kernel(xc_ref, xl_ref, mod_ref, n1_ref, w_ref, wukv_ref, g_ref, bd_ref, rope_ref, *rest):
    outs = rest[-(len(_QKV_WIDTHS) + len(_CACHE_WIDTHS)):]
    qa_o, ka_o, va_o, qb_o, kb_o, vb_o, qc_o, kc_o, vc_o, qd_o, kd_o, vd_o = outs[:len(_QKV_WIDTHS)]
    natk_o, natv_o, gqak_o, gqav_o, wink_o, winv_o, ckv_o, kpe_o = outs[len(_QKV_WIDTHS):]
    x = jnp.where(pl.program_id(0) >= N_TILES - CTX_TILES, xc_ref[...], xl_ref[...])
    mod = mod_ref[...]
    hb = _rms_mod(x, n1_ref[...], mod[:, D_MODEL:2 * D_MODEL], mod[:, 0:D_MODEL]).astype(BF16)
    bd64, bd128 = bd_ref[0], bd_ref[1]
    cos64, slo64, shi64 = rope_ref[0], rope_ref[1], rope_ref[2]
    cosm, slom, shim = rope_ref[3], rope_ref[4], rope_ref[5]
    inv64, inv96 = 1.0 / HEAD_DIM, 1.0 / MLA_QK
    low = _low_half((TM, LANES))
    held = {}

    def chunks(t):
        return [t[:, c:c + LANES] for c in range(0, t.shape[1], LANES)]

    def rope64(t):
        parts = [_rope(p, cos64, slo64, shi64, 16) for p in chunks(t)]
        return parts[0] if len(parts) == 1 else jnp.concatenate(parts, axis=1)

    def ropem(t):
        return jnp.concatenate([_rope(p, cosm, slom, shim, 8) for p in chunks(t)], axis=1)

    def store_q(ref, q, grouped):
        q = q * Q_SCALE
        slots = []
        for hd in range(4):
            part, src = q[:, (hd // 2) * LANES:(hd // 2 + 1) * LANES], hd % 2
            dst = hd // 2 if grouped else src
            if src != dst:
                part = pltpu.roll(part, HEAD_DIM, 1)
            slots.append(jnp.where(low if dst == 0 else ~low, part, 0.0))
        ref[...] = jnp.concatenate(slots, axis=1).astype(BF16)

    def post_latent(t):
        ckv = _group_norm(t[:, :LANES], bd128[:LANES, :LANES], 1.0 / MLA_KV_LORA, g_ref[8:9, :LANES])
        kpe = t[:, LANES:]
        ckv_o[...] = ckv.T
        kpe_o[...] = kpe.T[MLA_NOPE:MLA_QK, :]
        held["kpe"] = kpe
        held["kv"] = _dot(ckv.astype(BF16), wukv_ref[...])

    def post_kd(_):
        kv, kpe = held["kv"], held["kpe"]
        for j in range(2):
            kf = jnp.concatenate([kv[:, (2 * j) * LANES:(2 * j + 1) * LANES] + kpe,
                                  kv[:, (2 * j + 1) * LANES:(2 * j + 2) * LANES] + kpe], axis=1)
            kd_o[:, 2 * j * LANES:(2 * j + 2) * LANES] = ropem(
                _group_norm(kf, bd128, inv96, g_ref[7:8, :])).astype(BF16)
        vd_o[...] = kv[:, 4 * LANES:].astype(BF16)

    def post_qa(t):
        store_q(qa_o, _group_norm(t, bd64, inv64, g_ref[0:1, :]), grouped=False)

    def post_ka(t):
        k = _group_norm(t, bd64, inv64, g_ref[1:2, :])
        natk_o[...] = k.T
        ka_o[...] = k.astype(BF16)

    def post_va(t):
        natv_o[...] = t.T
        va_o[...] = t.astype(BF16)

    def post_q_rope(ref, row):
        def post(t):
            store_q(ref, rope64(_group_norm(t, bd64, inv64, g_ref[row:row + 1, :])), grouped=True)
        return post

    def post_kv(k_ref, v_ref, kc_ref, vc_ref, row):
        def post(t):
            k = _group_norm(t[:, :LANES], bd64[:LANES, :LANES], inv64, g_ref[row:row + 1, :LANES])
            kc_ref[...] = k.T
            k_ref[...] = rope64(k).astype(BF16)
            v = t[:, LANES:]
            vc_ref[...] = v.T
            v_ref[...] = v.astype(BF16)
        return post

    def post_qd(j):
        def post(t):
            qd_o[:, 2 * j * LANES:(2 * j + 2) * LANES] = ropem(
                _group_norm(t, bd128, inv96, g_ref[6:7, :])).astype(BF16)
        return post

    stages = [
        ((C_CKV, C_END), post_latent),
        ((C_QA, C_KA), post_qa),
        (None, post_kd),
        ((C_QB, C_KB), post_q_rope(qb_o, 2)),
        ((C_QC, C_KC), post_q_rope(qc_o, 4)),
        ((C_QD, C_QD + 2 * LANES), post_qd(0)),
        ((C_QD + 2 * LANES, C_CKV), post_qd(1)),
        ((C_KB, C_QC), post_kv(kb_o, vb_o, gqak_o, gqav_o, 3)),
        ((C_KC, C_QD), post_kv(kc_o, vc_o, wink_o, winv_o, 5)),
        ((C_KA, C_VA), post_ka),
        ((C_VA, C_QB), post_va),
    ]

    def issue(cols):
        return None if cols is None else _dot(hb, w_ref[:, cols[0]:cols[1]])

    _run_units([(functools.partial(issue, cols), post) for cols, post in stages], depth=2)


def _ctx_block(i):
    return jnp.minimum(i, CTX_TILES - 1)


def _lat_block(i):
    return jnp.maximum(i - CTX_TILES, 0)


def _inproj(l, x_ctx, x_lat, mods, norm1, w_qkv, w_ukv_p, gains, bd, rope_tab, caches):
    tile = lambda s: (s + CTX_TILES) % N_TILES
    tok = lambda w: pl.BlockSpec((TM, w), lambda s: (tile(s), 0))
    in_specs = [
        pl.BlockSpec((TM, D_MODEL), lambda s: (_ctx_block(tile(s)), 0)),
        pl.BlockSpec((TM, D_MODEL), lambda s: (_lat_block(tile(s)), 0)),
        pl.BlockSpec((None, None, 1, 6 * D_MODEL), lambda s: (l, _tile_group(tile(s)), 0, 0)),
        _resident((None, 1, D_MODEL), lambda s: (l, 0, 0)),
        _resident((None, D_MODEL, C_END), lambda s: (l, 0, 0)),
        _resident((None, MLA_KV_LORA, 768), lambda s: (l, 0, 0)),
        _resident((None, 16, 256), lambda s: (l, 0, 0)),
        _resident((2, 256, 256), lambda s: (0, 0, 0)),
        pl.BlockSpec((6, TM, LANES), lambda s: (0, _tile_pos_block(tile(s)), 0)),
    ]
    args = [x_ctx, x_lat, mods, norm1, w_qkv, w_ukv_p, gains, bd, rope_tab] + list(caches)
    aliases = {len(in_specs) + n: len(_QKV_WIDTHS) + n for n in range(len(caches))}
    in_specs = in_specs + [pl.BlockSpec(memory_space=pl.ANY)] * len(caches)
    cache_spec = lambda blk: pl.BlockSpec((None, None) + blk, lambda s: (jnp.maximum(s - CTX_TILES, 0), l, 0, 0))
    outs = pl.pallas_call(
        _inproj_kernel,
        grid=(N_TILES,),
        in_specs=in_specs,
        out_specs=[tok(w) for w in _QKV_WIDTHS] + [cache_spec(blk) for blk in _CACHE_BLOCKS],
        out_shape=[jax.ShapeDtypeStruct((N_TOK, w), BF16) for w in _QKV_WIDTHS]
        + [jax.ShapeDtypeStruct((BATCH, DEPTH) + blk, F32) for blk in _CACHE_BLOCKS],
        input_output_aliases=aliases,
        compiler_params=_cparams(("arbitrary",)),
        name=f"inproj_l{l}",
    )(*args)
    return outs[:len(_QKV_WIDTHS)], outs[len(_QKV_WIDTHS):]


def _scores(q, keys, scale=None):
    out = []
    for k, bias in keys:
        s = _dot_t(q, k)
        if scale is not None:
            s = s * scale
        if bias is not None:
            s = s + bias
        out.append(s)
    return out


def _softmax_pv(scores, values, sink=None):
    m = functools.reduce(jnp.maximum, [jnp.max(s, axis=-1, keepdims=True) for s in scores])
    if sink is not None:
        m = jnp.maximum(m, sink)
    denom = None
    out = None
    for s, v in zip(scores, values):
        e = jnp.exp(s - m)
        d = jnp.sum(e, axis=-1, keepdims=True)
        o = _dot(e.astype(BF16), v)
        denom = d if denom is None else denom + d
        out = o if out is None else out + o
    if sink is not None:
        denom = denom + jnp.exp(sink - m)
    return out / denom


def _pair_unit(q_ref, rows, j, keys, values, kv_half, sinks=None):
    m_rows = rows.stop - rows.start

    def issue():
        q2 = jnp.concatenate([q_ref[rows, (2 * j) * LANES:(2 * j + 1) * LANES],
                              q_ref[rows, (2 * j + 1) * LANES:(2 * j + 2) * LANES]], axis=0)
        return _scores(q2, keys())

    def finish(scores):
        sink = None
        if sinks is not None:
            row = lax.broadcasted_iota(jnp.int32, (2 * m_rows, 1), 0)
            sink = jnp.where(row < m_rows, sinks[0], sinks[1])
        o2 = _softmax_pv(scores, values(), sink)
        lo, hi = o2[:m_rows], o2[m_rows:]
        if kv_half == 0:
            hi = pltpu.roll(hi, HEAD_DIM, 1)
        elif kv_half == 1:
            lo = pltpu.roll(lo, HEAD_DIM, 1)
        return jnp.where(_low_half(lo.shape), lo, hi)

    return issue, finish


def _single_unit(q_ref, rows, hd, keys, values, scale):
    def issue():
        return _scores(q_ref[rows, hd * LANES:(hd + 1) * LANES], keys(), scale)

    def finish(scores):
        return _softmax_pv(scores, values())

    return issue, finish


def _merge_halves(o_even, o_odd):
    return jnp.where(_low_half(o_even.shape), o_even, o_odd)


def _ctx_attn_kernel(sink_ref, qa, ka, va, qb, kb, vb, qc, kc, vc, qd, kd, vd, o_ref):
    rows = slice(0, SEQ)
    units = []
    for j in range(2):
        c = slice(j * LANES, (j + 1) * LANES)
        units.append(_pair_unit(qa, rows, j, lambda c=c: [(ka[:, c], None)], lambda c=c: [va[:, c]], None))
    for q, k, v, with_sink in ((qb, kb, vb, False), (qc, kc, vc, True)):
        for j in range(2):
            sinks = (sink_ref[2 * j], sink_ref[2 * j + 1]) if with_sink else None
            units.append(_pair_unit(q, rows, j, lambda k=k: [(k[...], None)], lambda v=v: [v[...]], j, sinks))
    for hd in range(4):
        c = slice(hd * LANES, (hd + 1) * LANES)
        vs = slice((hd // 2) * LANES, (hd // 2 + 1) * LANES)
        units.append(_single_unit(qd, rows, hd, lambda c=c: [(kd[:, c], None)], lambda vs=vs: [vd[:, vs]],
                                  MLA_QK ** -0.5))
    outs = _run_units(units)
    for n in range(6):
        o_ref[:, n * LANES:(n + 1) * LANES] = outs[n].astype(BF16)
    for j in range(2):
        o_ref[:, (6 + j) * LANES:(7 + j) * LANES] = _merge_halves(outs[6 + 2 * j], outs[7 + 2 * j]).astype(BF16)


def _ctx_attention(l, sink_c, qkv):
    blk = lambda w: pl.BlockSpec((SEQ, w), lambda b: (b, 0))
    return pl.pallas_call(
        _ctx_attn_kernel,
        grid=(BATCH,),
        in_specs=[pl.BlockSpec(memory_space=pltpu.SMEM)] + [blk(w) for w in _QKV_WIDTHS],
        out_specs=pl.BlockSpec((SEQ, D_MODEL), lambda b: (b, 0)),
        out_shape=jax.ShapeDtypeStruct((N_CTX, D_MODEL), BF16),
        compiler_params=_cparams(("parallel",)),
        name=f"ctx_attn_l{l}",
    )(sink_c, *qkv)


LAT_TQ = 256
LAT_SUB = 128
LAT_QBLOCKS = DEC_SEQ // LAT_TQ
LAT_ROW0 = N_CTX // LAT_TQ
LAT_KV0 = N_CTX // DEC_SEQ


def _lat_qspec(width):
    return pl.BlockSpec((LAT_TQ, width), lambda b, i: (LAT_ROW0 + b * LAT_QBLOCKS + i, 0))


def _lat_kvspec(width):
    return pl.BlockSpec((DEC_SEQ, width), lambda b, i: (LAT_KV0 + b, 0))


def _lat_call(kernel, l, name, in_specs, args):
    return pl.pallas_call(
        kernel,
        grid=(DEC_BATCH, LAT_QBLOCKS),
        in_specs=in_specs,
        out_specs=pl.BlockSpec((LAT_TQ, 256), lambda b, i: (b * LAT_QBLOCKS + i, 0)),
        out_shape=jax.ShapeDtypeStruct((N_LAT, 256), BF16),
        compiler_params=_cparams(("parallel", "parallel")),
        name=f"{name}_l{l}",
    )(*args)


def _sub_rows(u):
    return slice(u * LAT_SUB, (u + 1) * LAT_SUB)


def _store_units(o_ref, outs):
    for n, o in enumerate(outs):
        u, j = divmod(n, 2)
        o_ref[_sub_rows(u), j * LANES:(j + 1) * LANES] = o.astype(BF16)


def _gqa_kernel(q_ref, k_ref, v_ref, kc_ref, vc_ref, o_ref):
    keys = lambda: [(kc_ref[...].astype(BF16), None), (k_ref[...], None)]
    values = lambda: [vc_ref[...].astype(BF16), v_ref[...]]
    units = [_pair_unit(q_ref, _sub_rows(u), j, keys, values, j) for u in range(LAT_TQ // LAT_SUB) for j in range(2)]
    _store_units(o_ref, _run_units(units))


def _lat_gqa(l, qb, kb, vb, cache_k, cache_v):
    cspec = pl.BlockSpec((None, None, PAST_LEN, LANES), lambda b, i: (b, l, 0, 0))
    return _lat_call(_gqa_kernel, l, "lat_gqa",
                     [_lat_qspec(512), _lat_kvspec(LANES), _lat_kvspec(LANES), cspec, cspec],
                     (qb, kb, vb, cache_k, cache_v))


WIN_KEYS = 3 * LAT_SUB


def _win_kernel(sink_ref, q_ref, k_ref, v_ref, kc_ref, vc_ref, o_ref):
    units = []
    for u in range(LAT_TQ // LAT_SUB):
        blk = pl.program_id(1) * (LAT_TQ // LAT_SUB) + u
        start = pl.multiple_of(jnp.clip((blk - 1) * LAT_SUB, 0, DEC_SEQ - WIN_KEYS), LAT_SUB)

        def keys(blk=blk, start=start):
            row = lax.broadcasted_iota(jnp.int32, (2 * LAT_SUB, WIN_KEYS), 0)
            qpos = blk * LAT_SUB + jnp.where(row < LAT_SUB, row, row - LAT_SUB)
            kpos = start + lax.broadcasted_iota(jnp.int32, (2 * LAT_SUB, WIN_KEYS), 1)
            bias = jnp.where(jnp.abs(qpos - kpos) <= WINDOW, 0.0, NEG_INF)
            return [(kc_ref[...].astype(BF16), None), (k_ref[pl.ds(start, WIN_KEYS), :], bias)]

        def values(start=start):
            return [vc_ref[...].astype(BF16), v_ref[pl.ds(start, WIN_KEYS), :]]

        for j in range(2):
            units.append(_pair_unit(q_ref, _sub_rows(u), j, keys, values, j, (sink_ref[2 * j], sink_ref[2 * j + 1])))
    _store_units(o_ref, _run_units(units))


def _lat_window(l, sink_c, qc, kc, vc, cache_k, cache_v):
    cspec = pl.BlockSpec((None, None, PAST_LEN, LANES), lambda b, i: (b, l, 0, 0))
    return _lat_call(_win_kernel, l, "lat_win",
                     [pl.BlockSpec(memory_space=pltpu.SMEM), _lat_qspec(512), _lat_kvspec(LANES),
                      _lat_kvspec(LANES), cspec, cspec],
                     (sink_c, qc, kc, vc, cache_k, cache_v))


def _mla_kernel(q_ref, k_ref, v_ref, kc_ref, vc_ref, o_ref):
    units = []
    for u in range(LAT_TQ // LAT_SUB):
        for hd in range(4):
            c = slice(hd * LANES, (hd + 1) * LANES)
            vs = slice((hd // 2) * LANES, (hd // 2 + 1) * LANES)
            units.append(_single_unit(q_ref, _sub_rows(u), hd,
                                      lambda c=c: [(kc_ref[:, c], None), (k_ref[:, c], None)],
                                      lambda vs=vs: [vc_ref[:, vs], v_ref[:, vs]], MLA_QK ** -0.5))
    outs = _run_units(units)
    _store_units(o_ref, [_merge_halves(outs[2 * n], outs[2 * n + 1]) for n in range(len(outs) // 2)])


def _lat_mla(l, qd, kd, vd, kd_ctx, vd_ctx):
    kcs = pl.BlockSpec((None, None, PAST_LEN, 512), lambda b, i: (l, b, 0, 0))
    vcs = pl.BlockSpec((None, None, PAST_LEN, 256), lambda b, i: (l, b, 0, 0))
    return _lat_call(_mla_kernel, l, "lat_mla",
                     [_lat_qspec(512), _lat_kvspec(512), _lat_kvspec(256), kcs, vcs],
                     (qd, kd, vd, kd_ctx, vd_ctx))


NA_KEYS = NA_ROWS * GRID_W
NA_ROWS_PER_STEP = LAT_TQ // GRID_W


def _na_shift(r):
    return jnp.where(r < NA_ROWS // 2, r,
                     jnp.where(r > GRID_ROWS - NA_ROWS // 2, r - (GRID_ROWS - NA_ROWS), NA_ROWS // 2))


def _na_kernel(q_ref, k_ref, v_ref, kc_ref, vc_ref, bias_ref, o_ref):
    units = []
    for rr in range(NA_ROWS_PER_STEP):
        r = pl.program_id(1) * NA_ROWS_PER_STEP + rr
        start = pl.multiple_of(jnp.clip(r - NA_ROWS // 2, 0, GRID_ROWS - NA_ROWS) * GRID_W, GRID_W)
        shift = _na_shift(r)
        rows = slice(rr * GRID_W, (rr + 1) * GRID_W)
        for j in range(2):
            c = slice(j * LANES, (j + 1) * LANES)

            def keys(j=j, c=c, start=start, shift=shift):
                bias = jnp.concatenate([bias_ref[2 * j, shift], bias_ref[2 * j + 1, shift]], axis=0)
                return [(kc_ref[:, c].astype(BF16), None), (k_ref[pl.ds(start, NA_KEYS), c], bias)]

            def values(c=c, start=start):
                return [vc_ref[:, c].astype(BF16), v_ref[pl.ds(start, NA_KEYS), c]]

            units.append(_pair_unit(q_ref, rows, j, keys, values, None))
    for n, o in enumerate(_run_units(units)):
        rr, j = divmod(n, 2)
        o_ref[rr * GRID_W:(rr + 1) * GRID_W, j * LANES:(j + 1) * LANES] = o.astype(BF16)


def _lat_na(l, qa, ka, va, cache_k, cache_v, na_bias):
    cspec = pl.BlockSpec((None, None, PAST_LEN, 256), lambda b, i: (b, l, 0, 0))
    bspec = _resident((None, 4, NA_ROWS, GRID_W, NA_KEYS), lambda b, i: (l, 0, 0, 0, 0))
    return _lat_call(_na_kernel, l, "lat_na",
                     [_lat_qspec(512), _lat_kvspec(256), _lat_kvspec(256), cspec, cspec, bspec],
                     (qa, ka, va, cache_k, cache_v, na_bias))


def _na_bias_kernel(rpb_ref, out_ref, t_ref):
    l = pl.program_id(0)
    n_dr, n_dc = 2 * NA_ROWS - 1, 2 * NA_COLS - 1
    shape = (GRID_W, LANES)
    qc = lax.broadcasted_iota(jnp.int32, shape, 0)
    lane = _lane_iota(shape)
    kc = lane & (GRID_W - 1)
    dc = jnp.clip(kc - qc, 1 - NA_COLS, NA_COLS - 1) + (NA_COLS - 1)
    c0 = jnp.clip(qc - NA_COLS // 2, 0, GRID_W - NA_COLS)
    ok = (kc >= c0) & (kc < c0 + NA_COLS)
    for hd in range(4):
        def fill(a, carry):
            base = ((l * 4 + hd) * n_dr + a) * n_dc
            t = jnp.zeros(shape, F32)
            for b in range(n_dc):
                t = jnp.where(dc == b, rpb_ref[base + b], t)
            t_ref[a] = jnp.where(ok, t, NEG_INF)
            return carry
        lax.fori_loop(0, n_dr, fill, 0)
        for d in range(NA_ROWS):
            for m in range(NA_ROWS // 2):
                lo = t_ref[2 * m - d + NA_ROWS - 1]
                hi = t_ref[2 * m + 1 - d + NA_ROWS - 1]
                out_ref[hd, d, :, m * LANES:(m + 1) * LANES] = jnp.where(lane < GRID_W, lo, hi)


def _na_bias(rpb_flat):
    return pl.pallas_call(
        _na_bias_kernel,
        grid=(DEPTH,),
        in_specs=[pl.BlockSpec(memory_space=pltpu.SMEM)],
        out_specs=pl.BlockSpec((None, 4, NA_ROWS, GRID_W, NA_KEYS), lambda l: (l, 0, 0, 0, 0)),
        out_shape=jax.ShapeDtypeStruct((DEPTH, 4, NA_ROWS, GRID_W, NA_KEYS), F32),
        scratch_shapes=[pltpu.VMEM((2 * NA_ROWS - 1, GRID_W, LANES), F32)],
        compiler_params=_cparams(("parallel",)),
        name="na_bias",
    )(rpb_flat)


def _mla_ctx_kernel(ckv_ref, kpe_ref, wukv_ref, g_ref, bd_ref, kd_o, vd_o):
    kv = _dot(ckv_ref[...].astype(BF16), wukv_ref[...])
    kpe = kpe_ref[...]
    for j in range(2):
        kf = jnp.concatenate([kv[:, (2 * j) * LANES:(2 * j + 1) * LANES] + kpe,
                              kv[:, (2 * j + 1) * LANES:(2 * j + 2) * LANES] + kpe], axis=1)
        kd_o[:, 2 * j * LANES:(2 * j + 2) * LANES] = _group_norm(
            kf, bd_ref[1], 1.0 / MLA_QK, g_ref[7:8, :]).astype(BF16)
    vd_o[...] = kv[:, 4 * LANES:].astype(BF16)


def _mla_ctx(cache_ckv, cache_kpe_p, w_ukv_p, gains, bd):
    return pl.pallas_call(
        _mla_ctx_kernel,
        grid=(DEPTH, DEC_BATCH),
        in_specs=[
            pl.BlockSpec((None, None, PAST_LEN, LANES), lambda l, b: (b, l, 0, 0)),
            pl.BlockSpec((None, None, PAST_LEN, LANES), lambda l, b: (b, l, 0, 0)),
            pl.BlockSpec((None, MLA_KV_LORA, 768), lambda l, b: (l, 0, 0)),
            pl.BlockSpec((None, 16, 256), lambda l, b: (l, 0, 0)),
            pl.BlockSpec((2, 256, 256), lambda l, b: (0, 0, 0)),
        ],
        out_specs=[pl.BlockSpec((None, None, PAST_LEN, 512), lambda l, b: (l, b, 0, 0)),
                   pl.BlockSpec((None, None, PAST_LEN, 256), lambda l, b: (l, b, 0, 0))],
        out_shape=[jax.ShapeDtypeStruct((DEPTH, DEC_BATCH, PAST_LEN, 512), BF16),
                   jax.ShapeDtypeStruct((DEPTH, DEC_BATCH, PAST_LEN, 256), BF16)],
        compiler_params=_cparams(("parallel", "parallel")),
        name="mla_ctx",
    )(cache_ckv, cache_kpe_p, w_ukv_p, gains, bd)


def _merge_kernel(xc_ref, xl_ref, oc_ref, oa_ref, ob_ref, occ_ref, od_ref, mod_ref, n1_ref, wg_ref, wb_ref, wo_ref,
                  x1_ref):
    is_ctx = pl.program_id(0) < CTX_TILES
    x = jnp.where(is_ctx, xc_ref[...], xl_ref[...])
    mod = mod_ref[...]
    h = _rms_mod(x, n1_ref[...], mod[:, D_MODEL:2 * D_MODEL], mod[:, 0:D_MODEL])
    hb = h.astype(BF16)
    merged = None
    for n, lat_ref in enumerate((oa_ref, ob_ref, occ_ref, od_ref)):
        gate = _dot(hb, wg_ref[:, n * D_MODEL:(n + 1) * D_MODEL])
        gate = 1.0 / (1.0 + jnp.exp(-gate))
        o = jnp.where(is_ctx, oc_ref[:, n * 256:(n + 1) * 256], lat_ref[...])
        br = _dot(o, wb_ref[n])
        merged = gate * br if merged is None else merged + gate * br
    out = _dot(merged.astype(BF16), wo_ref[...])
    x1_ref[...] = x + mod[:, 2 * D_MODEL:3 * D_MODEL] * out


def _merge(l, x_ctx, x_lat, o_ctx, o_lat, mods, norm1, w_gate, w_branch, w_out):
    tok = pl.BlockSpec((TM, D_MODEL), lambda i: (i, 0))
    ctx_tok = pl.BlockSpec((TM, D_MODEL), lambda i: (_ctx_block(i), 0))
    lat_tok = lambda w: pl.BlockSpec((TM, w), lambda i: (_lat_block(i), 0))
    return pl.pallas_call(
        _merge_kernel,
        grid=(N_TILES,),
        in_specs=[
            ctx_tok, lat_tok(D_MODEL), ctx_tok, lat_tok(256), lat_tok(256), lat_tok(256), lat_tok(256),
            pl.BlockSpec((None, None, 1, 6 * D_MODEL), lambda i: (l, _tile_group(i), 0, 0)),
            _resident((None, 1, D_MODEL), lambda i: (l, 0, 0)),
            _resident((None, D_MODEL, N_GATE), lambda i: (l, 0, 0)),
            _resident((None, 4, 256, D_MODEL), lambda i: (l, 0, 0, 0)),
            _resident((None, D_MODEL, D_MODEL), lambda i: (l, 0, 0)),
        ],
        out_specs=tok,
        out_shape=jax.ShapeDtypeStruct((N_TOK, D_MODEL), F32),
        compiler_params=_cparams(("parallel",)),
        name=f"merge_l{l}",
    )(x_ctx, x_lat, o_ctx, *o_lat, mods, norm1, w_gate, w_branch, w_out)


HALO = 8


def _ffn_kernel(x_ref, xp_ref, xn_ref, mod_ref, n2_ref, perm_ref, wup_ref, cw_ref, cb_ref, wdn_ref, *rest, split_out):
    u_ref = rest[-1]
    i = pl.program_id(0)
    p = (i - CTX_TILES) % LAT_TILES_PER_BATCH
    has_prev = (i >= CTX_TILES) & (p != 0)
    has_next = (i >= CTX_TILES) & (p != LAT_TILES_PER_BATCH - 1)
    x = x_ref[...]
    mod = mod_ref[...]
    n2 = n2_ref[...]
    sh2, sc2 = mod[:, 3 * D_MODEL:4 * D_MODEL], mod[:, 4 * D_MODEL:5 * D_MODEL]
    h_prev = jnp.where(has_prev, _rms_mod(xp_ref[...], n2, sc2, sh2), 0.0)
    h_next = jnp.where(has_next, _rms_mod(xn_ref[...], n2, sc2, sh2), 0.0)
    hb = _rms_mod(x, n2, sc2, sh2).astype(BF16)
    halo = jnp.concatenate([h_prev, h_next], axis=0).astype(BF16)
    lhs = jnp.concatenate([_dot(perm_ref[0], hb).astype(BF16), halo], axis=0)
    sub = lax.broadcasted_iota(jnp.int32, (HALO, FF_CHUNK), 0)

    def cols(c):
        return (slice(c * FF_CHUNK, (c + 1) * FF_CHUNK), slice(D_FF + c * FF_CHUNK, D_FF + (c + 1) * FF_CHUNK))

    def up(c):
        for half, cs in enumerate(cols(c)):
            lanes = slice(half * FF_CHUNK, (half + 1) * FF_CHUNK)
            u = _dot(lhs, wup_ref[:, cs])
            u_ref[c, HALO:HALO + TM, lanes] = u[:TM]
            prev_row, next_row = u[TM + HALO - 1:TM + HALO], u[TM + HALO:TM + HALO + 1]
            u_ref[c, 0:HALO, lanes] = jnp.where(sub == 0, prev_row, pltpu.roll(u[TM - HALO:TM], 1, 0))
            u_ref[c, HALO + TM:, lanes] = jnp.where(sub == HALO - 1, next_row, pltpu.roll(u[0:HALO], HALO - 1, 0))

    def both(ref, c):
        va, vg = cols(c)
        return jnp.concatenate([ref[:, va], ref[:, vg]], axis=1)

    acc = None
    up(0)
    up(1)
    for c in range(N_FF_CHUNKS):
        cw = both(cw_ref, c)
        t = (both(cb_ref, c) + u_ref[c, 0:TM, :] * cw[0:1] + u_ref[c, HALO:HALO + TM, :] * cw[1:2]
             + u_ref[c, 2 * HALO:2 * HALO + TM, :] * cw[2:3])
        a, g = t[:, :FF_CHUNK], t[:, FF_CHUNK:]
        act = ((g * (1.0 / (1.0 + jnp.exp(-g)))) * a).astype(BF16)
        act = _dot(perm_ref[1], act).astype(BF16)
        if c + 2 < N_FF_CHUNKS:
            up(c + 2)
        d = _dot(act, wdn_ref[c * FF_CHUNK:(c + 1) * FF_CHUNK, :])
        acc = d if acc is None else acc + d
    y = x + mod[:, 5 * D_MODEL:6 * D_MODEL] * acc
    if split_out:
        @pl.when(i < CTX_TILES)
        def _():
            rest[0][...] = y

        @pl.when(i >= CTX_TILES)
        def _():
            rest[1][...] = y
    else:
        rest[0][...] = y


def _row_permutations():
    r = np.arange(TM)
    token = r // HALO + (TM // HALO) * (r % HALO)
    p = (token[:, None] == np.arange(TM)[None, :]).astype(np.float32)
    return jnp.asarray(np.stack([p, p.T], axis=0), dtype=BF16)


def _ffn(l, x1, mods, norm2, perm, w_up, conv_w, conv_b, w_down, split_out):
    tok = pl.BlockSpec((TM, D_MODEL), lambda i: (i, 0))
    per = TM // HALO
    last = N_TOK // HALO - 1
    if split_out:
        out_specs = [pl.BlockSpec((TM, D_MODEL), lambda i: (jnp.minimum(i, CTX_TILES - 1), 0)),
                     pl.BlockSpec((TM, D_MODEL), lambda i: (jnp.maximum(i - CTX_TILES, 0), 0))]
        out_shape = [jax.ShapeDtypeStruct((N_CTX, D_MODEL), F32), jax.ShapeDtypeStruct((N_LAT, D_MODEL), F32)]
    else:
        out_specs, out_shape = tok, jax.ShapeDtypeStruct((N_TOK, D_MODEL), F32)
    return pl.pallas_call(
        functools.partial(_ffn_kernel, split_out=split_out),
        grid=(N_TILES,),
        in_specs=[
            tok,
            pl.BlockSpec((HALO, D_MODEL), lambda i: (jnp.maximum(i * per - 1, 0), 0)),
            pl.BlockSpec((HALO, D_MODEL), lambda i: (jnp.minimum((i + 1) * per, last), 0)),
            pl.BlockSpec((None, None, 1, 6 * D_MODEL), lambda i: (l, _tile_group(i), 0, 0)),
            _resident((None, 1, D_MODEL), lambda i: (l, 0, 0)),
            _resident((2, TM, TM), lambda i: (0, 0, 0)),
            _resident((None, D_MODEL, 2 * D_FF), lambda i: (l, 0, 0)),
            _resident((None, 3, 2 * D_FF), lambda i: (l, 0, 0)),
            _resident((None, 1, 2 * D_FF), lambda i: (l, 0, 0)),
            _resident((None, D_FF, D_MODEL), lambda i: (l, 0, 0)),
        ],
        out_specs=out_specs,
        out_shape=out_shape,
        scratch_shapes=[pltpu.VMEM((N_FF_CHUNKS, TM + 2 * HALO, 2 * FF_CHUNK), F32)],
        compiler_params=_cparams(("arbitrary",)),
        name=f"ffn_l{l}",
    )(x1, x1, x1, mods, norm2, perm, w_up, conv_w, conv_b, w_down)


def _cast_kernel(a_ref, o_ref):
    o_ref[...] = a_ref[...].astype(BF16)


def _cast_bf16(a, rows, name):
    _, r, c = a.shape
    spec = pl.BlockSpec((None, rows, c), lambda l, i: (l, i, 0))
    return pl.pallas_call(
        _cast_kernel, grid=(DEPTH, r // rows), in_specs=[spec], out_specs=spec,
        out_shape=jax.ShapeDtypeStruct(a.shape, BF16),
        compiler_params=_cparams(("parallel", "parallel")), name=name,
    )(a)


WPREP_COLS = 512
WPREP_QKV_BLOCKS = C_END // WPREP_COLS
WPREP_PLAIN = C_KC // WPREP_COLS
WPREP_BLOCKS = WPREP_QKV_BLOCKS + N_GATE // WPREP_COLS
_SRC_CKV = C_QD + 4 * MLA_QK


def _win_prep_start(j):
    return jnp.where(j < WPREP_PLAIN, WPREP_COLS * j,
                     jnp.where(j == WPREP_PLAIN, C_KC,
                               jnp.where(j == WPREP_PLAIN + 1, C_QD + 2 * MLA_QK,
                                         IN_QKV + WPREP_COLS * (j - WPREP_QKV_BLOCKS))))


def _win_prep_kernel(w_ref, q_ref, g_ref):
    j = pl.program_id(1)
    w = w_ref[0]
    zeros = lambda n: jnp.zeros((n, D_MODEL), F32)
    pad = zeros(LANES - MLA_QK)

    def two_heads(r0):
        return [w[r0:r0 + MLA_QK], pad, w[r0 + MLA_QK:r0 + 2 * MLA_QK], pad]

    @pl.when(j >= WPREP_QKV_BLOCKS)
    def _():
        g_ref[...] = w.T.astype(BF16)

    @pl.when(j < WPREP_PLAIN)
    def _():
        q_ref[...] = w.T.astype(BF16)

    @pl.when(j == WPREP_PLAIN)
    def _():
        n_kv = C_QD - C_KC
        q_ref[...] = jnp.concatenate([w[0:n_kv]] + two_heads(n_kv), axis=0).T.astype(BF16)

    @pl.when(j == WPREP_PLAIN + 1)
    def _():
        r_ckv = _SRC_CKV - (C_QD + 2 * MLA_QK)
        r_kpe = r_ckv + MLA_KV_LORA
        sel = two_heads(0) + [w[r_ckv:r_kpe], zeros(MLA_NOPE), w[r_kpe:r_kpe + MLA_ROPE], pad]
        q_ref[...] = jnp.concatenate(sel, axis=0).T.astype(BF16)


def _win_prep(w_in_t):
    window = pl.BlockSpec((pl.Element(1), pl.Element(WPREP_COLS), pl.Element(D_MODEL)),
                          lambda l, j: (l, pl.multiple_of(_win_prep_start(j), 32), 0))
    return pl.pallas_call(
        _win_prep_kernel,
        grid=(DEPTH, WPREP_BLOCKS),
        in_specs=[window],
        out_specs=[pl.BlockSpec((None, D_MODEL, WPREP_COLS), lambda l, j: (l, 0, jnp.minimum(j, WPREP_QKV_BLOCKS - 1))),
                   pl.BlockSpec((None, D_MODEL, WPREP_COLS), lambda l, j: (l, 0, jnp.maximum(j - WPREP_QKV_BLOCKS, 0)))],
        out_shape=[jax.ShapeDtypeStruct((DEPTH, D_MODEL, C_END), BF16),
                   jax.ShapeDtypeStruct((DEPTH, D_MODEL, N_GATE), BF16)],
        compiler_params=_cparams(("arbitrary", "arbitrary")),
        name="w_in_prep",
    )(w_in_t)


def _rope_tables():
    t = np.arange(DEC_SEQ)

    def angles(dim):
        half = dim // 2
        inv = ROPE_THETA ** (-np.arange(0, half, 2, dtype=np.float32) / half)
        row = (t // GRID_W).astype(np.float32)[:, None] * inv[None, :]
        col = (t % GRID_W).astype(np.float32)[:, None] * inv[None, :]
        return np.concatenate([row, row, col, col], axis=-1).astype(np.float32)

    def signed(ang, quarter):
        cos, sin = np.cos(ang), np.sin(ang)
        first = (np.arange(ang.shape[1]) % (2 * quarter)) < quarter
        return cos, np.where(first, -sin, 0.0), np.where(first, 0.0, sin)

    c64, lo64, hi64 = signed(angles(HEAD_DIM), 16)
    c64, lo64, hi64 = (np.tile(a, (1, 2)) for a in (c64, lo64, hi64))
    cm, lom, him = signed(angles(MLA_ROPE), 8)

    def mla_pad(a, fill):
        out = np.full((DEC_SEQ, LANES), fill, np.float32)
        out[:, MLA_NOPE:MLA_QK] = a
        return out

    tabs = [c64, lo64, hi64, mla_pad(cm, 1.0), mla_pad(lom, 0.0), mla_pad(him, 0.0)]
    ident = [np.ones, np.zeros, np.zeros, np.ones, np.zeros, np.zeros]
    full = [np.concatenate([f((TM, LANES), np.float32), a.astype(np.float32)], axis=0) for f, a in zip(ident, tabs)]
    return jnp.asarray(np.stack(full, axis=0))


def _block_diag_ones():
    idx = np.arange(256)
    mats = [(idx[:, None] // w == idx[None, :] // w).astype(np.float32) for w in (HEAD_DIM, LANES)]
    return jnp.asarray(np.stack(mats, axis=0), dtype=BF16)


def _prep_small(w_ukv, qn_a, kn_a, qn_b, kn_b, qn_c, kn_c, qn_d, kn_d, kvn_d):
    ukv = w_ukv.reshape(DEPTH, MLA_KV_LORA, 4, 2, HEAD_DIM)
    k_part = jnp.pad(ukv[:, :, :, 0], ((0, 0), (0, 0), (0, 0), (0, HEAD_DIM))).reshape(DEPTH, MLA_KV_LORA, 512)
    v_part = ukv[:, :, :, 1].reshape(DEPTH, MLA_KV_LORA, 256)
    w_ukv_p = jnp.concatenate([k_part, v_part], axis=-1).astype(BF16)

    def row(g, reps, width=256):
        t = jnp.tile(g, (1, reps))
        return jnp.pad(t, ((0, 0), (0, width - t.shape[1])))

    pad96 = lambda g: jnp.pad(g, ((0, 0), (0, LANES - MLA_QK)))
    rows = [row(qn_a, 4), row(kn_a, 4), row(qn_b, 4), row(kn_b, 2), row(qn_c, 4), row(kn_c, 2),
            row(pad96(qn_d), 2), row(pad96(kn_d), 2), row(kvn_d, 1)]
    gains = jnp.stack(rows + [jnp.zeros_like(rows[0])] * (16 - len(rows)), axis=1)
    return w_ukv_p, gains


def kernel(x_prompt, x_sample, cache_nat_k, cache_nat_v, cache_gqa_k, cache_gqa_v, cache_win_k, cache_win_v,
           cache_mla_ckv, cache_mla_kpe, c, c_ctx, w_ada, b_ada, norm1, norm2, w_in, qn_a, kn_a, rpb_a,
           qn_b, kn_b, qn_c, kn_c, sink_c, qn_d, kn_d, kvn_d, w_ukv, w_branch, w_out, w_up, conv_w, conv_b,
           w_down):
    w_ukv_p, gains = _prep_small(w_ukv, qn_a, kn_a, qn_b, kn_b, qn_c, kn_c, qn_d, kn_d, kvn_d)
    w_qkv, w_gate = _win_prep(jnp.swapaxes(w_in, 1, 2))
    perm = _row_permutations()
    w_branch_b = _cast_bf16(w_branch.reshape(DEPTH, 4 * 256, D_MODEL), 1024, "cast_w_branch").reshape(
        DEPTH, 4, 256, D_MODEL)
    w_out_b = _cast_bf16(w_out, 1024, "cast_w_out")
    w_up_b = _cast_bf16(w_up, 256, "cast_w_up")
    w_down_b = _cast_bf16(w_down, D_FF // 4, "cast_w_down")
    conv_b_r = conv_b.reshape(DEPTH, 1, 2 * D_FF)
    rope_tab = _rope_tables()
    bd = _block_diag_ones()
    norm1_r = norm1.reshape(DEPTH, 1, D_MODEL)
    norm2_r = norm2.reshape(DEPTH, 1, D_MODEL)

    cond8 = jnp.zeros((8, D_MODEL), F32).at[0].set(c_ctx).at[1:1 + DEC_BATCH].set(c)
    mods = _adaln(cond8, w_ada, b_ada).reshape(DEPTH, 8, 1, 6 * D_MODEL)

    rpb_flat = jnp.pad(rpb_a.reshape(-1), (0, 8192 - rpb_a.size))
    na_bias = _na_bias(rpb_flat)
    kpe_p = jnp.pad(cache_mla_kpe, ((0, 0), (0, 0), (0, 0), (MLA_NOPE, LANES - MLA_QK)))
    kd_ctx, vd_ctx = _mla_ctx(cache_mla_ckv, kpe_p, w_ukv_p, gains, bd)
    c_nat_k = cache_nat_k.reshape(DEC_BATCH, DEPTH, PAST_LEN, 256)
    c_nat_v = cache_nat_v.reshape(DEC_BATCH, DEPTH, PAST_LEN, 256)
    c_gqa_k = cache_gqa_k.reshape(DEC_BATCH, DEPTH, PAST_LEN, LANES)
    c_gqa_v = cache_gqa_v.reshape(DEC_BATCH, DEPTH, PAST_LEN, LANES)
    c_win_k = cache_win_k.reshape(DEC_BATCH, DEPTH, PAST_LEN, LANES)
    c_win_v = cache_win_v.reshape(DEC_BATCH, DEPTH, PAST_LEN, LANES)

    x_ctx, x_lat = x_prompt.reshape(N_CTX, D_MODEL), x_sample.reshape(N_LAT, D_MODEL)
    caches = [jnp.zeros((BATCH, DEPTH) + blk, F32) for blk in _CACHE_BLOCKS]
    for l in range(DEPTH):
        qkv, caches = _inproj(l, x_ctx, x_lat, mods, norm1_r, w_qkv, w_ukv_p, gains, bd, rope_tab, caches)
        qa, ka, va, qb, kb, vb, qc, kc, vc, qd, kd, vd = qkv
        sink_l = sink_c[l]
        o_ctx = _ctx_attention(l, sink_l, qkv)
        o_lat = (_lat_na(l, qa, ka, va, c_nat_k, c_nat_v, na_bias),
                 _lat_gqa(l, qb, kb, vb, c_gqa_k, c_gqa_v),
                 _lat_window(l, sink_l, qc, kc, vc, c_win_k, c_win_v),
                 _lat_mla(l, qd, kd, vd, kd_ctx, vd_ctx))
        x1 = _merge(l, x_ctx, x_lat, o_ctx, o_lat, mods, norm1_r, w_gate, w_branch_b, w_out_b)
        x_ctx, x_lat = _ffn(l, x1, mods, norm2_r, perm, w_up_b, conv_w, conv_b_r, w_down_b, split_out=True)
    y_prompt = x_ctx.reshape(BATCH, SEQ, D_MODEL)
    y_sample = x_lat.reshape(DEC_BATCH, DEC_SEQ, D_MODEL)
    heads = (4, 4, 2, 2, 2, 2)
    outs = [a.reshape(BATCH, DEPTH, h, HEAD_DIM, SEQ).transpose(0, 1, 4, 2, 3) for a, h in zip(caches[:6], heads)]
    return (y_prompt, y_sample, *outs, jnp.swapaxes(caches[6], 2, 3), jnp.swapaxes(caches[7], 2, 3))
```

```python
import functools

import jax
import jax.numpy as jnp
import numpy as np
from jax import lax
from jax.experimental import pallas as pl
from jax.experimental.pallas import tpu as pltpu

F32 = jnp.float32
BF16 = jnp.bfloat16

D_MODEL = 1024
BATCH = 16
SEQ = 256
DEPTH = 4
DEC_BATCH = 2
DEC_SEQ = 2048
PAST_LEN = 256
GRID_W = 64
HEAD_DIM = 64
NA_ROWS = 8
NA_COLS = 16
WINDOW = 128
MLA_NOPE = 64
MLA_ROPE = 32
MLA_QK = MLA_NOPE + MLA_ROPE
MLA_KV_LORA = 128
D_FF = 2816
ROPE_THETA = 10000.0
EPS = 1e-6
NEG_INF = -1e30

LANES = 128
TM = 256
N_CTX = BATCH * SEQ
N_LAT = DEC_BATCH * DEC_SEQ
N_TOK = N_CTX + N_LAT
CTX_TILES = N_CTX // TM
LAT_TILES_PER_BATCH = DEC_SEQ // TM
N_TILES = N_TOK // TM
GRID_ROWS = DEC_SEQ // GRID_W
FF_CHUNK = 256
N_FF_CHUNKS = D_FF // FF_CHUNK
VMEM_LIMIT = 56 * 1024 * 1024

C_QA, C_KA, C_VA = 0, 256, 512
C_QB, C_KB, C_VB = 768, 1024, 1152
C_QC, C_KC, C_VC = 1280, 1536, 1664
C_QD, C_CKV, C_KPE, C_END = 1792, 2304, 2432, 2560
N_GATE = 4 * D_MODEL
IN_QKV = 2336
LOG2E = float(np.log2(np.e))
Q_SCALE = HEAD_DIM ** -0.5 * LOG2E
MLA_Q_SCALE = MLA_QK ** -0.5 * LOG2E


def _cparams(sem):
    return pltpu.CompilerParams(dimension_semantics=sem, vmem_limit_bytes=VMEM_LIMIT)


def _tile_group(i):
    return jnp.where(i < CTX_TILES, 0, 1 + (i - CTX_TILES) // LAT_TILES_PER_BATCH)


def _tile_pos_block(i):
    return jnp.where(i < CTX_TILES, 0, 1 + (i - CTX_TILES) % LAT_TILES_PER_BATCH)


def _resident(shape, index_map):
    return pl.BlockSpec(shape, index_map, pipeline_mode=pl.Buffered(1))


def _dot(a, b):
    return jnp.dot(a, b, preferred_element_type=F32)


def _dot_t(a, b):
    return lax.dot_general(a, b, (((1,), (1,)), ((), ())), preferred_element_type=F32)


def _rms_mod(x, gain, scale, shift):
    ms = jnp.mean(x * x, axis=-1, keepdims=True)
    return (x * lax.rsqrt(ms + EPS)) * gain * (1.0 + scale) + shift


def _group_norm(t, ones_bd, inv_n, gain):
    ss = _dot((t * t).astype(BF16), ones_bd)
    return t * lax.rsqrt(ss * inv_n + EPS) * gain


def _rope(t, cos, sin_lo, sin_hi, half):
    return (t * cos + pltpu.roll(t, LANES - half, 1) * sin_lo + pltpu.roll(t, half, 1) * sin_hi)


def _lane_iota(shape):
    return lax.broadcasted_iota(jnp.int32, shape, len(shape) - 1)


def _low_half(shape):
    return _lane_iota(shape) < HEAD_DIM


def _run_units(units, depth=2):
    n = len(units)
    pending = {}
    outs = []
    for k in range(min(depth, n)):
        pending[k] = units[k][0]()
    for k in range(n):
        outs.append(units[k][1](pending.pop(k)))
        if k + depth < n:
            pending[k + depth] = units[k + depth][0]()
    return outs


N_COND = 1 + DEC_BATCH


def _adaln_kernel(cond_ref, w_ref, b_ref, out_ref):
    cnd = cond_ref[...]
    s_t = (cnd * (1.0 / (1.0 + jnp.exp(-cnd)))).T
    w = w_ref[...]
    out_ref[...] = jnp.zeros(out_ref.shape, F32)
    for r in range(N_COND):
        out_ref[r:r + 1, :] = jnp.sum(s_t[:, r:r + 1] * w, axis=0, keepdims=True) + b_ref[...]


def _adaln(cond8, w_ada, b_ada):
    n_col = 6 * D_MODEL
    tn = 1024
    return pl.pallas_call(
        _adaln_kernel,
        grid=(DEPTH, n_col // tn),
        in_specs=[
            pl.BlockSpec((8, D_MODEL), lambda l, n: (0, 0)),
            pl.BlockSpec((None, D_MODEL, tn), lambda l, n: (l, 0, n)),
            pl.BlockSpec((None, 1, tn), lambda l, n: (l, 0, n)),
        ],
        out_specs=pl.BlockSpec((None, 8, tn), lambda l, n: (l, 0, n)),
        out_shape=jax.ShapeDtypeStruct((DEPTH, 8, n_col), F32),
        compiler_params=_cparams(("parallel", "parallel")),
        name="adaln",
    )(cond8, w_ada, b_ada.reshape(DEPTH, 1, n_col))


_QKV_WIDTHS = (512, 256, 256, 512, 128, 128, 512, 128, 128, 512, 512, 256)
_CACHE_WIDTHS = (256, 256, 128, 128, 128, 128, MLA_KV_LORA, MLA_ROPE)
_CACHE_BLOCKS = tuple((w, SEQ) for w in _CACHE_WIDTHS)


TM_IN = 2 * TM
IN_STEPS = N_TOK // TM_IN
IN_CTX_STEPS = N_CTX // TM_IN


def _inproj_kernel(xc_ref, xl_ref, mod_ref, n1_ref, w_ref, wukv_ref, g_ref, bd_ref, rope_ref, *rest):
    outs = rest[-(len(_QKV_WIDTHS) + len(_CACHE_WIDTHS)):]
    is_ctx = pl.program_id(0) >= IN_STEPS - IN_CTX_STEPS
    units = []
    for half in range(TM_IN // TM):
        rows = slice(half * TM, (half + 1) * TM)
        views = [o.at[rows] for o in outs[:len(_QKV_WIDTHS)]] + [o.at[half] for o in outs[len(_QKV_WIDTHS):]]
        units += _inproj_half(is_ctx, xc_ref.at[rows], xl_ref.at[rows], mod_ref, n1_ref, w_ref, wukv_ref, g_ref,
                              bd_ref, rope_ref.at[:, rows, :], views)
    _run_units(units, depth=2)


def _inproj_half(is_ctx, xc_ref, xl_ref, mod_ref, n1_ref, w_ref, wukv_ref, g_ref, bd_ref, rope_ref, outs):
    qa_o, ka_o, va_o, qb_o, kb_o, vb_o, qc_o, kc_o, vc_o, qd_o, kd_o, vd_o = outs[:len(_QKV_WIDTHS)]
    natk_o, natv_o, gqak_o, gqav_o, wink_o, winv_o, ckv_o, kpe_o = outs[len(_QKV_WIDTHS):]
    x = jnp.where(is_ctx, xc_ref[...], xl_ref[...])
    mod = mod_ref[...]
    hb = _rms_mod(x, n1_ref[...], mod[:, D_MODEL:2 * D_MODEL], mod[:, 0:D_MODEL]).astype(BF16)
    bd64, bd128 = bd_ref[0], bd_ref[1]
    cos64, slo64, shi64 = rope_ref[0], rope_ref[1], rope_ref[2]
    cosm, slom, shim = rope_ref[3], rope_ref[4], rope_ref[5]
    inv64, inv96 = 1.0 / HEAD_DIM, 1.0 / MLA_QK
    low = _low_half((TM, LANES))
    held = {}

    def chunks(t):
        return [t[:, c:c + LANES] for c in range(0, t.shape[1], LANES)]

    def rope64(t):
        parts = [_rope(p, cos64, slo64, shi64, 16) for p in chunks(t)]
        return parts[0] if len(parts) == 1 else jnp.concatenate(parts, axis=1)

    def ropem(t):
        return jnp.concatenate([_rope(p, cosm, slom, shim, 8) for p in chunks(t)], axis=1)

    def store_q(ref, q, grouped):
        q = q * Q_SCALE
        slots = []
        for hd in range(4):
            part, src = q[:, (hd // 2) * LANES:(hd // 2 + 1) * LANES], hd % 2
            dst = hd // 2 if grouped else src
            if src != dst:
                part = pltpu.roll(part, HEAD_DIM, 1)
            slots.append(jnp.where(low if dst == 0 else ~low, part, 0.0))
        ref[...] = jnp.concatenate(slots, axis=1).astype(BF16)

    def post_latent(t):
        ckv = _group_norm(t[:, :LANES], bd128[:LANES, :LANES], 1.0 / MLA_KV_LORA, g_ref[8:9, :LANES])
        kpe = t[:, LANES:]
        ckv_o[...] = ckv.T
        kpe_o[...] = kpe.T[MLA_NOPE:MLA_QK, :]
        held["kpe"] = kpe
        held["kv"] = _dot(ckv.astype(BF16), wukv_ref[...])

    def post_kd(_):
        kv, kpe = held["kv"], held["kpe"]
        for j in range(2):
            kf = jnp.concatenate([kv[:, (2 * j) * LANES:(2 * j + 1) * LANES] + kpe,
                                  kv[:, (2 * j + 1) * LANES:(2 * j + 2) * LANES] + kpe], axis=1)
            kd_o[:, 2 * j * LANES:(2 * j + 2) * LANES] = ropem(
                _group_norm(kf, bd128, inv96, g_ref[7:8, :])).astype(BF16)
        vd_o[...] = kv[:, 4 * LANES:].astype(BF16)

    def post_qa(t):
        store_q(qa_o, _group_norm(t, bd64, inv64, g_ref[0:1, :]), grouped=False)

    def post_ka(t):
        k = _group_norm(t, bd64, inv64, g_ref[1:2, :])
        natk_o[...] = k.T
        ka_o[...] = k.astype(BF16)

    def post_va(t):
        natv_o[...] = t.T
        va_o[...] = t.astype(BF16)

    def post_q_rope(ref, row):
        def post(t):
            store_q(ref, rope64(_group_norm(t, bd64, inv64, g_ref[row:row + 1, :])), grouped=True)
        return post

    def post_kv(k_ref, v_ref, kc_ref, vc_ref, row):
        def post(t):
            k = _group_norm(t[:, :LANES], bd64[:LANES, :LANES], inv64, g_ref[row:row + 1, :LANES])
            kc_ref[...] = k.T
            k_ref[...] = rope64(k).astype(BF16)
            v = t[:, LANES:]
            vc_ref[...] = v.T
            v_ref[...] = v.astype(BF16)
        return post

    def post_qd(j):
        def post(t):
            qd_o[:, 2 * j * LANES:(2 * j + 2) * LANES] = (ropem(
                _group_norm(t, bd128, inv96, g_ref[6:7, :])) * MLA_Q_SCALE).astype(BF16)
        return post

    stages = [
        ((C_CKV, C_END), post_latent),
        ((C_QA, C_KA), post_qa),
        (None, post_kd),
        ((C_QB, C_KB), post_q_rope(qb_o, 2)),
        ((C_QC, C_KC), post_q_rope(qc_o, 4)),
        ((C_QD, C_QD + 2 * LANES), post_qd(0)),
        ((C_QD + 2 * LANES, C_CKV), post_qd(1)),
        ((C_KB, C_QC), post_kv(kb_o, vb_o, gqak_o, gqav_o, 3)),
        ((C_KC, C_QD), post_kv(kc_o, vc_o, wink_o, winv_o, 5)),
        ((C_KA, C_VA), post_ka),
        ((C_VA, C_QB), post_va),
    ]

    def issue(cols):
        return None if cols is None else _dot(hb, w_ref[:, cols[0]:cols[1]])

    return [(functools.partial(issue, cols), post) for cols, post in stages]


def _ctx_block(i):
    return jnp.minimum(i, CTX_TILES - 1)


def _lat_block(i):
    return jnp.maximum(i - CTX_TILES, 0)


def _inproj(l, x_ctx, x_lat, mods, norm1, w_qkv, w_ukv_p, gains, bd, rope_tab, caches):
    n_lat = IN_STEPS - IN_CTX_STEPS
    blk = lambda s: (s + IN_CTX_STEPS) % IN_STEPS
    lat_blk = lambda s: jnp.maximum(blk(s) - IN_CTX_STEPS, 0)
    per_batch = DEC_SEQ // TM_IN
    tok = lambda w: pl.BlockSpec((TM_IN, w), lambda s: (blk(s), 0))
    in_specs = [
        pl.BlockSpec((TM_IN, D_MODEL), lambda s: (jnp.minimum(blk(s), IN_CTX_STEPS - 1), 0)),
        pl.BlockSpec((TM_IN, D_MODEL), lambda s: (lat_blk(s), 0)),
        pl.BlockSpec((None, None, 1, 6 * D_MODEL),
                     lambda s: (l, jnp.where(s < n_lat, 1 + lat_blk(s) // per_batch, 0), 0, 0)),
        _resident((None, 1, D_MODEL), lambda s: (l, 0, 0)),
        _resident((None, D_MODEL, C_END), lambda s: (l, 0, 0)),
        _resident((None, MLA_KV_LORA, 768), lambda s: (l, 0, 0)),
        _resident((None, 16, 256), lambda s: (l, 0, 0)),
        _resident((2, 256, 256), lambda s: (0, 0, 0)),
        pl.BlockSpec((6, TM_IN, LANES), lambda s: (0, jnp.where(s < n_lat, 1 + lat_blk(s) % per_batch, 0), 0)),
    ]
    args = [x_ctx, x_lat, mods, norm1, w_qkv, w_ukv_p, gains, bd, rope_tab] + list(caches)
    aliases = {len(in_specs) + n: len(_QKV_WIDTHS) + n for n in range(len(caches))}
    in_specs = in_specs + [pl.BlockSpec(memory_space=pl.ANY)] * len(caches)
    cache_spec = lambda cb: pl.BlockSpec((TM_IN // SEQ, None) + cb, lambda s: (jnp.maximum(s - n_lat, 0), l, 0, 0))
    outs = pl.pallas_call(
        _inproj_kernel,
        grid=(IN_STEPS,),
        in_specs=in_specs,
        out_specs=[tok(w) for w in _QKV_WIDTHS] + [cache_spec(blk) for blk in _CACHE_BLOCKS],
        out_shape=[jax.ShapeDtypeStruct((N_TOK, w), BF16) for w in _QKV_WIDTHS]
        + [jax.ShapeDtypeStruct((BATCH, DEPTH) + blk, F32) for blk in _CACHE_BLOCKS],
        input_output_aliases=aliases,
        compiler_params=_cparams(("arbitrary",)),
        name=f"inproj_l{l}",
    )(*args)
    return outs[:len(_QKV_WIDTHS)], outs[len(_QKV_WIDTHS):]


def _scores(q, keys):
    out = []
    for k, bias in keys:
        s = _dot_t(q, k)
        if bias is not None:
            s = s + bias
        out.append(s)
    return out


def _softmax_pv(scores, values, sink=None):
    m = functools.reduce(jnp.maximum, [jnp.max(s, axis=-1, keepdims=True) for s in scores])
    if sink is not None:
        m = jnp.maximum(m, sink)
    denom = None
    out = None
    for s, v in zip(scores, values):
        e = jnp.exp2(s - m)
        d = jnp.sum(e, axis=-1, keepdims=True)
        o = _dot(e.astype(BF16), v)
        denom = d if denom is None else denom + d
        out = o if out is None else out + o
    if sink is not None:
        denom = denom + jnp.exp2(sink - m)
    return out / denom


def _pair_unit(q_ref, rows, j, keys, values, kv_half, sinks=None):
    m_rows = rows.stop - rows.start

    def issue():
        q2 = jnp.concatenate([q_ref[rows, (2 * j) * LANES:(2 * j + 1) * LANES],
                              q_ref[rows, (2 * j + 1) * LANES:(2 * j + 2) * LANES]], axis=0)
        return _scores(q2, keys())

    def finish(scores):
        sink = None
        if sinks is not None:
            row = lax.broadcasted_iota(jnp.int32, (2 * m_rows, 1), 0)
            sink = jnp.where(row < m_rows, sinks[0], sinks[1])
        o2 = _softmax_pv(scores, values(), sink)
        lo, hi = o2[:m_rows], o2[m_rows:]
        if kv_half == 0:
            hi = pltpu.roll(hi, HEAD_DIM, 1)
        elif kv_half == 1:
            lo = pltpu.roll(lo, HEAD_DIM, 1)
        return jnp.where(_low_half(lo.shape), lo, hi)

    return issue, finish


def _single_unit(q_ref, rows, hd, keys, values):
    def issue():
        return _scores(q_ref[rows, hd * LANES:(hd + 1) * LANES], keys())

    def finish(scores):
        return _softmax_pv(scores, values())

    return issue, finish


def _merge_halves(o_even, o_odd):
    return jnp.where(_low_half(o_even.shape), o_even, o_odd)


def _ctx_attn_kernel(sink_ref, qa, ka, va, qb, kb, vb, qc, kc, vc, qd, kd, vd, o_ref):
    rows = slice(0, SEQ)
    units = []
    for j in range(2):
        c = slice(j * LANES, (j + 1) * LANES)
        units.append(_pair_unit(qa, rows, j, lambda c=c: [(ka[:, c], None)], lambda c=c: [va[:, c]], None))
    for q, k, v, with_sink in ((qb, kb, vb, False), (qc, kc, vc, True)):
        for j in range(2):
            sinks = (sink_ref[2 * j] * LOG2E, sink_ref[2 * j + 1] * LOG2E) if with_sink else None
            units.append(_pair_unit(q, rows, j, lambda k=k: [(k[...], None)], lambda v=v: [v[...]], j, sinks))
    for hd in range(4):
        c = slice(hd * LANES, (hd + 1) * LANES)
        vs = slice((hd // 2) * LANES, (hd // 2 + 1) * LANES)
        units.append(_single_unit(qd, rows, hd, lambda c=c: [(kd[:, c], None)], lambda vs=vs: [vd[:, vs]]))
    outs = _run_units(units)
    for n in range(6):
        o_ref[:, n * LANES:(n + 1) * LANES] = outs[n].astype(BF16)
    for j in range(2):
        o_ref[:, (6 + j) * LANES:(7 + j) * LANES] = _merge_halves(outs[6 + 2 * j], outs[7 + 2 * j]).astype(BF16)


def _ctx_attention(l, sink_c, qkv):
    blk = lambda w: pl.BlockSpec((SEQ, w), lambda b: (b, 0))
    return pl.pallas_call(
        _ctx_attn_kernel,
        grid=(BATCH,),
        in_specs=[pl.BlockSpec(memory_space=pltpu.SMEM)] + [blk(w) for w in _QKV_WIDTHS],
        out_specs=pl.BlockSpec((SEQ, D_MODEL), lambda b: (b, 0)),
        out_shape=jax.ShapeDtypeStruct((N_CTX, D_MODEL), BF16),
        compiler_params=_cparams(("parallel",)),
        name=f"ctx_attn_l{l}",
    )(sink_c, *qkv)


LAT_TQ = 256
LAT_SUB = 128
LAT_QBLOCKS = DEC_SEQ // LAT_TQ
LAT_ROW0 = N_CTX // LAT_TQ
LAT_KV0 = N_CTX // DEC_SEQ


def _lat_qspec(width):
    return pl.BlockSpec((LAT_TQ, width), lambda b, i: (LAT_ROW0 + b * LAT_QBLOCKS + i, 0))


def _lat_kvspec(width):
    return pl.BlockSpec((DEC_SEQ, width), lambda b, i: (LAT_KV0 + b, 0))


def _lat_call(kernel, l, name, in_specs, args):
    return pl.pallas_call(
        kernel,
        grid=(DEC_BATCH, LAT_QBLOCKS),
        in_specs=in_specs,
        out_specs=pl.BlockSpec((LAT_TQ, 256), lambda b, i: (b * LAT_QBLOCKS + i, 0)),
        out_shape=jax.ShapeDtypeStruct((N_LAT, 256), BF16),
        compiler_params=_cparams(("parallel", "parallel")),
        name=f"{name}_l{l}",
    )(*args)


def _sub_rows(u):
    return slice(u * LAT_SUB, (u + 1) * LAT_SUB)


def _store_units(o_ref, outs):
    for n, o in enumerate(outs):
        u, j = divmod(n, 2)
        o_ref[_sub_rows(u), j * LANES:(j + 1) * LANES] = o.astype(BF16)


def _gqa_kernel(q_ref, k_ref, v_ref, kc_ref, vc_ref, o_ref):
    keys = lambda: [(kc_ref[...].astype(BF16), None), (k_ref[...], None)]
    values = lambda: [vc_ref[...].astype(BF16), v_ref[...]]
    units = [_pair_unit(q_ref, _sub_rows(u), j, keys, values, j) for u in range(LAT_TQ // LAT_SUB) for j in range(2)]
    _store_units(o_ref, _run_units(units))


def _lat_gqa(l, qb, kb, vb, cache_k, cache_v):
    cspec = pl.BlockSpec((None, None, PAST_LEN, LANES), lambda b, i: (b, l, 0, 0))
    return _lat_call(_gqa_kernel, l, "lat_gqa",
                     [_lat_qspec(512), _lat_kvspec(LANES), _lat_kvspec(LANES), cspec, cspec],
                     (qb, kb, vb, cache_k, cache_v))


WIN_KEYS = 3 * LAT_SUB


def _win_kernel(sink_ref, q_ref, k_ref, v_ref, kc_ref, vc_ref, o_ref):
    units = []
    for u in range(LAT_TQ // LAT_SUB):
        blk = pl.program_id(1) * (LAT_TQ // LAT_SUB) + u
        start = pl.multiple_of(jnp.clip((blk - 1) * LAT_SUB, 0, DEC_SEQ - WIN_KEYS), LAT_SUB)

        def keys(blk=blk, start=start):
            row = lax.broadcasted_iota(jnp.int32, (2 * LAT_SUB, WIN_KEYS), 0)
            qpos = blk * LAT_SUB + jnp.where(row < LAT_SUB, row, row - LAT_SUB)
            kpos = start + lax.broadcasted_iota(jnp.int32, (2 * LAT_SUB, WIN_KEYS), 1)
            bias = jnp.where(jnp.abs(qpos - kpos) <= WINDOW, 0.0, NEG_INF)
            return [(kc_ref[...].astype(BF16), None), (k_ref[pl.ds(start, WIN_KEYS), :], bias)]

        def values(start=start):
            return [vc_ref[...].astype(BF16), v_ref[pl.ds(start, WIN_KEYS), :]]

        for j in range(2):
            sinks = (sink_ref[2 * j] * LOG2E, sink_ref[2 * j + 1] * LOG2E)
            units.append(_pair_unit(q_ref, _sub_rows(u), j, keys, values, j, sinks))
    _store_units(o_ref, _run_units(units))


def _lat_window(l, sink_c, qc, kc, vc, cache_k, cache_v):
    cspec = pl.BlockSpec((None, None, PAST_LEN, LANES), lambda b, i: (b, l, 0, 0))
    return _lat_call(_win_kernel, l, "lat_win",
                     [pl.BlockSpec(memory_space=pltpu.SMEM), _lat_qspec(512), _lat_kvspec(LANES),
                      _lat_kvspec(LANES), cspec, cspec],
                     (sink_c, qc, kc, vc, cache_k, cache_v))


def _mla_kernel(q_ref, k_ref, v_ref, kc_ref, vc_ref, o_ref):
    units = []
    rows = slice(0, LAT_TQ)
    for hd in range(4):
        c = slice(hd * LANES, (hd + 1) * LANES)
        vs = slice((hd // 2) * LANES, (hd // 2 + 1) * LANES)
        units.append(_single_unit(q_ref, rows, hd,
                                  lambda c=c: [(kc_ref[:, c], None), (k_ref[:, c], None)],
                                  lambda vs=vs: [vc_ref[:, vs], v_ref[:, vs]]))
    outs = _run_units(units)
    for j in range(2):
        o_ref[:, j * LANES:(j + 1) * LANES] = _merge_halves(outs[2 * j], outs[2 * j + 1]).astype(BF16)


def _lat_mla(l, qd, kd, vd, kd_ctx, vd_ctx):
    kcs = pl.BlockSpec((None, None, PAST_LEN, 512), lambda b, i: (l, b, 0, 0))
    vcs = pl.BlockSpec((None, None, PAST_LEN, 256), lambda b, i: (l, b, 0, 0))
    return _lat_call(_mla_kernel, l, "lat_mla",
                     [_lat_qspec(512), _lat_kvspec(512), _lat_kvspec(256), kcs, vcs],
                     (qd, kd, vd, kd_ctx, vd_ctx))


NA_KEYS = NA_ROWS * GRID_W
NA_ROWS_PER_STEP = LAT_TQ // GRID_W


def _na_shift(r):
    return jnp.where(r < NA_ROWS // 2, r,
                     jnp.where(r > GRID_ROWS - NA_ROWS // 2, r - (GRID_ROWS - NA_ROWS), NA_ROWS // 2))


def _na_kernel(q_ref, k_ref, v_ref, kc_ref, vc_ref, bias_ref, o_ref):
    units = []
    for rr in range(NA_ROWS_PER_STEP):
        r = pl.program_id(1) * NA_ROWS_PER_STEP + rr
        start = pl.multiple_of(jnp.clip(r - NA_ROWS // 2, 0, GRID_ROWS - NA_ROWS) * GRID_W, GRID_W)
        shift = _na_shift(r)
        rows = slice(rr * GRID_W, (rr + 1) * GRID_W)
        for j in range(2):
            c = slice(j * LANES, (j + 1) * LANES)

            def keys(j=j, c=c, start=start, shift=shift):
                bias = jnp.concatenate([bias_ref[2 * j, shift], bias_ref[2 * j + 1, shift]], axis=0)
                return [(kc_ref[:, c].astype(BF16), None), (k_ref[pl.ds(start, NA_KEYS), c], bias)]

            def values(c=c, start=start):
                return [vc_ref[:, c].astype(BF16), v_ref[pl.ds(start, NA_KEYS), c]]

            units.append(_pair_unit(q_ref, rows, j, keys, values, None))
    for n, o in enumerate(_run_units(units)):
        rr, j = divmod(n, 2)
        o_ref[rr * GRID_W:(rr + 1) * GRID_W, j * LANES:(j + 1) * LANES] = o.astype(BF16)


def _lat_na(l, qa, ka, va, cache_k, cache_v, na_bias):
    cspec = pl.BlockSpec((None, None, PAST_LEN, 256), lambda b, i: (b, l, 0, 0))
    bspec = _resident((None, 4, NA_ROWS, GRID_W, NA_KEYS), lambda b, i: (l, 0, 0, 0, 0))
    return _lat_call(_na_kernel, l, "lat_na",
                     [_lat_qspec(512), _lat_kvspec(256), _lat_kvspec(256), cspec, cspec, bspec],
                     (qa, ka, va, cache_k, cache_v, na_bias))


def _na_bias_kernel(rpb_ref, out_ref, t_ref):
    l = pl.program_id(0)
    n_dr, n_dc = 2 * NA_ROWS - 1, 2 * NA_COLS - 1
    shape = (GRID_W, LANES)
    qc = lax.broadcasted_iota(jnp.int32, shape, 0)
    lane = _lane_iota(shape)
    kc = lane & (GRID_W - 1)
    dc = jnp.clip(kc - qc, 1 - NA_COLS, NA_COLS - 1) + (NA_COLS - 1)
    c0 = jnp.clip(qc - NA_COLS // 2, 0, GRID_W - NA_COLS)
    ok = (kc >= c0) & (kc < c0 + NA_COLS)
    for hd in range(4):
        def fill(a, carry):
            base = ((l * 4 + hd) * n_dr + a) * n_dc
            t = jnp.zeros(shape, F32)
            for b in range(n_dc):
                t = jnp.where(dc == b, rpb_ref[base + b], t)
            t_ref[a] = jnp.where(ok, t * LOG2E, NEG_INF)
            return carry
        lax.fori_loop(0, n_dr, fill, 0)
        for d in range(NA_ROWS):
            for m in range(NA_ROWS // 2):
                lo = t_ref[2 * m - d + NA_ROWS - 1]
                hi = t_ref[2 * m + 1 - d + NA_ROWS - 1]
                out_ref[hd, d, :, m * LANES:(m + 1) * LANES] = jnp.where(lane < GRID_W, lo, hi)


def _na_bias(rpb_flat):
    return pl.pallas_call(
        _na_bias_kernel,
        grid=(DEPTH,),
        in_specs=[pl.BlockSpec(memory_space=pltpu.SMEM)],
        out_specs=pl.BlockSpec((None, 4, NA_ROWS, GRID_W, NA_KEYS), lambda l: (l, 0, 0, 0, 0)),
        out_shape=jax.ShapeDtypeStruct((DEPTH, 4, NA_ROWS, GRID_W, NA_KEYS), F32),
        scratch_shapes=[pltpu.VMEM((2 * NA_ROWS - 1, GRID_W, LANES), F32)],
        compiler_params=_cparams(("parallel",)),
        name="na_bias",
    )(rpb_flat)


def _mla_ctx_kernel(ckv_ref, kpe_ref, wukv_ref, g_ref, bd_ref, kd_o, vd_o):
    kv = _dot(ckv_ref[...].astype(BF16), wukv_ref[...])
    kpe = kpe_ref[...]
    for j in range(2):
        kf = jnp.concatenate([kv[:, (2 * j) * LANES:(2 * j + 1) * LANES] + kpe,
                              kv[:, (2 * j + 1) * LANES:(2 * j + 2) * LANES] + kpe], axis=1)
        kd_o[:, 2 * j * LANES:(2 * j + 2) * LANES] = _group_norm(
            kf, bd_ref[1], 1.0 / MLA_QK, g_ref[7:8, :]).astype(BF16)
    vd_o[...] = kv[:, 4 * LANES:].astype(BF16)


def _mla_ctx(cache_ckv, cache_kpe_p, w_ukv_p, gains, bd):
    return pl.pallas_call(
        _mla_ctx_kernel,
        grid=(DEPTH, DEC_BATCH),
        in_specs=[
            pl.BlockSpec((None, None, PAST_LEN, LANES), lambda l, b: (b, l, 0, 0)),
            pl.BlockSpec((None, None, PAST_LEN, LANES), lambda l, b: (b, l, 0, 0)),
            pl.BlockSpec((None, MLA_KV_LORA, 768), lambda l, b: (l, 0, 0)),
            pl.BlockSpec((None, 16, 256), lambda l, b: (l, 0, 0)),
            pl.BlockSpec((2, 256, 256), lambda l, b: (0, 0, 0)),
        ],
        out_specs=[pl.BlockSpec((None, None, PAST_LEN, 512), lambda l, b: (l, b, 0, 0)),
                   pl.BlockSpec((None, None, PAST_LEN, 256), lambda l, b: (l, b, 0, 0))],
        out_shape=[jax.ShapeDtypeStruct((DEPTH, DEC_BATCH, PAST_LEN, 512), BF16),
                   jax.ShapeDtypeStruct((DEPTH, DEC_BATCH, PAST_LEN, 256), BF16)],
        compiler_params=_cparams(("parallel", "parallel")),
        name="mla_ctx",
    )(cache_ckv, cache_kpe_p, w_ukv_p, gains, bd)


def _merge_kernel(xc_ref, xl_ref, oc_ref, oa_ref, ob_ref, occ_ref, od_ref, mod_ref, n1_ref, wg_ref, wb_ref, wo_ref,
                  x1_ref):
    is_ctx = pl.program_id(0) < CTX_TILES
    x = jnp.where(is_ctx, xc_ref[...], xl_ref[...])
    mod = mod_ref[...]
    h = _rms_mod(x, n1_ref[...], mod[:, D_MODEL:2 * D_MODEL], mod[:, 0:D_MODEL])
    hb = h.astype(BF16)
    merged = None
    for n, lat_ref in enumerate((oa_ref, ob_ref, occ_ref, od_ref)):
        gate = _dot(hb, wg_ref[:, n * D_MODEL:(n + 1) * D_MODEL])
        gate = 1.0 / (1.0 + jnp.exp(-gate))
        o = jnp.where(is_ctx, oc_ref[:, n * 256:(n + 1) * 256], lat_ref[...])
        br = _dot(o, wb_ref[n])
        merged = gate * br if merged is None else merged + gate * br
    out = _dot(merged.astype(BF16), wo_ref[...])
    x1_ref[...] = x + mod[:, 2 * D_MODEL:3 * D_MODEL] * out


def _merge(l, x_ctx, x_lat, o_ctx, o_lat, mods, norm1, w_gate, w_branch, w_out):
    tok = pl.BlockSpec((TM, D_MODEL), lambda i: (i, 0))
    ctx_tok = pl.BlockSpec((TM, D_MODEL), lambda i: (_ctx_block(i), 0))
    lat_tok = lambda w: pl.BlockSpec((TM, w), lambda i: (_lat_block(i), 0))
    return pl.pallas_call(
        _merge_kernel,
        grid=(N_TILES,),
        in_specs=[
            ctx_tok, lat_tok(D_MODEL), ctx_tok, lat_tok(256), lat_tok(256), lat_tok(256), lat_tok(256),
            pl.BlockSpec((None, None, 1, 6 * D_MODEL), lambda i: (l, _tile_group(i), 0, 0)),
            _resident((None, 1, D_MODEL), lambda i: (l, 0, 0)),
            _resident((None, D_MODEL, N_GATE), lambda i: (l, 0, 0)),
            _resident((None, 4, 256, D_MODEL), lambda i: (l, 0, 0, 0)),
            _resident((None, D_MODEL, D_MODEL), lambda i: (l, 0, 0)),
        ],
        out_specs=tok,
        out_shape=jax.ShapeDtypeStruct((N_TOK, D_MODEL), F32),
        compiler_params=_cparams(("parallel",)),
        name=f"merge_l{l}",
    )(x_ctx, x_lat, o_ctx, *o_lat, mods, norm1, w_gate, w_branch, w_out)


HALO = 8


def _ffn_kernel(x_ref, xp_ref, xn_ref, mod_ref, n2_ref, perm_ref, wup_ref, cw_ref, cb_ref, wdn_ref, *rest, split_out):
    u_ref = rest[-1]
    i = pl.program_id(0)
    p = (i - CTX_TILES) % LAT_TILES_PER_BATCH
    has_prev = (i >= CTX_TILES) & (p != 0)
    has_next = (i >= CTX_TILES) & (p != LAT_TILES_PER_BATCH - 1)
    x = x_ref[...]
    mod = mod_ref[...]
    n2 = n2_ref[...]
    sh2, sc2 = mod[:, 3 * D_MODEL:4 * D_MODEL], mod[:, 4 * D_MODEL:5 * D_MODEL]
    h_prev = jnp.where(has_prev, _rms_mod(xp_ref[...], n2, sc2, sh2), 0.0)
    h_next = jnp.where(has_next, _rms_mod(xn_ref[...], n2, sc2, sh2), 0.0)
    hb = _rms_mod(x, n2, sc2, sh2).astype(BF16)
    halo = jnp.concatenate([h_prev, h_next], axis=0).astype(BF16)
    lhs = jnp.concatenate([_dot(perm_ref[0], hb).astype(BF16), halo], axis=0)
    sub = lax.broadcasted_iota(jnp.int32, (HALO, FF_CHUNK), 0)

    def cols(c):
        return (slice(c * FF_CHUNK, (c + 1) * FF_CHUNK), slice(D_FF + c * FF_CHUNK, D_FF + (c + 1) * FF_CHUNK))

    def up(c):
        for half, cs in enumerate(cols(c)):
            lanes = slice(half * FF_CHUNK, (half + 1) * FF_CHUNK)
            u = _dot(lhs, wup_ref[:, cs])
            u_ref[c, HALO:HALO + TM, lanes] = u[:TM]
            prev_row, next_row = u[TM + HALO - 1:TM + HALO], u[TM + HALO:TM + HALO + 1]
            u_ref[c, 0:HALO, lanes] = jnp.where(sub == 0, prev_row, pltpu.roll(u[TM - HALO:TM], 1, 0))
            u_ref[c, HALO + TM:, lanes] = jnp.where(sub == HALO - 1, next_row, pltpu.roll(u[0:HALO], HALO - 1, 0))

    def both(ref, c):
        va, vg = cols(c)
        return jnp.concatenate([ref[:, va], ref[:, vg]], axis=1)

    acc = None
    up(0)
    up(1)
    for c in range(N_FF_CHUNKS):
        cw = both(cw_ref, c)
        t = (both(cb_ref, c) + u_ref[c, 0:TM, :] * cw[0:1] + u_ref[c, HALO:HALO + TM, :] * cw[1:2]
             + u_ref[c, 2 * HALO:2 * HALO + TM, :] * cw[2:3])
        a, g = t[:, :FF_CHUNK], t[:, FF_CHUNK:]
        act = ((g * (1.0 / (1.0 + jnp.exp(-g)))) * a).astype(BF16)
        act = _dot(perm_ref[1], act).astype(BF16)
        if c + 2 < N_FF_CHUNKS:
            up(c + 2)
        d = _dot(act, wdn_ref[c * FF_CHUNK:(c + 1) * FF_CHUNK, :])
        acc = d if acc is None else acc + d
    y = x + mod[:, 5 * D_MODEL:6 * D_MODEL] * acc
    if split_out:
        @pl.when(i < CTX_TILES)
        def _():
            rest[0][...] = y

        @pl.when(i >= CTX_TILES)
        def _():
            rest[1][...] = y
    else:
        rest[0][...] = y


def _row_permutations():
    r = np.arange(TM)
    token = r // HALO + (TM // HALO) * (r % HALO)
    p = (token[:, None] == np.arange(TM)[None, :]).astype(np.float32)
    return jnp.asarray(np.stack([p, p.T], axis=0), dtype=BF16)


def _ffn(l, x1, mods, norm2, perm, w_up, conv_w, conv_b, w_down, split_out):
    tok = pl.BlockSpec((TM, D_MODEL), lambda i: (i, 0))
    per = TM // HALO
    last = N_TOK // HALO - 1
    if split_out:
        out_specs = [pl.BlockSpec((TM, D_MODEL), lambda i: (jnp.minimum(i, CTX_TILES - 1), 0)),
                     pl.BlockSpec((TM, D_MODEL), lambda i: (jnp.maximum(i - CTX_TILES, 0), 0))]
        out_shape = [jax.ShapeDtypeStruct((N_CTX, D_MODEL), F32), jax.ShapeDtypeStruct((N_LAT, D_MODEL), F32)]
    else:
        out_specs, out_shape = tok, jax.ShapeDtypeStruct((N_TOK, D_MODEL), F32)
    return pl.pallas_call(
        functools.partial(_ffn_kernel, split_out=split_out),
        grid=(N_TILES,),
        in_specs=[
            tok,
            pl.BlockSpec((HALO, D_MODEL), lambda i: (jnp.maximum(i * per - 1, 0), 0)),
            pl.BlockSpec((HALO, D_MODEL), lambda i: (jnp.minimum((i + 1) * per, last), 0)),
            pl.BlockSpec((None, None, 1, 6 * D_MODEL), lambda i: (l, _tile_group(i), 0, 0)),
            _resident((None, 1, D_MODEL), lambda i: (l, 0, 0)),
            _resident((2, TM, TM), lambda i: (0, 0, 0)),
            _resident((None, D_MODEL, 2 * D_FF), lambda i: (l, 0, 0)),
            _resident((None, 3, 2 * D_FF), lambda i: (l, 0, 0)),
            _resident((None, 1, 2 * D_FF), lambda i: (l, 0, 0)),
            _resident((None, D_FF, D_MODEL), lambda i: (l, 0, 0)),
        ],
        out_specs=out_specs,
        out_shape=out_shape,
        scratch_shapes=[pltpu.VMEM((N_FF_CHUNKS, TM + 2 * HALO, 2 * FF_CHUNK), F32)],
        compiler_params=_cparams(("arbitrary",)),
        name=f"ffn_l{l}",
    )(x1, x1, x1, mods, norm2, perm, w_up, conv_w, conv_b, w_down)


def _cast_kernel(a_ref, o_ref):
    o_ref[...] = a_ref[...].astype(BF16)


def _cast_bf16(a, rows, name):
    _, r, c = a.shape
    spec = pl.BlockSpec((None, rows, c), lambda l, i: (l, i, 0))
    return pl.pallas_call(
        _cast_kernel, grid=(DEPTH, r // rows), in_specs=[spec], out_specs=spec,
        out_shape=jax.ShapeDtypeStruct(a.shape, BF16),
        compiler_params=_cparams(("parallel", "parallel")), name=name,
    )(a)


WPREP_COLS = 512
WPREP_QKV_BLOCKS = C_END // WPREP_COLS
WPREP_PLAIN = C_KC // WPREP_COLS
WPREP_BLOCKS = WPREP_QKV_BLOCKS + N_GATE // WPREP_COLS
_SRC_CKV = C_QD + 4 * MLA_QK


def _win_prep_start(j):
    return jnp.where(j < WPREP_PLAIN, WPREP_COLS * j,
                     jnp.where(j == WPREP_PLAIN, C_KC,
                               jnp.where(j == WPREP_PLAIN + 1, C_QD + 2 * MLA_QK,
                                         IN_QKV + WPREP_COLS * (j - WPREP_QKV_BLOCKS))))


def _win_prep_kernel(w_ref, q_ref, g_ref):
    j = pl.program_id(1)
    w = w_ref[0]
    zeros = lambda n: jnp.zeros((n, D_MODEL), F32)
    pad = zeros(LANES - MLA_QK)

    def two_heads(r0):
        return [w[r0:r0 + MLA_QK], pad, w[r0 + MLA_QK:r0 + 2 * MLA_QK], pad]

    @pl.when(j >= WPREP_QKV_BLOCKS)
    def _():
        g_ref[...] = w.T.astype(BF16)

    @pl.when(j < WPREP_PLAIN)
    def _():
        q_ref[...] = w.T.astype(BF16)

    @pl.when(j == WPREP_PLAIN)
    def _():
        n_kv = C_QD - C_KC
        q_ref[...] = jnp.concatenate([w[0:n_kv]] + two_heads(n_kv), axis=0).T.astype(BF16)

    @pl.when(j == WPREP_PLAIN + 1)
    def _():
        r_ckv = _SRC_CKV - (C_QD + 2 * MLA_QK)
        r_kpe = r_ckv + MLA_KV_LORA
        sel = two_heads(0) + [w[r_ckv:r_kpe], zeros(MLA_NOPE), w[r_kpe:r_kpe + MLA_ROPE], pad]
        q_ref[...] = jnp.concatenate(sel, axis=0).T.astype(BF16)


def _win_prep(w_in_t):
    window = pl.BlockSpec((pl.Element(1), pl.Element(WPREP_COLS), pl.Element(D_MODEL)),
                          lambda l, j: (l, pl.multiple_of(_win_prep_start(j), 32), 0))
    return pl.pallas_call(
        _win_prep_kernel,
        grid=(DEPTH, WPREP_BLOCKS),
        in_specs=[window],
        out_specs=[pl.BlockSpec((None, D_MODEL, WPREP_COLS), lambda l, j: (l, 0, jnp.minimum(j, WPREP_QKV_BLOCKS - 1))),
                   pl.BlockSpec((None, D_MODEL, WPREP_COLS), lambda l, j: (l, 0, jnp.maximum(j - WPREP_QKV_BLOCKS, 0)))],
        out_shape=[jax.ShapeDtypeStruct((DEPTH, D_MODEL, C_END), BF16),
                   jax.ShapeDtypeStruct((DEPTH, D_MODEL, N_GATE), BF16)],
        compiler_params=_cparams(("arbitrary", "arbitrary")),
        name="w_in_prep",
    )(w_in_t)


def _rope_tables():
    t = np.arange(DEC_SEQ)

    def angles(dim):
        half = dim // 2
        inv = ROPE_THETA ** (-np.arange(0, half, 2, dtype=np.float32) / half)
        row = (t // GRID_W).astype(np.float32)[:, None] * inv[None, :]
        col = (t % GRID_W).astype(np.float32)[:, None] * inv[None, :]
        return np.concatenate([row, row, col, col], axis=-1).astype(np.float32)

    def signed(ang, quarter):
        cos, sin = np.cos(ang), np.sin(ang)
        first = (np.arange(ang.shape[1]) % (2 * quarter)) < quarter
        return cos, np.where(first, -sin, 0.0), np.where(first, 0.0, sin)

    c64, lo64, hi64 = signed(angles(HEAD_DIM), 16)
    c64, lo64, hi64 = (np.tile(a, (1, 2)) for a in (c64, lo64, hi64))
    cm, lom, him = signed(angles(MLA_ROPE), 8)

    def mla_pad(a, fill):
        out = np.full((DEC_SEQ, LANES), fill, np.float32)
        out[:, MLA_NOPE:MLA_QK] = a
        return out

    tabs = [c64, lo64, hi64, mla_pad(cm, 1.0), mla_pad(lom, 0.0), mla_pad(him, 0.0)]
    ident = [np.ones, np.zeros, np.zeros, np.ones, np.zeros, np.zeros]
    full = [np.concatenate([f((TM_IN, LANES), np.float32), a.astype(np.float32)], axis=0) for f, a in zip(ident, tabs)]
    return jnp.asarray(np.stack(full, axis=0))


def _block_diag_ones():
    idx = np.arange(256)
    mats = [(idx[:, None] // w == idx[None, :] // w).astype(np.float32) for w in (HEAD_DIM, LANES)]
    return jnp.asarray(np.stack(mats, axis=0), dtype=BF16)


def _prep_small(w_ukv, qn_a, kn_a, qn_b, kn_b, qn_c, kn_c, qn_d, kn_d, kvn_d):
    ukv = w_ukv.reshape(DEPTH, MLA_KV_LORA, 4, 2, HEAD_DIM)
    k_part = jnp.pad(ukv[:, :, :, 0], ((0, 0), (0, 0), (0, 0), (0, HEAD_DIM))).reshape(DEPTH, MLA_KV_LORA, 512)
    v_part = ukv[:, :, :, 1].reshape(DEPTH, MLA_KV_LORA, 256)
    w_ukv_p = jnp.concatenate([k_part, v_part], axis=-1).astype(BF16)

    def row(g, reps, width=256):
        t = jnp.tile(g, (1, reps))
        return jnp.pad(t, ((0, 0), (0, width - t.shape[1])))

    pad96 = lambda g: jnp.pad(g, ((0, 0), (0, LANES - MLA_QK)))
    rows = [row(qn_a, 4), row(kn_a, 4), row(qn_b, 4), row(kn_b, 2), row(qn_c, 4), row(kn_c, 2),
            row(pad96(qn_d), 2), row(pad96(kn_d), 2), row(kvn_d, 1)]
    gains = jnp.stack(rows + [jnp.zeros_like(rows[0])] * (16 - len(rows)), axis=1)
    return w_ukv_p, gains


def kernel(x_prompt, x_sample, cache_nat_k, cache_nat_v, cache_gqa_k, cache_gqa_v, cache_win_k, cache_win_v,
           cache_mla_ckv, cache_mla_kpe, c, c_ctx, w_ada, b_ada, norm1, norm2, w_in, qn_a, kn_a, rpb_a,
           qn_b, kn_b, qn_c, kn_c, sink_c, qn_d, kn_d, kvn_d, w_ukv, w_branch, w_out, w_up, conv_w, conv_b,
           w_down):
    w_ukv_p, gains = _prep_small(w_ukv, qn_a, kn_a, qn_b, kn_b, qn_c, kn_c, qn_d, kn_d, kvn_d)
    w_qkv, w_gate = _win_prep(jnp.swapaxes(w_in, 1, 2))
    perm = _row_permutations()
    w_branch_b = _cast_bf16(w_branch.reshape(DEPTH, 4 * 256, D_MODEL), 1024, "cast_w_branch").reshape(
        DEPTH, 4, 256, D_MODEL)
    w_out_b = _cast_bf16(w_out, 1024, "cast_w_out")
    w_up_b = _cast_bf16(w_up, 256, "cast_w_up")
    w_down_b = _cast_bf16(w_down, D_FF // 4, "cast_w_down")
    conv_b_r = conv_b.reshape(DEPTH, 1, 2 * D_FF)
    rope_tab = _rope_tables()
    bd = _block_diag_ones()
    norm1_r = norm1.reshape(DEPTH, 1, D_MODEL)
    norm2_r = norm2.reshape(DEPTH, 1, D_MODEL)

    cond8 = jnp.zeros((8, D_MODEL), F32).at[0].set(c_ctx).at[1:1 + DEC_BATCH].set(c)
    mods = _adaln(cond8, w_ada, b_ada).reshape(DEPTH, 8, 1, 6 * D_MODEL)

    rpb_flat = jnp.pad(rpb_a.reshape(-1), (0, 8192 - rpb_a.size))
    na_bias = _na_bias(rpb_flat)
    kpe_p = jnp.pad(cache_mla_kpe, ((0, 0), (0, 0), (0, 0), (MLA_NOPE, LANES - MLA_QK)))
    kd_ctx, vd_ctx = _mla_ctx(cache_mla_ckv, kpe_p, w_ukv_p, gains, bd)
    c_nat_k = cache_nat_k.reshape(DEC_BATCH, DEPTH, PAST_LEN, 256)
    c_nat_v = cache_nat_v.reshape(DEC_BATCH, DEPTH, PAST_LEN, 256)
    c_gqa_k = cache_gqa_k.reshape(DEC_BATCH, DEPTH, PAST_LEN, LANES)
    c_gqa_v = cache_gqa_v.reshape(DEC_BATCH, DEPTH, PAST_LEN, LANES)
    c_win_k = cache_win_k.reshape(DEC_BATCH, DEPTH, PAST_LEN, LANES)
    c_win_v = cache_win_v.reshape(DEC_BATCH, DEPTH, PAST_LEN, LANES)

    x_ctx, x_lat = x_prompt.reshape(N_CTX, D_MODEL), x_sample.reshape(N_LAT, D_MODEL)
    caches = [jnp.zeros((BATCH, DEPTH) + blk, F32) for blk in _CACHE_BLOCKS]
    for l in range(DEPTH):
        qkv, caches = _inproj(l, x_ctx, x_lat, mods, norm1_r, w_qkv, w_ukv_p, gains, bd, rope_tab, caches)
        qa, ka, va, qb, kb, vb, qc, kc, vc, qd, kd, vd = qkv
        sink_l = sink_c[l]
        o_ctx = _ctx_attention(l, sink_l, qkv)
        o_lat = (_lat_na(l, qa, ka, va, c_nat_k, c_nat_v, na_bias),
                 _lat_gqa(l, qb, kb, vb, c_gqa_k, c_gqa_v),
                 _lat_window(l, sink_l, qc, kc, vc, c_win_k, c_win_v),
                 _lat_mla(l, qd, kd, vd, kd_ctx, vd_ctx))
        x1 = _merge(l, x_ctx, x_lat, o_ctx, o_lat, mods, norm1_r, w_gate, w_branch_b, w_out_b)
        x_ctx, x_lat = _ffn(l, x1, mods, norm2_r, perm, w_up_b, conv_w, conv_b_r, w_down_b, split_out=True)
    y_prompt = x_ctx.reshape(BATCH, SEQ, D_MODEL)
    y_sample = x_lat.reshape(DEC_BATCH, DEC_SEQ, D_MODEL)
    heads = (4, 4, 2, 2, 2, 2)
    outs = [a.reshape(BATCH, DEPTH, h, HEAD_DIM, SEQ).transpose(0, 1, 4, 2, 3) for a, h in zip(caches[:6], heads)]
    return (y_prompt, y_sample, *outs, jnp.swapaxes(caches[6], 2, 3), jnp.swapaxes(caches[7], 2, 3))
```

```python
import functools

import jax
import jax.numpy as jnp
import numpy as np
from jax import lax
from jax.experimental import pallas as pl
from jax.experimental.pallas import tpu as pltpu

F32 = jnp.float32
BF16 = jnp.bfloat16

D_MODEL = 1024
BATCH = 16
SEQ = 256
DEPTH = 4
DEC_BATCH = 2
DEC_SEQ = 2048
PAST_LEN = 256
GRID_W = 64
HEAD_DIM = 64
NA_ROWS = 8
NA_COLS = 16
WINDOW = 128
MLA_NOPE = 64
MLA_ROPE = 32
MLA_QK = MLA_NOPE + MLA_ROPE
MLA_KV_LORA = 128
D_FF = 2816
ROPE_THETA = 10000.0
EPS = 1e-6
NEG_INF = -1e30

LANES = 128
TM = 256
N_CTX = BATCH * SEQ
N_LAT = DEC_BATCH * DEC_SEQ
N_TOK = N_CTX + N_LAT
CTX_TILES = N_CTX // TM
LAT_TILES_PER_BATCH = DEC_SEQ // TM
N_TILES = N_TOK // TM
GRID_ROWS = DEC_SEQ // GRID_W
FF_CHUNK = 256
N_FF_CHUNKS = D_FF // FF_CHUNK
VMEM_LIMIT = 56 * 1024 * 1024

C_QA, C_KA, C_VA = 0, 256, 512
C_QB, C_KB, C_VB = 768, 1024, 1152
C_QC, C_KC, C_VC = 1280, 1536, 1664
C_QD, C_CKV, C_KPE, C_END = 1792, 2304, 2432, 2560
N_GATE = 4 * D_MODEL
IN_QKV = 2336
LOG2E = float(np.log2(np.e))
Q_SCALE = HEAD_DIM ** -0.5 * LOG2E
MLA_Q_SCALE = MLA_QK ** -0.5 * LOG2E


def _cparams(sem):
    return pltpu.CompilerParams(dimension_semantics=sem, vmem_limit_bytes=VMEM_LIMIT)


def _tile_group(i):
    return jnp.where(i < CTX_TILES, 0, 1 + (i - CTX_TILES) // LAT_TILES_PER_BATCH)


def _tile_pos_block(i):
    return jnp.where(i < CTX_TILES, 0, 1 + (i - CTX_TILES) % LAT_TILES_PER_BATCH)


def _resident(shape, index_map):
    return pl.BlockSpec(shape, index_map, pipeline_mode=pl.Buffered(1))


def _dot(a, b):
    return jnp.dot(a, b, preferred_element_type=F32)


def _dot_t(a, b):
    return lax.dot_general(a, b, (((1,), (1,)), ((), ())), preferred_element_type=F32)


def _rms_mod(x, gain, scale, shift):
    ms = jnp.mean(x * x, axis=-1, keepdims=True)
    return (x * lax.rsqrt(ms + EPS)) * gain * (1.0 + scale) + shift


def _group_norm(t, ones_bd, inv_n, gain):
    ss = _dot((t * t).astype(BF16), ones_bd)
    return t * lax.rsqrt(ss * inv_n + EPS) * gain


def _rope(t, cos, sin_lo, sin_hi, half):
    return (t * cos + pltpu.roll(t, LANES - half, 1) * sin_lo + pltpu.roll(t, half, 1) * sin_hi)


def _lane_iota(shape):
    return lax.broadcasted_iota(jnp.int32, shape, len(shape) - 1)


def _low_half(shape):
    return _lane_iota(shape) < HEAD_DIM


def _run_units(units, depth=2):
    n = len(units)
    pending = {}
    outs = []
    for k in range(min(depth, n)):
        pending[k] = units[k][0]()
    for k in range(n):
        outs.append(units[k][1](pending.pop(k)))
        if k + depth < n:
            pending[k + depth] = units[k + depth][0]()
    return outs


N_COND = 1 + DEC_BATCH


def _adaln_kernel(cond_ref, w_ref, b_ref, out_ref):
    cnd = cond_ref[...]
    s_t = (cnd * (1.0 / (1.0 + jnp.exp(-cnd)))).T
    w = w_ref[...]
    out_ref[...] = jnp.zeros(out_ref.shape, F32)
    for r in range(N_COND):
        out_ref[r:r + 1, :] = jnp.sum(s_t[:, r:r + 1] * w, axis=0, keepdims=True) + b_ref[...]


def _adaln(cond8, w_ada, b_ada):
    n_col = 6 * D_MODEL
    tn = 1024
    return pl.pallas_call(
        _adaln_kernel,
        grid=(DEPTH, n_col // tn),
        in_specs=[
            pl.BlockSpec((8, D_MODEL), lambda l, n: (0, 0)),
            pl.BlockSpec((None, D_MODEL, tn), lambda l, n: (l, 0, n)),
            pl.BlockSpec((None, 1, tn), lambda l, n: (l, 0, n)),
        ],
        out_specs=pl.BlockSpec((None, 8, tn), lambda l, n: (l, 0, n)),
        out_shape=jax.ShapeDtypeStruct((DEPTH, 8, n_col), F32),
        compiler_params=_cparams(("parallel", "parallel")),
        name="adaln",
    )(cond8, w_ada, b_ada.reshape(DEPTH, 1, n_col))


_QKV_WIDTHS = (512, 256, 256, 512, 128, 128, 512, 128, 128, 512, 512, 256)
_CACHE_WIDTHS = (256, 256, 128, 128, 128, 128, MLA_KV_LORA, MLA_ROPE)
_CACHE_BLOCKS = tuple((w, SEQ) for w in _CACHE_WIDTHS)


TM_IN = 2 * TM
IN_STEPS = N_TOK // TM_IN
IN_CTX_STEPS = N_CTX // TM_IN


def _inproj_kernel(xc_ref, xl_ref, mod_ref, n1_ref, w_ref, wukv_ref, g_ref, bd_ref, rope_ref, *rest):
    outs = rest[-(len(_QKV_WIDTHS) + len(_CACHE_WIDTHS)):]
    is_ctx = pl.program_id(0) >= IN_STEPS - IN_CTX_STEPS
    units = []
    for half in range(TM_IN // TM):
        rows = slice(half * TM, (half + 1) * TM)
        views = [o.at[rows] for o in outs[:len(_QKV_WIDTHS)]] + [o.at[half] for o in outs[len(_QKV_WIDTHS):]]
        units += _inproj_half(is_ctx, xc_ref.at[rows], xl_ref.at[rows], mod_ref, n1_ref, w_ref, wukv_ref, g_ref,
                              bd_ref, rope_ref.at[:, rows, :], views)
    _run_units(units, depth=2)


def _inproj_half(is_ctx, xc_ref, xl_ref, mod_ref, n1_ref, w_ref, wukv_ref, g_ref, bd_ref, rope_ref, outs):
    qa_o, ka_o, va_o, qb_o, kb_o, vb_o, qc_o, kc_o, vc_o, qd_o, kd_o, vd_o = outs[:len(_QKV_WIDTHS)]
    natk_o, natv_o, gqak_o, gqav_o, wink_o, winv_o, ckv_o, kpe_o = outs[len(_QKV_WIDTHS):]
    x = jnp.where(is_ctx, xc_ref[...], xl_ref[...])
    mod = mod_ref[...]
    hb = _rms_mod(x, n1_ref[...], mod[:, D_MODEL:2 * D_MODEL], mod[:, 0:D_MODEL]).astype(BF16)
    bd64, bd128 = bd_ref[0], bd_ref[1]
    cos64, slo64, shi64 = rope_ref[0], rope_ref[1], rope_ref[2]
    cosm, slom, shim = rope_ref[3], rope_ref[4], rope_ref[5]
    inv64, inv96 = 1.0 / HEAD_DIM, 1.0 / MLA_QK
    low = _low_half((TM, LANES))
    held = {}

    def chunks(t):
        return [t[:, c:c + LANES] for c in range(0, t.shape[1], LANES)]

    def rope64(t):
        parts = [_rope(p, cos64, slo64, shi64, 16) for p in chunks(t)]
        return parts[0] if len(parts) == 1 else jnp.concatenate(parts, axis=1)

    def ropem(t):
        return jnp.concatenate([_rope(p, cosm, slom, shim, 8) for p in chunks(t)], axis=1)

    def store_q(ref, q, grouped):
        q = q * Q_SCALE
        slots = []
        for hd in range(4):
            part, src = q[:, (hd // 2) * LANES:(hd // 2 + 1) * LANES], hd % 2
            dst = hd // 2 if grouped else src
            if src != dst:
                part = pltpu.roll(part, HEAD_DIM, 1)
            slots.append(jnp.where(low if dst == 0 else ~low, part, 0.0))
        ref[...] = jnp.concatenate(slots, axis=1).astype(BF16)

    def post_latent(t):
        ckv = _group_norm(t[:, :LANES], bd128[:LANES, :LANES], 1.0 / MLA_KV_LORA, g_ref[8:9, :LANES])
        kpe = t[:, LANES:]
        ckv_o[...] = ckv.T
        kpe_o[...] = kpe.T[MLA_NOPE:MLA_QK, :]
        held["kpe"] = kpe
        held["kv"] = _dot(ckv.astype(BF16), wukv_ref[...])

    def post_kd(_):
        kv, kpe = held["kv"], held["kpe"]
        for j in range(2):
            kf = jnp.concatenate([kv[:, (2 * j) * LANES:(2 * j + 1) * LANES] + kpe,
                                  kv[:, (2 * j + 1) * LANES:(2 * j + 2) * LANES] + kpe], axis=1)
            kd_o[:, 2 * j * LANES:(2 * j + 2) * LANES] = ropem(
                _group_norm(kf, bd128, inv96, g_ref[7:8, :])).astype(BF16)
        vd_o[...] = kv[:, 4 * LANES:].astype(BF16)

    def post_qa(t):
        store_q(qa_o, _group_norm(t, bd64, inv64, g_ref[0:1, :]), grouped=False)

    def post_ka(t):
        k = _group_norm(t, bd64, inv64, g_ref[1:2, :])
        natk_o[...] = k.T
        ka_o[...] = k.astype(BF16)

    def post_va(t):
        natv_o[...] = t.T
        va_o[...] = t.astype(BF16)

    def post_q_rope(ref, row):
        def post(t):
            store_q(ref, rope64(_group_norm(t, bd64, inv64, g_ref[row:row + 1, :])), grouped=True)
        return post

    def post_kv(k_ref, v_ref, kc_ref, vc_ref, row):
        def post(t):
            k = _group_norm(t[:, :LANES], bd64[:LANES, :LANES], inv64, g_ref[row:row + 1, :LANES])
            kc_ref[...] = k.T
            k_ref[...] = rope64(k).astype(BF16)
            v = t[:, LANES:]
            vc_ref[...] = v.T
            v_ref[...] = v.astype(BF16)
        return post

    def post_qd(j):
        def post(t):
            qd_o[:, 2 * j * LANES:(2 * j + 2) * LANES] = (ropem(
                _group_norm(t, bd128, inv96, g_ref[6:7, :])) * MLA_Q_SCALE).astype(BF16)
        return post

    stages = [
        ((C_CKV, C_END), post_latent),
        ((C_QA, C_KA), post_qa),
        (None, post_kd),
        ((C_QB, C_KB), post_q_rope(qb_o, 2)),
        ((C_QC, C_KC), post_q_rope(qc_o, 4)),
        ((C_QD, C_QD + 2 * LANES), post_qd(0)),
        ((C_QD + 2 * LANES, C_CKV), post_qd(1)),
        ((C_KB, C_QC), post_kv(kb_o, vb_o, gqak_o, gqav_o, 3)),
        ((C_KC, C_QD), post_kv(kc_o, vc_o, wink_o, winv_o, 5)),
        ((C_KA, C_VA), post_ka),
        ((C_VA, C_QB), post_va),
    ]

    def issue(cols):
        return None if cols is None else _dot(hb, w_ref[:, cols[0]:cols[1]])

    return [(functools.partial(issue, cols), post) for cols, post in stages]


def _ctx_block(i):
    return jnp.minimum(i, CTX_TILES - 1)


def _lat_block(i):
    return jnp.maximum(i - CTX_TILES, 0)


def _inproj(l, x_ctx, x_lat, mods, norm1, w_qkv, w_ukv_p, gains, bd, rope_tab, caches):
    n_lat = IN_STEPS - IN_CTX_STEPS
    blk = lambda s: (s + IN_CTX_STEPS) % IN_STEPS
    lat_blk = lambda s: jnp.maximum(blk(s) - IN_CTX_STEPS, 0)
    per_batch = DEC_SEQ // TM_IN
    tok = lambda w: pl.BlockSpec((TM_IN, w), lambda s: (blk(s), 0))
    in_specs = [
        pl.BlockSpec((TM_IN, D_MODEL), lambda s: (jnp.minimum(blk(s), IN_CTX_STEPS - 1), 0)),
        pl.BlockSpec((TM_IN, D_MODEL), lambda s: (lat_blk(s), 0)),
        pl.BlockSpec((None, None, 1, 6 * D_MODEL),
                     lambda s: (l, jnp.where(s < n_lat, 1 + lat_blk(s) // per_batch, 0), 0, 0)),
        _resident((None, 1, D_MODEL), lambda s: (l, 0, 0)),
        _resident((None, D_MODEL, C_END), lambda s: (l, 0, 0)),
        _resident((None, MLA_KV_LORA, 768), lambda s: (l, 0, 0)),
        _resident((None, 16, 256), lambda s: (l, 0, 0)),
        _resident((2, 256, 256), lambda s: (0, 0, 0)),
        pl.BlockSpec((6, TM_IN, LANES), lambda s: (0, jnp.where(s < n_lat, 1 + lat_blk(s) % per_batch, 0), 0)),
    ]
    args = [x_ctx, x_lat, mods, norm1, w_qkv, w_ukv_p, gains, bd, rope_tab] + list(caches)
    aliases = {len(in_specs) + n: len(_QKV_WIDTHS) + n for n in range(len(caches))}
    in_specs = in_specs + [pl.BlockSpec(memory_space=pl.ANY)] * len(caches)
    cache_spec = lambda cb: pl.BlockSpec((TM_IN // SEQ, None) + cb, lambda s: (jnp.maximum(s - n_lat, 0), l, 0, 0))
    outs = pl.pallas_call(
        _inproj_kernel,
        grid=(IN_STEPS,),
        in_specs=in_specs,
        out_specs=[tok(w) for w in _QKV_WIDTHS] + [cache_spec(blk) for blk in _CACHE_BLOCKS],
        out_shape=[jax.ShapeDtypeStruct((N_TOK, w), BF16) for w in _QKV_WIDTHS]
        + [jax.ShapeDtypeStruct((BATCH, DEPTH) + blk, F32) for blk in _CACHE_BLOCKS],
        input_output_aliases=aliases,
        compiler_params=_cparams(("arbitrary",)),
        name=f"inproj_l{l}",
    )(*args)
    return outs[:len(_QKV_WIDTHS)], outs[len(_QKV_WIDTHS):]


def _scores(q, keys):
    out = []
    for k, bias in keys:
        s = _dot_t(q, k)
        if bias is not None:
            s = s + bias
        out.append(s)
    return out


def _softmax_pv(scores, values, sink=None):
    m = functools.reduce(jnp.maximum, [jnp.max(s, axis=-1, keepdims=True) for s in scores])
    if sink is not None:
        m = jnp.maximum(m, sink)
    denom = None
    out = None
    for s, v in zip(scores, values):
        e = jnp.exp2(s - m)
        d = jnp.sum(e, axis=-1, keepdims=True)
        o = _dot(e.astype(BF16), v)
        denom = d if denom is None else denom + d
        out = o if out is None else out + o
    if sink is not None:
        denom = denom + jnp.exp2(sink - m)
    return out / denom


def _pair_unit(q_ref, rows, j, keys, values, kv_half, sinks=None):
    m_rows = rows.stop - rows.start

    def issue():
        q2 = jnp.concatenate([q_ref[rows, (2 * j) * LANES:(2 * j + 1) * LANES],
                              q_ref[rows, (2 * j + 1) * LANES:(2 * j + 2) * LANES]], axis=0)
        return _scores(q2, keys())

    def finish(scores):
        sink = None
        if sinks is not None:
            row = lax.broadcasted_iota(jnp.int32, (2 * m_rows, 1), 0)
            sink = jnp.where(row < m_rows, sinks[0], sinks[1])
        o2 = _softmax_pv(scores, values(), sink)
        lo, hi = o2[:m_rows], o2[m_rows:]
        if kv_half == 0:
            hi = pltpu.roll(hi, HEAD_DIM, 1)
        elif kv_half == 1:
            lo = pltpu.roll(lo, HEAD_DIM, 1)
        return jnp.where(_low_half(lo.shape), lo, hi)

    return issue, finish


def _single_unit(q_ref, rows, hd, keys, values):
    def issue():
        return _scores(q_ref[rows, hd * LANES:(hd + 1) * LANES], keys())

    def finish(scores):
        return _softmax_pv(scores, values())

    return issue, finish


def _merge_halves(o_even, o_odd):
    return jnp.where(_low_half(o_even.shape), o_even, o_odd)


def _ctx_attn_kernel(sink_ref, qa, ka, va, qb, kb, vb, qc, kc, vc, qd, kd, vd, o_ref):
    rows = slice(0, SEQ)
    units = []
    for j in range(2):
        c = slice(j * LANES, (j + 1) * LANES)
        units.append(_pair_unit(qa, rows, j, lambda c=c: [(ka[:, c], None)], lambda c=c: [va[:, c]], None))
    for q, k, v, with_sink in ((qb, kb, vb, False), (qc, kc, vc, True)):
        for j in range(2):
            sinks = (sink_ref[2 * j] * LOG2E, sink_ref[2 * j + 1] * LOG2E) if with_sink else None
            units.append(_pair_unit(q, rows, j, lambda k=k: [(k[...], None)], lambda v=v: [v[...]], j, sinks))
    for hd in range(4):
        c = slice(hd * LANES, (hd + 1) * LANES)
        vs = slice((hd // 2) * LANES, (hd // 2 + 1) * LANES)
        units.append(_single_unit(qd, rows, hd, lambda c=c: [(kd[:, c], None)], lambda vs=vs: [vd[:, vs]]))
    outs = _run_units(units)
    for n in range(6):
        o_ref[:, n * LANES:(n + 1) * LANES] = outs[n].astype(BF16)
    for j in range(2):
        o_ref[:, (6 + j) * LANES:(7 + j) * LANES] = _merge_halves(outs[6 + 2 * j], outs[7 + 2 * j]).astype(BF16)


def _ctx_attention(l, sink_c, qkv):
    blk = lambda w: pl.BlockSpec((SEQ, w), lambda b: (b, 0))
    return pl.pallas_call(
        _ctx_attn_kernel,
        grid=(BATCH,),
        in_specs=[pl.BlockSpec(memory_space=pltpu.SMEM)] + [blk(w) for w in _QKV_WIDTHS],
        out_specs=pl.BlockSpec((SEQ, D_MODEL), lambda b: (b, 0)),
        out_shape=jax.ShapeDtypeStruct((N_CTX, D_MODEL), BF16),
        compiler_params=_cparams(("parallel",)),
        name=f"ctx_attn_l{l}",
    )(sink_c, *qkv)


LAT_TQ = 256
LAT_SUB = 128
LAT_QBLOCKS = DEC_SEQ // LAT_TQ
LAT_ROW0 = N_CTX // LAT_TQ
LAT_KV0 = N_CTX // DEC_SEQ


def _lat_qspec(width):
    return pl.BlockSpec((LAT_TQ, width), lambda b, i: (LAT_ROW0 + b * LAT_QBLOCKS + i, 0))


def _lat_kvspec(width):
    return pl.BlockSpec((DEC_SEQ, width), lambda b, i: (LAT_KV0 + b, 0))


def _lat_call(kernel, l, name, in_specs, args):
    return pl.pallas_call(
        kernel,
        grid=(DEC_BATCH, LAT_QBLOCKS),
        in_specs=in_specs,
        out_specs=pl.BlockSpec((LAT_TQ, 256), lambda b, i: (b * LAT_QBLOCKS + i, 0)),
        out_shape=jax.ShapeDtypeStruct((N_LAT, 256), BF16),
        compiler_params=_cparams(("parallel", "parallel")),
        name=f"{name}_l{l}",
    )(*args)


def _sub_rows(u):
    return slice(u * LAT_SUB, (u + 1) * LAT_SUB)


def _store_units(o_ref, outs):
    for n, o in enumerate(outs):
        u, j = divmod(n, 2)
        o_ref[_sub_rows(u), j * LANES:(j + 1) * LANES] = o.astype(BF16)


def _gqa_kernel(q_ref, k_ref, v_ref, kc_ref, vc_ref, o_ref):
    keys = lambda: [(kc_ref[...].astype(BF16), None), (k_ref[...], None)]
    values = lambda: [vc_ref[...].astype(BF16), v_ref[...]]
    units = [_pair_unit(q_ref, _sub_rows(u), j, keys, values, j) for u in range(LAT_TQ // LAT_SUB) for j in range(2)]
    _store_units(o_ref, _run_units(units))


def _lat_gqa(l, qb, kb, vb, cache_k, cache_v):
    cspec = pl.BlockSpec((None, None, PAST_LEN, LANES), lambda b, i: (b, l, 0, 0))
    return _lat_call(_gqa_kernel, l, "lat_gqa",
                     [_lat_qspec(512), _lat_kvspec(LANES), _lat_kvspec(LANES), cspec, cspec],
                     (qb, kb, vb, cache_k, cache_v))


WIN_KEYS = 3 * LAT_SUB


def _win_kernel(sink_ref, q_ref, k_ref, v_ref, kc_ref, vc_ref, o_ref):
    units = []
    for u in range(LAT_TQ // LAT_SUB):
        blk = pl.program_id(1) * (LAT_TQ // LAT_SUB) + u
        start = pl.multiple_of(jnp.clip((blk - 1) * LAT_SUB, 0, DEC_SEQ - WIN_KEYS), LAT_SUB)

        def keys(blk=blk, start=start):
            row = lax.broadcasted_iota(jnp.int32, (2 * LAT_SUB, WIN_KEYS), 0)
            qpos = blk * LAT_SUB + jnp.where(row < LAT_SUB, row, row - LAT_SUB)
            kpos = start + lax.broadcasted_iota(jnp.int32, (2 * LAT_SUB, WIN_KEYS), 1)
            bias = jnp.where(jnp.abs(qpos - kpos) <= WINDOW, 0.0, NEG_INF)
            return [(kc_ref[...].astype(BF16), None), (k_ref[pl.ds(start, WIN_KEYS), :], bias)]

        def values(start=start):
            return [vc_ref[...].astype(BF16), v_ref[pl.ds(start, WIN_KEYS), :]]

        for j in range(2):
            sinks = (sink_ref[2 * j] * LOG2E, sink_ref[2 * j + 1] * LOG2E)
            units.append(_pair_unit(q_ref, _sub_rows(u), j, keys, values, j, sinks))
    _store_units(o_ref, _run_units(units))


def _lat_window(l, sink_c, qc, kc, vc, cache_k, cache_v):
    cspec = pl.BlockSpec((None, None, PAST_LEN, LANES), lambda b, i: (b, l, 0, 0))
    return _lat_call(_win_kernel, l, "lat_win",
                     [pl.BlockSpec(memory_space=pltpu.SMEM), _lat_qspec(512), _lat_kvspec(LANES),
                      _lat_kvspec(LANES), cspec, cspec],
                     (sink_c, qc, kc, vc, cache_k, cache_v))


def _mla_kernel(q_ref, k_ref, v_ref, kc_ref, vc_ref, o_ref):
    units = []
    rows = slice(0, LAT_TQ)
    for hd in range(4):
        c = slice(hd * LANES, (hd + 1) * LANES)
        vs = slice((hd // 2) * LANES, (hd // 2 + 1) * LANES)
        units.append(_single_unit(q_ref, rows, hd,
                                  lambda c=c: [(kc_ref[:, c], None), (k_ref[:, c], None)],
                                  lambda vs=vs: [vc_ref[:, vs], v_ref[:, vs]]))
    outs = _run_units(units)
    for j in range(2):
        o_ref[:, j * LANES:(j + 1) * LANES] = _merge_halves(outs[2 * j], outs[2 * j + 1]).astype(BF16)


def _lat_mla(l, qd, kd, vd, kd_ctx, vd_ctx):
    kcs = pl.BlockSpec((None, None, PAST_LEN, 512), lambda b, i: (l, b, 0, 0))
    vcs = pl.BlockSpec((None, None, PAST_LEN, 256), lambda b, i: (l, b, 0, 0))
    return _lat_call(_mla_kernel, l, "lat_mla",
                     [_lat_qspec(512), _lat_kvspec(512), _lat_kvspec(256), kcs, vcs],
                     (qd, kd, vd, kd_ctx, vd_ctx))


NA_KEYS = NA_ROWS * GRID_W
NA_ROWS_PER_STEP = LAT_TQ // GRID_W


def _na_shift(r):
    return jnp.where(r < NA_ROWS // 2, r,
                     jnp.where(r > GRID_ROWS - NA_ROWS // 2, r - (GRID_ROWS - NA_ROWS), NA_ROWS // 2))


def _na_kernel(q_ref, k_ref, v_ref, kc_ref, vc_ref, bias_ref, o_ref):
    units = []
    for rr in range(NA_ROWS_PER_STEP):
        r = pl.program_id(1) * NA_ROWS_PER_STEP + rr
        start = pl.multiple_of(jnp.clip(r - NA_ROWS // 2, 0, GRID_ROWS - NA_ROWS) * GRID_W, GRID_W)
        shift = _na_shift(r)
        rows = slice(rr * GRID_W, (rr + 1) * GRID_W)
        for j in range(2):
            c = slice(j * LANES, (j + 1) * LANES)

            def keys(j=j, c=c, start=start, shift=shift):
                bias = jnp.concatenate([bias_ref[2 * j, shift], bias_ref[2 * j + 1, shift]], axis=0)
                return [(kc_ref[:, c].astype(BF16), None), (k_ref[pl.ds(start, NA_KEYS), c], bias)]

            def values(c=c, start=start):
                return [vc_ref[:, c].astype(BF16), v_ref[pl.ds(start, NA_KEYS), c]]

            units.append(_pair_unit(q_ref, rows, j, keys, values, None))
    for n, o in enumerate(_run_units(units)):
        rr, j = divmod(n, 2)
        o_ref[rr * GRID_W:(rr + 1) * GRID_W, j * LANES:(j + 1) * LANES] = o.astype(BF16)


def _lat_na(l, qa, ka, va, cache_k, cache_v, na_bias):
    cspec = pl.BlockSpec((None, None, PAST_LEN, 256), lambda b, i: (b, l, 0, 0))
    bspec = _resident((None, 4, NA_ROWS, GRID_W, NA_KEYS), lambda b, i: (l, 0, 0, 0, 0))
    return _lat_call(_na_kernel, l, "lat_na",
                     [_lat_qspec(512), _lat_kvspec(256), _lat_kvspec(256), cspec, cspec, bspec],
                     (qa, ka, va, cache_k, cache_v, na_bias))


def _na_bias_kernel(rpb_ref, out_ref, t_ref):
    l = pl.program_id(0)
    n_dr, n_dc = 2 * NA_ROWS - 1, 2 * NA_COLS - 1
    shape = (GRID_W, LANES)
    qc = lax.broadcasted_iota(jnp.int32, shape, 0)
    lane = _lane_iota(shape)
    kc = lane & (GRID_W - 1)
    dc = jnp.clip(kc - qc, 1 - NA_COLS, NA_COLS - 1) + (NA_COLS - 1)
    c0 = jnp.clip(qc - NA_COLS // 2, 0, GRID_W - NA_COLS)
    ok = (kc >= c0) & (kc < c0 + NA_COLS)
    for hd in range(4):
        def fill(a, carry):
            base = ((l * 4 + hd) * n_dr + a) * n_dc
            t = jnp.zeros(shape, F32)
            for b in range(n_dc):
                t = jnp.where(dc == b, rpb_ref[base + b], t)
            t_ref[a] = jnp.where(ok, t * LOG2E, NEG_INF)
            return carry
        lax.fori_loop(0, n_dr, fill, 0)
        for d in range(NA_ROWS):
            for m in range(NA_ROWS // 2):
                lo = t_ref[2 * m - d + NA_ROWS - 1]
                hi = t_ref[2 * m + 1 - d + NA_ROWS - 1]
                out_ref[hd, d, :, m * LANES:(m + 1) * LANES] = jnp.where(lane < GRID_W, lo, hi)


def _na_bias(rpb_flat):
    return pl.pallas_call(
        _na_bias_kernel,
        grid=(DEPTH,),
        in_specs=[pl.BlockSpec(memory_space=pltpu.SMEM)],
        out_specs=pl.BlockSpec((None, 4, NA_ROWS, GRID_W, NA_KEYS), lambda l: (l, 0, 0, 0, 0)),
        out_shape=jax.ShapeDtypeStruct((DEPTH, 4, NA_ROWS, GRID_W, NA_KEYS), F32),
        scratch_shapes=[pltpu.VMEM((2 * NA_ROWS - 1, GRID_W, LANES), F32)],
        compiler_params=_cparams(("parallel",)),
        name="na_bias",
    )(rpb_flat)


def _mla_ctx_kernel(ckv_ref, kpe_ref, wukv_ref, g_ref, bd_ref, kd_o, vd_o):
    kv = _dot(ckv_ref[...].astype(BF16), wukv_ref[...])
    kpe = kpe_ref[...]
    for j in range(2):
        kf = jnp.concatenate([kv[:, (2 * j) * LANES:(2 * j + 1) * LANES] + kpe,
                              kv[:, (2 * j + 1) * LANES:(2 * j + 2) * LANES] + kpe], axis=1)
        kd_o[:, 2 * j * LANES:(2 * j + 2) * LANES] = _group_norm(
            kf, bd_ref[1], 1.0 / MLA_QK, g_ref[7:8, :]).astype(BF16)
    vd_o[...] = kv[:, 4 * LANES:].astype(BF16)


def _mla_ctx(cache_ckv, cache_kpe_p, w_ukv_p, gains, bd):
    return pl.pallas_call(
        _mla_ctx_kernel,
        grid=(DEPTH, DEC_BATCH),
        in_specs=[
            pl.BlockSpec((None, None, PAST_LEN, LANES), lambda l, b: (b, l, 0, 0)),
            pl.BlockSpec((None, None, PAST_LEN, LANES), lambda l, b: (b, l, 0, 0)),
            pl.BlockSpec((None, MLA_KV_LORA, 768), lambda l, b: (l, 0, 0)),
            pl.BlockSpec((None, 16, 256), lambda l, b: (l, 0, 0)),
            pl.BlockSpec((2, 256, 256), lambda l, b: (0, 0, 0)),
        ],
        out_specs=[pl.BlockSpec((None, None, PAST_LEN, 512), lambda l, b: (l, b, 0, 0)),
                   pl.BlockSpec((None, None, PAST_LEN, 256), lambda l, b: (l, b, 0, 0))],
        out_shape=[jax.ShapeDtypeStruct((DEPTH, DEC_BATCH, PAST_LEN, 512), BF16),
                   jax.ShapeDtypeStruct((DEPTH, DEC_BATCH, PAST_LEN, 256), BF16)],
        compiler_params=_cparams(("parallel", "parallel")),
        name="mla_ctx",
    )(cache_ckv, cache_kpe_p, w_ukv_p, gains, bd)


def _merge_kernel(xc_ref, xl_ref, oc_ref, oa_ref, ob_ref, occ_ref, od_ref, mod_ref, n1_ref, wg_ref, wb_ref, wo_ref,
                  x1_ref):
    is_ctx = pl.program_id(0) < IN_CTX_STEPS
    mod = mod_ref[...]
    lat_refs = (oa_ref, ob_ref, occ_ref, od_ref)

    def half_units(half):
        rows = slice(half * TM, (half + 1) * TM)
        st = {"merged": None}
        st["x"] = jnp.where(is_ctx, xc_ref[rows, :], xl_ref[rows, :])
        st["hb"] = _rms_mod(st["x"], n1_ref[...], mod[:, D_MODEL:2 * D_MODEL], mod[:, 0:D_MODEL]).astype(BF16)

        def branch(n):
            def issue():
                o = jnp.where(is_ctx, oc_ref[rows, n * 256:(n + 1) * 256], lat_refs[n][rows, :])
                return _dot(st["hb"], wg_ref[:, n * D_MODEL:(n + 1) * D_MODEL]), _dot(o, wb_ref[n])

            def finish(res):
                gate, br = res
                t = (1.0 / (1.0 + jnp.exp(-gate))) * br
                st["merged"] = t if st["merged"] is None else st["merged"] + t
            return issue, finish

        def out_issue():
            return _dot(st["merged"].astype(BF16), wo_ref[...])

        def out_finish(out):
            x1_ref[rows, :] = st["x"] + mod[:, 2 * D_MODEL:3 * D_MODEL] * out

        return [branch(n) for n in range(4)], (out_issue, out_finish)

    (a_br, a_out), (b_br, b_out) = half_units(0), half_units(1)
    idle = (lambda: None, lambda _: None)
    _run_units(a_br + b_br[:2] + [a_out] + b_br[2:] + [idle, b_out], depth=2)


def _merge(l, x_ctx, x_lat, o_ctx, o_lat, mods, norm1, w_gate, w_branch, w_out):
    per_batch = DEC_SEQ // TM_IN
    ctx_blk = lambda i: jnp.minimum(i, IN_CTX_STEPS - 1)
    lat_blk = lambda i: jnp.maximum(i - IN_CTX_STEPS, 0)
    tok = pl.BlockSpec((TM_IN, D_MODEL), lambda i: (i, 0))
    ctx_tok = pl.BlockSpec((TM_IN, D_MODEL), lambda i: (ctx_blk(i), 0))
    lat_tok = lambda w: pl.BlockSpec((TM_IN, w), lambda i: (lat_blk(i), 0))
    return pl.pallas_call(
        _merge_kernel,
        grid=(IN_STEPS,),
        in_specs=[
            ctx_tok, lat_tok(D_MODEL), ctx_tok, lat_tok(256), lat_tok(256), lat_tok(256), lat_tok(256),
            pl.BlockSpec((None, None, 1, 6 * D_MODEL),
                         lambda i: (l, jnp.where(i < IN_CTX_STEPS, 0, 1 + lat_blk(i) // per_batch), 0, 0)),
            _resident((None, 1, D_MODEL), lambda i: (l, 0, 0)),
            _resident((None, D_MODEL, N_GATE), lambda i: (l, 0, 0)),
            _resident((None, 4, 256, D_MODEL), lambda i: (l, 0, 0, 0)),
            _resident((None, D_MODEL, D_MODEL), lambda i: (l, 0, 0)),
        ],
        out_specs=tok,
        out_shape=jax.ShapeDtypeStruct((N_TOK, D_MODEL), F32),
        compiler_params=_cparams(("parallel",)),
        name=f"merge_l{l}",
    )(x_ctx, x_lat, o_ctx, *o_lat, mods, norm1, w_gate, w_branch, w_out)


HALO = 8


def _sublane_transpose(tiles):
    sub = lax.broadcasted_iota(jnp.int32, tiles[0].shape, 0)
    cur = list(tiles)
    for d in (4, 2, 1):
        keep = (sub & d) == 0
        nxt = list(cur)
        for a in range(HALO):
            if a & d:
                continue
            nxt[a] = jnp.where(keep, cur[a], pltpu.roll(cur[a + d], d, 0))
            nxt[a + d] = jnp.where(keep, pltpu.roll(cur[a], HALO - d, 0), cur[a + d])
        cur = nxt
    return cur


ROW_TILES = TM // HALO
PERM_GROUPS = ROW_TILES // HALO


def _permute_rows(x, to_permuted):
    tile = lambda j: x[j * HALO:(j + 1) * HALO, :]
    out = [None] * ROW_TILES
    for m in range(PERM_GROUPS):
        nat = [m + PERM_GROUPS * s for s in range(HALO)]
        perm = [HALO * m + i for i in range(HALO)]
        src, dst = (nat, perm) if to_permuted else (perm, nat)
        for j, t in zip(dst, _sublane_transpose([tile(j) for j in src])):
            out[j] = t
    return jnp.concatenate(out, axis=0)


def _ffn_tile(x, x_prev, x_next, has_prev, has_next, mod, n2, wup_ref, cw_ref, cb_ref, wdn_ref, u_ref):
    sh2, sc2 = mod[:, 3 * D_MODEL:4 * D_MODEL], mod[:, 4 * D_MODEL:5 * D_MODEL]
    h_prev = jnp.where(has_prev, _rms_mod(x_prev, n2, sc2, sh2), 0.0)
    h_next = jnp.where(has_next, _rms_mod(x_next, n2, sc2, sh2), 0.0)
    hp = _permute_rows(_rms_mod(x, n2, sc2, sh2), True)
    lhs = jnp.concatenate([hp, h_prev, h_next], axis=0).astype(BF16)
    sub = lax.broadcasted_iota(jnp.int32, (HALO, FF_CHUNK), 0)

    def cols(c):
        return (slice(c * FF_CHUNK, (c + 1) * FF_CHUNK), slice(D_FF + c * FF_CHUNK, D_FF + (c + 1) * FF_CHUNK))

    def up(c):
        for half, cs in enumerate(cols(c)):
            lanes = slice(half * FF_CHUNK, (half + 1) * FF_CHUNK)
            u = _dot(lhs, wup_ref[:, cs])
            u_ref[c, HALO:HALO + TM, lanes] = u[:TM]
            prev_row, next_row = u[TM + HALO - 1:TM + HALO], u[TM + HALO:TM + HALO + 1]
            u_ref[c, 0:HALO, lanes] = jnp.where(sub == 0, prev_row, pltpu.roll(u[TM - HALO:TM], 1, 0))
            u_ref[c, HALO + TM:, lanes] = jnp.where(sub == HALO - 1, next_row, pltpu.roll(u[0:HALO], HALO - 1, 0))

    def both(ref, c):
        va, vg = cols(c)
        return jnp.concatenate([ref[:, va], ref[:, vg]], axis=1)

    state = {"acc": None}

    def chunk(c, between):
        cw = both(cw_ref, c)
        t = (both(cb_ref, c) + u_ref[c, 0:TM, :] * cw[0:1] + u_ref[c, HALO:HALO + TM, :] * cw[1:2]
             + u_ref[c, 2 * HALO:2 * HALO + TM, :] * cw[2:3])
        a, g = t[:, :FF_CHUNK], t[:, FF_CHUNK:]
        act = ((g * (1.0 / (1.0 + jnp.exp(-g)))) * a).astype(BF16)
        between()
        d = _dot(act, wdn_ref[c * FF_CHUNK:(c + 1) * FF_CHUNK, :])
        state["acc"] = d if state["acc"] is None else state["acc"] + d

    def finish():
        return x + mod[:, 5 * D_MODEL:6 * D_MODEL] * _permute_rows(state["acc"], False)

    return up, chunk, finish


def _ffn_kernel(x_ref, xp_ref, xn_ref, mod_ref, n2_ref, wup_ref, cw_ref, cb_ref, wdn_ref, yc_ref, yl_ref, u_ref):
    i = pl.program_id(0)
    latent = i >= IN_CTX_STEPS
    first = (i - IN_CTX_STEPS) % (DEC_SEQ // TM_IN) == 0
    last = (i - IN_CTX_STEPS) % (DEC_SEQ // TM_IN) == DEC_SEQ // TM_IN - 1
    mod = mod_ref[...]
    n2 = n2_ref[...]
    make = [
        lambda: _ffn_tile(x_ref[0:TM, :], xp_ref[...], x_ref[TM:TM + HALO, :], latent & ~first, latent, mod, n2,
                          wup_ref, cw_ref, cb_ref, wdn_ref, u_ref.at[0]),
        lambda: _ffn_tile(x_ref[TM:, :], x_ref[TM - HALO:TM, :], xn_ref[...], latent, latent & ~last, mod, n2,
                          wup_ref, cw_ref, cb_ref, wdn_ref, u_ref.at[1]),
    ]
    tiles = {}

    def tile(t):
        if t not in tiles:
            tiles[t] = make[t]()
        return tiles[t]

    tasks = [(t, c) for t in range(len(make)) for c in range(N_FF_CHUNKS)]
    early = N_FF_CHUNKS // 2 + 1
    ups = tasks[:early] + [tasks[N_FF_CHUNKS]] + tasks[early:N_FF_CHUNKS] + tasks[N_FF_CHUNKS + 1:]
    depth = 3
    for t, c in ups[:depth]:
        tile(t)[0](c)
    ys = []
    for n, (t, c) in enumerate(tasks):
        ahead = ups[n + depth] if n + depth < len(ups) else None
        tile(t)[1](c, (lambda a=ahead: tile(a[0])[0](a[1])) if ahead else (lambda: None))
        if c == N_FF_CHUNKS - 1:
            ys.append(tile(t)[2]())
    y = jnp.concatenate(ys, axis=0)

    @pl.when(i < IN_CTX_STEPS)
    def _():
        yc_ref[...] = y

    @pl.when(latent)
    def _():
        yl_ref[...] = y


def _ffn(l, x1, mods, norm2, w_up, conv_w, conv_b, w_down):
    per = TM_IN // HALO
    last = N_TOK // HALO - 1
    per_batch = DEC_SEQ // TM_IN
    lat_blk = lambda i: jnp.maximum(i - IN_CTX_STEPS, 0)
    return pl.pallas_call(
        _ffn_kernel,
        grid=(IN_STEPS,),
        in_specs=[
            pl.BlockSpec((TM_IN, D_MODEL), lambda i: (i, 0)),
            pl.BlockSpec((HALO, D_MODEL), lambda i: (jnp.maximum(i * per - 1, 0), 0)),
            pl.BlockSpec((HALO, D_MODEL), lambda i: (jnp.minimum((i + 1) * per, last), 0)),
            pl.BlockSpec((None, None, 1, 6 * D_MODEL),
                         lambda i: (l, jnp.where(i < IN_CTX_STEPS, 0, 1 + lat_blk(i) // per_batch), 0, 0)),
            _resident((None, 1, D_MODEL), lambda i: (l, 0, 0)),
            _resident((None, D_MODEL, 2 * D_FF), lambda i: (l, 0, 0)),
            _resident((None, 3, 2 * D_FF), lambda i: (l, 0, 0)),
            _resident((None, 1, 2 * D_FF), lambda i: (l, 0, 0)),
            _resident((None, D_FF, D_MODEL), lambda i: (l, 0, 0)),
        ],
        out_specs=[pl.BlockSpec((TM_IN, D_MODEL), lambda i: (jnp.minimum(i, IN_CTX_STEPS - 1), 0)),
                   pl.BlockSpec((TM_IN, D_MODEL), lambda i: (lat_blk(i), 0))],
        out_shape=[jax.ShapeDtypeStruct((N_CTX, D_MODEL), F32), jax.ShapeDtypeStruct((N_LAT, D_MODEL), F32)],
        scratch_shapes=[pltpu.VMEM((TM_IN // TM, N_FF_CHUNKS, TM + 2 * HALO, 2 * FF_CHUNK), F32)],
        compiler_params=_cparams(("arbitrary",)),
        name=f"ffn_l{l}",
    )(x1, x1, x1, mods, norm2, w_up, conv_w, conv_b, w_down)


def _cast_kernel(a_ref, o_ref):
    o_ref[...] = a_ref[...].astype(BF16)


def _cast_bf16(a, rows, name):
    _, r, c = a.shape
    spec = pl.BlockSpec((None, rows, c), lambda l, i: (l, i, 0))
    return pl.pallas_call(
        _cast_kernel, grid=(DEPTH, r // rows), in_specs=[spec], out_specs=spec,
        out_shape=jax.ShapeDtypeStruct(a.shape, BF16),
        compiler_params=_cparams(("parallel", "parallel")), name=name,
    )(a)


WPREP_COLS = 512
WPREP_QKV_BLOCKS = C_END // WPREP_COLS
WPREP_PLAIN = C_KC // WPREP_COLS
WPREP_BLOCKS = WPREP_QKV_BLOCKS + N_GATE // WPREP_COLS
_SRC_CKV = C_QD + 4 * MLA_QK


def _win_prep_start(j):
    return jnp.where(j < WPREP_PLAIN, WPREP_COLS * j,
                     jnp.where(j == WPREP_PLAIN, C_KC,
                               jnp.where(j == WPREP_PLAIN + 1, C_QD + 2 * MLA_QK,
                                         IN_QKV + WPREP_COLS * (j - WPREP_QKV_BLOCKS))))


def _win_prep_kernel(w_ref, q_ref, g_ref):
    j = pl.program_id(1)
    w = w_ref[0]
    zeros = lambda n: jnp.zeros((n, D_MODEL), F32)
    pad = zeros(LANES - MLA_QK)

    def two_heads(r0):
        return [w[r0:r0 + MLA_QK], pad, w[r0 + MLA_QK:r0 + 2 * MLA_QK], pad]

    @pl.when(j >= WPREP_QKV_BLOCKS)
    def _():
        g_ref[...] = w.T.astype(BF16)

    @pl.when(j < WPREP_PLAIN)
    def _():
        q_ref[...] = w.T.astype(BF16)

    @pl.when(j == WPREP_PLAIN)
    def _():
        n_kv = C_QD - C_KC
        q_ref[...] = jnp.concatenate([w[0:n_kv]] + two_heads(n_kv), axis=0).T.astype(BF16)

    @pl.when(j == WPREP_PLAIN + 1)
    def _():
        r_ckv = _SRC_CKV - (C_QD + 2 * MLA_QK)
        r_kpe = r_ckv + MLA_KV_LORA
        sel = two_heads(0) + [w[r_ckv:r_kpe], zeros(MLA_NOPE), w[r_kpe:r_kpe + MLA_ROPE], pad]
        q_ref[...] = jnp.concatenate(sel, axis=0).T.astype(BF16)


def _win_prep(w_in_t):
    window = pl.BlockSpec((pl.Element(1), pl.Element(WPREP_COLS), pl.Element(D_MODEL)),
                          lambda l, j: (l, pl.multiple_of(_win_prep_start(j), 32), 0))
    return pl.pallas_call(
        _win_prep_kernel,
        grid=(DEPTH, WPREP_BLOCKS),
        in_specs=[window],
        out_specs=[pl.BlockSpec((None, D_MODEL, WPREP_COLS), lambda l, j: (l, 0, jnp.minimum(j, WPREP_QKV_BLOCKS - 1))),
                   pl.BlockSpec((None, D_MODEL, WPREP_COLS), lambda l, j: (l, 0, jnp.maximum(j - WPREP_QKV_BLOCKS, 0)))],
        out_shape=[jax.ShapeDtypeStruct((DEPTH, D_MODEL, C_END), BF16),
                   jax.ShapeDtypeStruct((DEPTH, D_MODEL, N_GATE), BF16)],
        compiler_params=_cparams(("arbitrary", "arbitrary")),
        name="w_in_prep",
    )(w_in_t)


def _rope_tables():
    t = np.arange(DEC_SEQ)

    def angles(dim):
        half = dim // 2
        inv = ROPE_THETA ** (-np.arange(0, half, 2, dtype=np.float32) / half)
        row = (t // GRID_W).astype(np.float32)[:, None] * inv[None, :]
        col = (t % GRID_W).astype(np.float32)[:, None] * inv[None, :]
        return np.concatenate([row, row, col, col], axis=-1).astype(np.float32)

    def signed(ang, quarter):
        cos, sin = np.cos(ang), np.sin(ang)
        first = (np.arange(ang.shape[1]) % (2 * quarter)) < quarter
        return cos, np.where(first, -sin, 0.0), np.where(first, 0.0, sin)

    c64, lo64, hi64 = signed(angles(HEAD_DIM), 16)
    c64, lo64, hi64 = (np.tile(a, (1, 2)) for a in (c64, lo64, hi64))
    cm, lom, him = signed(angles(MLA_ROPE), 8)

    def mla_pad(a, fill):
        out = np.full((DEC_SEQ, LANES), fill, np.float32)
        out[:, MLA_NOPE:MLA_QK] = a
        return out

    tabs = [c64, lo64, hi64, mla_pad(cm, 1.0), mla_pad(lom, 0.0), mla_pad(him, 0.0)]
    ident = [np.ones, np.zeros, np.zeros, np.ones, np.zeros, np.zeros]
    full = [np.concatenate([f((TM_IN, LANES), np.float32), a.astype(np.float32)], axis=0) for f, a in zip(ident, tabs)]
    return jnp.asarray(np.stack(full, axis=0))


def _block_diag_ones():
    idx = np.arange(256)
    mats = [(idx[:, None] // w == idx[None, :] // w).astype(np.float32) for w in (HEAD_DIM, LANES)]
    return jnp.asarray(np.stack(mats, axis=0), dtype=BF16)


def _prep_small(w_ukv, qn_a, kn_a, qn_b, kn_b, qn_c, kn_c, qn_d, kn_d, kvn_d):
    ukv = w_ukv.reshape(DEPTH, MLA_KV_LORA, 4, 2, HEAD_DIM)
    k_part = jnp.pad(ukv[:, :, :, 0], ((0, 0), (0, 0), (0, 0), (0, HEAD_DIM))).reshape(DEPTH, MLA_KV_LORA, 512)
    v_part = ukv[:, :, :, 1].reshape(DEPTH, MLA_KV_LORA, 256)
    w_ukv_p = jnp.concatenate([k_part, v_part], axis=-1).astype(BF16)

    def row(g, reps, width=256):
        t = jnp.tile(g, (1, reps))
        return jnp.pad(t, ((0, 0), (0, width - t.shape[1])))

    pad96 = lambda g: jnp.pad(g, ((0, 0), (0, LANES - MLA_QK)))
    rows = [row(qn_a, 4), row(kn_a, 4), row(qn_b, 4), row(kn_b, 2), row(qn_c, 4), row(kn_c, 2),
            row(pad96(qn_d), 2), row(pad96(kn_d), 2), row(kvn_d, 1)]
    gains = jnp.stack(rows + [jnp.zeros_like(rows[0])] * (16 - len(rows)), axis=1)
    return w_ukv_p, gains


def kernel(x_prompt, x_sample, cache_nat_k, cache_nat_v, cache_gqa_k, cache_gqa_v, cache_win_k, cache_win_v,
           cache_mla_ckv, cache_mla_kpe, c, c_ctx, w_ada, b_ada, norm1, norm2, w_in, qn_a, kn_a, rpb_a,
           qn_b, kn_b, qn_c, kn_c, sink_c, qn_d, kn_d, kvn_d, w_ukv, w_branch, w_out, w_up, conv_w, conv_b,
           w_down):
    w_ukv_p, gains = _prep_small(w_ukv, qn_a, kn_a, qn_b, kn_b, qn_c, kn_c, qn_d, kn_d, kvn_d)
    w_qkv, w_gate = _win_prep(jnp.swapaxes(w_in, 1, 2))
    w_branch_b = _cast_bf16(w_branch.reshape(DEPTH, 4 * 256, D_MODEL), 1024, "cast_w_branch").reshape(
        DEPTH, 4, 256, D_MODEL)
    w_out_b = _cast_bf16(w_out, 1024, "cast_w_out")
    w_up_b = _cast_bf16(w_up, 256, "cast_w_up")
    w_down_b = _cast_bf16(w_down, D_FF // 4, "cast_w_down")
    conv_b_r = conv_b.reshape(DEPTH, 1, 2 * D_FF)
    rope_tab = _rope_tables()
    bd = _block_diag_ones()
    norm1_r = norm1.reshape(DEPTH, 1, D_MODEL)
    norm2_r = norm2.reshape(DEPTH, 1, D_MODEL)

    cond8 = jnp.zeros((8, D_MODEL), F32).at[0].set(c_ctx).at[1:1 + DEC_BATCH].set(c)
    mods = _adaln(cond8, w_ada, b_ada).reshape(DEPTH, 8, 1, 6 * D_MODEL)

    rpb_flat = jnp.pad(rpb_a.reshape(-1), (0, 8192 - rpb_a.size))
    na_bias = _na_bias(rpb_flat)
    kpe_p = jnp.pad(cache_mla_kpe, ((0, 0), (0, 0), (0, 0), (MLA_NOPE, LANES - MLA_QK)))
    kd_ctx, vd_ctx = _mla_ctx(cache_mla_ckv, kpe_p, w_ukv_p, gains, bd)
    c_nat_k = cache_nat_k.reshape(DEC_BATCH, DEPTH, PAST_LEN, 256)
    c_nat_v = cache_nat_v.reshape(DEC_BATCH, DEPTH, PAST_LEN, 256)
    c_gqa_k = cache_gqa_k.reshape(DEC_BATCH, DEPTH, PAST_LEN, LANES)
    c_gqa_v = cache_gqa_v.reshape(DEC_BATCH, DEPTH, PAST_LEN, LANES)
    c_win_k = cache_win_k.reshape(DEC_BATCH, DEPTH, PAST_LEN, LANES)
    c_win_v = cache_win_v.reshape(DEC_BATCH, DEPTH, PAST_LEN, LANES)

    x_ctx, x_lat = x_prompt.reshape(N_CTX, D_MODEL), x_sample.reshape(N_LAT, D_MODEL)
    caches = [jnp.zeros((BATCH, DEPTH) + blk, F32) for blk in _CACHE_BLOCKS]
    for l in range(DEPTH):
        qkv, caches = _inproj(l, x_ctx, x_lat, mods, norm1_r, w_qkv, w_ukv_p, gains, bd, rope_tab, caches)
        qa, ka, va, qb, kb, vb, qc, kc, vc, qd, kd, vd = qkv
        sink_l = sink_c[l]
        o_ctx = _ctx_attention(l, sink_l, qkv)
        o_lat = (_lat_na(l, qa, ka, va, c_nat_k, c_nat_v, na_bias),
                 _lat_gqa(l, qb, kb, vb, c_gqa_k, c_gqa_v),
                 _lat_window(l, sink_l, qc, kc, vc, c_win_k, c_win_v),
                 _lat_mla(l, qd, kd, vd, kd_ctx, vd_ctx))
        x1 = _merge(l, x_ctx, x_lat, o_ctx, o_lat, mods, norm1_r, w_gate, w_branch_b, w_out_b)
        x_ctx, x_lat = _ffn(l, x1, mods, norm2_r, w_up_b, conv_w, conv_b_r, w_down_b)
    y_prompt = x_ctx.reshape(BATCH, SEQ, D_MODEL)
    y_sample = x_lat.reshape(DEC_BATCH, DEC_SEQ, D_MODEL)
    heads = (4, 4, 2, 2, 2, 2)
    outs = [a.reshape(BATCH, DEPTH, h, HEAD_DIM, SEQ).transpose(0, 1, 4, 2, 3) for a, h in zip(caches[:6], heads)]
    return (y_prompt, y_sample, *outs, jnp.swapaxes(caches[6], 2, 3), jnp.swapaxes(caches[7], 2, 3))
```

```python
import functools

import jax
import jax.numpy as jnp
import numpy as np
from jax import lax
from jax.experimental import pallas as pl
from jax.experimental.pallas import tpu as pltpu

F32 = jnp.float32
BF16 = jnp.bfloat16

D_MODEL = 1024
BATCH = 16
SEQ = 256
DEPTH = 4
DEC_BATCH = 2
DEC_SEQ = 2048
PAST_LEN = 256
GRID_W = 64
HEAD_DIM = 64
NA_ROWS = 8
NA_COLS = 16
WINDOW = 128
MLA_NOPE = 64
MLA_ROPE = 32
MLA_QK = MLA_NOPE + MLA_ROPE
MLA_KV_LORA = 128
D_FF = 2816
ROPE_THETA = 10000.0
EPS = 1e-6
NEG_INF = -1e30

LANES = 128
TM = 256
N_CTX = BATCH * SEQ
N_LAT = DEC_BATCH * DEC_SEQ
N_TOK = N_CTX + N_LAT
CTX_TILES = N_CTX // TM
LAT_TILES_PER_BATCH = DEC_SEQ // TM
N_TILES = N_TOK // TM
GRID_ROWS = DEC_SEQ // GRID_W
FF_CHUNK = 256
N_FF_CHUNKS = D_FF // FF_CHUNK
VMEM_LIMIT = 56 * 1024 * 1024

C_QA, C_KA, C_VA = 0, 256, 512
C_QB, C_KB, C_VB = 768, 1024, 1152
C_QC, C_KC, C_VC = 1280, 1536, 1664
C_QD, C_CKV, C_KPE, C_END = 1792, 2304, 2432, 2560
N_GATE = 4 * D_MODEL
IN_QKV = 2336
LOG2E = float(np.log2(np.e))
Q_SCALE = HEAD_DIM ** -0.5 * LOG2E
MLA_Q_SCALE = MLA_QK ** -0.5 * LOG2E


def _cparams(sem):
    return pltpu.CompilerParams(dimension_semantics=sem, vmem_limit_bytes=VMEM_LIMIT)


def _tile_group(i):
    return jnp.where(i < CTX_TILES, 0, 1 + (i - CTX_TILES) // LAT_TILES_PER_BATCH)


def _tile_pos_block(i):
    return jnp.where(i < CTX_TILES, 0, 1 + (i - CTX_TILES) % LAT_TILES_PER_BATCH)


def _resident(shape, index_map):
    return pl.BlockSpec(shape, index_map, pipeline_mode=pl.Buffered(1))


def _dot(a, b):
    return jnp.dot(a, b, preferred_element_type=F32)


def _dot_t(a, b):
    return lax.dot_general(a, b, (((1,), (1,)), ((), ())), preferred_element_type=F32)


def _rms_mod(x, gain, scale, shift):
    ms = jnp.mean(x * x, axis=-1, keepdims=True)
    return (x * lax.rsqrt(ms + EPS)) * gain * (1.0 + scale) + shift


def _group_norm(t, ones_bd, inv_n, gain):
    ss = _dot((t * t).astype(BF16), ones_bd)
    return t * lax.rsqrt(ss * inv_n + EPS) * gain


def _rope(t, cos, sin_lo, sin_hi, half):
    return (t * cos + pltpu.roll(t, LANES - half, 1) * sin_lo + pltpu.roll(t, half, 1) * sin_hi)


def _lane_iota(shape):
    return lax.broadcasted_iota(jnp.int32, shape, len(shape) - 1)


def _low_half(shape):
    return _lane_iota(shape) < HEAD_DIM


def _run_units(units, depth=2):
    n = len(units)
    pending = {}
    outs = []
    for k in range(min(depth, n)):
        pending[k] = units[k][0]()
    for k in range(n):
        outs.append(units[k][1](pending.pop(k)))
        if k + depth < n:
            pending[k + depth] = units[k + depth][0]()
    return outs


N_COND = 1 + DEC_BATCH


def _adaln_kernel(cond_ref, w_ref, b_ref, out_ref):
    cnd = cond_ref[...]
    s_t = (cnd * (1.0 / (1.0 + jnp.exp(-cnd)))).T
    w = w_ref[...]
    out_ref[...] = jnp.zeros(out_ref.shape, F32)
    for r in range(N_COND):
        out_ref[r:r + 1, :] = jnp.sum(s_t[:, r:r + 1] * w, axis=0, keepdims=True) + b_ref[...]


def _adaln(cond8, w_ada, b_ada):
    n_col = 6 * D_MODEL
    tn = 1024
    return pl.pallas_call(
        _adaln_kernel,
        grid=(DEPTH, n_col // tn),
        in_specs=[
            pl.BlockSpec((8, D_MODEL), lambda l, n: (0, 0)),
            pl.BlockSpec((None, D_MODEL, tn), lambda l, n: (l, 0, n)),
            pl.BlockSpec((None, 1, tn), lambda l, n: (l, 0, n)),
        ],
        out_specs=pl.BlockSpec((None, 8, tn), lambda l, n: (l, 0, n)),
        out_shape=jax.ShapeDtypeStruct((DEPTH, 8, n_col), F32),
        compiler_params=_cparams(("parallel", "parallel")),
        name="adaln",
    )(cond8, w_ada, b_ada.reshape(DEPTH, 1, n_col))


_QKV_WIDTHS = (512, 256, 256, 512, 128, 128, 512, 128, 128, 512, 512, 256)
_CACHE_WIDTHS = (256, 256, 128, 128, 128, 128, MLA_KV_LORA, MLA_ROPE)
_CACHE_BLOCKS = tuple((w, SEQ) for w in _CACHE_WIDTHS)


TM_IN = 2 * TM
IN_STEPS = N_TOK // TM_IN
IN_CTX_STEPS = N_CTX // TM_IN


def _inproj_kernel(xc_ref, xl_ref, mod_ref, n1_ref, w_ref, wukv_ref, g_ref, bd_ref, rope_ref, *rest):
    outs = rest[-(len(_QKV_WIDTHS) + len(_CACHE_WIDTHS)):]
    is_ctx = pl.program_id(0) >= IN_STEPS - IN_CTX_STEPS
    units = []
    for half in range(TM_IN // TM):
        rows = slice(half * TM, (half + 1) * TM)
        views = [o.at[rows] for o in outs[:len(_QKV_WIDTHS)]] + [o.at[half] for o in outs[len(_QKV_WIDTHS):]]
        units += _inproj_half(is_ctx, xc_ref.at[rows], xl_ref.at[rows], mod_ref, n1_ref, w_ref, wukv_ref, g_ref,
                              bd_ref, rope_ref.at[:, rows, :], views)
    _run_units(units, depth=2)


def _inproj_half(is_ctx, xc_ref, xl_ref, mod_ref, n1_ref, w_ref, wukv_ref, g_ref, bd_ref, rope_ref, outs):
    qa_o, ka_o, va_o, qb_o, kb_o, vb_o, qc_o, kc_o, vc_o, qd_o, kd_o, vd_o = outs[:len(_QKV_WIDTHS)]
    natk_o, natv_o, gqak_o, gqav_o, wink_o, winv_o, ckv_o, kpe_o = outs[len(_QKV_WIDTHS):]
    x = jnp.where(is_ctx, xc_ref[...], xl_ref[...])
    mod = mod_ref[...]
    hb = _rms_mod(x, n1_ref[...], mod[:, D_MODEL:2 * D_MODEL], mod[:, 0:D_MODEL]).astype(BF16)
    bd64, bd128 = bd_ref[0], bd_ref[1]
    cos64, slo64, shi64 = rope_ref[0], rope_ref[1], rope_ref[2]
    cosm, slom, shim = rope_ref[3], rope_ref[4], rope_ref[5]
    inv64, inv96 = 1.0 / HEAD_DIM, 1.0 / MLA_QK
    low = _low_half((TM, LANES))
    held = {}

    def chunks(t):
        return [t[:, c:c + LANES] for c in range(0, t.shape[1], LANES)]

    def rope64(t):
        parts = [_rope(p, cos64, slo64, shi64, 16) for p in chunks(t)]
        return parts[0] if len(parts) == 1 else jnp.concatenate(parts, axis=1)

    def ropem(t):
        return jnp.concatenate([_rope(p, cosm, slom, shim, 8) for p in chunks(t)], axis=1)

    def store_q(ref, q, grouped):
        q = q * Q_SCALE
        slots = []
        for hd in range(4):
            part, src = q[:, (hd // 2) * LANES:(hd // 2 + 1) * LANES], hd % 2
            dst = hd // 2 if grouped else src
            if src != dst:
                part = pltpu.roll(part, HEAD_DIM, 1)
            slots.append(jnp.where(low if dst == 0 else ~low, part, 0.0))
        ref[...] = jnp.concatenate(slots, axis=1).astype(BF16)

    def post_latent(t):
        ckv = _group_norm(t[:, :LANES], bd128[:LANES, :LANES], 1.0 / MLA_KV_LORA, g_ref[8:9, :LANES])
        kpe = t[:, LANES:]
        ckv_o[...] = ckv.T
        kpe_o[...] = kpe.T[MLA_NOPE:MLA_QK, :]
        held["kpe"] = kpe
        held["kv"] = _dot(ckv.astype(BF16), wukv_ref[...])

    def post_kd(_):
        kv, kpe = held["kv"], held["kpe"]
        for j in range(2):
            kf = jnp.concatenate([kv[:, (2 * j) * LANES:(2 * j + 1) * LANES] + kpe,
                                  kv[:, (2 * j + 1) * LANES:(2 * j + 2) * LANES] + kpe], axis=1)
            kd_o[:, 2 * j * LANES:(2 * j + 2) * LANES] = ropem(
                _group_norm(kf, bd128, inv96, g_ref[7:8, :])).astype(BF16)
        vd_o[...] = kv[:, 4 * LANES:].astype(BF16)

    def post_qa(t):
        store_q(qa_o, _group_norm(t, bd64, inv64, g_ref[0:1, :]), grouped=False)

    def post_ka(t):
        k = _group_norm(t, bd64, inv64, g_ref[1:2, :])
        natk_o[...] = k.T
        ka_o[...] = k.astype(BF16)

    def post_va(t):
        natv_o[...] = t.T
        va_o[...] = t.astype(BF16)

    def post_q_rope(ref, row):
        def post(t):
            store_q(ref, rope64(_group_norm(t, bd64, inv64, g_ref[row:row + 1, :])), grouped=True)
        return post

    def post_kv(k_ref, v_ref, kc_ref, vc_ref, row):
        def post(t):
            k = _group_norm(t[:, :LANES], bd64[:LANES, :LANES], inv64, g_ref[row:row + 1, :LANES])
            kc_ref[...] = k.T
            k_ref[...] = rope64(k).astype(BF16)
            v = t[:, LANES:]
            vc_ref[...] = v.T
            v_ref[...] = v.astype(BF16)
        return post

    def post_qd(j):
        def post(t):
            qd_o[:, 2 * j * LANES:(2 * j + 2) * LANES] = (ropem(
                _group_norm(t, bd128, inv96, g_ref[6:7, :])) * MLA_Q_SCALE).astype(BF16)
        return post

    stages = [
        ((C_CKV, C_END), post_latent),
        ((C_QA, C_KA), post_qa),
        (None, post_kd),
        ((C_QB, C_KB), post_q_rope(qb_o, 2)),
        ((C_QC, C_KC), post_q_rope(qc_o, 4)),
        ((C_QD, C_QD + 2 * LANES), post_qd(0)),
        ((C_QD + 2 * LANES, C_CKV), post_qd(1)),
        ((C_KB, C_QC), post_kv(kb_o, vb_o, gqak_o, gqav_o, 3)),
        ((C_KC, C_QD), post_kv(kc_o, vc_o, wink_o, winv_o, 5)),
        ((C_KA, C_VA), post_ka),
        ((C_VA, C_QB), post_va),
    ]

    def issue(cols):
        return None if cols is None else _dot(hb, w_ref[:, cols[0]:cols[1]])

    return [(functools.partial(issue, cols), post) for cols, post in stages]


def _ctx_block(i):
    return jnp.minimum(i, CTX_TILES - 1)


def _lat_block(i):
    return jnp.maximum(i - CTX_TILES, 0)


def _inproj(l, x_ctx, x_lat, mods, norm1, w_qkv, w_ukv_p, gains, bd, rope_tab, caches):
    n_lat = IN_STEPS - IN_CTX_STEPS
    blk = lambda s: (s + IN_CTX_STEPS) % IN_STEPS
    lat_blk = lambda s: jnp.maximum(blk(s) - IN_CTX_STEPS, 0)
    per_batch = DEC_SEQ // TM_IN
    tok = lambda w: pl.BlockSpec((TM_IN, w), lambda s: (blk(s), 0))
    in_specs = [
        pl.BlockSpec((TM_IN, D_MODEL), lambda s: (jnp.minimum(blk(s), IN_CTX_STEPS - 1), 0)),
        pl.BlockSpec((TM_IN, D_MODEL), lambda s: (lat_blk(s), 0)),
        pl.BlockSpec((None, None, 1, 6 * D_MODEL),
                     lambda s: (l, jnp.where(s < n_lat, 1 + lat_blk(s) // per_batch, 0), 0, 0)),
        _resident((None, 1, D_MODEL), lambda s: (l, 0, 0)),
        _resident((None, D_MODEL, C_END), lambda s: (l, 0, 0)),
        _resident((None, MLA_KV_LORA, 768), lambda s: (l, 0, 0)),
        _resident((None, 16, 256), lambda s: (l, 0, 0)),
        _resident((2, 256, 256), lambda s: (0, 0, 0)),
        pl.BlockSpec((6, TM_IN, LANES), lambda s: (0, jnp.where(s < n_lat, 1 + lat_blk(s) % per_batch, 0), 0)),
    ]
    args = [x_ctx, x_lat, mods, norm1, w_qkv, w_ukv_p, gains, bd, rope_tab] + list(caches)
    aliases = {len(in_specs) + n: len(_QKV_WIDTHS) + n for n in range(len(caches))}
    in_specs = in_specs + [pl.BlockSpec(memory_space=pl.ANY)] * len(caches)
    cache_spec = lambda cb: pl.BlockSpec((TM_IN // SEQ, None) + cb, lambda s: (jnp.maximum(s - n_lat, 0), l, 0, 0))
    outs = pl.pallas_call(
        _inproj_kernel,
        grid=(IN_STEPS,),
        in_specs=in_specs,
        out_specs=[tok(w) for w in _QKV_WIDTHS] + [cache_spec(blk) for blk in _CACHE_BLOCKS],
        out_shape=[jax.ShapeDtypeStruct((N_TOK, w), BF16) for w in _QKV_WIDTHS]
        + [jax.ShapeDtypeStruct((BATCH, DEPTH) + blk, F32) for blk in _CACHE_BLOCKS],
        input_output_aliases=aliases,
        compiler_params=_cparams(("arbitrary",)),
        name=f"inproj_l{l}",
    )(*args)
    return outs[:len(_QKV_WIDTHS)], outs[len(_QKV_WIDTHS):]


def _scores(q, keys):
    out = []
    for k, bias in keys:
        s = _dot_t(q, k)
        if bias is not None:
            s = s + bias
        out.append(s)
    return out


def _softmax_pv(scores, values, sink=None):
    m = functools.reduce(jnp.maximum, [jnp.max(s, axis=-1, keepdims=True) for s in scores])
    if sink is not None:
        m = jnp.maximum(m, sink)
    denom = None
    out = None
    for s, v in zip(scores, values):
        e = jnp.exp2(s - m)
        d = jnp.sum(e, axis=-1, keepdims=True)
        o = _dot(e.astype(BF16), v)
        denom = d if denom is None else denom + d
        out = o if out is None else out + o
    if sink is not None:
        denom = denom + jnp.exp2(sink - m)
    return out / denom


def _pair_unit(q_ref, rows, j, keys, values, kv_half, sinks=None):
    m_rows = rows.stop - rows.start

    def issue():
        q2 = jnp.concatenate([q_ref[rows, (2 * j) * LANES:(2 * j + 1) * LANES],
                              q_ref[rows, (2 * j + 1) * LANES:(2 * j + 2) * LANES]], axis=0)
        return _scores(q2, keys())

    def finish(scores):
        sink = None
        if sinks is not None:
            row = lax.broadcasted_iota(jnp.int32, (2 * m_rows, 1), 0)
            sink = jnp.where(row < m_rows, sinks[0], sinks[1])
        o2 = _softmax_pv(scores, values(), sink)
        lo, hi = o2[:m_rows], o2[m_rows:]
        if kv_half == 0:
            hi = pltpu.roll(hi, HEAD_DIM, 1)
        elif kv_half == 1:
            lo = pltpu.roll(lo, HEAD_DIM, 1)
        return jnp.where(_low_half(lo.shape), lo, hi)

    return issue, finish


def _single_unit(q_ref, rows, hd, keys, values):
    def issue():
        return _scores(q_ref[rows, hd * LANES:(hd + 1) * LANES], keys())

    def finish(scores):
        return _softmax_pv(scores, values())

    return issue, finish


def _merge_halves(o_even, o_odd):
    return jnp.where(_low_half(o_even.shape), o_even, o_odd)


def _ctx_attn_kernel(sink_ref, qa, ka, va, qb, kb, vb, qc, kc, vc, qd, kd, vd, o_ref):
    rows = slice(0, SEQ)
    units = []
    for j in range(2):
        c = slice(j * LANES, (j + 1) * LANES)
        units.append(_pair_unit(qa, rows, j, lambda c=c: [(ka[:, c], None)], lambda c=c: [va[:, c]], None))
    for q, k, v, with_sink in ((qb, kb, vb, False), (qc, kc, vc, True)):
        for j in range(2):
            sinks = (sink_ref[2 * j] * LOG2E, sink_ref[2 * j + 1] * LOG2E) if with_sink else None
            units.append(_pair_unit(q, rows, j, lambda k=k: [(k[...], None)], lambda v=v: [v[...]], j, sinks))
    for hd in range(4):
        c = slice(hd * LANES, (hd + 1) * LANES)
        vs = slice((hd // 2) * LANES, (hd // 2 + 1) * LANES)
        units.append(_single_unit(qd, rows, hd, lambda c=c: [(kd[:, c], None)], lambda vs=vs: [vd[:, vs]]))
    outs = _run_units(units)
    for n in range(6):
        o_ref[:, n * LANES:(n + 1) * LANES] = outs[n].astype(BF16)
    for j in range(2):
        o_ref[:, (6 + j) * LANES:(7 + j) * LANES] = _merge_halves(outs[6 + 2 * j], outs[7 + 2 * j]).astype(BF16)


def _ctx_attention(l, sink_c, qkv):
    blk = lambda w: pl.BlockSpec((SEQ, w), lambda b: (b, 0))
    return pl.pallas_call(
        _ctx_attn_kernel,
        grid=(BATCH,),
        in_specs=[pl.BlockSpec(memory_space=pltpu.SMEM)] + [blk(w) for w in _QKV_WIDTHS],
        out_specs=pl.BlockSpec((SEQ, D_MODEL), lambda b: (b, 0)),
        out_shape=jax.ShapeDtypeStruct((N_CTX, D_MODEL), BF16),
        compiler_params=_cparams(("parallel",)),
        name=f"ctx_attn_l{l}",
    )(sink_c, *qkv)


LAT_TQ = 256
LAT_SUB = 128
LAT_QBLOCKS = DEC_SEQ // LAT_TQ
LAT_ROW0 = N_CTX // LAT_TQ
LAT_KV0 = N_CTX // DEC_SEQ


def _lat_qspec(width):
    return pl.BlockSpec((LAT_TQ, width), lambda b, i: (LAT_ROW0 + b * LAT_QBLOCKS + i, 0))


def _lat_kvspec(width):
    return pl.BlockSpec((DEC_SEQ, width), lambda b, i: (LAT_KV0 + b, 0))


def _lat_call(kernel, l, name, in_specs, args):
    return pl.pallas_call(
        kernel,
        grid=(DEC_BATCH, LAT_QBLOCKS),
        in_specs=in_specs,
        out_specs=pl.BlockSpec((LAT_TQ, 256), lambda b, i: (b * LAT_QBLOCKS + i, 0)),
        out_shape=jax.ShapeDtypeStruct((N_LAT, 256), BF16),
        compiler_params=_cparams(("parallel", "parallel")),
        name=f"{name}_l{l}",
    )(*args)


def _sub_rows(u):
    return slice(u * LAT_SUB, (u + 1) * LAT_SUB)


def _store_units(o_ref, outs):
    for n, o in enumerate(outs):
        u, j = divmod(n, 2)
        o_ref[_sub_rows(u), j * LANES:(j + 1) * LANES] = o.astype(BF16)


def _gqa_kernel(q_ref, k_ref, v_ref, kc_ref, vc_ref, o_ref):
    keys = lambda: [(kc_ref[...].astype(BF16), None), (k_ref[...], None)]
    values = lambda: [vc_ref[...].astype(BF16), v_ref[...]]
    units = [_pair_unit(q_ref, _sub_rows(u), j, keys, values, j) for u in range(LAT_TQ // LAT_SUB) for j in range(2)]
    _store_units(o_ref, _run_units(units))


def _lat_gqa(l, qb, kb, vb, cache_k, cache_v):
    cspec = pl.BlockSpec((None, None, PAST_LEN, LANES), lambda b, i: (b, l, 0, 0))
    return _lat_call(_gqa_kernel, l, "lat_gqa",
                     [_lat_qspec(512), _lat_kvspec(LANES), _lat_kvspec(LANES), cspec, cspec],
                     (qb, kb, vb, cache_k, cache_v))


WIN_KEYS = 3 * LAT_SUB


def _win_kernel(sink_ref, q_ref, k_ref, v_ref, kc_ref, vc_ref, o_ref):
    units = []
    for u in range(LAT_TQ // LAT_SUB):
        blk = pl.program_id(1) * (LAT_TQ // LAT_SUB) + u
        start = pl.multiple_of(jnp.clip((blk - 1) * LAT_SUB, 0, DEC_SEQ - WIN_KEYS), LAT_SUB)

        def keys(blk=blk, start=start):
            row = lax.broadcasted_iota(jnp.int32, (2 * LAT_SUB, WIN_KEYS), 0)
            qpos = blk * LAT_SUB + jnp.where(row < LAT_SUB, row, row - LAT_SUB)
            kpos = start + lax.broadcasted_iota(jnp.int32, (2 * LAT_SUB, WIN_KEYS), 1)
            bias = jnp.where(jnp.abs(qpos - kpos) <= WINDOW, 0.0, NEG_INF)
            return [(kc_ref[...].astype(BF16), None), (k_ref[pl.ds(start, WIN_KEYS), :], bias)]

        def values(start=start):
            return [vc_ref[...].astype(BF16), v_ref[pl.ds(start, WIN_KEYS), :]]

        for j in range(2):
            sinks = (sink_ref[2 * j] * LOG2E, sink_ref[2 * j + 1] * LOG2E)
            units.append(_pair_unit(q_ref, _sub_rows(u), j, keys, values, j, sinks))
    _store_units(o_ref, _run_units(units))


def _lat_window(l, sink_c, qc, kc, vc, cache_k, cache_v):
    cspec = pl.BlockSpec((None, None, PAST_LEN, LANES), lambda b, i: (b, l, 0, 0))
    return _lat_call(_win_kernel, l, "lat_win",
                     [pl.BlockSpec(memory_space=pltpu.SMEM), _lat_qspec(512), _lat_kvspec(LANES),
                      _lat_kvspec(LANES), cspec, cspec],
                     (sink_c, qc, kc, vc, cache_k, cache_v))


def _mla_kernel(q_ref, k_ref, v_ref, kc_ref, vc_ref, o_ref):
    units = []
    rows = slice(0, LAT_TQ)
    for hd in range(4):
        c = slice(hd * LANES, (hd + 1) * LANES)
        vs = slice((hd // 2) * LANES, (hd // 2 + 1) * LANES)
        units.append(_single_unit(q_ref, rows, hd,
                                  lambda c=c: [(kc_ref[:, c], None), (k_ref[:, c], None)],
                                  lambda vs=vs: [vc_ref[:, vs], v_ref[:, vs]]))
    outs = _run_units(units)
    for j in range(2):
        o_ref[:, j * LANES:(j + 1) * LANES] = _merge_halves(outs[2 * j], outs[2 * j + 1]).astype(BF16)


def _lat_mla(l, qd, kd, vd, kd_ctx, vd_ctx):
    kcs = pl.BlockSpec((None, None, PAST_LEN, 512), lambda b, i: (l, b, 0, 0))
    vcs = pl.BlockSpec((None, None, PAST_LEN, 256), lambda b, i: (l, b, 0, 0))
    return _lat_call(_mla_kernel, l, "lat_mla",
                     [_lat_qspec(512), _lat_kvspec(512), _lat_kvspec(256), kcs, vcs],
                     (qd, kd, vd, kd_ctx, vd_ctx))


NA_KEYS = NA_ROWS * GRID_W
NA_ROWS_PER_STEP = LAT_TQ // GRID_W


def _na_shift(r):
    return jnp.where(r < NA_ROWS // 2, r,
                     jnp.where(r > GRID_ROWS - NA_ROWS // 2, r - (GRID_ROWS - NA_ROWS), NA_ROWS // 2))


def _na_kernel(q_ref, k_ref, v_ref, kc_ref, vc_ref, bias_ref, o_ref):
    units = []
    for rr in range(NA_ROWS_PER_STEP):
        r = pl.program_id(1) * NA_ROWS_PER_STEP + rr
        start = pl.multiple_of(jnp.clip(r - NA_ROWS // 2, 0, GRID_ROWS - NA_ROWS) * GRID_W, GRID_W)
        shift = _na_shift(r)
        rows = slice(rr * GRID_W, (rr + 1) * GRID_W)
        for j in range(2):
            c = slice(j * LANES, (j + 1) * LANES)

            def keys(j=j, c=c, start=start, shift=shift):
                bias = jnp.concatenate([bias_ref[2 * j, shift], bias_ref[2 * j + 1, shift]], axis=0)
                return [(kc_ref[:, c].astype(BF16), None), (k_ref[pl.ds(start, NA_KEYS), c], bias)]

            def values(c=c, start=start):
                return [vc_ref[:, c].astype(BF16), v_ref[pl.ds(start, NA_KEYS), c]]

            units.append(_pair_unit(q_ref, rows, j, keys, values, None))
    for n, o in enumerate(_run_units(units)):
        rr, j = divmod(n, 2)
        o_ref[rr * GRID_W:(rr + 1) * GRID_W, j * LANES:(j + 1) * LANES] = o.astype(BF16)


def _lat_na(l, qa, ka, va, cache_k, cache_v, na_bias):
    cspec = pl.BlockSpec((None, None, PAST_LEN, 256), lambda b, i: (b, l, 0, 0))
    bspec = _resident((None, 4, NA_ROWS, GRID_W, NA_KEYS), lambda b, i: (l, 0, 0, 0, 0))
    return _lat_call(_na_kernel, l, "lat_na",
                     [_lat_qspec(512), _lat_kvspec(256), _lat_kvspec(256), cspec, cspec, bspec],
                     (qa, ka, va, cache_k, cache_v, na_bias))


def _na_bias_kernel(rpb_ref, out_ref, t_ref):
    l = pl.program_id(0)
    n_dr, n_dc = 2 * NA_ROWS - 1, 2 * NA_COLS - 1
    shape = (GRID_W, LANES)
    qc = lax.broadcasted_iota(jnp.int32, shape, 0)
    lane = _lane_iota(shape)
    kc = lane & (GRID_W - 1)
    dc = jnp.clip(kc - qc, 1 - NA_COLS, NA_COLS - 1) + (NA_COLS - 1)
    c0 = jnp.clip(qc - NA_COLS // 2, 0, GRID_W - NA_COLS)
    ok = (kc >= c0) & (kc < c0 + NA_COLS)
    for hd in range(4):
        def fill(a, carry):
            base = ((l * 4 + hd) * n_dr + a) * n_dc
            t = jnp.zeros(shape, F32)
            for b in range(n_dc):
                t = jnp.where(dc == b, rpb_ref[base + b], t)
            t_ref[a] = jnp.where(ok, t * LOG2E, NEG_INF)
            return carry
        lax.fori_loop(0, n_dr, fill, 0)
        for d in range(NA_ROWS):
            for m in range(NA_ROWS // 2):
                lo = t_ref[2 * m - d + NA_ROWS - 1]
                hi = t_ref[2 * m + 1 - d + NA_ROWS - 1]
                out_ref[hd, d, :, m * LANES:(m + 1) * LANES] = jnp.where(lane < GRID_W, lo, hi)


def _na_bias(rpb_flat):
    return pl.pallas_call(
        _na_bias_kernel,
        grid=(DEPTH,),
        in_specs=[pl.BlockSpec(memory_space=pltpu.SMEM)],
        out_specs=pl.BlockSpec((None, 4, NA_ROWS, GRID_W, NA_KEYS), lambda l: (l, 0, 0, 0, 0)),
        out_shape=jax.ShapeDtypeStruct((DEPTH, 4, NA_ROWS, GRID_W, NA_KEYS), F32),
        scratch_shapes=[pltpu.VMEM((2 * NA_ROWS - 1, GRID_W, LANES), F32)],
        compiler_params=_cparams(("parallel",)),
        name="na_bias",
    )(rpb_flat)


def _mla_ctx_kernel(ckv_ref, kpe_ref, wukv_ref, g_ref, bd_ref, kd_o, vd_o):
    kv = _dot(ckv_ref[...].astype(BF16), wukv_ref[...])
    kpe = kpe_ref[...]
    for j in range(2):
        kf = jnp.concatenate([kv[:, (2 * j) * LANES:(2 * j + 1) * LANES] + kpe,
                              kv[:, (2 * j + 1) * LANES:(2 * j + 2) * LANES] + kpe], axis=1)
        kd_o[:, 2 * j * LANES:(2 * j + 2) * LANES] = _group_norm(
            kf, bd_ref[1], 1.0 / MLA_QK, g_ref[7:8, :]).astype(BF16)
    vd_o[...] = kv[:, 4 * LANES:].astype(BF16)


def _mla_ctx(cache_ckv, cache_kpe_p, w_ukv_p, gains, bd):
    return pl.pallas_call(
        _mla_ctx_kernel,
        grid=(DEPTH, DEC_BATCH),
        in_specs=[
            pl.BlockSpec((None, None, PAST_LEN, LANES), lambda l, b: (b, l, 0, 0)),
            pl.BlockSpec((None, None, PAST_LEN, LANES), lambda l, b: (b, l, 0, 0)),
            pl.BlockSpec((None, MLA_KV_LORA, 768), lambda l, b: (l, 0, 0)),
            pl.BlockSpec((None, 16, 256), lambda l, b: (l, 0, 0)),
            pl.BlockSpec((2, 256, 256), lambda l, b: (0, 0, 0)),
        ],
        out_specs=[pl.BlockSpec((None, None, PAST_LEN, 512), lambda l, b: (l, b, 0, 0)),
                   pl.BlockSpec((None, None, PAST_LEN, 256), lambda l, b: (l, b, 0, 0))],
        out_shape=[jax.ShapeDtypeStruct((DEPTH, DEC_BATCH, PAST_LEN, 512), BF16),
                   jax.ShapeDtypeStruct((DEPTH, DEC_BATCH, PAST_LEN, 256), BF16)],
        compiler_params=_cparams(("parallel", "parallel")),
        name="mla_ctx",
    )(cache_ckv, cache_kpe_p, w_ukv_p, gains, bd)


def _merge_kernel(xc_ref, xl_ref, oc_ref, oa_ref, ob_ref, occ_ref, od_ref, mod_ref, n1_ref, wg_ref, wb_ref, wo_ref,
                  x1_ref):
    is_ctx = pl.program_id(0) < IN_CTX_STEPS
    mod = mod_ref[...]
    lat_refs = (oa_ref, ob_ref, occ_ref, od_ref)

    def half_units(half):
        rows = slice(half * TM, (half + 1) * TM)
        st = {"merged": None}
        st["x"] = jnp.where(is_ctx, xc_ref[rows, :], xl_ref[rows, :])
        st["hb"] = _rms_mod(st["x"], n1_ref[...], mod[:, D_MODEL:2 * D_MODEL], mod[:, 0:D_MODEL]).astype(BF16)

        def branch(n):
            def issue():
                o = jnp.where(is_ctx, oc_ref[rows, n * 256:(n + 1) * 256], lat_refs[n][rows, :])
                return _dot(st["hb"], wg_ref[:, n * D_MODEL:(n + 1) * D_MODEL]), _dot(o, wb_ref[n])

            def finish(res):
                gate, br = res
                t = (1.0 / (1.0 + jnp.exp(-gate))) * br
                st["merged"] = t if st["merged"] is None else st["merged"] + t
            return issue, finish

        def out_issue():
            return _dot(st["merged"].astype(BF16), wo_ref[...])

        def out_finish(out):
            x1_ref[rows, :] = st["x"] + mod[:, 2 * D_MODEL:3 * D_MODEL] * out

        return [branch(n) for n in range(4)], (out_issue, out_finish)

    (a_br, a_out), (b_br, b_out) = half_units(0), half_units(1)
    idle = (lambda: None, lambda _: None)
    _run_units(a_br + b_br[:2] + [a_out] + b_br[2:] + [idle, b_out], depth=2)


def _merge(l, x_ctx, x_lat, o_ctx, o_lat, mods, norm1, w_gate, w_branch, w_out):
    per_batch = DEC_SEQ // TM_IN
    ctx_blk = lambda i: jnp.minimum(i, IN_CTX_STEPS - 1)
    lat_blk = lambda i: jnp.maximum(i - IN_CTX_STEPS, 0)
    tok = pl.BlockSpec((TM_IN, D_MODEL), lambda i: (i, 0))
    ctx_tok = pl.BlockSpec((TM_IN, D_MODEL), lambda i: (ctx_blk(i), 0))
    lat_tok = lambda w: pl.BlockSpec((TM_IN, w), lambda i: (lat_blk(i), 0))
    return pl.pallas_call(
        _merge_kernel,
        grid=(IN_STEPS,),
        in_specs=[
            ctx_tok, lat_tok(D_MODEL), ctx_tok, lat_tok(256), lat_tok(256), lat_tok(256), lat_tok(256),
            pl.BlockSpec((None, None, 1, 6 * D_MODEL),
                         lambda i: (l, jnp.where(i < IN_CTX_STEPS, 0, 1 + lat_blk(i) // per_batch), 0, 0)),
            _resident((None, 1, D_MODEL), lambda i: (l, 0, 0)),
            _resident((None, D_MODEL, N_GATE), lambda i: (l, 0, 0)),
            _resident((None, 4, 256, D_MODEL), lambda i: (l, 0, 0, 0)),
            _resident((None, D_MODEL, D_MODEL), lambda i: (l, 0, 0)),
        ],
        out_specs=tok,
        out_shape=jax.ShapeDtypeStruct((N_TOK, D_MODEL), F32),
        compiler_params=_cparams(("parallel",)),
        name=f"merge_l{l}",
    )(x_ctx, x_lat, o_ctx, *o_lat, mods, norm1, w_gate, w_branch, w_out)


HALO = 8


def _sublane_transpose(tiles):
    sub = lax.broadcasted_iota(jnp.int32, tiles[0].shape, 0)
    cur = list(tiles)
    for d in (4, 2, 1):
        keep = (sub & d) == 0
        nxt = list(cur)
        for a in range(HALO):
            if a & d:
                continue
            nxt[a] = jnp.where(keep, cur[a], pltpu.roll(cur[a + d], d, 0))
            nxt[a + d] = jnp.where(keep, pltpu.roll(cur[a], HALO - d, 0), cur[a + d])
        cur = nxt
    return cur


ROW_TILES = TM // HALO
PERM_GROUPS = ROW_TILES // HALO


def _permute_rows(x, to_permuted):
    tile = lambda j: x[j * HALO:(j + 1) * HALO, :]
    out = [None] * ROW_TILES
    for m in range(PERM_GROUPS):
        nat = [m + PERM_GROUPS * s for s in range(HALO)]
        perm = [HALO * m + i for i in range(HALO)]
        src, dst = (nat, perm) if to_permuted else (perm, nat)
        for j, t in zip(dst, _sublane_transpose([tile(j) for j in src])):
            out[j] = t
    return jnp.concatenate(out, axis=0)


def _ffn_tile(x, x_prev, x_next, has_prev, has_next, mod, n2, wup_ref, cw_ref, cb_ref, wdn_ref, u_ref):
    sh2, sc2 = mod[:, 3 * D_MODEL:4 * D_MODEL], mod[:, 4 * D_MODEL:5 * D_MODEL]
    h_prev = jnp.where(has_prev, _rms_mod(x_prev, n2, sc2, sh2), 0.0)
    h_next = jnp.where(has_next, _rms_mod(x_next, n2, sc2, sh2), 0.0)
    hp = _permute_rows(_rms_mod(x, n2, sc2, sh2), True)
    lhs = jnp.concatenate([hp, h_prev, h_next], axis=0).astype(BF16)
    sub = lax.broadcasted_iota(jnp.int32, (HALO, FF_CHUNK), 0)

    def cols(c):
        return (slice(c * FF_CHUNK, (c + 1) * FF_CHUNK), slice(D_FF + c * FF_CHUNK, D_FF + (c + 1) * FF_CHUNK))

    def up(c):
        for half, cs in enumerate(cols(c)):
            lanes = slice(half * FF_CHUNK, (half + 1) * FF_CHUNK)
            u = _dot(lhs, wup_ref[:, cs])
            u_ref[c, HALO:HALO + TM, lanes] = u[:TM]
            prev_row, next_row = u[TM + HALO - 1:TM + HALO], u[TM + HALO:TM + HALO + 1]
            u_ref[c, 0:HALO, lanes] = jnp.where(sub == 0, prev_row, pltpu.roll(u[TM - HALO:TM], 1, 0))
            u_ref[c, HALO + TM:, lanes] = jnp.where(sub == HALO - 1, next_row, pltpu.roll(u[0:HALO], HALO - 1, 0))

    def both(ref, c):
        va, vg = cols(c)
        return jnp.concatenate([ref[:, va], ref[:, vg]], axis=1)

    state = {"acc": None}

    def chunk(c, between):
        cw = both(cw_ref, c)
        t = (both(cb_ref, c) + u_ref[c, 0:TM, :] * cw[0:1] + u_ref[c, HALO:HALO + TM, :] * cw[1:2]
             + u_ref[c, 2 * HALO:2 * HALO + TM, :] * cw[2:3])
        a, g = t[:, :FF_CHUNK], t[:, FF_CHUNK:]
        act = ((g * (1.0 / (1.0 + jnp.exp(-g)))) * a).astype(BF16)
        between()
        d = _dot(act, wdn_ref[c * FF_CHUNK:(c + 1) * FF_CHUNK, :])
        state["acc"] = d if state["acc"] is None else state["acc"] + d

    def finish():
        return x + mod[:, 5 * D_MODEL:6 * D_MODEL] * _permute_rows(state["acc"], False)

    return up, chunk, finish


W_BLOCKS = N_FF_CHUNKS
WUP_BLOCK_COLS = 2 * D_FF // W_BLOCKS
WDN_BLOCK_ROWS = D_FF // W_BLOCKS


def _load_bf16_weights(layer, wup_hbm, wdn_hbm, wup_ref, wdn_ref, up_stage, dn_stage, sems):
    def up_copy(b):
        return pltpu.make_async_copy(wup_hbm.at[layer, :, pl.ds(b * WUP_BLOCK_COLS, WUP_BLOCK_COLS)],
                                     up_stage.at[b % 2], sems.at[0, b % 2])

    def dn_copy(b):
        return pltpu.make_async_copy(wdn_hbm.at[layer, pl.ds(b * WDN_BLOCK_ROWS, WDN_BLOCK_ROWS), :],
                                     dn_stage.at[b % 2], sems.at[1, b % 2])

    up_copy(0).start()
    dn_copy(0).start()
    for b in range(W_BLOCKS):
        if b + 1 < W_BLOCKS:
            up_copy(b + 1).start()
            dn_copy(b + 1).start()
        up_copy(b).wait()
        wup_ref[:, b * WUP_BLOCK_COLS:(b + 1) * WUP_BLOCK_COLS] = up_stage[b % 2].astype(BF16)
        dn_copy(b).wait()
        wdn_ref[b * WDN_BLOCK_ROWS:(b + 1) * WDN_BLOCK_ROWS, :] = dn_stage[b % 2].astype(BF16)


def _ffn_kernel(x_ref, xp_ref, xn_ref, mod_ref, n2_ref, wup_hbm, cw_ref, cb_ref, wdn_hbm, yc_ref, yl_ref,
                u_ref, wup_ref, wdn_ref, up_stage, dn_stage, sems, *, layer):
    i = pl.program_id(0)

    @pl.when(i == 0)
    def _():
        _load_bf16_weights(layer, wup_hbm, wdn_hbm, wup_ref, wdn_ref, up_stage, dn_stage, sems)

    latent = i >= IN_CTX_STEPS
    first = (i - IN_CTX_STEPS) % (DEC_SEQ // TM_IN) == 0
    last = (i - IN_CTX_STEPS) % (DEC_SEQ // TM_IN) == DEC_SEQ // TM_IN - 1
    mod = mod_ref[...]
    n2 = n2_ref[...]
    make = [
        lambda: _ffn_tile(x_ref[0:TM, :], xp_ref[...], x_ref[TM:TM + HALO, :], latent & ~first, latent, mod, n2,
                          wup_ref, cw_ref, cb_ref, wdn_ref, u_ref.at[0]),
        lambda: _ffn_tile(x_ref[TM:, :], x_ref[TM - HALO:TM, :], xn_ref[...], latent, latent & ~last, mod, n2,
                          wup_ref, cw_ref, cb_ref, wdn_ref, u_ref.at[1]),
    ]
    tiles = {}

    def tile(t):
        if t not in tiles:
            tiles[t] = make[t]()
        return tiles[t]

    tasks = [(t, c) for t in range(len(make)) for c in range(N_FF_CHUNKS)]
    early = N_FF_CHUNKS // 2 + 1
    ups = tasks[:early] + [tasks[N_FF_CHUNKS]] + tasks[early:N_FF_CHUNKS] + tasks[N_FF_CHUNKS + 1:]
    depth = 3
    for t, c in ups[:depth]:
        tile(t)[0](c)
    ys = []
    for n, (t, c) in enumerate(tasks):
        ahead = ups[n + depth] if n + depth < len(ups) else None
        tile(t)[1](c, (lambda a=ahead: tile(a[0])[0](a[1])) if ahead else (lambda: None))
        if c == N_FF_CHUNKS - 1:
            ys.append(tile(t)[2]())
    y = jnp.concatenate(ys, axis=0)

    @pl.when(i < IN_CTX_STEPS)
    def _():
        yc_ref[...] = y

    @pl.when(latent)
    def _():
        yl_ref[...] = y


def _ffn(l, x1, mods, norm2, w_up, conv_w, conv_b, w_down):
    per = TM_IN // HALO
    last = N_TOK // HALO - 1
    per_batch = DEC_SEQ // TM_IN
    lat_blk = lambda i: jnp.maximum(i - IN_CTX_STEPS, 0)
    return pl.pallas_call(
        functools.partial(_ffn_kernel, layer=l),
        grid=(IN_STEPS,),
        in_specs=[
            pl.BlockSpec((TM_IN, D_MODEL), lambda i: (i, 0)),
            pl.BlockSpec((HALO, D_MODEL), lambda i: (jnp.maximum(i * per - 1, 0), 0)),
            pl.BlockSpec((HALO, D_MODEL), lambda i: (jnp.minimum((i + 1) * per, last), 0)),
            pl.BlockSpec((None, None, 1, 6 * D_MODEL),
                         lambda i: (l, jnp.where(i < IN_CTX_STEPS, 0, 1 + lat_blk(i) // per_batch), 0, 0)),
            _resident((None, 1, D_MODEL), lambda i: (l, 0, 0)),
            pl.BlockSpec(memory_space=pl.ANY),
            _resident((None, 3, 2 * D_FF), lambda i: (l, 0, 0)),
            _resident((None, 1, 2 * D_FF), lambda i: (l, 0, 0)),
            pl.BlockSpec(memory_space=pl.ANY),
        ],
        out_specs=[pl.BlockSpec((TM_IN, D_MODEL), lambda i: (jnp.minimum(i, IN_CTX_STEPS - 1), 0)),
                   pl.BlockSpec((TM_IN, D_MODEL), lambda i: (lat_blk(i), 0))],
        out_shape=[jax.ShapeDtypeStruct((N_CTX, D_MODEL), F32), jax.ShapeDtypeStruct((N_LAT, D_MODEL), F32)],
        scratch_shapes=[
            pltpu.VMEM((TM_IN // TM, N_FF_CHUNKS, TM + 2 * HALO, 2 * FF_CHUNK), F32),
            pltpu.VMEM((D_MODEL, 2 * D_FF), BF16),
            pltpu.VMEM((D_FF, D_MODEL), BF16),
            pltpu.VMEM((2, D_MODEL, WUP_BLOCK_COLS), F32),
            pltpu.VMEM((2, WDN_BLOCK_ROWS, D_MODEL), F32),
            pltpu.SemaphoreType.DMA((2, 2)),
        ],
        compiler_params=_cparams(("arbitrary",)),
        name=f"ffn_l{l}",
    )(x1, x1, x1, mods, norm2, w_up, conv_w, conv_b, w_down)


def _cast_kernel(a_ref, o_ref):
    o_ref[...] = a_ref[...].astype(BF16)


def _cast_bf16(a, rows, name):
    _, r, c = a.shape
    spec = pl.BlockSpec((None, rows, c), lambda l, i: (l, i, 0))
    return pl.pallas_call(
        _cast_kernel, grid=(DEPTH, r // rows), in_specs=[spec], out_specs=spec,
        out_shape=jax.ShapeDtypeStruct(a.shape, BF16),
        compiler_params=_cparams(("parallel", "parallel")), name=name,
    )(a)


WPREP_COLS = 512
WPREP_QKV_BLOCKS = C_END // WPREP_COLS
WPREP_PLAIN = C_KC // WPREP_COLS
WPREP_BLOCKS = WPREP_QKV_BLOCKS + N_GATE // WPREP_COLS
_SRC_CKV = C_QD + 4 * MLA_QK


def _win_prep_start(j):
    return jnp.where(j < WPREP_PLAIN, WPREP_COLS * j,
                     jnp.where(j == WPREP_PLAIN, C_KC,
                               jnp.where(j == WPREP_PLAIN + 1, C_QD + 2 * MLA_QK,
                                         IN_QKV + WPREP_COLS * (j - WPREP_QKV_BLOCKS))))


def _win_prep_kernel(w_ref, q_ref, g_ref):
    j = pl.program_id(1)
    w = w_ref[0]
    zeros = lambda n: jnp.zeros((n, D_MODEL), F32)
    pad = zeros(LANES - MLA_QK)

    def two_heads(r0):
        return [w[r0:r0 + MLA_QK], pad, w[r0 + MLA_QK:r0 + 2 * MLA_QK], pad]

    @pl.when(j >= WPREP_QKV_BLOCKS)
    def _():
        g_ref[...] = w.T.astype(BF16)

    @pl.when(j < WPREP_PLAIN)
    def _():
        q_ref[...] = w.T.astype(BF16)

    @pl.when(j == WPREP_PLAIN)
    def _():
        n_kv = C_QD - C_KC
        q_ref[...] = jnp.concatenate([w[0:n_kv]] + two_heads(n_kv), axis=0).T.astype(BF16)

    @pl.when(j == WPREP_PLAIN + 1)
    def _():
        r_ckv = _SRC_CKV - (C_QD + 2 * MLA_QK)
        r_kpe = r_ckv + MLA_KV_LORA
        sel = two_heads(0) + [w[r_ckv:r_kpe], zeros(MLA_NOPE), w[r_kpe:r_kpe + MLA_ROPE], pad]
        q_ref[...] = jnp.concatenate(sel, axis=0).T.astype(BF16)


def _win_prep(w_in_t):
    window = pl.BlockSpec((pl.Element(1), pl.Element(WPREP_COLS), pl.Element(D_MODEL)),
                          lambda l, j: (l, pl.multiple_of(_win_prep_start(j), 32), 0))
    return pl.pallas_call(
        _win_prep_kernel,
        grid=(DEPTH, WPREP_BLOCKS),
        in_specs=[window],
        out_specs=[pl.BlockSpec((None, D_MODEL, WPREP_COLS), lambda l, j: (l, 0, jnp.minimum(j, WPREP_QKV_BLOCKS - 1))),
                   pl.BlockSpec((None, D_MODEL, WPREP_COLS), lambda l, j: (l, 0, jnp.maximum(j - WPREP_QKV_BLOCKS, 0)))],
        out_shape=[jax.ShapeDtypeStruct((DEPTH, D_MODEL, C_END), BF16),
                   jax.ShapeDtypeStruct((DEPTH, D_MODEL, N_GATE), BF16)],
        compiler_params=_cparams(("arbitrary", "arbitrary")),
        name="w_in_prep",
    )(w_in_t)


def _rope_tables():
    t = np.arange(DEC_SEQ)

    def angles(dim):
        half = dim // 2
        inv = ROPE_THETA ** (-np.arange(0, half, 2, dtype=np.float32) / half)
        row = (t // GRID_W).astype(np.float32)[:, None] * inv[None, :]
        col = (t % GRID_W).astype(np.float32)[:, None] * inv[None, :]
        return np.concatenate([row, row, col, col], axis=-1).astype(np.float32)

    def signed(ang, quarter):
        cos, sin = np.cos(ang), np.sin(ang)
        first = (np.arange(ang.shape[1]) % (2 * quarter)) < quarter
        return cos, np.where(first, -sin, 0.0), np.where(first, 0.0, sin)

    c64, lo64, hi64 = signed(angles(HEAD_DIM), 16)
    c64, lo64, hi64 = (np.tile(a, (1, 2)) for a in (c64, lo64, hi64))
    cm, lom, him = signed(angles(MLA_ROPE), 8)

    def mla_pad(a, fill):
        out = np.full((DEC_SEQ, LANES), fill, np.float32)
        out[:, MLA_NOPE:MLA_QK] = a
        return out

    tabs = [c64, lo64, hi64, mla_pad(cm, 1.0), mla_pad(lom, 0.0), mla_pad(him, 0.0)]
    ident = [np.ones, np.zeros, np.zeros, np.ones, np.zeros, np.zeros]
    full = [np.concatenate([f((TM_IN, LANES), np.float32), a.astype(np.float32)], axis=0) for f, a in zip(ident, tabs)]
    return jnp.asarray(np.stack(full, axis=0))


def _block_diag_ones():
    idx = np.arange(256)
    mats = [(idx[:, None] // w == idx[None, :] // w).astype(np.float32) for w in (HEAD_DIM, LANES)]
    return jnp.asarray(np.stack(mats, axis=0), dtype=BF16)


def _prep_small(w_ukv, qn_a, kn_a, qn_b, kn_b, qn_c, kn_c, qn_d, kn_d, kvn_d):
    ukv = w_ukv.reshape(DEPTH, MLA_KV_LORA, 4, 2, HEAD_DIM)
    k_part = jnp.pad(ukv[:, :, :, 0], ((0, 0), (0, 0), (0, 0), (0, HEAD_DIM))).reshape(DEPTH, MLA_KV_LORA, 512)
    v_part = ukv[:, :, :, 1].reshape(DEPTH, MLA_KV_LORA, 256)
    w_ukv_p = jnp.concatenate([k_part, v_part], axis=-1).astype(BF16)

    def row(g, reps, width=256):
        t = jnp.tile(g, (1, reps))
        return jnp.pad(t, ((0, 0), (0, width - t.shape[1])))

    pad96 = lambda g: jnp.pad(g, ((0, 0), (0, LANES - MLA_QK)))
    rows = [row(qn_a, 4), row(kn_a, 4), row(qn_b, 4), row(kn_b, 2), row(qn_c, 4), row(kn_c, 2),
            row(pad96(qn_d), 2), row(pad96(kn_d), 2), row(kvn_d, 1)]
    gains = jnp.stack(rows + [jnp.zeros_like(rows[0])] * (16 - len(rows)), axis=1)
    return w_ukv_p, gains


def kernel(x_prompt, x_sample, cache_nat_k, cache_nat_v, cache_gqa_k, cache_gqa_v, cache_win_k, cache_win_v,
           cache_mla_ckv, cache_mla_kpe, c, c_ctx, w_ada, b_ada, norm1, norm2, w_in, qn_a, kn_a, rpb_a,
           qn_b, kn_b, qn_c, kn_c, sink_c, qn_d, kn_d, kvn_d, w_ukv, w_branch, w_out, w_up, conv_w, conv_b,
           w_down):
    w_ukv_p, gains = _prep_small(w_ukv, qn_a, kn_a, qn_b, kn_b, qn_c, kn_c, qn_d, kn_d, kvn_d)
    w_qkv, w_gate = _win_prep(jnp.swapaxes(w_in, 1, 2))
    w_branch_b = _cast_bf16(w_branch.reshape(DEPTH, 4 * 256, D_MODEL), 1024, "cast_w_branch").reshape(
        DEPTH, 4, 256, D_MODEL)
    w_out_b = _cast_bf16(w_out, 1024, "cast_w_out")
    conv_b_r = conv_b.reshape(DEPTH, 1, 2 * D_FF)
    rope_tab = _rope_tables()
    bd = _block_diag_ones()
    norm1_r = norm1.reshape(DEPTH, 1, D_MODEL)
    norm2_r = norm2.reshape(DEPTH, 1, D_MODEL)

    cond8 = jnp.zeros((8, D_MODEL), F32).at[0].set(c_ctx).at[1:1 + DEC_BATCH].set(c)
    mods = _adaln(cond8, w_ada, b_ada).reshape(DEPTH, 8, 1, 6 * D_MODEL)

    rpb_flat = jnp.pad(rpb_a.reshape(-1), (0, 8192 - rpb_a.size))
    na_bias = _na_bias(rpb_flat)
    kpe_p = jnp.pad(cache_mla_kpe, ((0, 0), (0, 0), (0, 0), (MLA_NOPE, LANES - MLA_QK)))
    kd_ctx, vd_ctx = _mla_ctx(cache_mla_ckv, kpe_p, w_ukv_p, gains, bd)
    c_nat_k = cache_nat_k.reshape(DEC_BATCH, DEPTH, PAST_LEN, 256)
    c_nat_v = cache_nat_v.reshape(DEC_BATCH, DEPTH, PAST_LEN, 256)
    c_gqa_k = cache_gqa_k.reshape(DEC_BATCH, DEPTH, PAST_LEN, LANES)
    c_gqa_v = cache_gqa_v.reshape(DEC_BATCH, DEPTH, PAST_LEN, LANES)
    c_win_k = cache_win_k.reshape(DEC_BATCH, DEPTH, PAST_LEN, LANES)
    c_win_v = cache_win_v.reshape(DEC_BATCH, DEPTH, PAST_LEN, LANES)

    x_ctx, x_lat = x_prompt.reshape(N_CTX, D_MODEL), x_sample.reshape(N_LAT, D_MODEL)
    caches = [jnp.zeros((BATCH, DEPTH) + blk, F32) for blk in _CACHE_BLOCKS]
    for l in range(DEPTH):
        qkv, caches = _inproj(l, x_ctx, x_lat, mods, norm1_r, w_qkv, w_ukv_p, gains, bd, rope_tab, caches)
        qa, ka, va, qb, kb, vb, qc, kc, vc, qd, kd, vd = qkv
        sink_l = sink_c[l]
        o_ctx = _ctx_attention(l, sink_l, qkv)
        o_lat = (_lat_na(l, qa, ka, va, c_nat_k, c_nat_v, na_bias),
                 _lat_gqa(l, qb, kb, vb, c_gqa_k, c_gqa_v),
                 _lat_window(l, sink_l, qc, kc, vc, c_win_k, c_win_v),
                 _lat_mla(l, qd, kd, vd, kd_ctx, vd_ctx))
        x1 = _merge(l, x_ctx, x_lat, o_ctx, o_lat, mods, norm1_r, w_gate, w_branch_b, w_out_b)
        x_ctx, x_lat = _ffn(l, x1, mods, norm2_r, w_up, conv_w, conv_b_r, w_down)
    y_prompt = x_ctx.reshape(BATCH, SEQ, D_MODEL)
    y_sample = x_lat.reshape(DEC_BATCH, DEC_SEQ, D_MODEL)
    heads = (4, 4, 2, 2, 2, 2)
    outs = [a.reshape(BATCH, DEPTH, h, HEAD_DIM, SEQ).transpose(0, 1, 4, 2, 3) for a, h in zip(caches[:6], heads)]
    return (y_prompt, y_sample, *outs, jnp.swapaxes(caches[6], 2, 3), jnp.swapaxes(caches[7], 2, 3))
```

```python
import functools

import jax
import jax.numpy as jnp
import numpy as np
from jax import lax
from jax.experimental import pallas as pl
from jax.experimental.pallas import tpu as pltpu

F32 = jnp.float32
BF16 = jnp.bfloat16

D_MODEL = 1024
BATCH = 16
SEQ = 256
DEPTH = 4
DEC_BATCH = 2
DEC_SEQ = 2048
PAST_LEN = 256
GRID_W = 64
HEAD_DIM = 64
NA_ROWS = 8
NA_COLS = 16
WINDOW = 128
MLA_NOPE = 64
MLA_ROPE = 32
MLA_QK = MLA_NOPE + MLA_ROPE
MLA_KV_LORA = 128
D_FF = 2816
ROPE_THETA = 10000.0
EPS = 1e-6
NEG_INF = -1e30

LANES = 128
TM = 256
N_CTX = BATCH * SEQ
N_LAT = DEC_BATCH * DEC_SEQ
N_TOK = N_CTX + N_LAT
CTX_TILES = N_CTX // TM
LAT_TILES_PER_BATCH = DEC_SEQ // TM
N_TILES = N_TOK // TM
GRID_ROWS = DEC_SEQ // GRID_W
FF_CHUNK = 256
N_FF_CHUNKS = D_FF // FF_CHUNK
VMEM_LIMIT = 56 * 1024 * 1024

C_QA, C_KA, C_VA = 0, 256, 512
C_QB, C_KB, C_VB = 768, 1024, 1152
C_QC, C_KC, C_VC = 1280, 1536, 1664
C_QD, C_CKV, C_KPE, C_END = 1792, 2304, 2432, 2560
N_GATE = 4 * D_MODEL
IN_QKV = 2336
LOG2E = float(np.log2(np.e))
Q_SCALE = HEAD_DIM ** -0.5 * LOG2E
MLA_Q_SCALE = MLA_QK ** -0.5 * LOG2E


def _cparams(sem):
    return pltpu.CompilerParams(dimension_semantics=sem, vmem_limit_bytes=VMEM_LIMIT)


def _tile_group(i):
    return jnp.where(i < CTX_TILES, 0, 1 + (i - CTX_TILES) // LAT_TILES_PER_BATCH)


def _tile_pos_block(i):
    return jnp.where(i < CTX_TILES, 0, 1 + (i - CTX_TILES) % LAT_TILES_PER_BATCH)


def _resident(shape, index_map):
    return pl.BlockSpec(shape, index_map, pipeline_mode=pl.Buffered(1))


def _dot(a, b):
    return jnp.dot(a, b, preferred_element_type=F32)


def _dot_t(a, b):
    return lax.dot_general(a, b, (((1,), (1,)), ((), ())), preferred_element_type=F32)


def _rms_mod(x, gain, scale, shift):
    ms = jnp.mean(x * x, axis=-1, keepdims=True)
    return (x * lax.rsqrt(ms + EPS)) * gain * (1.0 + scale) + shift


def _group_norm(t, ones_bd, inv_n, gain):
    ss = _dot((t * t).astype(BF16), ones_bd)
    return t * lax.rsqrt(ss * inv_n + EPS) * gain


def _rope(t, cos, sin_lo, sin_hi, half):
    return (t * cos + pltpu.roll(t, LANES - half, 1) * sin_lo + pltpu.roll(t, half, 1) * sin_hi)


def _lane_iota(shape):
    return lax.broadcasted_iota(jnp.int32, shape, len(shape) - 1)


def _low_half(shape):
    return _lane_iota(shape) < HEAD_DIM


def _run_units(units, depth=2):
    n = len(units)
    pending = {}
    outs = []
    for k in range(min(depth, n)):
        pending[k] = units[k][0]()
    for k in range(n):
        outs.append(units[k][1](pending.pop(k)))
        if k + depth < n:
            pending[k + depth] = units[k + depth][0]()
    return outs


N_COND = 1 + DEC_BATCH


def _adaln_kernel(cond_ref, w_ref, b_ref, out_ref):
    cnd = cond_ref[...]
    s_t = (cnd * (1.0 / (1.0 + jnp.exp(-cnd)))).T
    w = w_ref[...]
    out_ref[...] = jnp.zeros(out_ref.shape, F32)
    for r in range(N_COND):
        out_ref[r:r + 1, :] = jnp.sum(s_t[:, r:r + 1] * w, axis=0, keepdims=True) + b_ref[...]


def _adaln(cond8, w_ada, b_ada):
    n_col = 6 * D_MODEL
    tn = 1024
    return pl.pallas_call(
        _adaln_kernel,
        grid=(DEPTH, n_col // tn),
        in_specs=[
            pl.BlockSpec((8, D_MODEL), lambda l, n: (0, 0)),
            pl.BlockSpec((None, D_MODEL, tn), lambda l, n: (l, 0, n)),
            pl.BlockSpec((None, 1, tn), lambda l, n: (l, 0, n)),
        ],
        out_specs=pl.BlockSpec((None, 8, tn), lambda l, n: (l, 0, n)),
        out_shape=jax.ShapeDtypeStruct((DEPTH, 8, n_col), F32),
        compiler_params=_cparams(("parallel", "parallel")),
        name="adaln",
    )(cond8, w_ada, b_ada.reshape(DEPTH, 1, n_col))


_QKV_WIDTHS = (512, 256, 256, 512, 128, 128, 512, 128, 128, 512, 512, 256)
_CACHE_WIDTHS = (256, 256, 128, 128, 128, 128, MLA_KV_LORA, MLA_ROPE)
_CACHE_BLOCKS = tuple((w, SEQ) for w in _CACHE_WIDTHS)


TM_IN = 2 * TM
IN_STEPS = N_TOK // TM_IN
IN_CTX_STEPS = N_CTX // TM_IN


def _inproj_kernel(xc_ref, xl_ref, mod_ref, n1_ref, wt_hbm, wukv_ref, g_ref, bd_ref, rope_ref, *rest, layer):
    n_out = len(_QKV_WIDTHS) + len(_CACHE_WIDTHS)
    outs = rest[len(_CACHE_WIDTHS):len(_CACHE_WIDTHS) + n_out]
    w_ref, stage, sems = rest[len(_CACHE_WIDTHS) + n_out:]

    @pl.when(pl.program_id(0) == 0)
    def _():
        _load_transposed_bf16(wt_hbm, layer, QKV_BLOCK_STARTS, _qkv_block_rows, w_ref, stage, sems)

    is_ctx = pl.program_id(0) >= IN_STEPS - IN_CTX_STEPS
    units = []
    for half in range(TM_IN // TM):
        rows = slice(half * TM, (half + 1) * TM)
        views = [o.at[rows] for o in outs[:len(_QKV_WIDTHS)]] + [o.at[half] for o in outs[len(_QKV_WIDTHS):]]
        units += _inproj_half(is_ctx, xc_ref.at[rows], xl_ref.at[rows], mod_ref, n1_ref, w_ref, wukv_ref, g_ref,
                              bd_ref, rope_ref.at[:, rows, :], views)
    _run_units(units, depth=2)


def _inproj_half(is_ctx, xc_ref, xl_ref, mod_ref, n1_ref, w_ref, wukv_ref, g_ref, bd_ref, rope_ref, outs):
    qa_o, ka_o, va_o, qb_o, kb_o, vb_o, qc_o, kc_o, vc_o, qd_o, kd_o, vd_o = outs[:len(_QKV_WIDTHS)]
    natk_o, natv_o, gqak_o, gqav_o, wink_o, winv_o, ckv_o, kpe_o = outs[len(_QKV_WIDTHS):]
    x = jnp.where(is_ctx, xc_ref[...], xl_ref[...])
    mod = mod_ref[...]
    hb = _rms_mod(x, n1_ref[...], mod[:, D_MODEL:2 * D_MODEL], mod[:, 0:D_MODEL]).astype(BF16)
    bd64, bd128 = bd_ref[0], bd_ref[1]
    cos64, slo64, shi64 = rope_ref[0], rope_ref[1], rope_ref[2]
    cosm, slom, shim = rope_ref[3], rope_ref[4], rope_ref[5]
    inv64, inv96 = 1.0 / HEAD_DIM, 1.0 / MLA_QK
    low = _low_half((TM, LANES))
    held = {}

    def chunks(t):
        return [t[:, c:c + LANES] for c in range(0, t.shape[1], LANES)]

    def rope64(t):
        parts = [_rope(p, cos64, slo64, shi64, 16) for p in chunks(t)]
        return parts[0] if len(parts) == 1 else jnp.concatenate(parts, axis=1)

    def ropem(t):
        return jnp.concatenate([_rope(p, cosm, slom, shim, 8) for p in chunks(t)], axis=1)

    def store_q(ref, q, grouped):
        q = q * Q_SCALE
        slots = []
        for hd in range(4):
            part, src = q[:, (hd // 2) * LANES:(hd // 2 + 1) * LANES], hd % 2
            dst = hd // 2 if grouped else src
            if src != dst:
                part = pltpu.roll(part, HEAD_DIM, 1)
            slots.append(jnp.where(low if dst == 0 else ~low, part, 0.0))
        ref[...] = jnp.concatenate(slots, axis=1).astype(BF16)

    def post_latent(t):
        ckv = _group_norm(t[:, :LANES], bd128[:LANES, :LANES], 1.0 / MLA_KV_LORA, g_ref[8:9, :LANES])
        kpe = t[:, LANES:]
        ckv_o[...] = ckv.T
        kpe_o[...] = kpe.T[MLA_NOPE:MLA_QK, :]
        held["kpe"] = kpe
        held["kv"] = _dot(ckv.astype(BF16), wukv_ref[...])

    def post_kd(_):
        kv, kpe = held["kv"], held["kpe"]
        for j in range(2):
            kf = jnp.concatenate([kv[:, (2 * j) * LANES:(2 * j + 1) * LANES] + kpe,
                                  kv[:, (2 * j + 1) * LANES:(2 * j + 2) * LANES] + kpe], axis=1)
            kd_o[:, 2 * j * LANES:(2 * j + 2) * LANES] = ropem(
                _group_norm(kf, bd128, inv96, g_ref[7:8, :])).astype(BF16)
        vd_o[...] = kv[:, 4 * LANES:].astype(BF16)

    def post_qa(t):
        store_q(qa_o, _group_norm(t, bd64, inv64, g_ref[0:1, :]), grouped=False)

    def post_ka(t):
        k = _group_norm(t, bd64, inv64, g_ref[1:2, :])
        natk_o[...] = k.T
        ka_o[...] = k.astype(BF16)

    def post_va(t):
        natv_o[...] = t.T
        va_o[...] = t.astype(BF16)

    def post_q_rope(ref, row):
        def post(t):
            store_q(ref, rope64(_group_norm(t, bd64, inv64, g_ref[row:row + 1, :])), grouped=True)
        return post

    def post_kv(k_ref, v_ref, kc_ref, vc_ref, row):
        def post(t):
            k = _group_norm(t[:, :LANES], bd64[:LANES, :LANES], inv64, g_ref[row:row + 1, :LANES])
            kc_ref[...] = k.T
            k_ref[...] = rope64(k).astype(BF16)
            v = t[:, LANES:]
            vc_ref[...] = v.T
            v_ref[...] = v.astype(BF16)
        return post

    def post_qd(j):
        def post(t):
            qd_o[:, 2 * j * LANES:(2 * j + 2) * LANES] = (ropem(
                _group_norm(t, bd128, inv96, g_ref[6:7, :])) * MLA_Q_SCALE).astype(BF16)
        return post

    stages = [
        ((C_CKV, C_END), post_latent),
        ((C_QA, C_KA), post_qa),
        (None, post_kd),
        ((C_QB, C_KB), post_q_rope(qb_o, 2)),
        ((C_QC, C_KC), post_q_rope(qc_o, 4)),
        ((C_QD, C_QD + 2 * LANES), post_qd(0)),
        ((C_QD + 2 * LANES, C_CKV), post_qd(1)),
        ((C_KB, C_QC), post_kv(kb_o, vb_o, gqak_o, gqav_o, 3)),
        ((C_KC, C_QD), post_kv(kc_o, vc_o, wink_o, winv_o, 5)),
        ((C_KA, C_VA), post_ka),
        ((C_VA, C_QB), post_va),
    ]

    def issue(cols):
        return None if cols is None else _dot(hb, w_ref[:, cols[0]:cols[1]])

    return [(functools.partial(issue, cols), post) for cols, post in stages]


def _ctx_block(i):
    return jnp.minimum(i, CTX_TILES - 1)


def _lat_block(i):
    return jnp.maximum(i - CTX_TILES, 0)


def _inproj(l, x_ctx, x_lat, mods, norm1, w_in_t, w_ukv_p, gains, bd, rope_tab, caches):
    n_lat = IN_STEPS - IN_CTX_STEPS
    blk = lambda s: (s + IN_CTX_STEPS) % IN_STEPS
    lat_blk = lambda s: jnp.maximum(blk(s) - IN_CTX_STEPS, 0)
    per_batch = DEC_SEQ // TM_IN
    tok = lambda w: pl.BlockSpec((TM_IN, w), lambda s: (blk(s), 0))
    in_specs = [
        pl.BlockSpec((TM_IN, D_MODEL), lambda s: (jnp.minimum(blk(s), IN_CTX_STEPS - 1), 0)),
        pl.BlockSpec((TM_IN, D_MODEL), lambda s: (lat_blk(s), 0)),
        pl.BlockSpec((None, None, 1, 6 * D_MODEL),
                     lambda s: (l, jnp.where(s < n_lat, 1 + lat_blk(s) // per_batch, 0), 0, 0)),
        _resident((None, 1, D_MODEL), lambda s: (l, 0, 0)),
        pl.BlockSpec(memory_space=pl.ANY),
        _resident((None, MLA_KV_LORA, 768), lambda s: (l, 0, 0)),
        _resident((None, 16, 256), lambda s: (l, 0, 0)),
        _resident((2, 256, 256), lambda s: (0, 0, 0)),
        pl.BlockSpec((6, TM_IN, LANES), lambda s: (0, jnp.where(s < n_lat, 1 + lat_blk(s) % per_batch, 0), 0)),
    ]
    args = [x_ctx, x_lat, mods, norm1, w_in_t, w_ukv_p, gains, bd, rope_tab] + list(caches)
    aliases = {len(in_specs) + n: len(_QKV_WIDTHS) + n for n in range(len(caches))}
    in_specs = in_specs + [pl.BlockSpec(memory_space=pl.ANY)] * len(caches)
    cache_spec = lambda cb: pl.BlockSpec((TM_IN // SEQ, None) + cb, lambda s: (jnp.maximum(s - n_lat, 0), l, 0, 0))
    outs = pl.pallas_call(
        functools.partial(_inproj_kernel, layer=l),
        grid=(IN_STEPS,),
        in_specs=in_specs,
        out_specs=[tok(w) for w in _QKV_WIDTHS] + [cache_spec(blk) for blk in _CACHE_BLOCKS],
        out_shape=[jax.ShapeDtypeStruct((N_TOK, w), BF16) for w in _QKV_WIDTHS]
        + [jax.ShapeDtypeStruct((BATCH, DEPTH) + blk, F32) for blk in _CACHE_BLOCKS],
        scratch_shapes=[pltpu.VMEM((D_MODEL, C_END), BF16), pltpu.VMEM((2, WT_ROWS, D_MODEL), F32),
                        pltpu.SemaphoreType.DMA((2,))],
        input_output_aliases=aliases,
        compiler_params=_cparams(("arbitrary",)),
        name=f"inproj_l{l}",
    )(*args)
    return outs[:len(_QKV_WIDTHS)], outs[len(_QKV_WIDTHS):]


def _scores(q, keys):
    out = []
    for k, bias in keys:
        s = _dot_t(q, k)
        if bias is not None:
            s = s + bias
        out.append(s)
    return out


def _softmax_pv(scores, values, sink=None):
    m = functools.reduce(jnp.maximum, [jnp.max(s, axis=-1, keepdims=True) for s in scores])
    if sink is not None:
        m = jnp.maximum(m, sink)
    denom = None
    out = None
    for s, v in zip(scores, values):
        e = jnp.exp2(s - m)
        d = jnp.sum(e, axis=-1, keepdims=True)
        o = _dot(e.astype(BF16), v)
        denom = d if denom is None else denom + d
        out = o if out is None else out + o
    if sink is not None:
        denom = denom + jnp.exp2(sink - m)
    return out / denom


def _pair_unit(q_ref, rows, j, keys, values, kv_half, sinks=None):
    m_rows = rows.stop - rows.start

    def issue():
        q2 = jnp.concatenate([q_ref[rows, (2 * j) * LANES:(2 * j + 1) * LANES],
                              q_ref[rows, (2 * j + 1) * LANES:(2 * j + 2) * LANES]], axis=0)
        return _scores(q2, keys())

    def finish(scores):
        sink = None
        if sinks is not None:
            row = lax.broadcasted_iota(jnp.int32, (2 * m_rows, 1), 0)
            sink = jnp.where(row < m_rows, sinks[0], sinks[1])
        o2 = _softmax_pv(scores, values(), sink)
        lo, hi = o2[:m_rows], o2[m_rows:]
        if kv_half == 0:
            hi = pltpu.roll(hi, HEAD_DIM, 1)
        elif kv_half == 1:
            lo = pltpu.roll(lo, HEAD_DIM, 1)
        return jnp.where(_low_half(lo.shape), lo, hi)

    return issue, finish


def _single_unit(q_ref, rows, hd, keys, values):
    def issue():
        return _scores(q_ref[rows, hd * LANES:(hd + 1) * LANES], keys())

    def finish(scores):
        return _softmax_pv(scores, values())

    return issue, finish


def _merge_halves(o_even, o_odd):
    return jnp.where(_low_half(o_even.shape), o_even, o_odd)


def _ctx_attn_kernel(sink_ref, qa, ka, va, qb, kb, vb, qc, kc, vc, qd, kd, vd, o_ref):
    rows = slice(0, SEQ)
    units = []
    for j in range(2):
        c = slice(j * LANES, (j + 1) * LANES)
        units.append(_pair_unit(qa, rows, j, lambda c=c: [(ka[:, c], None)], lambda c=c: [va[:, c]], None))
    for q, k, v, with_sink in ((qb, kb, vb, False), (qc, kc, vc, True)):
        for j in range(2):
            sinks = (sink_ref[2 * j] * LOG2E, sink_ref[2 * j + 1] * LOG2E) if with_sink else None
            units.append(_pair_unit(q, rows, j, lambda k=k: [(k[...], None)], lambda v=v: [v[...]], j, sinks))
    for hd in range(4):
        c = slice(hd * LANES, (hd + 1) * LANES)
        vs = slice((hd // 2) * LANES, (hd // 2 + 1) * LANES)
        units.append(_single_unit(qd, rows, hd, lambda c=c: [(kd[:, c], None)], lambda vs=vs: [vd[:, vs]]))
    outs = _run_units(units)
    for n in range(6):
        o_ref[:, n * LANES:(n + 1) * LANES] = outs[n].astype(BF16)
    for j in range(2):
        o_ref[:, (6 + j) * LANES:(7 + j) * LANES] = _merge_halves(outs[6 + 2 * j], outs[7 + 2 * j]).astype(BF16)


def _ctx_attention(l, sink_c, qkv):
    blk = lambda w: pl.BlockSpec((SEQ, w), lambda b: (b, 0))
    return pl.pallas_call(
        _ctx_attn_kernel,
        grid=(BATCH,),
        in_specs=[pl.BlockSpec(memory_space=pltpu.SMEM)] + [blk(w) for w in _QKV_WIDTHS],
        out_specs=pl.BlockSpec((SEQ, D_MODEL), lambda b: (b, 0)),
        out_shape=jax.ShapeDtypeStruct((N_CTX, D_MODEL), BF16),
        compiler_params=_cparams(("parallel",)),
        name=f"ctx_attn_l{l}",
    )(sink_c, *qkv)


LAT_TQ = 256
LAT_SUB = 128
LAT_QBLOCKS = DEC_SEQ // LAT_TQ
LAT_ROW0 = N_CTX // LAT_TQ
LAT_KV0 = N_CTX // DEC_SEQ


def _lat_qspec(width):
    return pl.BlockSpec((LAT_TQ, width), lambda b, i: (LAT_ROW0 + b * LAT_QBLOCKS + i, 0))


def _lat_kvspec(width):
    return pl.BlockSpec((DEC_SEQ, width), lambda b, i: (LAT_KV0 + b, 0))


def _lat_call(kernel, l, name, in_specs, args):
    return pl.pallas_call(
        kernel,
        grid=(DEC_BATCH, LAT_QBLOCKS),
        in_specs=in_specs,
        out_specs=pl.BlockSpec((LAT_TQ, 256), lambda b, i: (b * LAT_QBLOCKS + i, 0)),
        out_shape=jax.ShapeDtypeStruct((N_LAT, 256), BF16),
        compiler_params=_cparams(("parallel", "parallel")),
        name=f"{name}_l{l}",
    )(*args)


def _sub_rows(u):
    return slice(u * LAT_SUB, (u + 1) * LAT_SUB)


def _store_units(o_ref, outs):
    for n, o in enumerate(outs):
        u, j = divmod(n, 2)
        o_ref[_sub_rows(u), j * LANES:(j + 1) * LANES] = o.astype(BF16)


def _gqa_kernel(q_ref, k_ref, v_ref, kc_ref, vc_ref, o_ref):
    keys = lambda: [(kc_ref[...].astype(BF16), None), (k_ref[...], None)]
    values = lambda: [vc_ref[...].astype(BF16), v_ref[...]]
    units = [_pair_unit(q_ref, _sub_rows(u), j, keys, values, j) for u in range(LAT_TQ // LAT_SUB) for j in range(2)]
    _store_units(o_ref, _run_units(units))


def _lat_gqa(l, qb, kb, vb, cache_k, cache_v):
    cspec = pl.BlockSpec((None, None, PAST_LEN, LANES), lambda b, i: (b, l, 0, 0))
    return _lat_call(_gqa_kernel, l, "lat_gqa",
                     [_lat_qspec(512), _lat_kvspec(LANES), _lat_kvspec(LANES), cspec, cspec],
                     (qb, kb, vb, cache_k, cache_v))


WIN_KEYS = 3 * LAT_SUB


def _win_kernel(sink_ref, q_ref, k_ref, v_ref, kc_ref, vc_ref, o_ref):
    units = []
    for u in range(LAT_TQ // LAT_SUB):
        blk = pl.program_id(1) * (LAT_TQ // LAT_SUB) + u
        start = pl.multiple_of(jnp.clip((blk - 1) * LAT_SUB, 0, DEC_SEQ - WIN_KEYS), LAT_SUB)

        def keys(blk=blk, start=start):
            row = lax.broadcasted_iota(jnp.int32, (2 * LAT_SUB, WIN_KEYS), 0)
            qpos = blk * LAT_SUB + jnp.where(row < LAT_SUB, row, row - LAT_SUB)
            kpos = start + lax.broadcasted_iota(jnp.int32, (2 * LAT_SUB, WIN_KEYS), 1)
            bias = jnp.where(jnp.abs(qpos - kpos) <= WINDOW, 0.0, NEG_INF)
            return [(kc_ref[...].astype(BF16), None), (k_ref[pl.ds(start, WIN_KEYS), :], bias)]

        def values(start=start):
            return [vc_ref[...].astype(BF16), v_ref[pl.ds(start, WIN_KEYS), :]]

        for j in range(2):
            sinks = (sink_ref[2 * j] * LOG2E, sink_ref[2 * j + 1] * LOG2E)
            units.append(_pair_unit(q_ref, _sub_rows(u), j, keys, values, j, sinks))
    _store_units(o_ref, _run_units(units, depth=4))


def _lat_window(l, sink_c, qc, kc, vc, cache_k, cache_v):
    cspec = pl.BlockSpec((None, None, PAST_LEN, LANES), lambda b, i: (b, l, 0, 0))
    return _lat_call(_win_kernel, l, "lat_win",
                     [pl.BlockSpec(memory_space=pltpu.SMEM), _lat_qspec(512), _lat_kvspec(LANES),
                      _lat_kvspec(LANES), cspec, cspec],
                     (sink_c, qc, kc, vc, cache_k, cache_v))


def _mla_kernel(q_ref, k_ref, v_ref, kc_ref, vc_ref, o_ref):
    units = []
    rows = slice(0, LAT_TQ)
    for hd in range(4):
        c = slice(hd * LANES, (hd + 1) * LANES)
        vs = slice((hd // 2) * LANES, (hd // 2 + 1) * LANES)
        units.append(_single_unit(q_ref, rows, hd,
                                  lambda c=c: [(kc_ref[:, c], None), (k_ref[:, c], None)],
                                  lambda vs=vs: [vc_ref[:, vs], v_ref[:, vs]]))
    outs = _run_units(units, depth=3)
    for j in range(2):
        o_ref[:, j * LANES:(j + 1) * LANES] = _merge_halves(outs[2 * j], outs[2 * j + 1]).astype(BF16)


def _lat_mla(l, qd, kd, vd, kd_ctx, vd_ctx):
    kcs = pl.BlockSpec((None, None, PAST_LEN, 512), lambda b, i: (l, b, 0, 0))
    vcs = pl.BlockSpec((None, None, PAST_LEN, 256), lambda b, i: (l, b, 0, 0))
    return _lat_call(_mla_kernel, l, "lat_mla",
                     [_lat_qspec(512), _lat_kvspec(512), _lat_kvspec(256), kcs, vcs],
                     (qd, kd, vd, kd_ctx, vd_ctx))


NA_KEYS = NA_ROWS * GRID_W
NA_ROWS_PER_STEP = LAT_TQ // GRID_W


def _na_shift(r):
    return jnp.where(r < NA_ROWS // 2, r,
                     jnp.where(r > GRID_ROWS - NA_ROWS // 2, r - (GRID_ROWS - NA_ROWS), NA_ROWS // 2))


def _na_kernel(q_ref, k_ref, v_ref, kc_ref, vc_ref, bias_ref, o_ref):
    units = []
    for rr in range(NA_ROWS_PER_STEP):
        r = pl.program_id(1) * NA_ROWS_PER_STEP + rr
        start = pl.multiple_of(jnp.clip(r - NA_ROWS // 2, 0, GRID_ROWS - NA_ROWS) * GRID_W, GRID_W)
        shift = _na_shift(r)
        rows = slice(rr * GRID_W, (rr + 1) * GRID_W)
        for j in range(2):
            c = slice(j * LANES, (j + 1) * LANES)

            def keys(j=j, c=c, start=start, shift=shift):
                bias = jnp.concatenate([bias_ref[2 * j, shift], bias_ref[2 * j + 1, shift]], axis=0)
                return [(kc_ref[:, c].astype(BF16), None), (k_ref[pl.ds(start, NA_KEYS), c], bias)]

            def values(c=c, start=start):
                return [vc_ref[:, c].astype(BF16), v_ref[pl.ds(start, NA_KEYS), c]]

            units.append(_pair_unit(q_ref, rows, j, keys, values, None))
    for n, o in enumerate(_run_units(units, depth=4)):
        rr, j = divmod(n, 2)
        o_ref[rr * GRID_W:(rr + 1) * GRID_W, j * LANES:(j + 1) * LANES] = o.astype(BF16)


def _lat_na(l, qa, ka, va, cache_k, cache_v, na_bias):
    cspec = pl.BlockSpec((None, None, PAST_LEN, 256), lambda b, i: (b, l, 0, 0))
    bspec = _resident((None, 4, NA_ROWS, GRID_W, NA_KEYS), lambda b, i: (l, 0, 0, 0, 0))
    return _lat_call(_na_kernel, l, "lat_na",
                     [_lat_qspec(512), _lat_kvspec(256), _lat_kvspec(256), cspec, cspec, bspec],
                     (qa, ka, va, cache_k, cache_v, na_bias))


def _na_bias_kernel(rpb_ref, out_ref, t_ref):
    l = pl.program_id(0)
    n_dr, n_dc = 2 * NA_ROWS - 1, 2 * NA_COLS - 1
    shape = (GRID_W, LANES)
    qc = lax.broadcasted_iota(jnp.int32, shape, 0)
    lane = _lane_iota(shape)
    kc = lane & (GRID_W - 1)
    dc = jnp.clip(kc - qc, 1 - NA_COLS, NA_COLS - 1) + (NA_COLS - 1)
    c0 = jnp.clip(qc - NA_COLS // 2, 0, GRID_W - NA_COLS)
    ok = (kc >= c0) & (kc < c0 + NA_COLS)
    for hd in range(4):
        def fill(a, carry):
            base = ((l * 4 + hd) * n_dr + a) * n_dc
            t = jnp.zeros(shape, F32)
            for b in range(n_dc):
                t = jnp.where(dc == b, rpb_ref[base + b], t)
            t_ref[a] = jnp.where(ok, t * LOG2E, NEG_INF)
            return carry
        lax.fori_loop(0, n_dr, fill, 0)
        for d in range(NA_ROWS):
            for m in range(NA_ROWS // 2):
                lo = t_ref[2 * m - d + NA_ROWS - 1]
                hi = t_ref[2 * m + 1 - d + NA_ROWS - 1]
                out_ref[hd, d, :, m * LANES:(m + 1) * LANES] = jnp.where(lane < GRID_W, lo, hi)


def _na_bias(rpb_flat):
    return pl.pallas_call(
        _na_bias_kernel,
        grid=(DEPTH,),
        in_specs=[pl.BlockSpec(memory_space=pltpu.SMEM)],
        out_specs=pl.BlockSpec((None, 4, NA_ROWS, GRID_W, NA_KEYS), lambda l: (l, 0, 0, 0, 0)),
        out_shape=jax.ShapeDtypeStruct((DEPTH, 4, NA_ROWS, GRID_W, NA_KEYS), F32),
        scratch_shapes=[pltpu.VMEM((2 * NA_ROWS - 1, GRID_W, LANES), F32)],
        compiler_params=_cparams(("parallel",)),
        name="na_bias",
    )(rpb_flat)


def _mla_ctx_kernel(ckv_ref, kpe_ref, wukv_ref, g_ref, bd_ref, kd_o, vd_o):
    kv = _dot(ckv_ref[...].astype(BF16), wukv_ref[...])
    kpe = kpe_ref[...]
    for j in range(2):
        kf = jnp.concatenate([kv[:, (2 * j) * LANES:(2 * j + 1) * LANES] + kpe,
                              kv[:, (2 * j + 1) * LANES:(2 * j + 2) * LANES] + kpe], axis=1)
        kd_o[:, 2 * j * LANES:(2 * j + 2) * LANES] = _group_norm(
            kf, bd_ref[1], 1.0 / MLA_QK, g_ref[7:8, :]).astype(BF16)
    vd_o[...] = kv[:, 4 * LANES:].astype(BF16)


def _mla_ctx(cache_ckv, cache_kpe_p, w_ukv_p, gains, bd):
    return pl.pallas_call(
        _mla_ctx_kernel,
        grid=(DEPTH, DEC_BATCH),
        in_specs=[
            pl.BlockSpec((None, None, PAST_LEN, LANES), lambda l, b: (b, l, 0, 0)),
            pl.BlockSpec((None, None, PAST_LEN, LANES), lambda l, b: (b, l, 0, 0)),
            pl.BlockSpec((None, MLA_KV_LORA, 768), lambda l, b: (l, 0, 0)),
            pl.BlockSpec((None, 16, 256), lambda l, b: (l, 0, 0)),
            pl.BlockSpec((2, 256, 256), lambda l, b: (0, 0, 0)),
        ],
        out_specs=[pl.BlockSpec((None, None, PAST_LEN, 512), lambda l, b: (l, b, 0, 0)),
                   pl.BlockSpec((None, None, PAST_LEN, 256), lambda l, b: (l, b, 0, 0))],
        out_shape=[jax.ShapeDtypeStruct((DEPTH, DEC_BATCH, PAST_LEN, 512), BF16),
                   jax.ShapeDtypeStruct((DEPTH, DEC_BATCH, PAST_LEN, 256), BF16)],
        compiler_params=_cparams(("parallel", "parallel")),
        name="mla_ctx",
    )(cache_ckv, cache_kpe_p, w_ukv_p, gains, bd)


def _merge_kernel(xc_ref, xl_ref, oc_ref, oa_ref, ob_ref, occ_ref, od_ref, mod_ref, n1_ref, wt_hbm, wb_ref, wo_ref,
                  x1_ref, wg_ref, stage, sems, *, layer):
    @pl.when(pl.program_id(0) == 0)
    def _():
        _load_transposed_bf16(wt_hbm, layer, GATE_BLOCK_STARTS, lambda b, w: w, wg_ref, stage, sems)

    is_ctx = pl.program_id(0) < IN_CTX_STEPS
    mod = mod_ref[...]
    lat_refs = (oa_ref, ob_ref, occ_ref, od_ref)

    def half_units(half):
        rows = slice(half * TM, (half + 1) * TM)
        st = {"merged": None}
        st["x"] = jnp.where(is_ctx, xc_ref[rows, :], xl_ref[rows, :])
        st["hb"] = _rms_mod(st["x"], n1_ref[...], mod[:, D_MODEL:2 * D_MODEL], mod[:, 0:D_MODEL]).astype(BF16)

        def branch(n):
            def issue():
                o = jnp.where(is_ctx, oc_ref[rows, n * 256:(n + 1) * 256], lat_refs[n][rows, :])
                return _dot(st["hb"], wg_ref[:, n * D_MODEL:(n + 1) * D_MODEL]), _dot(o, wb_ref[n])

            def finish(res):
                gate, br = res
                t = (1.0 / (1.0 + jnp.exp(-gate))) * br
                st["merged"] = t if st["merged"] is None else st["merged"] + t
            return issue, finish

        def out_issue():
            return _dot(st["merged"].astype(BF16), wo_ref[...])

        def out_finish(out):
            x1_ref[rows, :] = st["x"] + mod[:, 2 * D_MODEL:3 * D_MODEL] * out

        return [branch(n) for n in range(4)], (out_issue, out_finish)

    (a_br, a_out), (b_br, b_out) = half_units(0), half_units(1)
    idle = (lambda: None, lambda _: None)
    _run_units(a_br + b_br[:2] + [a_out] + b_br[2:] + [idle, b_out], depth=2)


def _merge(l, x_ctx, x_lat, o_ctx, o_lat, mods, norm1, w_in_t, w_branch, w_out):
    per_batch = DEC_SEQ // TM_IN
    ctx_blk = lambda i: jnp.minimum(i, IN_CTX_STEPS - 1)
    lat_blk = lambda i: jnp.maximum(i - IN_CTX_STEPS, 0)
    tok = pl.BlockSpec((TM_IN, D_MODEL), lambda i: (i, 0))
    ctx_tok = pl.BlockSpec((TM_IN, D_MODEL), lambda i: (ctx_blk(i), 0))
    lat_tok = lambda w: pl.BlockSpec((TM_IN, w), lambda i: (lat_blk(i), 0))
    return pl.pallas_call(
        functools.partial(_merge_kernel, layer=l),
        grid=(IN_STEPS,),
        in_specs=[
            ctx_tok, lat_tok(D_MODEL), ctx_tok, lat_tok(256), lat_tok(256), lat_tok(256), lat_tok(256),
            pl.BlockSpec((None, None, 1, 6 * D_MODEL),
                         lambda i: (l, jnp.where(i < IN_CTX_STEPS, 0, 1 + lat_blk(i) // per_batch), 0, 0)),
            _resident((None, 1, D_MODEL), lambda i: (l, 0, 0)),
            pl.BlockSpec(memory_space=pl.ANY),
            _resident((None, 4, 256, D_MODEL), lambda i: (l, 0, 0, 0)),
            _resident((None, D_MODEL, D_MODEL), lambda i: (l, 0, 0)),
        ],
        out_specs=tok,
        out_shape=jax.ShapeDtypeStruct((N_TOK, D_MODEL), F32),
        scratch_shapes=[pltpu.VMEM((D_MODEL, N_GATE), BF16), pltpu.VMEM((2, WT_ROWS, D_MODEL), F32),
                        pltpu.SemaphoreType.DMA((2,))],
        compiler_params=_cparams(("arbitrary",)),
        name=f"merge_l{l}",
    )(x_ctx, x_lat, o_ctx, *o_lat, mods, norm1, w_in_t, w_branch, w_out)


HALO = 8


def _sublane_transpose(tiles):
    sub = lax.broadcasted_iota(jnp.int32, tiles[0].shape, 0)
    cur = list(tiles)
    for d in (4, 2, 1):
        keep = (sub & d) == 0
        nxt = list(cur)
        for a in range(HALO):
            if a & d:
                continue
            nxt[a] = jnp.where(keep, cur[a], pltpu.roll(cur[a + d], d, 0))
            nxt[a + d] = jnp.where(keep, pltpu.roll(cur[a], HALO - d, 0), cur[a + d])
        cur = nxt
    return cur


ROW_TILES = TM // HALO
PERM_GROUPS = ROW_TILES // HALO


def _permute_rows(x, to_permuted):
    tile = lambda j: x[j * HALO:(j + 1) * HALO, :]
    out = [None] * ROW_TILES
    for m in range(PERM_GROUPS):
        nat = [m + PERM_GROUPS * s for s in range(HALO)]
        perm = [HALO * m + i for i in range(HALO)]
        src, dst = (nat, perm) if to_permuted else (perm, nat)
        for j, t in zip(dst, _sublane_transpose([tile(j) for j in src])):
            out[j] = t
    return jnp.concatenate(out, axis=0)


def _ffn_tile(x, x_prev, x_next, has_prev, has_next, mod, n2, wup_ref, cw_ref, cb_ref, wdn_ref, u_ref):
    sh2, sc2 = mod[:, 3 * D_MODEL:4 * D_MODEL], mod[:, 4 * D_MODEL:5 * D_MODEL]
    h_prev = jnp.where(has_prev, _rms_mod(x_prev, n2, sc2, sh2), 0.0)
    h_next = jnp.where(has_next, _rms_mod(x_next, n2, sc2, sh2), 0.0)
    hp = _permute_rows(_rms_mod(x, n2, sc2, sh2), True)
    lhs = jnp.concatenate([hp, h_prev, h_next], axis=0).astype(BF16)
    sub = lax.broadcasted_iota(jnp.int32, (HALO, FF_CHUNK), 0)

    def cols(c):
        return (slice(c * FF_CHUNK, (c + 1) * FF_CHUNK), slice(D_FF + c * FF_CHUNK, D_FF + (c + 1) * FF_CHUNK))

    def up(c):
        for half, cs in enumerate(cols(c)):
            lanes = slice(half * FF_CHUNK, (half + 1) * FF_CHUNK)
            u = _dot(lhs, wup_ref[:, cs])
            u_ref[c, HALO:HALO + TM, lanes] = u[:TM]
            prev_row, next_row = u[TM + HALO - 1:TM + HALO], u[TM + HALO:TM + HALO + 1]
            u_ref[c, 0:HALO, lanes] = jnp.where(sub == 0, prev_row, pltpu.roll(u[TM - HALO:TM], 1, 0))
            u_ref[c, HALO + TM:, lanes] = jnp.where(sub == HALO - 1, next_row, pltpu.roll(u[0:HALO], HALO - 1, 0))

    def both(ref, c):
        va, vg = cols(c)
        return jnp.concatenate([ref[:, va], ref[:, vg]], axis=1)

    state = {"acc": None}

    def chunk(c, between):
        cw = both(cw_ref, c)
        t = (both(cb_ref, c) + u_ref[c, 0:TM, :] * cw[0:1] + u_ref[c, HALO:HALO + TM, :] * cw[1:2]
             + u_ref[c, 2 * HALO:2 * HALO + TM, :] * cw[2:3])
        a, g = t[:, :FF_CHUNK], t[:, FF_CHUNK:]
        act = ((g * (1.0 / (1.0 + jnp.exp(-g)))) * a).astype(BF16)
        between()
        d = _dot(act, wdn_ref[c * FF_CHUNK:(c + 1) * FF_CHUNK, :])
        state["acc"] = d if state["acc"] is None else state["acc"] + d

    def finish():
        return x + mod[:, 5 * D_MODEL:6 * D_MODEL] * _permute_rows(state["acc"], False)

    return up, chunk, finish


W_BLOCKS = N_FF_CHUNKS
WUP_BLOCK_COLS = 2 * D_FF // W_BLOCKS
WDN_BLOCK_ROWS = D_FF // W_BLOCKS


def _load_bf16_weights(layer, wup_hbm, wdn_hbm, wup_ref, wdn_ref, up_stage, dn_stage, sems):
    def up_copy(b):
        return pltpu.make_async_copy(wup_hbm.at[layer, :, pl.ds(b * WUP_BLOCK_COLS, WUP_BLOCK_COLS)],
                                     up_stage.at[b % 2], sems.at[0, b % 2])

    def dn_copy(b):
        return pltpu.make_async_copy(wdn_hbm.at[layer, pl.ds(b * WDN_BLOCK_ROWS, WDN_BLOCK_ROWS), :],
                                     dn_stage.at[b % 2], sems.at[1, b % 2])

    up_copy(0).start()
    dn_copy(0).start()
    for b in range(W_BLOCKS):
        if b + 1 < W_BLOCKS:
            up_copy(b + 1).start()
            dn_copy(b + 1).start()
        up_copy(b).wait()
        wup_ref[:, b * WUP_BLOCK_COLS:(b + 1) * WUP_BLOCK_COLS] = up_stage[b % 2].astype(BF16)
        dn_copy(b).wait()
        wdn_ref[b * WDN_BLOCK_ROWS:(b + 1) * WDN_BLOCK_ROWS, :] = dn_stage[b % 2].astype(BF16)


def _ffn_kernel(x_ref, xp_ref, xn_ref, mod_ref, n2_ref, wup_hbm, cw_ref, cb_ref, wdn_hbm, yc_ref, yl_ref,
                u_ref, wup_ref, wdn_ref, up_stage, dn_stage, sems, *, layer):
    i = pl.program_id(0)

    @pl.when(i == 0)
    def _():
        _load_bf16_weights(layer, wup_hbm, wdn_hbm, wup_ref, wdn_ref, up_stage, dn_stage, sems)

    latent = i >= IN_CTX_STEPS
    first = (i - IN_CTX_STEPS) % (DEC_SEQ // TM_IN) == 0
    last = (i - IN_CTX_STEPS) % (DEC_SEQ // TM_IN) == DEC_SEQ // TM_IN - 1
    mod = mod_ref[...]
    n2 = n2_ref[...]
    make = [
        lambda: _ffn_tile(x_ref[0:TM, :], xp_ref[...], x_ref[TM:TM + HALO, :], latent & ~first, latent, mod, n2,
                          wup_ref, cw_ref, cb_ref, wdn_ref, u_ref.at[0]),
        lambda: _ffn_tile(x_ref[TM:, :], x_ref[TM - HALO:TM, :], xn_ref[...], latent, latent & ~last, mod, n2,
                          wup_ref, cw_ref, cb_ref, wdn_ref, u_ref.at[1]),
    ]
    tiles = {}

    def tile(t):
        if t not in tiles:
            tiles[t] = make[t]()
        return tiles[t]

    tasks = [(t, c) for t in range(len(make)) for c in range(N_FF_CHUNKS)]
    early = N_FF_CHUNKS // 2 + 1
    ups = tasks[:early] + [tasks[N_FF_CHUNKS]] + tasks[early:N_FF_CHUNKS] + tasks[N_FF_CHUNKS + 1:]
    depth = 3
    for t, c in ups[:depth]:
        tile(t)[0](c)
    ys = []
    for n, (t, c) in enumerate(tasks):
        ahead = ups[n + depth] if n + depth < len(ups) else None
        tile(t)[1](c, (lambda a=ahead: tile(a[0])[0](a[1])) if ahead else (lambda: None))
        if c == N_FF_CHUNKS - 1:
            ys.append(tile(t)[2]())
    y = jnp.concatenate(ys, axis=0)

    @pl.when(i < IN_CTX_STEPS)
    def _():
        yc_ref[...] = y

    @pl.when(latent)
    def _():
        yl_ref[...] = y


def _ffn(l, x1, mods, norm2, w_up, conv_w, conv_b, w_down):
    per = TM_IN // HALO
    last = N_TOK // HALO - 1
    per_batch = DEC_SEQ // TM_IN
    lat_blk = lambda i: jnp.maximum(i - IN_CTX_STEPS, 0)
    return pl.pallas_call(
        functools.partial(_ffn_kernel, layer=l),
        grid=(IN_STEPS,),
        in_specs=[
            pl.BlockSpec((TM_IN, D_MODEL), lambda i: (i, 0)),
            pl.BlockSpec((HALO, D_MODEL), lambda i: (jnp.maximum(i * per - 1, 0), 0)),
            pl.BlockSpec((HALO, D_MODEL), lambda i: (jnp.minimum((i + 1) * per, last), 0)),
            pl.BlockSpec((None, None, 1, 6 * D_MODEL),
                         lambda i: (l, jnp.where(i < IN_CTX_STEPS, 0, 1 + lat_blk(i) // per_batch), 0, 0)),
            _resident((None, 1, D_MODEL), lambda i: (l, 0, 0)),
            pl.BlockSpec(memory_space=pl.ANY),
            _resident((None, 3, 2 * D_FF), lambda i: (l, 0, 0)),
            _resident((None, 1, 2 * D_FF), lambda i: (l, 0, 0)),
            pl.BlockSpec(memory_space=pl.ANY),
        ],
        out_specs=[pl.BlockSpec((TM_IN, D_MODEL), lambda i: (jnp.minimum(i, IN_CTX_STEPS - 1), 0)),
                   pl.BlockSpec((TM_IN, D_MODEL), lambda i: (lat_blk(i), 0))],
        out_shape=[jax.ShapeDtypeStruct((N_CTX, D_MODEL), F32), jax.ShapeDtypeStruct((N_LAT, D_MODEL), F32)],
        scratch_shapes=[
            pltpu.VMEM((TM_IN // TM, N_FF_CHUNKS, TM + 2 * HALO, 2 * FF_CHUNK), F32),
            pltpu.VMEM((D_MODEL, 2 * D_FF), BF16),
            pltpu.VMEM((D_FF, D_MODEL), BF16),
            pltpu.VMEM((2, D_MODEL, WUP_BLOCK_COLS), F32),
            pltpu.VMEM((2, WDN_BLOCK_ROWS, D_MODEL), F32),
            pltpu.SemaphoreType.DMA((2, 2)),
        ],
        compiler_params=_cparams(("arbitrary",)),
        name=f"ffn_l{l}",
    )(x1, x1, x1, mods, norm2, w_up, conv_w, conv_b, w_down)


def _cast_kernel(a_ref, o_ref):
    o_ref[...] = a_ref[...].astype(BF16)


def _cast_bf16(a, rows, name):
    _, r, c = a.shape
    spec = pl.BlockSpec((None, rows, c), lambda l, i: (l, i, 0))
    return pl.pallas_call(
        _cast_kernel, grid=(DEPTH, r // rows), in_specs=[spec], out_specs=spec,
        out_shape=jax.ShapeDtypeStruct(a.shape, BF16),
        compiler_params=_cparams(("parallel", "parallel")), name=name,
    )(a)


WT_ROWS = 512
_SRC_CKV = C_QD + 4 * MLA_QK
QKV_BLOCK_STARTS = (0, WT_ROWS, 2 * WT_ROWS, C_KC, C_QD + 2 * MLA_QK)
GATE_BLOCK_STARTS = tuple(IN_QKV + WT_ROWS * b for b in range(N_GATE // WT_ROWS))


def _qkv_block_rows(b, w):
    zeros = lambda n: jnp.zeros((n, D_MODEL), F32)
    pad = zeros(LANES - MLA_QK)

    def two_heads(r0):
        return [w[r0:r0 + MLA_QK], pad, w[r0 + MLA_QK:r0 + 2 * MLA_QK], pad]

    if QKV_BLOCK_STARTS[b] == C_KC:
        n_kv = C_QD - C_KC
        return jnp.concatenate([w[0:n_kv]] + two_heads(n_kv), axis=0)
    if QKV_BLOCK_STARTS[b] == C_QD + 2 * MLA_QK:
        r_ckv = _SRC_CKV - QKV_BLOCK_STARTS[b]
        r_kpe = r_ckv + MLA_KV_LORA
        return jnp.concatenate(two_heads(0) + [w[r_ckv:r_kpe], zeros(MLA_NOPE), w[r_kpe:r_kpe + MLA_ROPE], pad],
                               axis=0)
    return w


def _load_transposed_bf16(wt_hbm, layer, starts, select, dst_ref, stage, sems):
    def copy(b):
        return pltpu.make_async_copy(wt_hbm.at[layer, pl.ds(starts[b], WT_ROWS), :], stage.at[b % 2], sems.at[b % 2])

    copy(0).start()
    for b in range(len(starts)):
        if b + 1 < len(starts):
            copy(b + 1).start()
        copy(b).wait()
        dst_ref[:, b * WT_ROWS:(b + 1) * WT_ROWS] = select(b, stage[b % 2]).T.astype(BF16)


def _rope_tables():
    t = np.arange(DEC_SEQ)

    def angles(dim):
        half = dim // 2
        inv = ROPE_THETA ** (-np.arange(0, half, 2, dtype=np.float32) / half)
        row = (t // GRID_W).astype(np.float32)[:, None] * inv[None, :]
        col = (t % GRID_W).astype(np.float32)[:, None] * inv[None, :]
        return np.concatenate([row, row, col, col], axis=-1).astype(np.float32)

    def signed(ang, quarter):
        cos, sin = np.cos(ang), np.sin(ang)
        first = (np.arange(ang.shape[1]) % (2 * quarter)) < quarter
        return cos, np.where(first, -sin, 0.0), np.where(first, 0.0, sin)

    c64, lo64, hi64 = signed(angles(HEAD_DIM), 16)
    c64, lo64, hi64 = (np.tile(a, (1, 2)) for a in (c64, lo64, hi64))
    cm, lom, him = signed(angles(MLA_ROPE), 8)

    def mla_pad(a, fill):
        out = np.full((DEC_SEQ, LANES), fill, np.float32)
        out[:, MLA_NOPE:MLA_QK] = a
        return out

    tabs = [c64, lo64, hi64, mla_pad(cm, 1.0), mla_pad(lom, 0.0), mla_pad(him, 0.0)]
    ident = [np.ones, np.zeros, np.zeros, np.ones, np.zeros, np.zeros]
    full = [np.concatenate([f((TM_IN, LANES), np.float32), a.astype(np.float32)], axis=0) for f, a in zip(ident, tabs)]
    return jnp.asarray(np.stack(full, axis=0))


def _block_diag_ones():
    idx = np.arange(256)
    mats = [(idx[:, None] // w == idx[None, :] // w).astype(np.float32) for w in (HEAD_DIM, LANES)]
    return jnp.asarray(np.stack(mats, axis=0), dtype=BF16)


def _prep_small(w_ukv, qn_a, kn_a, qn_b, kn_b, qn_c, kn_c, qn_d, kn_d, kvn_d):
    ukv = w_ukv.reshape(DEPTH, MLA_KV_LORA, 4, 2, HEAD_DIM)
    k_part = jnp.pad(ukv[:, :, :, 0], ((0, 0), (0, 0), (0, 0), (0, HEAD_DIM))).reshape(DEPTH, MLA_KV_LORA, 512)
    v_part = ukv[:, :, :, 1].reshape(DEPTH, MLA_KV_LORA, 256)
    w_ukv_p = jnp.concatenate([k_part, v_part], axis=-1).astype(BF16)

    def row(g, reps, width=256):
        t = jnp.tile(g, (1, reps))
        return jnp.pad(t, ((0, 0), (0, width - t.shape[1])))

    pad96 = lambda g: jnp.pad(g, ((0, 0), (0, LANES - MLA_QK)))
    rows = [row(qn_a, 4), row(kn_a, 4), row(qn_b, 4), row(kn_b, 2), row(qn_c, 4), row(kn_c, 2),
            row(pad96(qn_d), 2), row(pad96(kn_d), 2), row(kvn_d, 1)]
    gains = jnp.stack(rows + [jnp.zeros_like(rows[0])] * (16 - len(rows)), axis=1)
    return w_ukv_p, gains


def kernel(x_prompt, x_sample, cache_nat_k, cache_nat_v, cache_gqa_k, cache_gqa_v, cache_win_k, cache_win_v,
           cache_mla_ckv, cache_mla_kpe, c, c_ctx, w_ada, b_ada, norm1, norm2, w_in, qn_a, kn_a, rpb_a,
           qn_b, kn_b, qn_c, kn_c, sink_c, qn_d, kn_d, kvn_d, w_ukv, w_branch, w_out, w_up, conv_w, conv_b,
           w_down):
    w_ukv_p, gains = _prep_small(w_ukv, qn_a, kn_a, qn_b, kn_b, qn_c, kn_c, qn_d, kn_d, kvn_d)
    w_in_t = jnp.swapaxes(w_in, 1, 2)
    w_branch_b = _cast_bf16(w_branch.reshape(DEPTH, 4 * 256, D_MODEL), 1024, "cast_w_branch").reshape(
        DEPTH, 4, 256, D_MODEL)
    w_out_b = _cast_bf16(w_out, 1024, "cast_w_out")
    conv_b_r = conv_b.reshape(DEPTH, 1, 2 * D_FF)
    rope_tab = _rope_tables()
    bd = _block_diag_ones()
    norm1_r = norm1.reshape(DEPTH, 1, D_MODEL)
    norm2_r = norm2.reshape(DEPTH, 1, D_MODEL)

    cond8 = jnp.zeros((8, D_MODEL), F32).at[0].set(c_ctx).at[1:1 + DEC_BATCH].set(c)
    mods = _adaln(cond8, w_ada, b_ada).reshape(DEPTH, 8, 1, 6 * D_MODEL)

    rpb_flat = jnp.pad(rpb_a.reshape(-1), (0, 8192 - rpb_a.size))
    na_bias = _na_bias(rpb_flat)
    kpe_p = jnp.pad(cache_mla_kpe, ((0, 0), (0, 0), (0, 0), (MLA_NOPE, LANES - MLA_QK)))
    kd_ctx, vd_ctx = _mla_ctx(cache_mla_ckv, kpe_p, w_ukv_p, gains, bd)
    c_nat_k = cache_nat_k.reshape(DEC_BATCH, DEPTH, PAST_LEN, 256)
    c_nat_v = cache_nat_v.reshape(DEC_BATCH, DEPTH, PAST_LEN, 256)
    c_gqa_k = cache_gqa_k.reshape(DEC_BATCH, DEPTH, PAST_LEN, LANES)
    c_gqa_v = cache_gqa_v.reshape(DEC_BATCH, DEPTH, PAST_LEN, LANES)
    c_win_k = cache_win_k.reshape(DEC_BATCH, DEPTH, PAST_LEN, LANES)
    c_win_v = cache_win_v.reshape(DEC_BATCH, DEPTH, PAST_LEN, LANES)

    x_ctx, x_lat = x_prompt.reshape(N_CTX, D_MODEL), x_sample.reshape(N_LAT, D_MODEL)
    caches = [jnp.zeros((BATCH, DEPTH) + blk, F32) for blk in _CACHE_BLOCKS]
    for l in range(DEPTH):
        qkv, caches = _inproj(l, x_ctx, x_lat, mods, norm1_r, w_in_t, w_ukv_p, gains, bd, rope_tab, caches)
        qa, ka, va, qb, kb, vb, qc, kc, vc, qd, kd, vd = qkv
        sink_l = sink_c[l]
        o_ctx = _ctx_attention(l, sink_l, qkv)
        o_lat = (_lat_na(l, qa, ka, va, c_nat_k, c_nat_v, na_bias),
                 _lat_gqa(l, qb, kb, vb, c_gqa_k, c_gqa_v),
                 _lat_window(l, sink_l, qc, kc, vc, c_win_k, c_win_v),
                 _lat_mla(l, qd, kd, vd, kd_ctx, vd_ctx))
        x1 = _merge(l, x_ctx, x_lat, o_ctx, o_lat, mods, norm1_r, w_in_t, w_branch_b, w_out_b)
        x_ctx, x_lat = _ffn(l, x1, mods, norm2_r, w_up, conv_w, conv_b_r, w_down)
    y_prompt = x_ctx.reshape(BATCH, SEQ, D_MODEL)
    y_sample = x_lat.reshape(DEC_BATCH, DEC_SEQ, D_MODEL)
    heads = (4, 4, 2, 2, 2, 2)
    outs = [a.reshape(BATCH, DEPTH, h, HEAD_DIM, SEQ).transpose(0, 1, 4, 2, 3) for a, h in zip(caches[:6], heads)]
    return (y_prompt, y_sample, *outs, jnp.swapaxes(caches[6], 2, 3), jnp.swapaxes(caches[7], 2, 3))
```

```python
import functools

import jax
import jax.numpy as jnp
import numpy as np
from jax import lax
from jax.experimental import pallas as pl
from jax.experimental.pallas import tpu as pltpu

F32 = jnp.float32
BF16 = jnp.bfloat16

D_MODEL = 1024
BATCH = 16
SEQ = 256
DEPTH = 4
DEC_BATCH = 2
DEC_SEQ = 2048
PAST_LEN = 256
GRID_W = 64
HEAD_DIM = 64
NA_ROWS = 8
NA_COLS = 16
WINDOW = 128
MLA_NOPE = 64
MLA_ROPE = 32
MLA_QK = MLA_NOPE + MLA_ROPE
MLA_KV_LORA = 128
D_FF = 2816
ROPE_THETA = 10000.0
EPS = 1e-6
NEG_INF = -1e30

LANES = 128
TM = 256
N_CTX = BATCH * SEQ
N_LAT = DEC_BATCH * DEC_SEQ
N_TOK = N_CTX + N_LAT
GRID_ROWS = DEC_SEQ // GRID_W
FF_CHUNK = 256
N_FF_CHUNKS = D_FF // FF_CHUNK
VMEM_LIMIT = 56 * 1024 * 1024

C_QA, C_KA, C_VA = 0, 256, 512
C_QB, C_KB, C_VB = 768, 1024, 1152
C_QC, C_KC, C_VC = 1280, 1536, 1664
C_QD, C_CKV, C_KPE, C_END = 1792, 2304, 2432, 2560
N_GATE = 4 * D_MODEL
IN_QKV = 2336
LOG2E = float(np.log2(np.e))
Q_SCALE = HEAD_DIM ** -0.5 * LOG2E
MLA_Q_SCALE = MLA_QK ** -0.5 * LOG2E


def _cparams(sem):
    return pltpu.CompilerParams(dimension_semantics=sem, vmem_limit_bytes=VMEM_LIMIT)


def _resident(shape, index_map):
    return pl.BlockSpec(shape, index_map, pipeline_mode=pl.Buffered(1))


def _dot(a, b):
    return jnp.dot(a, b, preferred_element_type=F32)


def _dot_t(a, b):
    return lax.dot_general(a, b, (((1,), (1,)), ((), ())), preferred_element_type=F32)


def _rms_mod(x, gain, scale, shift):
    ms = jnp.mean(x * x, axis=-1, keepdims=True)
    return (x * lax.rsqrt(ms + EPS)) * gain * (1.0 + scale) + shift


def _group_norm(t, ones_bd, inv_n, gain):
    ss = _dot((t * t).astype(BF16), ones_bd)
    return t * lax.rsqrt(ss * inv_n + EPS) * gain


def _rope(t, cos, sin_lo, sin_hi, half):
    return (t * cos + pltpu.roll(t, LANES - half, 1) * sin_lo + pltpu.roll(t, half, 1) * sin_hi)


def _lane_iota(shape):
    return lax.broadcasted_iota(jnp.int32, shape, len(shape) - 1)


def _low_half(shape):
    return _lane_iota(shape) < HEAD_DIM


def _run_units(units, depth=2):
    n = len(units)
    pending = {}
    outs = []
    for k in range(min(depth, n)):
        pending[k] = units[k][0]()
    for k in range(n):
        outs.append(units[k][1](pending.pop(k)))
        if k + depth < n:
            pending[k + depth] = units[k + depth][0]()
    return outs


N_COND = 1 + DEC_BATCH


def _adaln_kernel(cond_ref, w_ref, b_ref, out_ref):
    cnd = cond_ref[...]
    s_t = (cnd * (1.0 / (1.0 + jnp.exp(-cnd)))).T
    w = w_ref[...]
    out_ref[...] = jnp.zeros(out_ref.shape, F32)
    for r in range(N_COND):
        out_ref[r:r + 1, :] = jnp.sum(s_t[:, r:r + 1] * w, axis=0, keepdims=True) + b_ref[...]


def _adaln(cond8, w_ada, b_ada):
    n_col = 6 * D_MODEL
    tn = 1024
    return pl.pallas_call(
        _adaln_kernel,
        grid=(DEPTH, n_col // tn),
        in_specs=[
            pl.BlockSpec((8, D_MODEL), lambda l, n: (0, 0)),
            pl.BlockSpec((None, D_MODEL, tn), lambda l, n: (l, 0, n)),
            pl.BlockSpec((None, 1, tn), lambda l, n: (l, 0, n)),
        ],
        out_specs=pl.BlockSpec((None, 8, tn), lambda l, n: (l, 0, n)),
        out_shape=jax.ShapeDtypeStruct((DEPTH, 8, n_col), F32),
        compiler_params=_cparams(("parallel", "parallel")),
        name="adaln",
    )(cond8, w_ada, b_ada.reshape(DEPTH, 1, n_col))


_QKV_WIDTHS = (512, 256, 256, 512, 128, 128, 512, 128, 128, 512, 512, 256)
_CACHE_WIDTHS = (256, 256, 128, 128, 128, 128, MLA_KV_LORA, MLA_ROPE)
_CACHE_BLOCKS = tuple((w, SEQ) for w in _CACHE_WIDTHS)


TM_IN = 2 * TM
IN_STEPS = N_TOK // TM_IN
IN_CTX_STEPS = N_CTX // TM_IN


def _inproj_kernel(xc_ref, xl_ref, mod_ref, n1_ref, wt_hbm, wukv_ref, g_ref, bd_ref, rope_ref, *rest, layer):
    n_out = len(_QKV_WIDTHS) + len(_CACHE_WIDTHS)
    outs = rest[len(_CACHE_WIDTHS):len(_CACHE_WIDTHS) + n_out]
    w_ref, stage, sems = rest[len(_CACHE_WIDTHS) + n_out:]

    @pl.when(pl.program_id(0) == 0)
    def _():
        _load_transposed_bf16(wt_hbm, layer, QKV_BLOCK_STARTS, _qkv_block_rows, w_ref, stage, sems)

    is_ctx = pl.program_id(0) >= IN_STEPS - IN_CTX_STEPS
    units = []
    for half in range(TM_IN // TM):
        rows = slice(half * TM, (half + 1) * TM)
        views = [o.at[rows] for o in outs[:len(_QKV_WIDTHS)]] + [o.at[half] for o in outs[len(_QKV_WIDTHS):]]
        units += _inproj_half(is_ctx, xc_ref.at[rows], xl_ref.at[rows], mod_ref, n1_ref, w_ref, wukv_ref, g_ref,
                              bd_ref, rope_ref.at[:, rows, :], views)
    _run_units(units, depth=2)


def _inproj_half(is_ctx, xc_ref, xl_ref, mod_ref, n1_ref, w_ref, wukv_ref, g_ref, bd_ref, rope_ref, outs):
    qa_o, ka_o, va_o, qb_o, kb_o, vb_o, qc_o, kc_o, vc_o, qd_o, kd_o, vd_o = outs[:len(_QKV_WIDTHS)]
    natk_o, natv_o, gqak_o, gqav_o, wink_o, winv_o, ckv_o, kpe_o = outs[len(_QKV_WIDTHS):]
    x = jnp.where(is_ctx, xc_ref[...], xl_ref[...])
    mod = mod_ref[...]
    hb = _rms_mod(x, n1_ref[...], mod[:, D_MODEL:2 * D_MODEL], mod[:, 0:D_MODEL]).astype(BF16)
    bd64, bd128 = bd_ref[0], bd_ref[1]
    cos64, slo64, shi64 = rope_ref[0], rope_ref[1], rope_ref[2]
    cosm, slom, shim = rope_ref[3], rope_ref[4], rope_ref[5]
    inv64, inv96 = 1.0 / HEAD_DIM, 1.0 / MLA_QK
    low = _low_half((TM, LANES))
    held = {}

    def chunks(t):
        return [t[:, c:c + LANES] for c in range(0, t.shape[1], LANES)]

    def rope64(t):
        parts = [_rope(p, cos64, slo64, shi64, 16) for p in chunks(t)]
        return parts[0] if len(parts) == 1 else jnp.concatenate(parts, axis=1)

    def ropem(t):
        return jnp.concatenate([_rope(p, cosm, slom, shim, 8) for p in chunks(t)], axis=1)

    def store_q(ref, q, grouped):
        q = q * Q_SCALE
        slots = []
        for hd in range(4):
            part, src = q[:, (hd // 2) * LANES:(hd // 2 + 1) * LANES], hd % 2
            dst = hd // 2 if grouped else src
            if src != dst:
                part = pltpu.roll(part, HEAD_DIM, 1)
            slots.append(jnp.where(low if dst == 0 else ~low, part, 0.0))
        ref[...] = jnp.concatenate(slots, axis=1).astype(BF16)

    def post_latent(t):
        ckv = _group_norm(t[:, :LANES], bd128[:LANES, :LANES], 1.0 / MLA_KV_LORA, g_ref[8:9, :LANES])
        kpe = t[:, LANES:]
        ckv_o[...] = ckv.T
        kpe_o[...] = kpe.T[MLA_NOPE:MLA_QK, :]
        held["kpe"] = kpe
        held["kv"] = _dot(ckv.astype(BF16), wukv_ref[...])

    def post_kd(_):
        kv, kpe = held["kv"], held["kpe"]
        for j in range(2):
            kf = jnp.concatenate([kv[:, (2 * j) * LANES:(2 * j + 1) * LANES] + kpe,
                                  kv[:, (2 * j + 1) * LANES:(2 * j + 2) * LANES] + kpe], axis=1)
            kd_o[:, 2 * j * LANES:(2 * j + 2) * LANES] = ropem(
                _group_norm(kf, bd128, inv96, g_ref[7:8, :])).astype(BF16)
        vd_o[...] = kv[:, 4 * LANES:].astype(BF16)

    def post_qa(t):
        store_q(qa_o, _group_norm(t, bd64, inv64, g_ref[0:1, :]), grouped=False)

    def post_ka(t):
        k = _group_norm(t, bd64, inv64, g_ref[1:2, :])
        natk_o[...] = k.T
        ka_o[...] = k.astype(BF16)

    def post_va(t):
        natv_o[...] = t.T
        va_o[...] = t.astype(BF16)

    def post_q_rope(ref, row):
        def post(t):
            store_q(ref, rope64(_group_norm(t, bd64, inv64, g_ref[row:row + 1, :])), grouped=True)
        return post

    def post_kv(k_ref, v_ref, kc_ref, vc_ref, row):
        def post(t):
            k = _group_norm(t[:, :LANES], bd64[:LANES, :LANES], inv64, g_ref[row:row + 1, :LANES])
            kc_ref[...] = k.T
            k_ref[...] = rope64(k).astype(BF16)
            v = t[:, LANES:]
            vc_ref[...] = v.T
            v_ref[...] = v.astype(BF16)
        return post

    def post_qd(j):
        def post(t):
            qd_o[:, 2 * j * LANES:(2 * j + 2) * LANES] = (ropem(
                _group_norm(t, bd128, inv96, g_ref[6:7, :])) * MLA_Q_SCALE).astype(BF16)
        return post

    stages = [
        ((C_CKV, C_END), post_latent),
        ((C_QA, C_KA), post_qa),
        (None, post_kd),
        ((C_QB, C_KB), post_q_rope(qb_o, 2)),
        ((C_QC, C_KC), post_q_rope(qc_o, 4)),
        ((C_QD, C_QD + 2 * LANES), post_qd(0)),
        ((C_QD + 2 * LANES, C_CKV), post_qd(1)),
        ((C_KB, C_QC), post_kv(kb_o, vb_o, gqak_o, gqav_o, 3)),
        ((C_KC, C_QD), post_kv(kc_o, vc_o, wink_o, winv_o, 5)),
        ((C_KA, C_VA), post_ka),
        ((C_VA, C_QB), post_va),
    ]

    def issue(cols):
        return None if cols is None else _dot(hb, w_ref[:, cols[0]:cols[1]])

    return [(functools.partial(issue, cols), post) for cols, post in stages]


def _inproj(l, x_ctx, x_lat, mods, norm1, w_in_t, w_ukv_p, gains, bd, rope_tab, caches):
    n_lat = IN_STEPS - IN_CTX_STEPS
    blk = lambda s: (s + IN_CTX_STEPS) % IN_STEPS
    lat_blk = lambda s: jnp.maximum(blk(s) - IN_CTX_STEPS, 0)
    per_batch = DEC_SEQ // TM_IN
    tok = lambda w: pl.BlockSpec((TM_IN, w), lambda s: (blk(s), 0))
    in_specs = [
        pl.BlockSpec((TM_IN, D_MODEL), lambda s: (jnp.minimum(blk(s), IN_CTX_STEPS - 1), 0)),
        pl.BlockSpec((TM_IN, D_MODEL), lambda s: (lat_blk(s), 0)),
        pl.BlockSpec((None, None, 1, 6 * D_MODEL),
                     lambda s: (l, jnp.where(s < n_lat, 1 + lat_blk(s) // per_batch, 0), 0, 0)),
        _resident((None, 1, D_MODEL), lambda s: (l, 0, 0)),
        pl.BlockSpec(memory_space=pl.ANY),
        _resident((None, MLA_KV_LORA, 768), lambda s: (l, 0, 0)),
        _resident((None, 16, 256), lambda s: (l, 0, 0)),
        _resident((2, 256, 256), lambda s: (0, 0, 0)),
        pl.BlockSpec((6, TM_IN, LANES), lambda s: (0, jnp.where(s < n_lat, 1 + lat_blk(s) % per_batch, 0), 0)),
    ]
    args = [x_ctx, x_lat, mods, norm1, w_in_t, w_ukv_p, gains, bd, rope_tab] + list(caches)
    aliases = {len(in_specs) + n: len(_QKV_WIDTHS) + n for n in range(len(caches))}
    in_specs = in_specs + [pl.BlockSpec(memory_space=pl.ANY)] * len(caches)
    cache_spec = lambda cb: pl.BlockSpec((TM_IN // SEQ, None) + cb, lambda s: (jnp.maximum(s - n_lat, 0), l, 0, 0))
    outs = pl.pallas_call(
        functools.partial(_inproj_kernel, layer=l),
        grid=(IN_STEPS,),
        in_specs=in_specs,
        out_specs=[tok(w) for w in _QKV_WIDTHS] + [cache_spec(blk) for blk in _CACHE_BLOCKS],
        out_shape=[jax.ShapeDtypeStruct((N_TOK, w), BF16) for w in _QKV_WIDTHS]
        + [jax.ShapeDtypeStruct((BATCH, DEPTH) + blk, F32) for blk in _CACHE_BLOCKS],
        scratch_shapes=[pltpu.VMEM((D_MODEL, C_END), BF16), pltpu.VMEM((2, WT_ROWS, D_MODEL), F32),
                        pltpu.SemaphoreType.DMA((2,))],
        input_output_aliases=aliases,
        compiler_params=_cparams(("arbitrary",)),
        name=f"inproj_l{l}",
    )(*args)
    return outs[:len(_QKV_WIDTHS)], outs[len(_QKV_WIDTHS):]


def _scores(q, keys):
    out = []
    for k, bias in keys:
        s = _dot_t(q, k)
        if bias is not None:
            s = s + bias
        out.append(s)
    return out


def _softmax_pv(scores, values, sink=None):
    m = functools.reduce(jnp.maximum, [jnp.max(s, axis=-1, keepdims=True) for s in scores])
    if sink is not None:
        m = jnp.maximum(m, sink)
    denom = None
    out = None
    for s, v in zip(scores, values):
        e = jnp.exp2(s - m)
        d = jnp.sum(e, axis=-1, keepdims=True)
        o = _dot(e.astype(BF16), v)
        denom = d if denom is None else denom + d
        out = o if out is None else out + o
    if sink is not None:
        denom = denom + jnp.exp2(sink - m)
    return out / denom


def _pair_unit(q_ref, rows, j, keys, values, kv_half, sinks=None):
    m_rows = rows.stop - rows.start

    def issue():
        q2 = jnp.concatenate([q_ref[rows, (2 * j) * LANES:(2 * j + 1) * LANES],
                              q_ref[rows, (2 * j + 1) * LANES:(2 * j + 2) * LANES]], axis=0)
        return _scores(q2, keys())

    def finish(scores):
        sink = None
        if sinks is not None:
            row = lax.broadcasted_iota(jnp.int32, (2 * m_rows, 1), 0)
            sink = jnp.where(row < m_rows, sinks[0], sinks[1])
        o2 = _softmax_pv(scores, values(), sink)
        lo, hi = o2[:m_rows], o2[m_rows:]
        if kv_half == 0:
            hi = pltpu.roll(hi, HEAD_DIM, 1)
        elif kv_half == 1:
            lo = pltpu.roll(lo, HEAD_DIM, 1)
        return jnp.where(_low_half(lo.shape), lo, hi)

    return issue, finish


def _single_unit(q_ref, rows, hd, keys, values):
    def issue():
        return _scores(q_ref[rows, hd * LANES:(hd + 1) * LANES], keys())

    def finish(scores):
        return _softmax_pv(scores, values())

    return issue, finish


def _merge_halves(o_even, o_odd):
    return jnp.where(_low_half(o_even.shape), o_even, o_odd)


def _ctx_attn_kernel(sink_ref, qa, ka, va, qb, kb, vb, qc, kc, vc, qd, kd, vd, o_ref):
    rows = slice(0, SEQ)
    units = []
    for j in range(2):
        c = slice(j * LANES, (j + 1) * LANES)
        units.append(_pair_unit(qa, rows, j, lambda c=c: [(ka[:, c], None)], lambda c=c: [va[:, c]], None))
    for q, k, v, with_sink in ((qb, kb, vb, False), (qc, kc, vc, True)):
        for j in range(2):
            sinks = (sink_ref[2 * j] * LOG2E, sink_ref[2 * j + 1] * LOG2E) if with_sink else None
            units.append(_pair_unit(q, rows, j, lambda k=k: [(k[...], None)], lambda v=v: [v[...]], j, sinks))
    for hd in range(4):
        c = slice(hd * LANES, (hd + 1) * LANES)
        vs = slice((hd // 2) * LANES, (hd // 2 + 1) * LANES)
        units.append(_single_unit(qd, rows, hd, lambda c=c: [(kd[:, c], None)], lambda vs=vs: [vd[:, vs]]))
    outs = _run_units(units)
    for n in range(6):
        o_ref[:, n * LANES:(n + 1) * LANES] = outs[n].astype(BF16)
    for j in range(2):
        o_ref[:, (6 + j) * LANES:(7 + j) * LANES] = _merge_halves(outs[6 + 2 * j], outs[7 + 2 * j]).astype(BF16)


def _ctx_attention(l, sink_c, qkv):
    blk = lambda w: pl.BlockSpec((SEQ, w), lambda b: (b, 0))
    return pl.pallas_call(
        _ctx_attn_kernel,
        grid=(BATCH,),
        in_specs=[pl.BlockSpec(memory_space=pltpu.SMEM)] + [blk(w) for w in _QKV_WIDTHS],
        out_specs=pl.BlockSpec((SEQ, D_MODEL), lambda b: (b, 0)),
        out_shape=jax.ShapeDtypeStruct((N_CTX, D_MODEL), BF16),
        compiler_params=_cparams(("parallel",)),
        name=f"ctx_attn_l{l}",
    )(sink_c, *qkv)


LAT_TQ = 256
LAT_SUB = 128
LAT_QBLOCKS = DEC_SEQ // LAT_TQ
LAT_ROW0 = N_CTX // LAT_TQ
LAT_KV0 = N_CTX // DEC_SEQ


def _lat_qspec(width):
    return pl.BlockSpec((LAT_TQ, width), lambda b, i: (LAT_ROW0 + b * LAT_QBLOCKS + i, 0))


def _lat_kvspec(width):
    return pl.BlockSpec((DEC_SEQ, width), lambda b, i: (LAT_KV0 + b, 0))


def _lat_call(kernel, l, name, in_specs, args):
    return pl.pallas_call(
        kernel,
        grid=(DEC_BATCH, LAT_QBLOCKS),
        in_specs=in_specs,
        out_specs=pl.BlockSpec((LAT_TQ, 256), lambda b, i: (b * LAT_QBLOCKS + i, 0)),
        out_shape=jax.ShapeDtypeStruct((N_LAT, 256), BF16),
        compiler_params=_cparams(("parallel", "parallel")),
        name=f"{name}_l{l}",
    )(*args)


def _sub_rows(u):
    return slice(u * LAT_SUB, (u + 1) * LAT_SUB)


def _store_units(o_ref, outs):
    for n, o in enumerate(outs):
        u, j = divmod(n, 2)
        o_ref[_sub_rows(u), j * LANES:(j + 1) * LANES] = o.astype(BF16)


def _gqa_kernel(q_ref, k_ref, v_ref, kc_ref, vc_ref, o_ref):
    keys = lambda: [(kc_ref[...].astype(BF16), None), (k_ref[...], None)]
    values = lambda: [vc_ref[...].astype(BF16), v_ref[...]]
    units = [_pair_unit(q_ref, _sub_rows(u), j, keys, values, j) for u in range(LAT_TQ // LAT_SUB) for j in range(2)]
    _store_units(o_ref, _run_units(units))


def _lat_gqa(l, qb, kb, vb, cache_k, cache_v):
    cspec = pl.BlockSpec((None, None, PAST_LEN, LANES), lambda b, i: (b, l, 0, 0))
    return _lat_call(_gqa_kernel, l, "lat_gqa",
                     [_lat_qspec(512), _lat_kvspec(LANES), _lat_kvspec(LANES), cspec, cspec],
                     (qb, kb, vb, cache_k, cache_v))


WIN_KEYS = 3 * LAT_SUB


def _win_kernel(sink_ref, q_ref, k_ref, v_ref, kc_ref, vc_ref, o_ref):
    units = []
    for u in range(LAT_TQ // LAT_SUB):
        blk = pl.program_id(1) * (LAT_TQ // LAT_SUB) + u
        start = pl.multiple_of(jnp.clip((blk - 1) * LAT_SUB, 0, DEC_SEQ - WIN_KEYS), LAT_SUB)

        def keys(blk=blk, start=start):
            row = lax.broadcasted_iota(jnp.int32, (2 * LAT_SUB, WIN_KEYS), 0)
            qpos = blk * LAT_SUB + jnp.where(row < LAT_SUB, row, row - LAT_SUB)
            kpos = start + lax.broadcasted_iota(jnp.int32, (2 * LAT_SUB, WIN_KEYS), 1)
            bias = jnp.where(jnp.abs(qpos - kpos) <= WINDOW, 0.0, NEG_INF)
            return [(kc_ref[...].astype(BF16), None), (k_ref[pl.ds(start, WIN_KEYS), :], bias)]

        def values(start=start):
            return [vc_ref[...].astype(BF16), v_ref[pl.ds(start, WIN_KEYS), :]]

        for j in range(2):
            sinks = (sink_ref[2 * j] * LOG2E, sink_ref[2 * j + 1] * LOG2E)
            units.append(_pair_unit(q_ref, _sub_rows(u), j, keys, values, j, sinks))
    _store_units(o_ref, _run_units(units, depth=4))


def _lat_window(l, sink_c, qc, kc, vc, cache_k, cache_v):
    cspec = pl.BlockSpec((None, None, PAST_LEN, LANES), lambda b, i: (b, l, 0, 0))
    return _lat_call(_win_kernel, l, "lat_win",
                     [pl.BlockSpec(memory_space=pltpu.SMEM), _lat_qspec(512), _lat_kvspec(LANES),
                      _lat_kvspec(LANES), cspec, cspec],
                     (sink_c, qc, kc, vc, cache_k, cache_v))


def _mla_kernel(q_ref, k_ref, v_ref, kc_ref, vc_ref, o_ref):
    units = []
    rows = slice(0, LAT_TQ)
    for hd in range(4):
        c = slice(hd * LANES, (hd + 1) * LANES)
        vs = slice((hd // 2) * LANES, (hd // 2 + 1) * LANES)
        units.append(_single_unit(q_ref, rows, hd,
                                  lambda c=c: [(kc_ref[:, c], None), (k_ref[:, c], None)],
                                  lambda vs=vs: [vc_ref[:, vs], v_ref[:, vs]]))
    outs = _run_units(units, depth=3)
    for j in range(2):
        o_ref[:, j * LANES:(j + 1) * LANES] = _merge_halves(outs[2 * j], outs[2 * j + 1]).astype(BF16)


def _lat_mla(l, qd, kd, vd, kd_ctx, vd_ctx):
    kcs = pl.BlockSpec((None, None, PAST_LEN, 512), lambda b, i: (l, b, 0, 0))
    vcs = pl.BlockSpec((None, None, PAST_LEN, 256), lambda b, i: (l, b, 0, 0))
    return _lat_call(_mla_kernel, l, "lat_mla",
                     [_lat_qspec(512), _lat_kvspec(512), _lat_kvspec(256), kcs, vcs],
                     (qd, kd, vd, kd_ctx, vd_ctx))


NA_KEYS = NA_ROWS * GRID_W
NA_ROWS_PER_STEP = LAT_TQ // GRID_W


def _na_shift(r):
    return jnp.where(r < NA_ROWS // 2, r,
                     jnp.where(r > GRID_ROWS - NA_ROWS // 2, r - (GRID_ROWS - NA_ROWS), NA_ROWS // 2))


def _na_kernel(q_ref, k_ref, v_ref, kc_ref, vc_ref, bias_ref, o_ref):
    units = []
    for rr in range(NA_ROWS_PER_STEP):
        r = pl.program_id(1) * NA_ROWS_PER_STEP + rr
        start = pl.multiple_of(jnp.clip(r - NA_ROWS // 2, 0, GRID_ROWS - NA_ROWS) * GRID_W, GRID_W)
        shift = _na_shift(r)
        rows = slice(rr * GRID_W, (rr + 1) * GRID_W)
        for j in range(2):
            c = slice(j * LANES, (j + 1) * LANES)

            def keys(j=j, c=c, start=start, shift=shift):
                bias = jnp.concatenate([bias_ref[2 * j, shift], bias_ref[2 * j + 1, shift]], axis=0)
                return [(kc_ref[:, c].astype(BF16), None), (k_ref[pl.ds(start, NA_KEYS), c], bias)]

            def values(c=c, start=start):
                return [vc_ref[:, c].astype(BF16), v_ref[pl.ds(start, NA_KEYS), c]]

            units.append(_pair_unit(q_ref, rows, j, keys, values, None))
    for n, o in enumerate(_run_units(units, depth=4)):
        rr, j = divmod(n, 2)
        o_ref[rr * GRID_W:(rr + 1) * GRID_W, j * LANES:(j + 1) * LANES] = o.astype(BF16)


def _lat_na(l, qa, ka, va, cache_k, cache_v, na_bias):
    cspec = pl.BlockSpec((None, None, PAST_LEN, 256), lambda b, i: (b, l, 0, 0))
    bspec = _resident((None, 4, NA_ROWS, GRID_W, NA_KEYS), lambda b, i: (l, 0, 0, 0, 0))
    return _lat_call(_na_kernel, l, "lat_na",
                     [_lat_qspec(512), _lat_kvspec(256), _lat_kvspec(256), cspec, cspec, bspec],
                     (qa, ka, va, cache_k, cache_v, na_bias))


def _na_bias_kernel(rpb_ref, out_ref, t_ref):
    l = pl.program_id(0)
    n_dr, n_dc = 2 * NA_ROWS - 1, 2 * NA_COLS - 1
    shape = (GRID_W, LANES)
    qc = lax.broadcasted_iota(jnp.int32, shape, 0)
    lane = _lane_iota(shape)
    kc = lane & (GRID_W - 1)
    dc = jnp.clip(kc - qc, 1 - NA_COLS, NA_COLS - 1) + (NA_COLS - 1)
    c0 = jnp.clip(qc - NA_COLS // 2, 0, GRID_W - NA_COLS)
    ok = (kc >= c0) & (kc < c0 + NA_COLS)
    for hd in range(4):
        def fill(a, carry):
            base = ((l * 4 + hd) * n_dr + a) * n_dc
            t = jnp.zeros(shape, F32)
            for b in range(n_dc):
                t = jnp.where(dc == b, rpb_ref[base + b], t)
            t_ref[a] = jnp.where(ok, t * LOG2E, NEG_INF)
            return carry
        lax.fori_loop(0, n_dr, fill, 0)
        for d in range(NA_ROWS):
            for m in range(NA_ROWS // 2):
                lo = t_ref[2 * m - d + NA_ROWS - 1]
                hi = t_ref[2 * m + 1 - d + NA_ROWS - 1]
                out_ref[hd, d, :, m * LANES:(m + 1) * LANES] = jnp.where(lane < GRID_W, lo, hi)


def _na_bias(rpb_flat):
    return pl.pallas_call(
        _na_bias_kernel,
        grid=(DEPTH,),
        in_specs=[pl.BlockSpec(memory_space=pltpu.SMEM)],
        out_specs=pl.BlockSpec((None, 4, NA_ROWS, GRID_W, NA_KEYS), lambda l: (l, 0, 0, 0, 0)),
        out_shape=jax.ShapeDtypeStruct((DEPTH, 4, NA_ROWS, GRID_W, NA_KEYS), F32),
        scratch_shapes=[pltpu.VMEM((2 * NA_ROWS - 1, GRID_W, LANES), F32)],
        compiler_params=_cparams(("parallel",)),
        name="na_bias",
    )(rpb_flat)


def _mla_ctx_kernel(ckv_ref, kpe_ref, wukv_ref, g_ref, bd_ref, kd_o, vd_o):
    kv = _dot(ckv_ref[...].astype(BF16), wukv_ref[...])
    kpe = kpe_ref[...]
    for j in range(2):
        kf = jnp.concatenate([kv[:, (2 * j) * LANES:(2 * j + 1) * LANES] + kpe,
                              kv[:, (2 * j + 1) * LANES:(2 * j + 2) * LANES] + kpe], axis=1)
        kd_o[:, 2 * j * LANES:(2 * j + 2) * LANES] = _group_norm(
            kf, bd_ref[1], 1.0 / MLA_QK, g_ref[7:8, :]).astype(BF16)
    vd_o[...] = kv[:, 4 * LANES:].astype(BF16)


def _mla_ctx(cache_ckv, cache_kpe_p, w_ukv_p, gains, bd):
    return pl.pallas_call(
        _mla_ctx_kernel,
        grid=(DEPTH, DEC_BATCH),
        in_specs=[
            pl.BlockSpec((None, None, PAST_LEN, LANES), lambda l, b: (b, l, 0, 0)),
            pl.BlockSpec((None, None, PAST_LEN, LANES), lambda l, b: (b, l, 0, 0)),
            pl.BlockSpec((None, MLA_KV_LORA, 768), lambda l, b: (l, 0, 0)),
            pl.BlockSpec((None, 16, 256), lambda l, b: (l, 0, 0)),
            pl.BlockSpec((2, 256, 256), lambda l, b: (0, 0, 0)),
        ],
        out_specs=[pl.BlockSpec((None, None, PAST_LEN, 512), lambda l, b: (l, b, 0, 0)),
                   pl.BlockSpec((None, None, PAST_LEN, 256), lambda l, b: (l, b, 0, 0))],
        out_shape=[jax.ShapeDtypeStruct((DEPTH, DEC_BATCH, PAST_LEN, 512), BF16),
                   jax.ShapeDtypeStruct((DEPTH, DEC_BATCH, PAST_LEN, 256), BF16)],
        compiler_params=_cparams(("parallel", "parallel")),
        name="mla_ctx",
    )(cache_ckv, cache_kpe_p, w_ukv_p, gains, bd)


def _merge_kernel(xc_ref, xl_ref, oc_ref, oa_ref, ob_ref, occ_ref, od_ref, mod_ref, n1_ref, wt_hbm, wb_hbm, wo_hbm,
                  x1_ref, wg_ref, wb_ref, wo_ref, stage, sems, *, layer):
    @pl.when(pl.program_id(0) == 0)
    def _():
        _load_transposed_bf16(wt_hbm, layer, GATE_BLOCK_STARTS, lambda b, w: w, wg_ref, stage, sems)
        _load_rows_bf16(wb_hbm, layer, wb_ref, stage, sems)
        _load_rows_bf16(wo_hbm, layer, wo_ref, stage, sems)

    is_ctx = pl.program_id(0) < IN_CTX_STEPS
    mod = mod_ref[...]
    lat_refs = (oa_ref, ob_ref, occ_ref, od_ref)

    def half_units(half):
        rows = slice(half * TM, (half + 1) * TM)
        st = {"merged": None}
        st["x"] = jnp.where(is_ctx, xc_ref[rows, :], xl_ref[rows, :])
        st["hb"] = _rms_mod(st["x"], n1_ref[...], mod[:, D_MODEL:2 * D_MODEL], mod[:, 0:D_MODEL]).astype(BF16)

        def branch(n):
            def issue():
                o = jnp.where(is_ctx, oc_ref[rows, n * 256:(n + 1) * 256], lat_refs[n][rows, :])
                return (_dot(st["hb"], wg_ref[:, n * D_MODEL:(n + 1) * D_MODEL]),
                        _dot(o, wb_ref[n * 256:(n + 1) * 256, :]))

            def finish(res):
                gate, br = res
                t = (1.0 / (1.0 + jnp.exp(-gate))) * br
                st["merged"] = t if st["merged"] is None else st["merged"] + t
            return issue, finish

        def out_issue():
            return _dot(st["merged"].astype(BF16), wo_ref[...])

        def out_finish(out):
            x1_ref[rows, :] = st["x"] + mod[:, 2 * D_MODEL:3 * D_MODEL] * out

        return [branch(n) for n in range(4)], (out_issue, out_finish)

    (a_br, a_out), (b_br, b_out) = half_units(0), half_units(1)
    idle = (lambda: None, lambda _: None)
    _run_units(a_br + b_br[:2] + [a_out] + b_br[2:] + [idle, b_out], depth=2)


def _merge(l, x_ctx, x_lat, o_ctx, o_lat, mods, norm1, w_in_t, w_branch, w_out):
    per_batch = DEC_SEQ // TM_IN
    ctx_blk = lambda i: jnp.minimum(i, IN_CTX_STEPS - 1)
    lat_blk = lambda i: jnp.maximum(i - IN_CTX_STEPS, 0)
    tok = pl.BlockSpec((TM_IN, D_MODEL), lambda i: (i, 0))
    ctx_tok = pl.BlockSpec((TM_IN, D_MODEL), lambda i: (ctx_blk(i), 0))
    lat_tok = lambda w: pl.BlockSpec((TM_IN, w), lambda i: (lat_blk(i), 0))
    return pl.pallas_call(
        functools.partial(_merge_kernel, layer=l),
        grid=(IN_STEPS,),
        in_specs=[
            ctx_tok, lat_tok(D_MODEL), ctx_tok, lat_tok(256), lat_tok(256), lat_tok(256), lat_tok(256),
            pl.BlockSpec((None, None, 1, 6 * D_MODEL),
                         lambda i: (l, jnp.where(i < IN_CTX_STEPS, 0, 1 + lat_blk(i) // per_batch), 0, 0)),
            _resident((None, 1, D_MODEL), lambda i: (l, 0, 0)),
            pl.BlockSpec(memory_space=pl.ANY),
            pl.BlockSpec(memory_space=pl.ANY),
            pl.BlockSpec(memory_space=pl.ANY),
        ],
        out_specs=tok,
        out_shape=jax.ShapeDtypeStruct((N_TOK, D_MODEL), F32),
        scratch_shapes=[pltpu.VMEM((D_MODEL, N_GATE), BF16), pltpu.VMEM((D_MODEL, D_MODEL), BF16),
                        pltpu.VMEM((D_MODEL, D_MODEL), BF16), pltpu.VMEM((2, WT_ROWS, D_MODEL), F32),
                        pltpu.SemaphoreType.DMA((2,))],
        compiler_params=_cparams(("arbitrary",)),
        name=f"merge_l{l}",
    )(x_ctx, x_lat, o_ctx, *o_lat, mods, norm1, w_in_t, w_branch, w_out)


HALO = 8


def _sublane_transpose(tiles):
    sub = lax.broadcasted_iota(jnp.int32, tiles[0].shape, 0)
    cur = list(tiles)
    for d in (4, 2, 1):
        keep = (sub & d) == 0
        nxt = list(cur)
        for a in range(HALO):
            if a & d:
                continue
            nxt[a] = jnp.where(keep, cur[a], pltpu.roll(cur[a + d], d, 0))
            nxt[a + d] = jnp.where(keep, pltpu.roll(cur[a], HALO - d, 0), cur[a + d])
        cur = nxt
    return cur


ROW_TILES = TM // HALO
PERM_GROUPS = ROW_TILES // HALO


def _permute_rows(x, to_permuted):
    tile = lambda j: x[j * HALO:(j + 1) * HALO, :]
    out = [None] * ROW_TILES
    for m in range(PERM_GROUPS):
        nat = [m + PERM_GROUPS * s for s in range(HALO)]
        perm = [HALO * m + i for i in range(HALO)]
        src, dst = (nat, perm) if to_permuted else (perm, nat)
        for j, t in zip(dst, _sublane_transpose([tile(j) for j in src])):
            out[j] = t
    return jnp.concatenate(out, axis=0)


def _ffn_tile(x, x_prev, x_next, has_prev, has_next, mod, n2, wup_ref, cw_ref, cb_ref, wdn_ref, u_ref):
    sh2, sc2 = mod[:, 3 * D_MODEL:4 * D_MODEL], mod[:, 4 * D_MODEL:5 * D_MODEL]
    h_prev = jnp.where(has_prev, _rms_mod(x_prev, n2, sc2, sh2), 0.0)
    h_next = jnp.where(has_next, _rms_mod(x_next, n2, sc2, sh2), 0.0)
    hp = _permute_rows(_rms_mod(x, n2, sc2, sh2), True)
    lhs = jnp.concatenate([hp, h_prev, h_next], axis=0).astype(BF16)
    sub = lax.broadcasted_iota(jnp.int32, (HALO, FF_CHUNK), 0)

    def cols(c):
        return (slice(c * FF_CHUNK, (c + 1) * FF_CHUNK), slice(D_FF + c * FF_CHUNK, D_FF + (c + 1) * FF_CHUNK))

    def up(c):
        for half, cs in enumerate(cols(c)):
            lanes = slice(half * FF_CHUNK, (half + 1) * FF_CHUNK)
            u = _dot(lhs, wup_ref[:, cs])
            u_ref[c, HALO:HALO + TM, lanes] = u[:TM]
            prev_row, next_row = u[TM + HALO - 1:TM + HALO], u[TM + HALO:TM + HALO + 1]
            u_ref[c, 0:HALO, lanes] = jnp.where(sub == 0, prev_row, pltpu.roll(u[TM - HALO:TM], 1, 0))
            u_ref[c, HALO + TM:, lanes] = jnp.where(sub == HALO - 1, next_row, pltpu.roll(u[0:HALO], HALO - 1, 0))

    def both(ref, c):
        va, vg = cols(c)
        return jnp.concatenate([ref[:, va], ref[:, vg]], axis=1)

    state = {"acc": None}

    def chunk(c, between):
        cw = both(cw_ref, c)
        t = (both(cb_ref, c) + u_ref[c, 0:TM, :] * cw[0:1] + u_ref[c, HALO:HALO + TM, :] * cw[1:2]
             + u_ref[c, 2 * HALO:2 * HALO + TM, :] * cw[2:3])
        a, g = t[:, :FF_CHUNK], t[:, FF_CHUNK:]
        act = ((g * (1.0 / (1.0 + jnp.exp(-g)))) * a).astype(BF16)
        between()
        d = _dot(act, wdn_ref[c * FF_CHUNK:(c + 1) * FF_CHUNK, :])
        state["acc"] = d if state["acc"] is None else state["acc"] + d

    def finish():
        return x + mod[:, 5 * D_MODEL:6 * D_MODEL] * _permute_rows(state["acc"], False)

    return up, chunk, finish


W_BLOCKS = N_FF_CHUNKS
WUP_BLOCK_COLS = 2 * D_FF // W_BLOCKS
WDN_BLOCK_ROWS = D_FF // W_BLOCKS


def _load_bf16_weights(layer, wup_hbm, wdn_hbm, wup_ref, wdn_ref, up_stage, dn_stage, sems):
    def up_copy(b):
        return pltpu.make_async_copy(wup_hbm.at[layer, :, pl.ds(b * WUP_BLOCK_COLS, WUP_BLOCK_COLS)],
                                     up_stage.at[b % 2], sems.at[0, b % 2])

    def dn_copy(b):
        return pltpu.make_async_copy(wdn_hbm.at[layer, pl.ds(b * WDN_BLOCK_ROWS, WDN_BLOCK_ROWS), :],
                                     dn_stage.at[b % 2], sems.at[1, b % 2])

    up_copy(0).start()
    dn_copy(0).start()
    for b in range(W_BLOCKS):
        if b + 1 < W_BLOCKS:
            up_copy(b + 1).start()
            dn_copy(b + 1).start()
        up_copy(b).wait()
        wup_ref[:, b * WUP_BLOCK_COLS:(b + 1) * WUP_BLOCK_COLS] = up_stage[b % 2].astype(BF16)
        dn_copy(b).wait()
        wdn_ref[b * WDN_BLOCK_ROWS:(b + 1) * WDN_BLOCK_ROWS, :] = dn_stage[b % 2].astype(BF16)


def _ffn_kernel(x_ref, xp_ref, xn_ref, mod_ref, n2_ref, wup_hbm, cw_ref, cb_ref, wdn_hbm, yc_ref, yl_ref,
                u_ref, wup_ref, wdn_ref, up_stage, dn_stage, sems, *, layer):
    i = pl.program_id(0)

    @pl.when(i == 0)
    def _():
        _load_bf16_weights(layer, wup_hbm, wdn_hbm, wup_ref, wdn_ref, up_stage, dn_stage, sems)

    latent = i >= IN_CTX_STEPS
    first = (i - IN_CTX_STEPS) % (DEC_SEQ // TM_IN) == 0
    last = (i - IN_CTX_STEPS) % (DEC_SEQ // TM_IN) == DEC_SEQ // TM_IN - 1
    mod = mod_ref[...]
    n2 = n2_ref[...]
    make = [
        lambda: _ffn_tile(x_ref[0:TM, :], xp_ref[...], x_ref[TM:TM + HALO, :], latent & ~first, latent, mod, n2,
                          wup_ref, cw_ref, cb_ref, wdn_ref, u_ref.at[0]),
        lambda: _ffn_tile(x_ref[TM:, :], x_ref[TM - HALO:TM, :], xn_ref[...], latent, latent & ~last, mod, n2,
                          wup_ref, cw_ref, cb_ref, wdn_ref, u_ref.at[1]),
    ]
    tiles = {}

    def tile(t):
        if t not in tiles:
            tiles[t] = make[t]()
        return tiles[t]

    tasks = [(t, c) for t in range(len(make)) for c in range(N_FF_CHUNKS)]
    early = N_FF_CHUNKS // 2 + 1
    ups = tasks[:early] + [tasks[N_FF_CHUNKS]] + tasks[early:N_FF_CHUNKS] + tasks[N_FF_CHUNKS + 1:]
    depth = 3
    for t, c in ups[:depth]:
        tile(t)[0](c)
    ys = []
    for n, (t, c) in enumerate(tasks):
        ahead = ups[n + depth] if n + depth < len(ups) else None
        tile(t)[1](c, (lambda a=ahead: tile(a[0])[0](a[1])) if ahead else (lambda: None))
        if c == N_FF_CHUNKS - 1:
            ys.append(tile(t)[2]())
    y = jnp.concatenate(ys, axis=0)

    @pl.when(i < IN_CTX_STEPS)
    def _():
        yc_ref[...] = y

    @pl.when(latent)
    def _():
        yl_ref[...] = y


def _ffn(l, x1, mods, norm2, w_up, conv_w, conv_b, w_down):
    per = TM_IN // HALO
    last = N_TOK // HALO - 1
    per_batch = DEC_SEQ // TM_IN
    lat_blk = lambda i: jnp.maximum(i - IN_CTX_STEPS, 0)
    return pl.pallas_call(
        functools.partial(_ffn_kernel, layer=l),
        grid=(IN_STEPS,),
        in_specs=[
            pl.BlockSpec((TM_IN, D_MODEL), lambda i: (i, 0)),
            pl.BlockSpec((HALO, D_MODEL), lambda i: (jnp.maximum(i * per - 1, 0), 0)),
            pl.BlockSpec((HALO, D_MODEL), lambda i: (jnp.minimum((i + 1) * per, last), 0)),
            pl.BlockSpec((None, None, 1, 6 * D_MODEL),
                         lambda i: (l, jnp.where(i < IN_CTX_STEPS, 0, 1 + lat_blk(i) // per_batch), 0, 0)),
            _resident((None, 1, D_MODEL), lambda i: (l, 0, 0)),
            pl.BlockSpec(memory_space=pl.ANY),
            _resident((None, 3, 2 * D_FF), lambda i: (l, 0, 0)),
            _resident((None, 1, 2 * D_FF), lambda i: (l, 0, 0)),
            pl.BlockSpec(memory_space=pl.ANY),
        ],
        out_specs=[pl.BlockSpec((TM_IN, D_MODEL), lambda i: (jnp.minimum(i, IN_CTX_STEPS - 1), 0)),
                   pl.BlockSpec((TM_IN, D_MODEL), lambda i: (lat_blk(i), 0))],
        out_shape=[jax.ShapeDtypeStruct((N_CTX, D_MODEL), F32), jax.ShapeDtypeStruct((N_LAT, D_MODEL), F32)],
        scratch_shapes=[
            pltpu.VMEM((TM_IN // TM, N_FF_CHUNKS, TM + 2 * HALO, 2 * FF_CHUNK), F32),
            pltpu.VMEM((D_MODEL, 2 * D_FF), BF16),
            pltpu.VMEM((D_FF, D_MODEL), BF16),
            pltpu.VMEM((2, D_MODEL, WUP_BLOCK_COLS), F32),
            pltpu.VMEM((2, WDN_BLOCK_ROWS, D_MODEL), F32),
            pltpu.SemaphoreType.DMA((2, 2)),
        ],
        compiler_params=_cparams(("arbitrary",)),
        name=f"ffn_l{l}",
    )(x1, x1, x1, mods, norm2, w_up, conv_w, conv_b, w_down)


WT_ROWS = 512
_SRC_CKV = C_QD + 4 * MLA_QK
QKV_BLOCK_STARTS = (0, WT_ROWS, 2 * WT_ROWS, C_KC, C_QD + 2 * MLA_QK)
GATE_BLOCK_STARTS = tuple(IN_QKV + WT_ROWS * b for b in range(N_GATE // WT_ROWS))


def _qkv_block_rows(b, w):
    zeros = lambda n: jnp.zeros((n, D_MODEL), F32)
    pad = zeros(LANES - MLA_QK)

    def two_heads(r0):
        return [w[r0:r0 + MLA_QK], pad, w[r0 + MLA_QK:r0 + 2 * MLA_QK], pad]

    if QKV_BLOCK_STARTS[b] == C_KC:
        n_kv = C_QD - C_KC
        return jnp.concatenate([w[0:n_kv]] + two_heads(n_kv), axis=0)
    if QKV_BLOCK_STARTS[b] == C_QD + 2 * MLA_QK:
        r_ckv = _SRC_CKV - QKV_BLOCK_STARTS[b]
        r_kpe = r_ckv + MLA_KV_LORA
        return jnp.concatenate(two_heads(0) + [w[r_ckv:r_kpe], zeros(MLA_NOPE), w[r_kpe:r_kpe + MLA_ROPE], pad],
                               axis=0)
    return w


def _load_rows_bf16(w_hbm, layer, dst_ref, stage, sems):
    n_blocks = dst_ref.shape[0] // WT_ROWS

    def copy(b):
        return pltpu.make_async_copy(w_hbm.at[layer, pl.ds(b * WT_ROWS, WT_ROWS), :], stage.at[b % 2], sems.at[b % 2])

    copy(0).start()
    for b in range(n_blocks):
        if b + 1 < n_blocks:
            copy(b + 1).start()
        copy(b).wait()
        dst_ref[b * WT_ROWS:(b + 1) * WT_ROWS, :] = stage[b % 2].astype(BF16)


def _load_transposed_bf16(wt_hbm, layer, starts, select, dst_ref, stage, sems):
    def copy(b):
        return pltpu.make_async_copy(wt_hbm.at[layer, pl.ds(starts[b], WT_ROWS), :], stage.at[b % 2], sems.at[b % 2])

    copy(0).start()
    for b in range(len(starts)):
        if b + 1 < len(starts):
            copy(b + 1).start()
        copy(b).wait()
        dst_ref[:, b * WT_ROWS:(b + 1) * WT_ROWS] = select(b, stage[b % 2]).T.astype(BF16)


def _rope_tables():
    t = np.arange(DEC_SEQ)

    def angles(dim):
        half = dim // 2
        inv = ROPE_THETA ** (-np.arange(0, half, 2, dtype=np.float32) / half)
        row = (t // GRID_W).astype(np.float32)[:, None] * inv[None, :]
        col = (t % GRID_W).astype(np.float32)[:, None] * inv[None, :]
        return np.concatenate([row, row, col, col], axis=-1).astype(np.float32)

    def signed(ang, quarter):
        cos, sin = np.cos(ang), np.sin(ang)
        first = (np.arange(ang.shape[1]) % (2 * quarter)) < quarter
        return cos, np.where(first, -sin, 0.0), np.where(first, 0.0, sin)

    c64, lo64, hi64 = signed(angles(HEAD_DIM), 16)
    c64, lo64, hi64 = (np.tile(a, (1, 2)) for a in (c64, lo64, hi64))
    cm, lom, him = signed(angles(MLA_ROPE), 8)

    def mla_pad(a, fill):
        out = np.full((DEC_SEQ, LANES), fill, np.float32)
        out[:, MLA_NOPE:MLA_QK] = a
        return out

    tabs = [c64, lo64, hi64, mla_pad(cm, 1.0), mla_pad(lom, 0.0), mla_pad(him, 0.0)]
    ident = [np.ones, np.zeros, np.zeros, np.ones, np.zeros, np.zeros]
    full = [np.concatenate([f((TM_IN, LANES), np.float32), a.astype(np.float32)], axis=0) for f, a in zip(ident, tabs)]
    return jnp.asarray(np.stack(full, axis=0))


def _block_diag_ones():
    idx = np.arange(256)
    mats = [(idx[:, None] // w == idx[None, :] // w).astype(np.float32) for w in (HEAD_DIM, LANES)]
    return jnp.asarray(np.stack(mats, axis=0), dtype=BF16)


def _prep_small(w_ukv, qn_a, kn_a, qn_b, kn_b, qn_c, kn_c, qn_d, kn_d, kvn_d):
    ukv = w_ukv.reshape(DEPTH, MLA_KV_LORA, 4, 2, HEAD_DIM)
    k_part = jnp.pad(ukv[:, :, :, 0], ((0, 0), (0, 0), (0, 0), (0, HEAD_DIM))).reshape(DEPTH, MLA_KV_LORA, 512)
    v_part = ukv[:, :, :, 1].reshape(DEPTH, MLA_KV_LORA, 256)
    w_ukv_p = jnp.concatenate([k_part, v_part], axis=-1).astype(BF16)

    def row(g, reps, width=256):
        t = jnp.tile(g, (1, reps))
        return jnp.pad(t, ((0, 0), (0, width - t.shape[1])))

    pad96 = lambda g: jnp.pad(g, ((0, 0), (0, LANES - MLA_QK)))
    rows = [row(qn_a, 4), row(kn_a, 4), row(qn_b, 4), row(kn_b, 2), row(qn_c, 4), row(kn_c, 2),
            row(pad96(qn_d), 2), row(pad96(kn_d), 2), row(kvn_d, 1)]
    gains = jnp.stack(rows + [jnp.zeros_like(rows[0])] * (16 - len(rows)), axis=1)
    return w_ukv_p, gains


def kernel(x_prompt, x_sample, cache_nat_k, cache_nat_v, cache_gqa_k, cache_gqa_v, cache_win_k, cache_win_v,
           cache_mla_ckv, cache_mla_kpe, c, c_ctx, w_ada, b_ada, norm1, norm2, w_in, qn_a, kn_a, rpb_a,
           qn_b, kn_b, qn_c, kn_c, sink_c, qn_d, kn_d, kvn_d, w_ukv, w_branch, w_out, w_up, conv_w, conv_b,
           w_down):
    w_ukv_p, gains = _prep_small(w_ukv, qn_a, kn_a, qn_b, kn_b, qn_c, kn_c, qn_d, kn_d, kvn_d)
    w_in_t = jnp.swapaxes(w_in, 1, 2)
    w_branch_r = w_branch.reshape(DEPTH, 4 * 256, D_MODEL)
    conv_b_r = conv_b.reshape(DEPTH, 1, 2 * D_FF)
    rope_tab = _rope_tables()
    bd = _block_diag_ones()
    norm1_r = norm1.reshape(DEPTH, 1, D_MODEL)
    norm2_r = norm2.reshape(DEPTH, 1, D_MODEL)

    cond8 = jnp.zeros((8, D_MODEL), F32).at[0].set(c_ctx).at[1:1 + DEC_BATCH].set(c)
    mods = _adaln(cond8, w_ada, b_ada).reshape(DEPTH, 8, 1, 6 * D_MODEL)

    rpb_flat = jnp.pad(rpb_a.reshape(-1), (0, 8192 - rpb_a.size))
    na_bias = _na_bias(rpb_flat)
    kpe_p = jnp.pad(cache_mla_kpe, ((0, 0), (0, 0), (0, 0), (MLA_NOPE, LANES - MLA_QK)))
    kd_ctx, vd_ctx = _mla_ctx(cache_mla_ckv, kpe_p, w_ukv_p, gains, bd)
    c_nat_k = cache_nat_k.reshape(DEC_BATCH, DEPTH, PAST_LEN, 256)
    c_nat_v = cache_nat_v.reshape(DEC_BATCH, DEPTH, PAST_LEN, 256)
    c_gqa_k = cache_gqa_k.reshape(DEC_BATCH, DEPTH, PAST_LEN, LANES)
    c_gqa_v = cache_gqa_v.reshape(DEC_BATCH, DEPTH, PAST_LEN, LANES)
    c_win_k = cache_win_k.reshape(DEC_BATCH, DEPTH, PAST_LEN, LANES)
    c_win_v = cache_win_v.reshape(DEC_BATCH, DEPTH, PAST_LEN, LANES)

    x_ctx, x_lat = x_prompt.reshape(N_CTX, D_MODEL), x_sample.reshape(N_LAT, D_MODEL)
    caches = [jnp.zeros((BATCH, DEPTH) + blk, F32) for blk in _CACHE_BLOCKS]
    for l in range(DEPTH):
        qkv, caches = _inproj(l, x_ctx, x_lat, mods, norm1_r, w_in_t, w_ukv_p, gains, bd, rope_tab, caches)
        qa, ka, va, qb, kb, vb, qc, kc, vc, qd, kd, vd = qkv
        sink_l = sink_c[l]
        o_ctx = _ctx_attention(l, sink_l, qkv)
        o_lat = (_lat_na(l, qa, ka, va, c_nat_k, c_nat_v, na_bias),
                 _lat_gqa(l, qb, kb, vb, c_gqa_k, c_gqa_v),
                 _lat_window(l, sink_l, qc, kc, vc, c_win_k, c_win_v),
                 _lat_mla(l, qd, kd, vd, kd_ctx, vd_ctx))
        x1 = _merge(l, x_ctx, x_lat, o_ctx, o_lat, mods, norm1_r, w_in_t, w_branch_r, w_out)
        x_ctx, x_lat = _ffn(l, x1, mods, norm2_r, w_up, conv_w, conv_b_r, w_down)
    y_prompt = x_ctx.reshape(BATCH, SEQ, D_MODEL)
    y_sample = x_lat.reshape(DEC_BATCH, DEC_SEQ, D_MODEL)
    heads = (4, 4, 2, 2, 2, 2)
    outs = [a.reshape(BATCH, DEPTH, h, HEAD_DIM, SEQ).transpose(0, 1, 4, 2, 3) for a, h in zip(caches[:6], heads)]
    return (y_prompt, y_sample, *outs, jnp.swapaxes(caches[6], 2, 3), jnp.swapaxes(caches[7], 2, 3))
```

```python
import functools

import jax
import jax.numpy as jnp
import numpy as np
from jax import lax
from jax.experimental import pallas as pl
from jax.experimental.pallas import tpu as pltpu

F32 = jnp.float32
BF16 = jnp.bfloat16

D_MODEL = 1024
BATCH = 16
SEQ = 256
DEPTH = 4
DEC_BATCH = 2
DEC_SEQ = 2048
PAST_LEN = 256
GRID_W = 64
HEAD_DIM = 64
NA_ROWS = 8
NA_COLS = 16
WINDOW = 128
MLA_NOPE = 64
MLA_ROPE = 32
MLA_QK = MLA_NOPE + MLA_ROPE
MLA_KV_LORA = 128
D_FF = 2816
ROPE_THETA = 10000.0
EPS = 1e-6
NEG_INF = -1e30

LANES = 128
TM = 256
N_CTX = BATCH * SEQ
N_LAT = DEC_BATCH * DEC_SEQ
N_TOK = N_CTX + N_LAT
GRID_ROWS = DEC_SEQ // GRID_W
FF_CHUNK = 256
N_FF_CHUNKS = D_FF // FF_CHUNK
VMEM_LIMIT = 56 * 1024 * 1024

C_QA, C_KA, C_VA = 0, 256, 512
C_QB, C_KB, C_VB = 768, 1024, 1152
C_QC, C_KC, C_VC = 1280, 1536, 1664
C_QD, C_CKV, C_KPE, C_END = 1792, 2304, 2432, 2560
N_GATE = 4 * D_MODEL
IN_QKV = 2336
LOG2E = float(np.log2(np.e))
Q_SCALE = HEAD_DIM ** -0.5 * LOG2E
MLA_Q_SCALE = MLA_QK ** -0.5 * LOG2E


def _cparams(sem):
    return pltpu.CompilerParams(dimension_semantics=sem, vmem_limit_bytes=VMEM_LIMIT)


def _resident(shape, index_map):
    return pl.BlockSpec(shape, index_map, pipeline_mode=pl.Buffered(1))


def _dot(a, b):
    return jnp.dot(a, b, preferred_element_type=F32)


def _dot_t(a, b):
    return lax.dot_general(a, b, (((1,), (1,)), ((), ())), preferred_element_type=F32)


def _rms_mod(x, gain, scale, shift):
    ms = jnp.mean(x * x, axis=-1, keepdims=True)
    return (x * lax.rsqrt(ms + EPS)) * gain * (1.0 + scale) + shift


def _group_norm(t, ones_bd, inv_n, gain):
    ss = _dot((t * t).astype(BF16), ones_bd)
    return t * lax.rsqrt(ss * inv_n + EPS) * gain


def _rope(t, cos, sin_lo, sin_hi, half):
    return (t * cos + pltpu.roll(t, LANES - half, 1) * sin_lo + pltpu.roll(t, half, 1) * sin_hi)


def _lane_iota(shape):
    return lax.broadcasted_iota(jnp.int32, shape, len(shape) - 1)


def _low_half(shape):
    return _lane_iota(shape) < HEAD_DIM


def _run_units(units, depth=2):
    n = len(units)
    pending = {}
    outs = []
    for k in range(min(depth, n)):
        pending[k] = units[k][0]()
    for k in range(n):
        outs.append(units[k][1](pending.pop(k)))
        if k + depth < n:
            pending[k + depth] = units[k + depth][0]()
    return outs


N_COND = 1 + DEC_BATCH


def _adaln_kernel(cond_ref, w_ref, b_ref, out_ref):
    cnd = cond_ref[...]
    s_t = (cnd * (1.0 / (1.0 + jnp.exp(-cnd)))).T
    w = w_ref[...]
    out_ref[...] = jnp.zeros(out_ref.shape, F32)
    for r in range(N_COND):
        out_ref[r:r + 1, :] = jnp.sum(s_t[:, r:r + 1] * w, axis=0, keepdims=True) + b_ref[...]


def _adaln(cond8, w_ada, b_ada):
    n_col = 6 * D_MODEL
    tn = 2048
    return pl.pallas_call(
        _adaln_kernel,
        grid=(DEPTH, n_col // tn),
        in_specs=[
            pl.BlockSpec((8, D_MODEL), lambda l, n: (0, 0)),
            pl.BlockSpec((None, D_MODEL, tn), lambda l, n: (l, 0, n)),
            pl.BlockSpec((None, 1, tn), lambda l, n: (l, 0, n)),
        ],
        out_specs=pl.BlockSpec((None, 8, tn), lambda l, n: (l, 0, n)),
        out_shape=jax.ShapeDtypeStruct((DEPTH, 8, n_col), F32),
        compiler_params=_cparams(("parallel", "parallel")),
        name="adaln",
    )(cond8, w_ada, b_ada.reshape(DEPTH, 1, n_col))


_QKV_WIDTHS = (512, 256, 256, 512, 128, 128, 512, 128, 128, 512, 512, 256)
_CACHE_WIDTHS = (256, 256, 128, 128, 128, 128, MLA_KV_LORA, MLA_ROPE)
_CACHE_BLOCKS = tuple((w, SEQ) for w in _CACHE_WIDTHS)


def _zeros_kernel(*outs):
    for o in outs:
        o[...] = jnp.zeros(o.shape, F32)


def _zero_caches():
    return pl.pallas_call(
        _zeros_kernel,
        grid=(BATCH,),
        in_specs=[],
        out_specs=[pl.BlockSpec((None, DEPTH) + blk, lambda b: (b, 0, 0, 0)) for blk in _CACHE_BLOCKS],
        out_shape=[jax.ShapeDtypeStruct((BATCH, DEPTH) + blk, F32) for blk in _CACHE_BLOCKS],
        compiler_params=_cparams(("parallel",)),
        name="zero_caches",
    )()


TM_IN = 2 * TM
IN_STEPS = N_TOK // TM_IN
IN_CTX_STEPS = N_CTX // TM_IN


def _inproj_kernel(xc_ref, xl_ref, mod_ref, n1_ref, wt_hbm, wukv_ref, g_ref, bd_ref, rope_ref, *rest, layer):
    n_out = len(_QKV_WIDTHS) + len(_CACHE_WIDTHS)
    outs = rest[len(_CACHE_WIDTHS):len(_CACHE_WIDTHS) + n_out]
    w_ref, stage, sems = rest[len(_CACHE_WIDTHS) + n_out:]

    @pl.when(pl.program_id(0) == 0)
    def _():
        _load_transposed_bf16(wt_hbm, layer, QKV_BLOCK_STARTS, _qkv_block_rows, w_ref, stage, sems)

    is_ctx = pl.program_id(0) >= IN_STEPS - IN_CTX_STEPS
    units = []
    for half in range(TM_IN // TM):
        rows = slice(half * TM, (half + 1) * TM)
        views = [o.at[rows] for o in outs[:len(_QKV_WIDTHS)]] + [o.at[half] for o in outs[len(_QKV_WIDTHS):]]
        units += _inproj_half(is_ctx, xc_ref.at[rows], xl_ref.at[rows], mod_ref, n1_ref, w_ref, wukv_ref, g_ref,
                              bd_ref, rope_ref.at[:, rows, :], views)
    _run_units(units, depth=2)


def _inproj_half(is_ctx, xc_ref, xl_ref, mod_ref, n1_ref, w_ref, wukv_ref, g_ref, bd_ref, rope_ref, outs):
    qa_o, ka_o, va_o, qb_o, kb_o, vb_o, qc_o, kc_o, vc_o, qd_o, kd_o, vd_o = outs[:len(_QKV_WIDTHS)]
    natk_o, natv_o, gqak_o, gqav_o, wink_o, winv_o, ckv_o, kpe_o = outs[len(_QKV_WIDTHS):]
    x = jnp.where(is_ctx, xc_ref[...], xl_ref[...])
    mod = mod_ref[...]
    hb = _rms_mod(x, n1_ref[...], mod[:, D_MODEL:2 * D_MODEL], mod[:, 0:D_MODEL]).astype(BF16)
    bd64, bd128 = bd_ref[0], bd_ref[1]
    cos64, slo64, shi64 = rope_ref[0], rope_ref[1], rope_ref[2]
    cosm, slom, shim = rope_ref[3], rope_ref[4], rope_ref[5]
    inv64, inv96 = 1.0 / HEAD_DIM, 1.0 / MLA_QK
    low = _low_half((TM, LANES))
    held = {}

    def chunks(t):
        return [t[:, c:c + LANES] for c in range(0, t.shape[1], LANES)]

    def rope64(t):
        parts = [_rope(p, cos64, slo64, shi64, 16) for p in chunks(t)]
        return parts[0] if len(parts) == 1 else jnp.concatenate(parts, axis=1)

    def ropem(t):
        return jnp.concatenate([_rope(p, cosm, slom, shim, 8) for p in chunks(t)], axis=1)

    def store_q(ref, q, grouped):
        q = q * Q_SCALE
        slots = []
        for hd in range(4):
            part, src = q[:, (hd // 2) * LANES:(hd // 2 + 1) * LANES], hd % 2
            dst = hd // 2 if grouped else src
            if src != dst:
                part = pltpu.roll(part, HEAD_DIM, 1)
            slots.append(jnp.where(low if dst == 0 else ~low, part, 0.0))
        ref[...] = jnp.concatenate(slots, axis=1).astype(BF16)

    def post_latent(t):
        ckv = _group_norm(t[:, :LANES], bd128[:LANES, :LANES], 1.0 / MLA_KV_LORA, g_ref[8:9, :LANES])
        kpe = t[:, LANES:]
        ckv_o[...] = ckv.T
        kpe_o[...] = kpe.T[MLA_NOPE:MLA_QK, :]
        held["kpe"] = kpe
        held["kv"] = _dot(ckv.astype(BF16), wukv_ref[...])

    def post_kd(_):
        kv, kpe = held["kv"], held["kpe"]
        for j in range(2):
            kf = jnp.concatenate([kv[:, (2 * j) * LANES:(2 * j + 1) * LANES] + kpe,
                                  kv[:, (2 * j + 1) * LANES:(2 * j + 2) * LANES] + kpe], axis=1)
            kd_o[:, 2 * j * LANES:(2 * j + 2) * LANES] = ropem(
                _group_norm(kf, bd128, inv96, g_ref[7:8, :])).astype(BF16)
        vd_o[...] = kv[:, 4 * LANES:].astype(BF16)

    def post_qa(t):
        store_q(qa_o, _group_norm(t, bd64, inv64, g_ref[0:1, :]), grouped=False)

    def post_ka(t):
        k = _group_norm(t, bd64, inv64, g_ref[1:2, :])
        natk_o[...] = k.T
        ka_o[...] = k.astype(BF16)

    def post_va(t):
        natv_o[...] = t.T
        va_o[...] = t.astype(BF16)

    def post_q_rope(ref, row):
        def post(t):
            store_q(ref, rope64(_group_norm(t, bd64, inv64, g_ref[row:row + 1, :])), grouped=True)
        return post

    def post_kv(k_ref, v_ref, kc_ref, vc_ref, row):
        def post(t):
            k = _group_norm(t[:, :LANES], bd64[:LANES, :LANES], inv64, g_ref[row:row + 1, :LANES])
            kc_ref[...] = k.T
            k_ref[...] = rope64(k).astype(BF16)
            v = t[:, LANES:]
            vc_ref[...] = v.T
            v_ref[...] = v.astype(BF16)
        return post

    def post_qd(j):
        def post(t):
            qd_o[:, 2 * j * LANES:(2 * j + 2) * LANES] = (ropem(
                _group_norm(t, bd128, inv96, g_ref[6:7, :])) * MLA_Q_SCALE).astype(BF16)
        return post

    stages = [
        ((C_CKV, C_END), post_latent),
        ((C_QA, C_KA), post_qa),
        (None, post_kd),
        ((C_QB, C_KB), post_q_rope(qb_o, 2)),
        ((C_QC, C_KC), post_q_rope(qc_o, 4)),
        ((C_QD, C_QD + 2 * LANES), post_qd(0)),
        ((C_QD + 2 * LANES, C_CKV), post_qd(1)),
        ((C_KB, C_QC), post_kv(kb_o, vb_o, gqak_o, gqav_o, 3)),
        ((C_KC, C_QD), post_kv(kc_o, vc_o, wink_o, winv_o, 5)),
        ((C_KA, C_VA), post_ka),
        ((C_VA, C_QB), post_va),
    ]

    def issue(cols):
        return None if cols is None else _dot(hb, w_ref[:, cols[0]:cols[1]])

    return [(functools.partial(issue, cols), post) for cols, post in stages]


def _inproj(l, x_ctx, x_lat, mods, norm1, w_in_t, w_ukv_p, gains, bd, rope_tab, caches):
    n_lat = IN_STEPS - IN_CTX_STEPS
    blk = lambda s: (s + IN_CTX_STEPS) % IN_STEPS
    lat_blk = lambda s: jnp.maximum(blk(s) - IN_CTX_STEPS, 0)
    per_batch = DEC_SEQ // TM_IN
    tok = lambda w: pl.BlockSpec((TM_IN, w), lambda s: (blk(s), 0))
    in_specs = [
        pl.BlockSpec((TM_IN, D_MODEL), lambda s: (jnp.minimum(blk(s), IN_CTX_STEPS - 1), 0)),
        pl.BlockSpec((TM_IN, D_MODEL), lambda s: (lat_blk(s), 0)),
        pl.BlockSpec((None, None, 1, 6 * D_MODEL),
                     lambda s: (l, jnp.where(s < n_lat, 1 + lat_blk(s) // per_batch, 0), 0, 0)),
        _resident((None, 1, D_MODEL), lambda s: (l, 0, 0)),
        pl.BlockSpec(memory_space=pl.ANY),
        _resident((None, MLA_KV_LORA, 768), lambda s: (l, 0, 0)),
        _resident((None, 16, 256), lambda s: (l, 0, 0)),
        _resident((2, 256, 256), lambda s: (0, 0, 0)),
        pl.BlockSpec((6, TM_IN, LANES), lambda s: (0, jnp.where(s < n_lat, 1 + lat_blk(s) % per_batch, 0), 0)),
    ]
    args = [x_ctx, x_lat, mods, norm1, w_in_t, w_ukv_p, gains, bd, rope_tab] + list(caches)
    aliases = {len(in_specs) + n: len(_QKV_WIDTHS) + n for n in range(len(caches))}
    in_specs = in_specs + [pl.BlockSpec(memory_space=pl.ANY)] * len(caches)
    cache_spec = lambda cb: pl.BlockSpec((TM_IN // SEQ, None) + cb, lambda s: (jnp.maximum(s - n_lat, 0), l, 0, 0))
    outs = pl.pallas_call(
        functools.partial(_inproj_kernel, layer=l),
        grid=(IN_STEPS,),
        in_specs=in_specs,
        out_specs=[tok(w) for w in _QKV_WIDTHS] + [cache_spec(blk) for blk in _CACHE_BLOCKS],
        out_shape=[jax.ShapeDtypeStruct((N_TOK, w), BF16) for w in _QKV_WIDTHS]
        + [jax.ShapeDtypeStruct((BATCH, DEPTH) + blk, F32) for blk in _CACHE_BLOCKS],
        scratch_shapes=[pltpu.VMEM((D_MODEL, C_END), BF16), pltpu.VMEM((2, WT_ROWS, D_MODEL), F32),
                        pltpu.SemaphoreType.DMA((2,))],
        input_output_aliases=aliases,
        compiler_params=_cparams(("arbitrary",)),
        name=f"inproj_l{l}",
    )(*args)
    return outs[:len(_QKV_WIDTHS)], outs[len(_QKV_WIDTHS):]


def _scores(q, keys):
    out = []
    for k, bias in keys:
        s = _dot_t(q, k)
        if bias is not None:
            s = s + bias
        out.append(s)
    return out


def _softmax_pv(scores, values, sink=None):
    m = functools.reduce(jnp.maximum, [jnp.max(s, axis=-1, keepdims=True) for s in scores])
    if sink is not None:
        m = jnp.maximum(m, sink)
    denom = None
    out = None
    for s, v in zip(scores, values):
        e = jnp.exp2(s - m)
        d = jnp.sum(e, axis=-1, keepdims=True)
        o = _dot(e.astype(BF16), v)
        denom = d if denom is None else denom + d
        out = o if out is None else out + o
    if sink is not None:
        denom = denom + jnp.exp2(sink - m)
    return out / denom


def _pair_unit(q_ref, rows, j, keys, values, kv_half, sinks=None):
    m_rows = rows.stop - rows.start

    def issue():
        q2 = jnp.concatenate([q_ref[rows, (2 * j) * LANES:(2 * j + 1) * LANES],
                              q_ref[rows, (2 * j + 1) * LANES:(2 * j + 2) * LANES]], axis=0)
        return _scores(q2, keys())

    def finish(scores):
        sink = None
        if sinks is not None:
            row = lax.broadcasted_iota(jnp.int32, (2 * m_rows, 1), 0)
            sink = jnp.where(row < m_rows, sinks[0], sinks[1])
        o2 = _softmax_pv(scores, values(), sink)
        lo, hi = o2[:m_rows], o2[m_rows:]
        if kv_half == 0:
            hi = pltpu.roll(hi, HEAD_DIM, 1)
        elif kv_half == 1:
            lo = pltpu.roll(lo, HEAD_DIM, 1)
        return jnp.where(_low_half(lo.shape), lo, hi)

    return issue, finish


def _single_unit(q_ref, rows, hd, keys, values):
    def issue():
        return _scores(q_ref[rows, hd * LANES:(hd + 1) * LANES], keys())

    def finish(scores):
        return _softmax_pv(scores, values())

    return issue, finish


def _merge_halves(o_even, o_odd):
    return jnp.where(_low_half(o_even.shape), o_even, o_odd)


def _ctx_attn_kernel(sink_ref, qa, ka, va, qb, kb, vb, qc, kc, vc, qd, kd, vd, o_ref):
    rows = slice(0, SEQ)
    units = []
    for j in range(2):
        c = slice(j * LANES, (j + 1) * LANES)
        units.append(_pair_unit(qa, rows, j, lambda c=c: [(ka[:, c], None)], lambda c=c: [va[:, c]], None))
    for q, k, v, with_sink in ((qb, kb, vb, False), (qc, kc, vc, True)):
        for j in range(2):
            sinks = (sink_ref[2 * j] * LOG2E, sink_ref[2 * j + 1] * LOG2E) if with_sink else None
            units.append(_pair_unit(q, rows, j, lambda k=k: [(k[...], None)], lambda v=v: [v[...]], j, sinks))
    for hd in range(4):
        c = slice(hd * LANES, (hd + 1) * LANES)
        vs = slice((hd // 2) * LANES, (hd // 2 + 1) * LANES)
        units.append(_single_unit(qd, rows, hd, lambda c=c: [(kd[:, c], None)], lambda vs=vs: [vd[:, vs]]))
    outs = _run_units(units)
    for n in range(6):
        o_ref[:, n * LANES:(n + 1) * LANES] = outs[n].astype(BF16)
    for j in range(2):
        o_ref[:, (6 + j) * LANES:(7 + j) * LANES] = _merge_halves(outs[6 + 2 * j], outs[7 + 2 * j]).astype(BF16)


def _ctx_attention(l, sink_c, qkv):
    blk = lambda w: pl.BlockSpec((SEQ, w), lambda b: (b, 0))
    return pl.pallas_call(
        _ctx_attn_kernel,
        grid=(BATCH,),
        in_specs=[pl.BlockSpec(memory_space=pltpu.SMEM)] + [blk(w) for w in _QKV_WIDTHS],
        out_specs=pl.BlockSpec((SEQ, D_MODEL), lambda b: (b, 0)),
        out_shape=jax.ShapeDtypeStruct((N_CTX, D_MODEL), BF16),
        compiler_params=_cparams(("parallel",)),
        name=f"ctx_attn_l{l}",
    )(sink_c, *qkv)


LAT_TQ = 256
LAT_SUB = 128
LAT_QBLOCKS = DEC_SEQ // LAT_TQ
LAT_ROW0 = N_CTX // LAT_TQ
LAT_KV0 = N_CTX // DEC_SEQ


def _lat_qspec(width):
    return pl.BlockSpec((LAT_TQ, width), lambda b, i: (LAT_ROW0 + b * LAT_QBLOCKS + i, 0))


def _lat_kvspec(width):
    return pl.BlockSpec((DEC_SEQ, width), lambda b, i: (LAT_KV0 + b, 0))


def _lat_call(kernel, l, name, in_specs, args):
    return pl.pallas_call(
        kernel,
        grid=(DEC_BATCH, LAT_QBLOCKS),
        in_specs=in_specs,
        out_specs=pl.BlockSpec((LAT_TQ, 256), lambda b, i: (b * LAT_QBLOCKS + i, 0)),
        out_shape=jax.ShapeDtypeStruct((N_LAT, 256), BF16),
        compiler_params=_cparams(("parallel", "parallel")),
        name=f"{name}_l{l}",
    )(*args)


def _sub_rows(u):
    return slice(u * LAT_SUB, (u + 1) * LAT_SUB)


def _store_units(o_ref, outs):
    for n, o in enumerate(outs):
        u, j = divmod(n, 2)
        o_ref[_sub_rows(u), j * LANES:(j + 1) * LANES] = o.astype(BF16)


def _gqa_kernel(q_ref, k_ref, v_ref, kc_ref, vc_ref, o_ref):
    keys = lambda: [(kc_ref[...].astype(BF16), None), (k_ref[...], None)]
    values = lambda: [vc_ref[...].astype(BF16), v_ref[...]]
    units = [_pair_unit(q_ref, _sub_rows(u), j, keys, values, j) for u in range(LAT_TQ // LAT_SUB) for j in range(2)]
    _store_units(o_ref, _run_units(units))


def _lat_gqa(l, qb, kb, vb, cache_k, cache_v):
    cspec = pl.BlockSpec((None, None, PAST_LEN, LANES), lambda b, i: (b, l, 0, 0))
    return _lat_call(_gqa_kernel, l, "lat_gqa",
                     [_lat_qspec(512), _lat_kvspec(LANES), _lat_kvspec(LANES), cspec, cspec],
                     (qb, kb, vb, cache_k, cache_v))


WIN_KEYS = 3 * LAT_SUB


def _win_kernel(sink_ref, q_ref, k_ref, v_ref, kc_ref, vc_ref, o_ref):
    units = []
    for u in range(LAT_TQ // LAT_SUB):
        blk = pl.program_id(1) * (LAT_TQ // LAT_SUB) + u
        start = pl.multiple_of(jnp.clip((blk - 1) * LAT_SUB, 0, DEC_SEQ - WIN_KEYS), LAT_SUB)

        def keys(blk=blk, start=start):
            row = lax.broadcasted_iota(jnp.int32, (2 * LAT_SUB, WIN_KEYS), 0)
            qpos = blk * LAT_SUB + jnp.where(row < LAT_SUB, row, row - LAT_SUB)
            kpos = start + lax.broadcasted_iota(jnp.int32, (2 * LAT_SUB, WIN_KEYS), 1)
            bias = jnp.where(jnp.abs(qpos - kpos) <= WINDOW, 0.0, NEG_INF)
            return [(kc_ref[...].astype(BF16), None), (k_ref[pl.ds(start, WIN_KEYS), :], bias)]

        def values(start=start):
            return [vc_ref[...].astype(BF16), v_ref[pl.ds(start, WIN_KEYS), :]]

        for j in range(2):
            sinks = (sink_ref[2 * j] * LOG2E, sink_ref[2 * j + 1] * LOG2E)
            units.append(_pair_unit(q_ref, _sub_rows(u), j, keys, values, j, sinks))
    _store_units(o_ref, _run_units(units, depth=4))


def _lat_window(l, sink_c, qc, kc, vc, cache_k, cache_v):
    cspec = pl.BlockSpec((None, None, PAST_LEN, LANES), lambda b, i: (b, l, 0, 0))
    return _lat_call(_win_kernel, l, "lat_win",
                     [pl.BlockSpec(memory_space=pltpu.SMEM), _lat_qspec(512), _lat_kvspec(LANES),
                      _lat_kvspec(LANES), cspec, cspec],
                     (sink_c, qc, kc, vc, cache_k, cache_v))


def _mla_kernel(q_ref, k_ref, v_ref, kc_ref, vc_ref, o_ref):
    units = []
    rows = slice(0, LAT_TQ)
    for hd in range(4):
        c = slice(hd * LANES, (hd + 1) * LANES)
        vs = slice((hd // 2) * LANES, (hd // 2 + 1) * LANES)
        units.append(_single_unit(q_ref, rows, hd,
                                  lambda c=c: [(kc_ref[:, c], None), (k_ref[:, c], None)],
                                  lambda vs=vs: [vc_ref[:, vs], v_ref[:, vs]]))
    outs = _run_units(units, depth=3)
    for j in range(2):
        o_ref[:, j * LANES:(j + 1) * LANES] = _merge_halves(outs[2 * j], outs[2 * j + 1]).astype(BF16)


def _lat_mla(l, qd, kd, vd, kd_ctx, vd_ctx):
    kcs = pl.BlockSpec((None, None, PAST_LEN, 512), lambda b, i: (l, b, 0, 0))
    vcs = pl.BlockSpec((None, None, PAST_LEN, 256), lambda b, i: (l, b, 0, 0))
    return _lat_call(_mla_kernel, l, "lat_mla",
                     [_lat_qspec(512), _lat_kvspec(512), _lat_kvspec(256), kcs, vcs],
                     (qd, kd, vd, kd_ctx, vd_ctx))


NA_KEYS = NA_ROWS * GRID_W
NA_ROWS_PER_STEP = LAT_TQ // GRID_W


def _na_shift(r):
    return jnp.where(r < NA_ROWS // 2, r,
                     jnp.where(r > GRID_ROWS - NA_ROWS // 2, r - (GRID_ROWS - NA_ROWS), NA_ROWS // 2))


def _na_kernel(q_ref, k_ref, v_ref, kc_ref, vc_ref, bias_ref, o_ref):
    units = []
    for rr in range(NA_ROWS_PER_STEP):
        r = pl.program_id(1) * NA_ROWS_PER_STEP + rr
        start = pl.multiple_of(jnp.clip(r - NA_ROWS // 2, 0, GRID_ROWS - NA_ROWS) * GRID_W, GRID_W)
        shift = _na_shift(r)
        rows = slice(rr * GRID_W, (rr + 1) * GRID_W)
        for j in range(2):
            c = slice(j * LANES, (j + 1) * LANES)

            def keys(j=j, c=c, start=start, shift=shift):
                bias = jnp.concatenate([bias_ref[2 * j, shift], bias_ref[2 * j + 1, shift]], axis=0)
                return [(kc_ref[:, c].astype(BF16), None), (k_ref[pl.ds(start, NA_KEYS), c], bias)]

            def values(c=c, start=start):
                return [vc_ref[:, c].astype(BF16), v_ref[pl.ds(start, NA_KEYS), c]]

            units.append(_pair_unit(q_ref, rows, j, keys, values, None))
    for n, o in enumerate(_run_units(units, depth=4)):
        rr, j = divmod(n, 2)
        o_ref[rr * GRID_W:(rr + 1) * GRID_W, j * LANES:(j + 1) * LANES] = o.astype(BF16)


def _lat_na(l, qa, ka, va, cache_k, cache_v, na_bias):
    cspec = pl.BlockSpec((None, None, PAST_LEN, 256), lambda b, i: (b, l, 0, 0))
    bspec = _resident((None, 4, NA_ROWS, GRID_W, NA_KEYS), lambda b, i: (l, 0, 0, 0, 0))
    return _lat_call(_na_kernel, l, "lat_na",
                     [_lat_qspec(512), _lat_kvspec(256), _lat_kvspec(256), cspec, cspec, bspec],
                     (qa, ka, va, cache_k, cache_v, na_bias))


def _na_bias_kernel(rpb_ref, out_ref, t_ref):
    l = pl.program_id(0)
    n_dr, n_dc = 2 * NA_ROWS - 1, 2 * NA_COLS - 1
    lane = _lane_iota((GRID_W, LANES))
    lane8 = _lane_iota((HALO, LANES))
    sub8 = lax.broadcasted_iota(jnp.int32, (HALO, LANES), 0)
    clamp = lambda d: jnp.clip(d, 1 - NA_COLS, NA_COLS - 1) + (NA_COLS - 1)
    idx_lo = clamp(jnp.where(lane8 < GRID_W, lane8, lane8 - LANES))
    idx_hi = clamp(lane8 - GRID_W)
    kc = lane8 & (GRID_W - 1)
    for hd in range(4):
        def fill(a, carry):
            base = ((l * 4 + hd) * n_dr + a) * n_dc
            g_lo = jnp.zeros((HALO, LANES), F32)
            g_hi = jnp.zeros((HALO, LANES), F32)
            for b in range(n_dc):
                g_lo = jnp.where(idx_lo == b, rpb_ref[base + b], g_lo)
                g_hi = jnp.where(idx_hi == b, rpb_ref[base + b], g_hi)
            for v in range(GRID_W // HALO):
                t = jnp.where(lane8 < GRID_W,
                              pltpu.roll(g_lo, HALO * v, 1, stride=1, stride_axis=0),
                              pltpu.roll(g_hi, HALO * v, 1, stride=1, stride_axis=0))
                c0 = jnp.clip(sub8 + (HALO * v - NA_COLS // 2), 0, GRID_W - NA_COLS)
                ok = (kc >= c0) & (kc < c0 + NA_COLS)
                t_ref[a, HALO * v:HALO * (v + 1), :] = jnp.where(ok, t * LOG2E, NEG_INF)
            return carry
        lax.fori_loop(0, n_dr, fill, 0)
        for d in range(NA_ROWS):
            for m in range(NA_ROWS // 2):
                lo = t_ref[2 * m - d + NA_ROWS - 1]
                hi = t_ref[2 * m + 1 - d + NA_ROWS - 1]
                out_ref[hd, d, :, m * LANES:(m + 1) * LANES] = jnp.where(lane < GRID_W, lo, hi)


def _na_bias(rpb_flat):
    return pl.pallas_call(
        _na_bias_kernel,
        grid=(DEPTH,),
        in_specs=[pl.BlockSpec(memory_space=pltpu.SMEM)],
        out_specs=pl.BlockSpec((None, 4, NA_ROWS, GRID_W, NA_KEYS), lambda l: (l, 0, 0, 0, 0)),
        out_shape=jax.ShapeDtypeStruct((DEPTH, 4, NA_ROWS, GRID_W, NA_KEYS), F32),
        scratch_shapes=[pltpu.VMEM((2 * NA_ROWS - 1, GRID_W, LANES), F32)],
        compiler_params=_cparams(("parallel",)),
        name="na_bias",
    )(rpb_flat)


def _mla_ctx_kernel(ckv_ref, kpe_ref, wukv_ref, g_ref, bd_ref, kd_o, vd_o):
    kv = _dot(ckv_ref[...].astype(BF16), wukv_ref[...])
    kpe = kpe_ref[...]
    for j in range(2):
        kf = jnp.concatenate([kv[:, (2 * j) * LANES:(2 * j + 1) * LANES] + kpe,
                              kv[:, (2 * j + 1) * LANES:(2 * j + 2) * LANES] + kpe], axis=1)
        kd_o[:, 2 * j * LANES:(2 * j + 2) * LANES] = _group_norm(
            kf, bd_ref[1], 1.0 / MLA_QK, g_ref[7:8, :]).astype(BF16)
    vd_o[...] = kv[:, 4 * LANES:].astype(BF16)


def _mla_ctx(cache_ckv, cache_kpe_p, w_ukv_p, gains, bd):
    return pl.pallas_call(
        _mla_ctx_kernel,
        grid=(DEPTH, DEC_BATCH),
        in_specs=[
            pl.BlockSpec((None, None, PAST_LEN, LANES), lambda l, b: (b, l, 0, 0)),
            pl.BlockSpec((None, None, PAST_LEN, LANES), lambda l, b: (b, l, 0, 0)),
            pl.BlockSpec((None, MLA_KV_LORA, 768), lambda l, b: (l, 0, 0)),
            pl.BlockSpec((None, 16, 256), lambda l, b: (l, 0, 0)),
            pl.BlockSpec((2, 256, 256), lambda l, b: (0, 0, 0)),
        ],
        out_specs=[pl.BlockSpec((None, None, PAST_LEN, 512), lambda l, b: (l, b, 0, 0)),
                   pl.BlockSpec((None, None, PAST_LEN, 256), lambda l, b: (l, b, 0, 0))],
        out_shape=[jax.ShapeDtypeStruct((DEPTH, DEC_BATCH, PAST_LEN, 512), BF16),
                   jax.ShapeDtypeStruct((DEPTH, DEC_BATCH, PAST_LEN, 256), BF16)],
        compiler_params=_cparams(("parallel", "parallel")),
        name="mla_ctx",
    )(cache_ckv, cache_kpe_p, w_ukv_p, gains, bd)


def _merge_kernel(xc_ref, xl_ref, oc_ref, oa_ref, ob_ref, occ_ref, od_ref, mod_ref, n1_ref, wt_hbm, wb_hbm, wo_hbm,
                  x1_ref, wg_ref, wb_ref, wo_ref, stage, sems, *, layer):
    @pl.when(pl.program_id(0) == 0)
    def _():
        _load_transposed_bf16(wt_hbm, layer, GATE_BLOCK_STARTS, lambda b, w: w, wg_ref, stage, sems)
        _load_rows_bf16(wb_hbm, layer, wb_ref, stage, sems)
        _load_rows_bf16(wo_hbm, layer, wo_ref, stage, sems)

    is_ctx = pl.program_id(0) < IN_CTX_STEPS
    mod = mod_ref[...]
    lat_refs = (oa_ref, ob_ref, occ_ref, od_ref)

    def half_units(half):
        rows = slice(half * TM, (half + 1) * TM)
        st = {"merged": None}
        st["x"] = jnp.where(is_ctx, xc_ref[rows, :], xl_ref[rows, :])
        st["hb"] = _rms_mod(st["x"], n1_ref[...], mod[:, D_MODEL:2 * D_MODEL], mod[:, 0:D_MODEL]).astype(BF16)

        def branch(n):
            def issue():
                o = jnp.where(is_ctx, oc_ref[rows, n * 256:(n + 1) * 256], lat_refs[n][rows, :])
                return (_dot(st["hb"], wg_ref[:, n * D_MODEL:(n + 1) * D_MODEL]),
                        _dot(o, wb_ref[n * 256:(n + 1) * 256, :]))

            def finish(res):
                gate, br = res
                t = (1.0 / (1.0 + jnp.exp(-gate))) * br
                st["merged"] = t if st["merged"] is None else st["merged"] + t
            return issue, finish

        def out_issue():
            return _dot(st["merged"].astype(BF16), wo_ref[...])

        def out_finish(out):
            x1_ref[rows, :] = st["x"] + mod[:, 2 * D_MODEL:3 * D_MODEL] * out

        return [branch(n) for n in range(4)], (out_issue, out_finish)

    (a_br, a_out), (b_br, b_out) = half_units(0), half_units(1)
    idle = (lambda: None, lambda _: None)
    _run_units(a_br + b_br[:2] + [a_out] + b_br[2:] + [idle, b_out], depth=2)


def _merge(l, x_ctx, x_lat, o_ctx, o_lat, mods, norm1, w_in_t, w_branch, w_out):
    per_batch = DEC_SEQ // TM_IN
    ctx_blk = lambda i: jnp.minimum(i, IN_CTX_STEPS - 1)
    lat_blk = lambda i: jnp.maximum(i - IN_CTX_STEPS, 0)
    tok = pl.BlockSpec((TM_IN, D_MODEL), lambda i: (i, 0))
    ctx_tok = pl.BlockSpec((TM_IN, D_MODEL), lambda i: (ctx_blk(i), 0))
    lat_tok = lambda w: pl.BlockSpec((TM_IN, w), lambda i: (lat_blk(i), 0))
    return pl.pallas_call(
        functools.partial(_merge_kernel, layer=l),
        grid=(IN_STEPS,),
        in_specs=[
            ctx_tok, lat_tok(D_MODEL), ctx_tok, lat_tok(256), lat_tok(256), lat_tok(256), lat_tok(256),
            pl.BlockSpec((None, None, 1, 6 * D_MODEL),
                         lambda i: (l, jnp.where(i < IN_CTX_STEPS, 0, 1 + lat_blk(i) // per_batch), 0, 0)),
            _resident((None, 1, D_MODEL), lambda i: (l, 0, 0)),
            pl.BlockSpec(memory_space=pl.ANY),
            pl.BlockSpec(memory_space=pl.ANY),
            pl.BlockSpec(memory_space=pl.ANY),
        ],
        out_specs=tok,
        out_shape=jax.ShapeDtypeStruct((N_TOK, D_MODEL), F32),
        scratch_shapes=[pltpu.VMEM((D_MODEL, N_GATE), BF16), pltpu.VMEM((D_MODEL, D_MODEL), BF16),
                        pltpu.VMEM((D_MODEL, D_MODEL), BF16), pltpu.VMEM((2, WT_ROWS, D_MODEL), F32),
                        pltpu.SemaphoreType.DMA((2,))],
        compiler_params=_cparams(("arbitrary",)),
        name=f"merge_l{l}",
    )(x_ctx, x_lat, o_ctx, *o_lat, mods, norm1, w_in_t, w_branch, w_out)


HALO = 8


def _sublane_transpose(tiles):
    sub = lax.broadcasted_iota(jnp.int32, tiles[0].shape, 0)
    cur = list(tiles)
    for d in (4, 2, 1):
        keep = (sub & d) == 0
        nxt = list(cur)
        for a in range(HALO):
            if a & d:
                continue
            nxt[a] = jnp.where(keep, cur[a], pltpu.roll(cur[a + d], d, 0))
            nxt[a + d] = jnp.where(keep, pltpu.roll(cur[a], HALO - d, 0), cur[a + d])
        cur = nxt
    return cur


ROW_TILES = TM // HALO
PERM_GROUPS = ROW_TILES // HALO


def _permute_rows(x, to_permuted):
    tile = lambda j: x[j * HALO:(j + 1) * HALO, :]
    out = [None] * ROW_TILES
    for m in range(PERM_GROUPS):
        nat = [m + PERM_GROUPS * s for s in range(HALO)]
        perm = [HALO * m + i for i in range(HALO)]
        src, dst = (nat, perm) if to_permuted else (perm, nat)
        for j, t in zip(dst, _sublane_transpose([tile(j) for j in src])):
            out[j] = t
    return jnp.concatenate(out, axis=0)


def _ffn_tile(x, x_prev, x_next, has_prev, has_next, mod, n2, wup_ref, cw_ref, cb_ref, wdn_ref, u_ref):
    sh2, sc2 = mod[:, 3 * D_MODEL:4 * D_MODEL], mod[:, 4 * D_MODEL:5 * D_MODEL]
    h_prev = jnp.where(has_prev, _rms_mod(x_prev, n2, sc2, sh2), 0.0)
    h_next = jnp.where(has_next, _rms_mod(x_next, n2, sc2, sh2), 0.0)
    hp = _permute_rows(_rms_mod(x, n2, sc2, sh2), True)
    lhs = jnp.concatenate([hp, h_prev, h_next], axis=0).astype(BF16)
    sub = lax.broadcasted_iota(jnp.int32, (HALO, FF_CHUNK), 0)

    def cols(c):
        return (slice(c * FF_CHUNK, (c + 1) * FF_CHUNK), slice(D_FF + c * FF_CHUNK, D_FF + (c + 1) * FF_CHUNK))

    def up(c):
        for half, cs in enumerate(cols(c)):
            lanes = slice(half * FF_CHUNK, (half + 1) * FF_CHUNK)
            u = _dot(lhs, wup_ref[:, cs])
            u_ref[c, HALO:HALO + TM, lanes] = u[:TM]
            prev_row, next_row = u[TM + HALO - 1:TM + HALO], u[TM + HALO:TM + HALO + 1]
            u_ref[c, 0:HALO, lanes] = jnp.where(sub == 0, prev_row, pltpu.roll(u[TM - HALO:TM], 1, 0))
            u_ref[c, HALO + TM:, lanes] = jnp.where(sub == HALO - 1, next_row, pltpu.roll(u[0:HALO], HALO - 1, 0))

    def both(ref, c):
        va, vg = cols(c)
        return jnp.concatenate([ref[:, va], ref[:, vg]], axis=1)

    state = {"acc": None}

    def chunk(c, between):
        cw = both(cw_ref, c)
        t = (both(cb_ref, c) + u_ref[c, 0:TM, :] * cw[0:1] + u_ref[c, HALO:HALO + TM, :] * cw[1:2]
             + u_ref[c, 2 * HALO:2 * HALO + TM, :] * cw[2:3])
        a, g = t[:, :FF_CHUNK], t[:, FF_CHUNK:]
        act = ((g * (1.0 / (1.0 + jnp.exp(-g)))) * a).astype(BF16)
        between()
        d = _dot(act, wdn_ref[c * FF_CHUNK:(c + 1) * FF_CHUNK, :])
        state["acc"] = d if state["acc"] is None else state["acc"] + d

    def finish():
        return x + mod[:, 5 * D_MODEL:6 * D_MODEL] * _permute_rows(state["acc"], False)

    return up, chunk, finish


W_BLOCKS = N_FF_CHUNKS
WUP_BLOCK_COLS = 2 * D_FF // W_BLOCKS
WDN_BLOCK_ROWS = D_FF // W_BLOCKS


def _load_bf16_weights(layer, wup_hbm, wdn_hbm, wup_ref, wdn_ref, up_stage, dn_stage, sems):
    def up_copy(b):
        return pltpu.make_async_copy(wup_hbm.at[layer, :, pl.ds(b * WUP_BLOCK_COLS, WUP_BLOCK_COLS)],
                                     up_stage.at[b % 2], sems.at[0, b % 2])

    def dn_copy(b):
        return pltpu.make_async_copy(wdn_hbm.at[layer, pl.ds(b * WDN_BLOCK_ROWS, WDN_BLOCK_ROWS), :],
                                     dn_stage.at[b % 2], sems.at[1, b % 2])

    up_copy(0).start()
    dn_copy(0).start()
    for b in range(W_BLOCKS):
        if b + 1 < W_BLOCKS:
            up_copy(b + 1).start()
            dn_copy(b + 1).start()
        up_copy(b).wait()
        wup_ref[:, b * WUP_BLOCK_COLS:(b + 1) * WUP_BLOCK_COLS] = up_stage[b % 2].astype(BF16)
        dn_copy(b).wait()
        wdn_ref[b * WDN_BLOCK_ROWS:(b + 1) * WDN_BLOCK_ROWS, :] = dn_stage[b % 2].astype(BF16)


def _ffn_kernel(x_ref, xp_ref, xn_ref, mod_ref, n2_ref, wup_hbm, cw_ref, cb_ref, wdn_hbm, yc_ref, yl_ref,
                u_ref, wup_ref, wdn_ref, up_stage, dn_stage, sems, *, layer):
    i = pl.program_id(0)

    @pl.when(i == 0)
    def _():
        _load_bf16_weights(layer, wup_hbm, wdn_hbm, wup_ref, wdn_ref, up_stage, dn_stage, sems)

    latent = i >= IN_CTX_STEPS
    first = (i - IN_CTX_STEPS) % (DEC_SEQ // TM_IN) == 0
    last = (i - IN_CTX_STEPS) % (DEC_SEQ // TM_IN) == DEC_SEQ // TM_IN - 1
    mod = mod_ref[...]
    n2 = n2_ref[...]
    make = [
        lambda: _ffn_tile(x_ref[0:TM, :], xp_ref[...], x_ref[TM:TM + HALO, :], latent & ~first, latent, mod, n2,
                          wup_ref, cw_ref, cb_ref, wdn_ref, u_ref.at[0]),
        lambda: _ffn_tile(x_ref[TM:, :], x_ref[TM - HALO:TM, :], xn_ref[...], latent, latent & ~last, mod, n2,
                          wup_ref, cw_ref, cb_ref, wdn_ref, u_ref.at[1]),
    ]
    tiles = {}

    def tile(t):
        if t not in tiles:
            tiles[t] = make[t]()
        return tiles[t]

    tasks = [(t, c) for t in range(len(make)) for c in range(N_FF_CHUNKS)]
    early = N_FF_CHUNKS // 2 + 1
    ups = tasks[:early] + [tasks[N_FF_CHUNKS]] + tasks[early:N_FF_CHUNKS] + tasks[N_FF_CHUNKS + 1:]
    depth = 3
    for t, c in ups[:depth]:
        tile(t)[0](c)
    ys = []
    for n, (t, c) in enumerate(tasks):
        ahead = ups[n + depth] if n + depth < len(ups) else None
        tile(t)[1](c, (lambda a=ahead: tile(a[0])[0](a[1])) if ahead else (lambda: None))
        if c == N_FF_CHUNKS - 1:
            ys.append(tile(t)[2]())
    y = jnp.concatenate(ys, axis=0)

    @pl.when(i < IN_CTX_STEPS)
    def _():
        yc_ref[...] = y

    @pl.when(latent)
    def _():
        yl_ref[...] = y


def _ffn(l, x1, mods, norm2, w_up, conv_w, conv_b, w_down):
    per = TM_IN // HALO
    last = N_TOK // HALO - 1
    per_batch = DEC_SEQ // TM_IN
    lat_blk = lambda i: jnp.maximum(i - IN_CTX_STEPS, 0)
    return pl.pallas_call(
        functools.partial(_ffn_kernel, layer=l),
        grid=(IN_STEPS,),
        in_specs=[
            pl.BlockSpec((TM_IN, D_MODEL), lambda i: (i, 0)),
            pl.BlockSpec((HALO, D_MODEL), lambda i: (jnp.maximum(i * per - 1, 0), 0)),
            pl.BlockSpec((HALO, D_MODEL), lambda i: (jnp.minimum((i + 1) * per, last), 0)),
            pl.BlockSpec((None, None, 1, 6 * D_MODEL),
                         lambda i: (l, jnp.where(i < IN_CTX_STEPS, 0, 1 + lat_blk(i) // per_batch), 0, 0)),
            _resident((None, 1, D_MODEL), lambda i: (l, 0, 0)),
            pl.BlockSpec(memory_space=pl.ANY),
            _resident((None, 3, 2 * D_FF), lambda i: (l, 0, 0)),
            _resident((None, 1, 2 * D_FF), lambda i: (l, 0, 0)),
            pl.BlockSpec(memory_space=pl.ANY),
        ],
        out_specs=[pl.BlockSpec((TM_IN, D_MODEL), lambda i: (jnp.minimum(i, IN_CTX_STEPS - 1), 0)),
                   pl.BlockSpec((TM_IN, D_MODEL), lambda i: (lat_blk(i), 0))],
        out_shape=[jax.ShapeDtypeStruct((N_CTX, D_MODEL), F32), jax.ShapeDtypeStruct((N_LAT, D_MODEL), F32)],
        scratch_shapes=[
            pltpu.VMEM((TM_IN // TM, N_FF_CHUNKS, TM + 2 * HALO, 2 * FF_CHUNK), F32),
            pltpu.VMEM((D_MODEL, 2 * D_FF), BF16),
            pltpu.VMEM((D_FF, D_MODEL), BF16),
            pltpu.VMEM((2, D_MODEL, WUP_BLOCK_COLS), F32),
            pltpu.VMEM((2, WDN_BLOCK_ROWS, D_MODEL), F32),
            pltpu.SemaphoreType.DMA((2, 2)),
        ],
        compiler_params=_cparams(("arbitrary",)),
        name=f"ffn_l{l}",
    )(x1, x1, x1, mods, norm2, w_up, conv_w, conv_b, w_down)


WT_ROWS = 512
_SRC_CKV = C_QD + 4 * MLA_QK
QKV_BLOCK_STARTS = (0, WT_ROWS, 2 * WT_ROWS, C_KC, C_QD + 2 * MLA_QK)
GATE_BLOCK_STARTS = tuple(IN_QKV + WT_ROWS * b for b in range(N_GATE // WT_ROWS))


def _qkv_block_rows(b, w):
    zeros = lambda n: jnp.zeros((n, D_MODEL), F32)
    pad = zeros(LANES - MLA_QK)

    def two_heads(r0):
        return [w[r0:r0 + MLA_QK], pad, w[r0 + MLA_QK:r0 + 2 * MLA_QK], pad]

    if QKV_BLOCK_STARTS[b] == C_KC:
        n_kv = C_QD - C_KC
        return jnp.concatenate([w[0:n_kv]] + two_heads(n_kv), axis=0)
    if QKV_BLOCK_STARTS[b] == C_QD + 2 * MLA_QK:
        r_ckv = _SRC_CKV - QKV_BLOCK_STARTS[b]
        r_kpe = r_ckv + MLA_KV_LORA
        return jnp.concatenate(two_heads(0) + [w[r_ckv:r_kpe], zeros(MLA_NOPE), w[r_kpe:r_kpe + MLA_ROPE], pad],
                               axis=0)
    return w


def _load_rows_bf16(w_hbm, layer, dst_ref, stage, sems):
    n_blocks = dst_ref.shape[0] // WT_ROWS

    def copy(b):
        return pltpu.make_async_copy(w_hbm.at[layer, pl.ds(b * WT_ROWS, WT_ROWS), :], stage.at[b % 2], sems.at[b % 2])

    copy(0).start()
    for b in range(n_blocks):
        if b + 1 < n_blocks:
            copy(b + 1).start()
        copy(b).wait()
        dst_ref[b * WT_ROWS:(b + 1) * WT_ROWS, :] = stage[b % 2].astype(BF16)


def _load_transposed_bf16(wt_hbm, layer, starts, select, dst_ref, stage, sems):
    def copy(b):
        return pltpu.make_async_copy(wt_hbm.at[layer, pl.ds(starts[b], WT_ROWS), :], stage.at[b % 2], sems.at[b % 2])

    copy(0).start()
    for b in range(len(starts)):
        if b + 1 < len(starts):
            copy(b + 1).start()
        copy(b).wait()
        dst_ref[:, b * WT_ROWS:(b + 1) * WT_ROWS] = select(b, stage[b % 2]).T.astype(BF16)


def _rope_tables():
    t = np.arange(DEC_SEQ)

    def angles(dim):
        half = dim // 2
        inv = ROPE_THETA ** (-np.arange(0, half, 2, dtype=np.float32) / half)
        row = (t // GRID_W).astype(np.float32)[:, None] * inv[None, :]
        col = (t % GRID_W).astype(np.float32)[:, None] * inv[None, :]
        return np.concatenate([row, row, col, col], axis=-1).astype(np.float32)

    def signed(ang, quarter):
        cos, sin = np.cos(ang), np.sin(ang)
        first = (np.arange(ang.shape[1]) % (2 * quarter)) < quarter
        return cos, np.where(first, -sin, 0.0), np.where(first, 0.0, sin)

    c64, lo64, hi64 = signed(angles(HEAD_DIM), 16)
    c64, lo64, hi64 = (np.tile(a, (1, 2)) for a in (c64, lo64, hi64))
    cm, lom, him = signed(angles(MLA_ROPE), 8)

    def mla_pad(a, fill):
        out = np.full((DEC_SEQ, LANES), fill, np.float32)
        out[:, MLA_NOPE:MLA_QK] = a
        return out

    tabs = [c64, lo64, hi64, mla_pad(cm, 1.0), mla_pad(lom, 0.0), mla_pad(him, 0.0)]
    ident = [np.ones, np.zeros, np.zeros, np.ones, np.zeros, np.zeros]
    full = [np.concatenate([f((TM_IN, LANES), np.float32), a.astype(np.float32)], axis=0) for f, a in zip(ident, tabs)]
    return jnp.asarray(np.stack(full, axis=0))


def _block_diag_ones():
    idx = np.arange(256)
    mats = [(idx[:, None] // w == idx[None, :] // w).astype(np.float32) for w in (HEAD_DIM, LANES)]
    return jnp.asarray(np.stack(mats, axis=0), dtype=BF16)


def _prep_small(w_ukv, qn_a, kn_a, qn_b, kn_b, qn_c, kn_c, qn_d, kn_d, kvn_d):
    ukv = w_ukv.reshape(DEPTH, MLA_KV_LORA, 4, 2, HEAD_DIM)
    k_part = jnp.pad(ukv[:, :, :, 0], ((0, 0), (0, 0), (0, 0), (0, HEAD_DIM))).reshape(DEPTH, MLA_KV_LORA, 512)
    v_part = ukv[:, :, :, 1].reshape(DEPTH, MLA_KV_LORA, 256)
    w_ukv_p = jnp.concatenate([k_part, v_part], axis=-1).astype(BF16)

    def row(g, reps, width=256):
        t = jnp.tile(g, (1, reps))
        return jnp.pad(t, ((0, 0), (0, width - t.shape[1])))

    pad96 = lambda g: jnp.pad(g, ((0, 0), (0, LANES - MLA_QK)))
    rows = [row(qn_a, 4), row(kn_a, 4), row(qn_b, 4), row(kn_b, 2), row(qn_c, 4), row(kn_c, 2),
            row(pad96(qn_d), 2), row(pad96(kn_d), 2), row(kvn_d, 1)]
    gains = jnp.stack(rows + [jnp.zeros_like(rows[0])] * (16 - len(rows)), axis=1)
    return w_ukv_p, gains


def kernel(x_prompt, x_sample, cache_nat_k, cache_nat_v, cache_gqa_k, cache_gqa_v, cache_win_k, cache_win_v,
           cache_mla_ckv, cache_mla_kpe, c, c_ctx, w_ada, b_ada, norm1, norm2, w_in, qn_a, kn_a, rpb_a,
           qn_b, kn_b, qn_c, kn_c, sink_c, qn_d, kn_d, kvn_d, w_ukv, w_branch, w_out, w_up, conv_w, conv_b,
           w_down):
    w_ukv_p, gains = _prep_small(w_ukv, qn_a, kn_a, qn_b, kn_b, qn_c, kn_c, qn_d, kn_d, kvn_d)
    w_in_t = jnp.swapaxes(w_in, 1, 2)
    w_branch_r = w_branch.reshape(DEPTH, 4 * 256, D_MODEL)
    conv_b_r = conv_b.reshape(DEPTH, 1, 2 * D_FF)
    rope_tab = _rope_tables()
    bd = _block_diag_ones()
    norm1_r = norm1.reshape(DEPTH, 1, D_MODEL)
    norm2_r = norm2.reshape(DEPTH, 1, D_MODEL)

    cond8 = jnp.zeros((8, D_MODEL), F32).at[0].set(c_ctx).at[1:1 + DEC_BATCH].set(c)
    mods = _adaln(cond8, w_ada, b_ada).reshape(DEPTH, 8, 1, 6 * D_MODEL)

    rpb_flat = jnp.pad(rpb_a.reshape(-1), (0, 8192 - rpb_a.size))
    na_bias = _na_bias(rpb_flat)
    kpe_p = jnp.pad(cache_mla_kpe, ((0, 0), (0, 0), (0, 0), (MLA_NOPE, LANES - MLA_QK)))
    kd_ctx, vd_ctx = _mla_ctx(cache_mla_ckv, kpe_p, w_ukv_p, gains, bd)
    c_nat_k = cache_nat_k.reshape(DEC_BATCH, DEPTH, PAST_LEN, 256)
    c_nat_v = cache_nat_v.reshape(DEC_BATCH, DEPTH, PAST_LEN, 256)
    c_gqa_k = cache_gqa_k.reshape(DEC_BATCH, DEPTH, PAST_LEN, LANES)
    c_gqa_v = cache_gqa_v.reshape(DEC_BATCH, DEPTH, PAST_LEN, LANES)
    c_win_k = cache_win_k.reshape(DEC_BATCH, DEPTH, PAST_LEN, LANES)
    c_win_v = cache_win_v.reshape(DEC_BATCH, DEPTH, PAST_LEN, LANES)

    x_ctx, x_lat = x_prompt.reshape(N_CTX, D_MODEL), x_sample.reshape(N_LAT, D_MODEL)
    caches = _zero_caches()
    for l in range(DEPTH):
        qkv, caches = _inproj(l, x_ctx, x_lat, mods, norm1_r, w_in_t, w_ukv_p, gains, bd, rope_tab, caches)
        qa, ka, va, qb, kb, vb, qc, kc, vc, qd, kd, vd = qkv
        sink_l = sink_c[l]
        o_ctx = _ctx_attention(l, sink_l, qkv)
        o_lat = (_lat_na(l, qa, ka, va, c_nat_k, c_nat_v, na_bias),
                 _lat_gqa(l, qb, kb, vb, c_gqa_k, c_gqa_v),
                 _lat_window(l, sink_l, qc, kc, vc, c_win_k, c_win_v),
                 _lat_mla(l, qd, kd, vd, kd_ctx, vd_ctx))
        x1 = _merge(l, x_ctx, x_lat, o_ctx, o_lat, mods, norm1_r, w_in_t, w_branch_r, w_out)
        x_ctx, x_lat = _ffn(l, x1, mods, norm2_r, w_up, conv_w, conv_b_r, w_down)
    y_prompt = x_ctx.reshape(BATCH, SEQ, D_MODEL)
    y_sample = x_lat.reshape(DEC_BATCH, DEC_SEQ, D_MODEL)
    heads = (4, 4, 2, 2, 2, 2)
    outs = [a.reshape(BATCH, DEPTH, h, HEAD_DIM, SEQ).transpose(0, 1, 4, 2, 3) for a, h in zip(caches[:6], heads)]
    return (y_prompt, y_sample, *outs, jnp.swapaxes(caches[6], 2, 3), jnp.swapaxes(caches[7], 2, 3))
```

```python
import functools

import jax
import jax.numpy as jnp
import numpy as np
from jax import lax
from jax.experimental import pallas as pl
from jax.experimental.pallas import tpu as pltpu

F32 = jnp.float32
BF16 = jnp.bfloat16

D_MODEL = 1024
BATCH = 16
SEQ = 256
DEPTH = 4
DEC_BATCH = 2
DEC_SEQ = 2048
PAST_LEN = 256
GRID_W = 64
HEAD_DIM = 64
NA_ROWS = 8
NA_COLS = 16
WINDOW = 128
MLA_NOPE = 64
MLA_ROPE = 32
MLA_QK = MLA_NOPE + MLA_ROPE
MLA_KV_LORA = 128
D_FF = 2816
ROPE_THETA = 10000.0
EPS = 1e-6
NEG_INF = -1e30

LANES = 128
TM = 256
N_CTX = BATCH * SEQ
N_LAT = DEC_BATCH * DEC_SEQ
N_TOK = N_CTX + N_LAT
GRID_ROWS = DEC_SEQ // GRID_W
FF_CHUNK = 256
N_FF_CHUNKS = D_FF // FF_CHUNK
VMEM_LIMIT = 56 * 1024 * 1024

C_QA, C_KA, C_VA = 0, 256, 512
C_QB, C_KB, C_VB = 768, 1024, 1152
C_QC, C_KC, C_VC = 1280, 1536, 1664
C_QD, C_CKV, C_KPE, C_END = 1792, 2304, 2432, 2560
N_GATE = 4 * D_MODEL
IN_QKV = 2336
LOG2E = float(np.log2(np.e))
Q_SCALE = HEAD_DIM ** -0.5 * LOG2E
MLA_Q_SCALE = MLA_QK ** -0.5 * LOG2E


def _cparams(sem):
    return pltpu.CompilerParams(dimension_semantics=sem, vmem_limit_bytes=VMEM_LIMIT)


def _resident(shape, index_map):
    return pl.BlockSpec(shape, index_map, pipeline_mode=pl.Buffered(1))


def _dot(a, b):
    return jnp.dot(a, b, preferred_element_type=F32)


def _dot_t(a, b):
    return lax.dot_general(a, b, (((1,), (1,)), ((), ())), preferred_element_type=F32)


def _rms_mod(x, gain, scale, shift):
    ms = jnp.mean(x * x, axis=-1, keepdims=True)
    return (x * lax.rsqrt(ms + EPS)) * gain * (1.0 + scale) + shift


def _group_norm(t, ones_bd, inv_n, gain):
    ss = _dot((t * t).astype(BF16), ones_bd)
    return t * lax.rsqrt(ss * inv_n + EPS) * gain


def _rope(t, cos, sin_lo, sin_hi, half):
    return (t * cos + pltpu.roll(t, LANES - half, 1) * sin_lo + pltpu.roll(t, half, 1) * sin_hi)


def _lane_iota(shape):
    return lax.broadcasted_iota(jnp.int32, shape, len(shape) - 1)


def _low_half(shape):
    return _lane_iota(shape) < HEAD_DIM


def _run_units(units, depth=2):
    n = len(units)
    pending = {}
    outs = []
    for k in range(min(depth, n)):
        pending[k] = units[k][0]()
    for k in range(n):
        outs.append(units[k][1](pending.pop(k)))
        if k + depth < n:
            pending[k + depth] = units[k + depth][0]()
    return outs


N_COND = 1 + DEC_BATCH


def _adaln_kernel(cond_ref, w_ref, b_ref, out_ref):
    cnd = cond_ref[...]
    s_t = (cnd * (1.0 / (1.0 + jnp.exp(-cnd)))).T
    w = w_ref[...]
    out_ref[...] = jnp.zeros(out_ref.shape, F32)
    for r in range(N_COND):
        out_ref[r:r + 1, :] = jnp.sum(s_t[:, r:r + 1] * w, axis=0, keepdims=True) + b_ref[...]


def _adaln(cond8, w_ada, b_ada):
    n_col = 6 * D_MODEL
    tn = 2048
    return pl.pallas_call(
        _adaln_kernel,
        grid=(DEPTH, n_col // tn),
        in_specs=[
            pl.BlockSpec((8, D_MODEL), lambda l, n: (0, 0)),
            pl.BlockSpec((None, D_MODEL, tn), lambda l, n: (l, 0, n)),
            pl.BlockSpec((None, 1, tn), lambda l, n: (l, 0, n)),
        ],
        out_specs=pl.BlockSpec((None, 8, tn), lambda l, n: (l, 0, n)),
        out_shape=jax.ShapeDtypeStruct((DEPTH, 8, n_col), F32),
        compiler_params=_cparams(("parallel", "parallel")),
        name="adaln",
    )(cond8, w_ada, b_ada.reshape(DEPTH, 1, n_col))


_QKV_WIDTHS = (512, 256, 256, 512, 128, 128, 512, 128, 128, 512, 512, 256)
_CACHE_WIDTHS = (256, 256, 128, 128, 128, 128, MLA_KV_LORA, MLA_ROPE)
_CACHE_BLOCKS = tuple((w, SEQ) for w in _CACHE_WIDTHS)


def _zeros_kernel(*outs):
    for o in outs:
        o[...] = jnp.zeros(o.shape, F32)


def _zero_caches():
    return pl.pallas_call(
        _zeros_kernel,
        grid=(BATCH,),
        in_specs=[],
        out_specs=[pl.BlockSpec((None, DEPTH) + blk, lambda b: (b, 0, 0, 0)) for blk in _CACHE_BLOCKS],
        out_shape=[jax.ShapeDtypeStruct((BATCH, DEPTH) + blk, F32) for blk in _CACHE_BLOCKS],
        compiler_params=_cparams(("parallel",)),
        name="zero_caches",
    )()


TM_IN = 2 * TM
IN_STEPS = N_TOK // TM_IN
IN_CTX_STEPS = N_CTX // TM_IN


def _inproj_kernel(xc_ref, xl_ref, mod_ref, n1_ref, wt_hbm, wukv_ref, g_ref, bd_ref, rope_ref, *rest, layer):
    n_out = len(_QKV_WIDTHS) + len(_CACHE_WIDTHS)
    outs = rest[len(_CACHE_WIDTHS):len(_CACHE_WIDTHS) + n_out]
    w_ref, stage, sems = rest[len(_CACHE_WIDTHS) + n_out:]

    @pl.when(pl.program_id(0) == 0)
    def _():
        _load_transposed_bf16(wt_hbm, layer, QKV_BLOCK_STARTS, _qkv_block_rows, w_ref, stage, sems)

    is_ctx = pl.program_id(0) >= IN_STEPS - IN_CTX_STEPS
    units = []
    for half in range(TM_IN // TM):
        rows = slice(half * TM, (half + 1) * TM)
        views = [o.at[rows] for o in outs[:len(_QKV_WIDTHS)]] + [o.at[half] for o in outs[len(_QKV_WIDTHS):]]
        units += _inproj_half(is_ctx, xc_ref.at[rows], xl_ref.at[rows], mod_ref, n1_ref, w_ref, wukv_ref, g_ref,
                              bd_ref, rope_ref.at[:, rows, :], views)
    _run_units(units, depth=2)


def _inproj_half(is_ctx, xc_ref, xl_ref, mod_ref, n1_ref, w_ref, wukv_ref, g_ref, bd_ref, rope_ref, outs):
    qa_o, ka_o, va_o, qb_o, kb_o, vb_o, qc_o, kc_o, vc_o, qd_o, kd_o, vd_o = outs[:len(_QKV_WIDTHS)]
    natk_o, natv_o, gqak_o, gqav_o, wink_o, winv_o, ckv_o, kpe_o = outs[len(_QKV_WIDTHS):]
    x = jnp.where(is_ctx, xc_ref[...], xl_ref[...])
    mod = mod_ref[...]
    hb = _rms_mod(x, n1_ref[...], mod[:, D_MODEL:2 * D_MODEL], mod[:, 0:D_MODEL]).astype(BF16)
    bd64, bd128 = bd_ref[0], bd_ref[1]
    cos64, slo64, shi64 = rope_ref[0], rope_ref[1], rope_ref[2]
    cosm, slom, shim = rope_ref[3], rope_ref[4], rope_ref[5]
    inv64, inv96 = 1.0 / HEAD_DIM, 1.0 / MLA_QK
    low = _low_half((TM, LANES))
    held = {}

    def chunks(t):
        return [t[:, c:c + LANES] for c in range(0, t.shape[1], LANES)]

    def rope64(t):
        parts = [_rope(p, cos64, slo64, shi64, 16) for p in chunks(t)]
        return parts[0] if len(parts) == 1 else jnp.concatenate(parts, axis=1)

    def ropem(t):
        return jnp.concatenate([_rope(p, cosm, slom, shim, 8) for p in chunks(t)], axis=1)

    def store_q(ref, q, grouped):
        q = q * Q_SCALE
        slots = []
        for hd in range(4):
            part, src = q[:, (hd // 2) * LANES:(hd // 2 + 1) * LANES], hd % 2
            dst = hd // 2 if grouped else src
            if src != dst:
                part = pltpu.roll(part, HEAD_DIM, 1)
            slots.append(jnp.where(low if dst == 0 else ~low, part, 0.0))
        ref[...] = jnp.concatenate(slots, axis=1).astype(BF16)

    def post_latent(t):
        ckv = _group_norm(t[:, :LANES], bd128[:LANES, :LANES], 1.0 / MLA_KV_LORA, g_ref[8:9, :LANES])
        kpe = t[:, LANES:]
        ckv_o[...] = ckv.T
        kpe_o[...] = kpe.T[MLA_NOPE:MLA_QK, :]
        held["kpe"] = kpe
        held["kv"] = _dot(ckv.astype(BF16), wukv_ref[...])

    def post_kd(_):
        kv, kpe = held["kv"], held["kpe"]
        for j in range(2):
            kf = jnp.concatenate([kv[:, (2 * j) * LANES:(2 * j + 1) * LANES] + kpe,
                                  kv[:, (2 * j + 1) * LANES:(2 * j + 2) * LANES] + kpe], axis=1)
            kd_o[:, 2 * j * LANES:(2 * j + 2) * LANES] = ropem(
                _group_norm(kf, bd128, inv96, g_ref[7:8, :])).astype(BF16)
        vd_o[...] = kv[:, 4 * LANES:].astype(BF16)

    def post_qa(t):
        store_q(qa_o, _group_norm(t, bd64, inv64, g_ref[0:1, :]), grouped=False)

    def post_ka(t):
        k = _group_norm(t, bd64, inv64, g_ref[1:2, :])
        natk_o[...] = k.T
        ka_o[...] = k.astype(BF16)

    def post_va(t):
        natv_o[...] = t.T
        va_o[...] = t.astype(BF16)

    def post_q_rope(ref, row):
        def post(t):
            store_q(ref, rope64(_group_norm(t, bd64, inv64, g_ref[row:row + 1, :])), grouped=True)
        return post

    def post_kv(k_ref, v_ref, kc_ref, vc_ref, row):
        def post(t):
            k = _group_norm(t[:, :LANES], bd64[:LANES, :LANES], inv64, g_ref[row:row + 1, :LANES])
            kc_ref[...] = k.T
            k_ref[...] = rope64(k).astype(BF16)
            v = t[:, LANES:]
            vc_ref[...] = v.T
            v_ref[...] = v.astype(BF16)
        return post

    def post_qd(j):
        def post(t):
            qd_o[:, 2 * j * LANES:(2 * j + 2) * LANES] = (ropem(
                _group_norm(t, bd128, inv96, g_ref[6:7, :])) * MLA_Q_SCALE).astype(BF16)
        return post

    stages = [
        ((C_CKV, C_END), post_latent),
        ((C_QA, C_KA), post_qa),
        (None, post_kd),
        ((C_QB, C_KB), post_q_rope(qb_o, 2)),
        ((C_QC, C_KC), post_q_rope(qc_o, 4)),
        ((C_QD, C_QD + 2 * LANES), post_qd(0)),
        ((C_QD + 2 * LANES, C_CKV), post_qd(1)),
        ((C_KB, C_QC), post_kv(kb_o, vb_o, gqak_o, gqav_o, 3)),
        ((C_KC, C_QD), post_kv(kc_o, vc_o, wink_o, winv_o, 5)),
        ((C_KA, C_VA), post_ka),
        ((C_VA, C_QB), post_va),
    ]

    def issue(cols):
        return None if cols is None else _dot(hb, w_ref[:, cols[0]:cols[1]])

    return [(functools.partial(issue, cols), post) for cols, post in stages]


def _inproj(l, x_ctx, x_lat, mods, norm1, w_in_t, w_ukv_p, gains, bd, rope_tab, caches):
    n_lat = IN_STEPS - IN_CTX_STEPS
    blk = lambda s: (s + IN_CTX_STEPS) % IN_STEPS
    lat_blk = lambda s: jnp.maximum(blk(s) - IN_CTX_STEPS, 0)
    per_batch = DEC_SEQ // TM_IN
    tok = lambda w: pl.BlockSpec((TM_IN, w), lambda s: (blk(s), 0))
    in_specs = [
        pl.BlockSpec((TM_IN, D_MODEL), lambda s: (jnp.minimum(blk(s), IN_CTX_STEPS - 1), 0)),
        pl.BlockSpec((TM_IN, D_MODEL), lambda s: (lat_blk(s), 0)),
        pl.BlockSpec((None, None, 1, 6 * D_MODEL),
                     lambda s: (l, jnp.where(s < n_lat, 1 + lat_blk(s) // per_batch, 0), 0, 0)),
        _resident((None, 1, D_MODEL), lambda s: (l, 0, 0)),
        pl.BlockSpec(memory_space=pl.ANY),
        _resident((None, MLA_KV_LORA, 768), lambda s: (l, 0, 0)),
        _resident((None, 16, 256), lambda s: (l, 0, 0)),
        _resident((2, 256, 256), lambda s: (0, 0, 0)),
        pl.BlockSpec((6, TM_IN, LANES), lambda s: (0, jnp.where(s < n_lat, 1 + lat_blk(s) % per_batch, 0), 0)),
    ]
    args = [x_ctx, x_lat, mods, norm1, w_in_t, w_ukv_p, gains, bd, rope_tab] + list(caches)
    aliases = {len(in_specs) + n: len(_QKV_WIDTHS) + n for n in range(len(caches))}
    in_specs = in_specs + [pl.BlockSpec(memory_space=pl.ANY)] * len(caches)
    cache_spec = lambda cb: pl.BlockSpec((TM_IN // SEQ, None) + cb, lambda s: (jnp.maximum(s - n_lat, 0), l, 0, 0))
    outs = pl.pallas_call(
        functools.partial(_inproj_kernel, layer=l),
        grid=(IN_STEPS,),
        in_specs=in_specs,
        out_specs=[tok(w) for w in _QKV_WIDTHS] + [cache_spec(blk) for blk in _CACHE_BLOCKS],
        out_shape=[jax.ShapeDtypeStruct((N_TOK, w), BF16) for w in _QKV_WIDTHS]
        + [jax.ShapeDtypeStruct((BATCH, DEPTH) + blk, F32) for blk in _CACHE_BLOCKS],
        scratch_shapes=[pltpu.VMEM((D_MODEL, C_END), BF16), pltpu.VMEM((2, WT_ROWS, D_MODEL), F32),
                        pltpu.SemaphoreType.DMA((2,))],
        input_output_aliases=aliases,
        compiler_params=_cparams(("arbitrary",)),
        name=f"inproj_l{l}",
    )(*args)
    return outs[:len(_QKV_WIDTHS)], outs[len(_QKV_WIDTHS):]


def _scores(q, keys):
    out = []
    for k, bias in keys:
        s = _dot_t(q, k)
        if bias is not None:
            s = s + bias
        out.append(s)
    return out


def _softmax_pv(scores, values, sink=None):
    m = functools.reduce(jnp.maximum, [jnp.max(s, axis=-1, keepdims=True) for s in scores])
    if sink is not None:
        m = jnp.maximum(m, sink)
    denom = None
    out = None
    for s, v in zip(scores, values):
        e = jnp.exp2(s - m)
        d = jnp.sum(e, axis=-1, keepdims=True)
        o = _dot(e.astype(BF16), v)
        denom = d if denom is None else denom + d
        out = o if out is None else out + o
    if sink is not None:
        denom = denom + jnp.exp2(sink - m)
    return out / denom


def _pair_unit(q_ref, rows, j, keys, values, kv_half, sinks=None):
    m_rows = rows.stop - rows.start

    def issue():
        q2 = jnp.concatenate([q_ref[rows, (2 * j) * LANES:(2 * j + 1) * LANES],
                              q_ref[rows, (2 * j + 1) * LANES:(2 * j + 2) * LANES]], axis=0)
        return _scores(q2, keys())

    def finish(scores):
        sink = None
        if sinks is not None:
            row = lax.broadcasted_iota(jnp.int32, (2 * m_rows, 1), 0)
            sink = jnp.where(row < m_rows, sinks[0], sinks[1])
        o2 = _softmax_pv(scores, values(), sink)
        lo, hi = o2[:m_rows], o2[m_rows:]
        if kv_half == 0:
            hi = pltpu.roll(hi, HEAD_DIM, 1)
        elif kv_half == 1:
            lo = pltpu.roll(lo, HEAD_DIM, 1)
        return jnp.where(_low_half(lo.shape), lo, hi)

    return issue, finish


def _single_unit(q_ref, rows, hd, keys, values):
    def issue():
        return _scores(q_ref[rows, hd * LANES:(hd + 1) * LANES], keys())

    def finish(scores):
        return _softmax_pv(scores, values())

    return issue, finish


def _merge_halves(o_even, o_odd):
    return jnp.where(_low_half(o_even.shape), o_even, o_odd)


def _ctx_attn_kernel(sink_ref, qa, ka, va, qb, kb, vb, qc, kc, vc, qd, kd, vd, o_ref):
    rows = slice(0, SEQ)
    units = []
    for j in range(2):
        c = slice(j * LANES, (j + 1) * LANES)
        units.append(_pair_unit(qa, rows, j, lambda c=c: [(ka[:, c], None)], lambda c=c: [va[:, c]], None))
    for q, k, v, with_sink in ((qb, kb, vb, False), (qc, kc, vc, True)):
        for j in range(2):
            sinks = (sink_ref[2 * j] * LOG2E, sink_ref[2 * j + 1] * LOG2E) if with_sink else None
            units.append(_pair_unit(q, rows, j, lambda k=k: [(k[...], None)], lambda v=v: [v[...]], j, sinks))
    for hd in range(4):
        c = slice(hd * LANES, (hd + 1) * LANES)
        vs = slice((hd // 2) * LANES, (hd // 2 + 1) * LANES)
        units.append(_single_unit(qd, rows, hd, lambda c=c: [(kd[:, c], None)], lambda vs=vs: [vd[:, vs]]))
    outs = _run_units(units)
    for n in range(6):
        o_ref[:, n * LANES:(n + 1) * LANES] = outs[n].astype(BF16)
    for j in range(2):
        o_ref[:, (6 + j) * LANES:(7 + j) * LANES] = _merge_halves(outs[6 + 2 * j], outs[7 + 2 * j]).astype(BF16)


def _ctx_attention(l, sink_c, qkv):
    blk = lambda w: pl.BlockSpec((SEQ, w), lambda b: (b, 0))
    return pl.pallas_call(
        _ctx_attn_kernel,
        grid=(BATCH,),
        in_specs=[pl.BlockSpec(memory_space=pltpu.SMEM)] + [blk(w) for w in _QKV_WIDTHS],
        out_specs=pl.BlockSpec((SEQ, D_MODEL), lambda b: (b, 0)),
        out_shape=jax.ShapeDtypeStruct((N_CTX, D_MODEL), BF16),
        compiler_params=_cparams(("parallel",)),
        name=f"ctx_attn_l{l}",
    )(sink_c, *qkv)


LAT_TQ = 256
LAT_SUB = 128
LAT_QBLOCKS = DEC_SEQ // LAT_TQ
LAT_ROW0 = N_CTX // LAT_TQ
LAT_KV0 = N_CTX // DEC_SEQ


def _lat_qspec(width):
    return pl.BlockSpec((LAT_TQ, width), lambda b, i: (LAT_ROW0 + b * LAT_QBLOCKS + i, 0))


def _lat_kvspec(width):
    return pl.BlockSpec((DEC_SEQ, width), lambda b, i: (LAT_KV0 + b, 0))


def _lat_call(kernel, l, name, in_specs, args):
    return pl.pallas_call(
        kernel,
        grid=(DEC_BATCH, LAT_QBLOCKS),
        in_specs=in_specs,
        out_specs=pl.BlockSpec((LAT_TQ, 256), lambda b, i: (b * LAT_QBLOCKS + i, 0)),
        out_shape=jax.ShapeDtypeStruct((N_LAT, 256), BF16),
        compiler_params=_cparams(("parallel", "parallel")),
        name=f"{name}_l{l}",
    )(*args)


def _sub_rows(u):
    return slice(u * LAT_SUB, (u + 1) * LAT_SUB)


def _store_units(o_ref, outs):
    for n, o in enumerate(outs):
        u, j = divmod(n, 2)
        o_ref[_sub_rows(u), j * LANES:(j + 1) * LANES] = o.astype(BF16)


def _gqa_kernel(q_ref, k_ref, v_ref, kc_ref, vc_ref, o_ref):
    keys = lambda: [(kc_ref[...].astype(BF16), None), (k_ref[...], None)]
    values = lambda: [vc_ref[...].astype(BF16), v_ref[...]]
    units = [_pair_unit(q_ref, _sub_rows(u), j, keys, values, j) for u in range(LAT_TQ // LAT_SUB) for j in range(2)]
    _store_units(o_ref, _run_units(units))


def _lat_gqa(l, qb, kb, vb, cache_k, cache_v):
    cspec = pl.BlockSpec((None, None, PAST_LEN, LANES), lambda b, i: (b, l, 0, 0))
    return _lat_call(_gqa_kernel, l, "lat_gqa",
                     [_lat_qspec(512), _lat_kvspec(LANES), _lat_kvspec(LANES), cspec, cspec],
                     (qb, kb, vb, cache_k, cache_v))


WIN_KEYS = 3 * LAT_SUB


def _win_kernel(sink_ref, q_ref, k_ref, v_ref, kc_ref, vc_ref, o_ref):
    units = []
    for u in range(LAT_TQ // LAT_SUB):
        blk = pl.program_id(1) * (LAT_TQ // LAT_SUB) + u
        start = pl.multiple_of(jnp.clip((blk - 1) * LAT_SUB, 0, DEC_SEQ - WIN_KEYS), LAT_SUB)

        def keys(blk=blk, start=start):
            row = lax.broadcasted_iota(jnp.int32, (2 * LAT_SUB, WIN_KEYS), 0)
            qpos = blk * LAT_SUB + jnp.where(row < LAT_SUB, row, row - LAT_SUB)
            kpos = start + lax.broadcasted_iota(jnp.int32, (2 * LAT_SUB, WIN_KEYS), 1)
            bias = jnp.where(jnp.abs(qpos - kpos) <= WINDOW, 0.0, NEG_INF)
            return [(kc_ref[...].astype(BF16), None), (k_ref[pl.ds(start, WIN_KEYS), :], bias)]

        def values(start=start):
            return [vc_ref[...].astype(BF16), v_ref[pl.ds(start, WIN_KEYS), :]]

        for j in range(2):
            sinks = (sink_ref[2 * j] * LOG2E, sink_ref[2 * j + 1] * LOG2E)
            units.append(_pair_unit(q_ref, _sub_rows(u), j, keys, values, j, sinks))
    _store_units(o_ref, _run_units(units, depth=4))


def _lat_window(l, sink_c, qc, kc, vc, cache_k, cache_v):
    cspec = pl.BlockSpec((None, None, PAST_LEN, LANES), lambda b, i: (b, l, 0, 0))
    return _lat_call(_win_kernel, l, "lat_win",
                     [pl.BlockSpec(memory_space=pltpu.SMEM), _lat_qspec(512), _lat_kvspec(LANES),
                      _lat_kvspec(LANES), cspec, cspec],
                     (sink_c, qc, kc, vc, cache_k, cache_v))


def _mla_kernel(q_ref, k_ref, v_ref, kc_ref, vc_ref, o_ref):
    units = []
    rows = slice(0, LAT_TQ)
    for hd in range(4):
        c = slice(hd * LANES, (hd + 1) * LANES)
        vs = slice((hd // 2) * LANES, (hd // 2 + 1) * LANES)
        units.append(_single_unit(q_ref, rows, hd,
                                  lambda c=c: [(kc_ref[:, c], None), (k_ref[:, c], None)],
                                  lambda vs=vs: [vc_ref[:, vs], v_ref[:, vs]]))
    outs = _run_units(units, depth=3)
    for j in range(2):
        o_ref[:, j * LANES:(j + 1) * LANES] = _merge_halves(outs[2 * j], outs[2 * j + 1]).astype(BF16)


def _lat_mla(l, qd, kd, vd, kd_ctx, vd_ctx):
    kcs = pl.BlockSpec((None, None, PAST_LEN, 512), lambda b, i: (l, b, 0, 0))
    vcs = pl.BlockSpec((None, None, PAST_LEN, 256), lambda b, i: (l, b, 0, 0))
    return _lat_call(_mla_kernel, l, "lat_mla",
                     [_lat_qspec(512), _lat_kvspec(512), _lat_kvspec(256), kcs, vcs],
                     (qd, kd, vd, kd_ctx, vd_ctx))


NA_KEYS = NA_ROWS * GRID_W
NA_ROWS_PER_STEP = LAT_TQ // GRID_W


def _na_shift(r):
    return jnp.where(r < NA_ROWS // 2, r,
                     jnp.where(r > GRID_ROWS - NA_ROWS // 2, r - (GRID_ROWS - NA_ROWS), NA_ROWS // 2))


def _na_kernel(q_ref, k_ref, v_ref, kc_ref, vc_ref, bias_ref, o_ref):
    units = []
    for rr in range(NA_ROWS_PER_STEP):
        r = pl.program_id(1) * NA_ROWS_PER_STEP + rr
        start = pl.multiple_of(jnp.clip(r - NA_ROWS // 2, 0, GRID_ROWS - NA_ROWS) * GRID_W, GRID_W)
        shift = _na_shift(r)
        rows = slice(rr * GRID_W, (rr + 1) * GRID_W)
        for j in range(2):
            c = slice(j * LANES, (j + 1) * LANES)

            def keys(j=j, c=c, start=start, shift=shift):
                bias = jnp.concatenate([bias_ref[2 * j, shift], bias_ref[2 * j + 1, shift]], axis=0)
                return [(kc_ref[:, c].astype(BF16), None), (k_ref[pl.ds(start, NA_KEYS), c], bias)]

            def values(c=c, start=start):
                return [vc_ref[:, c].astype(BF16), v_ref[pl.ds(start, NA_KEYS), c]]

            units.append(_pair_unit(q_ref, rows, j, keys, values, None))
    for n, o in enumerate(_run_units(units, depth=4)):
        rr, j = divmod(n, 2)
        o_ref[rr * GRID_W:(rr + 1) * GRID_W, j * LANES:(j + 1) * LANES] = o.astype(BF16)


def _lat_na(l, qa, ka, va, cache_k, cache_v, na_bias):
    cspec = pl.BlockSpec((None, None, PAST_LEN, 256), lambda b, i: (b, l, 0, 0))
    bspec = _resident((None, 4, NA_ROWS, GRID_W, NA_KEYS), lambda b, i: (l, 0, 0, 0, 0))
    return _lat_call(_na_kernel, l, "lat_na",
                     [_lat_qspec(512), _lat_kvspec(256), _lat_kvspec(256), cspec, cspec, bspec],
                     (qa, ka, va, cache_k, cache_v, na_bias))


def _na_bias_kernel(rpb_ref, out_ref, t_ref):
    l = pl.program_id(0)
    n_dr, n_dc = 2 * NA_ROWS - 1, 2 * NA_COLS - 1
    lane = _lane_iota((GRID_W, LANES))
    lane8 = _lane_iota((HALO, LANES))
    sub8 = lax.broadcasted_iota(jnp.int32, (HALO, LANES), 0)
    clamp = lambda d: jnp.clip(d, 1 - NA_COLS, NA_COLS - 1) + (NA_COLS - 1)
    idx_lo = clamp(jnp.where(lane8 < GRID_W, lane8, lane8 - LANES))
    idx_hi = clamp(lane8 - GRID_W)
    kc = lane8 & (GRID_W - 1)
    for hd in range(4):
        def fill(a, carry):
            base = ((l * 4 + hd) * n_dr + a) * n_dc
            g_lo = jnp.zeros((HALO, LANES), F32)
            g_hi = jnp.zeros((HALO, LANES), F32)
            for b in range(n_dc):
                g_lo = jnp.where(idx_lo == b, rpb_ref[base + b], g_lo)
                g_hi = jnp.where(idx_hi == b, rpb_ref[base + b], g_hi)
            for v in range(GRID_W // HALO):
                t = jnp.where(lane8 < GRID_W,
                              pltpu.roll(g_lo, HALO * v, 1, stride=1, stride_axis=0),
                              pltpu.roll(g_hi, HALO * v, 1, stride=1, stride_axis=0))
                c0 = jnp.clip(sub8 + (HALO * v - NA_COLS // 2), 0, GRID_W - NA_COLS)
                ok = (kc >= c0) & (kc < c0 + NA_COLS)
                t_ref[a, HALO * v:HALO * (v + 1), :] = jnp.where(ok, t * LOG2E, NEG_INF)
            return carry
        lax.fori_loop(0, n_dr, fill, 0, unroll=True)
        for d in range(NA_ROWS):
            for m in range(NA_ROWS // 2):
                lo = t_ref[2 * m - d + NA_ROWS - 1]
                hi = t_ref[2 * m + 1 - d + NA_ROWS - 1]
                out_ref[hd, d, :, m * LANES:(m + 1) * LANES] = jnp.where(lane < GRID_W, lo, hi)


def _na_bias(rpb_flat):
    return pl.pallas_call(
        _na_bias_kernel,
        grid=(DEPTH,),
        in_specs=[pl.BlockSpec(memory_space=pltpu.SMEM)],
        out_specs=pl.BlockSpec((None, 4, NA_ROWS, GRID_W, NA_KEYS), lambda l: (l, 0, 0, 0, 0)),
        out_shape=jax.ShapeDtypeStruct((DEPTH, 4, NA_ROWS, GRID_W, NA_KEYS), F32),
        scratch_shapes=[pltpu.VMEM((2 * NA_ROWS - 1, GRID_W, LANES), F32)],
        compiler_params=_cparams(("parallel",)),
        name="na_bias",
    )(rpb_flat)


def _mla_ctx_kernel(ckv_ref, kpe_ref, wukv_ref, g_ref, bd_ref, kd_o, vd_o):
    kv = _dot(ckv_ref[...].astype(BF16), wukv_ref[...])
    kpe = kpe_ref[...]
    for j in range(2):
        kf = jnp.concatenate([kv[:, (2 * j) * LANES:(2 * j + 1) * LANES] + kpe,
                              kv[:, (2 * j + 1) * LANES:(2 * j + 2) * LANES] + kpe], axis=1)
        kd_o[:, 2 * j * LANES:(2 * j + 2) * LANES] = _group_norm(
            kf, bd_ref[1], 1.0 / MLA_QK, g_ref[7:8, :]).astype(BF16)
    vd_o[...] = kv[:, 4 * LANES:].astype(BF16)


def _mla_ctx(cache_ckv, cache_kpe_p, w_ukv_p, gains, bd):
    return pl.pallas_call(
        _mla_ctx_kernel,
        grid=(DEPTH, DEC_BATCH),
        in_specs=[
            pl.BlockSpec((None, None, PAST_LEN, LANES), lambda l, b: (b, l, 0, 0)),
            pl.BlockSpec((None, None, PAST_LEN, LANES), lambda l, b: (b, l, 0, 0)),
            pl.BlockSpec((None, MLA_KV_LORA, 768), lambda l, b: (l, 0, 0)),
            pl.BlockSpec((None, 16, 256), lambda l, b: (l, 0, 0)),
            pl.BlockSpec((2, 256, 256), lambda l, b: (0, 0, 0)),
        ],
        out_specs=[pl.BlockSpec((None, None, PAST_LEN, 512), lambda l, b: (l, b, 0, 0)),
                   pl.BlockSpec((None, None, PAST_LEN, 256), lambda l, b: (l, b, 0, 0))],
        out_shape=[jax.ShapeDtypeStruct((DEPTH, DEC_BATCH, PAST_LEN, 512), BF16),
                   jax.ShapeDtypeStruct((DEPTH, DEC_BATCH, PAST_LEN, 256), BF16)],
        compiler_params=_cparams(("parallel", "parallel")),
        name="mla_ctx",
    )(cache_ckv, cache_kpe_p, w_ukv_p, gains, bd)


def _merge_kernel(xc_ref, xl_ref, oc_ref, oa_ref, ob_ref, occ_ref, od_ref, mod_ref, n1_ref, wt_hbm, wb_hbm, wo_hbm,
                  x1_ref, wg_ref, wb_ref, wo_ref, stage, sems, *, layer):
    @pl.when(pl.program_id(0) == 0)
    def _():
        _load_transposed_bf16(wt_hbm, layer, GATE_BLOCK_STARTS, lambda b, w: w, wg_ref, stage, sems)
        _load_rows_bf16(wb_hbm, layer, wb_ref, stage, sems)
        _load_rows_bf16(wo_hbm, layer, wo_ref, stage, sems)

    is_ctx = pl.program_id(0) < IN_CTX_STEPS
    mod = mod_ref[...]
    lat_refs = (oa_ref, ob_ref, occ_ref, od_ref)

    def half_units(half):
        rows = slice(half * TM, (half + 1) * TM)
        st = {"merged": None}
        st["x"] = jnp.where(is_ctx, xc_ref[rows, :], xl_ref[rows, :])
        st["hb"] = _rms_mod(st["x"], n1_ref[...], mod[:, D_MODEL:2 * D_MODEL], mod[:, 0:D_MODEL]).astype(BF16)

        def branch(n):
            def issue():
                o = jnp.where(is_ctx, oc_ref[rows, n * 256:(n + 1) * 256], lat_refs[n][rows, :])
                return (_dot(st["hb"], wg_ref[:, n * D_MODEL:(n + 1) * D_MODEL]),
                        _dot(o, wb_ref[n * 256:(n + 1) * 256, :]))

            def finish(res):
                gate, br = res
                t = (1.0 / (1.0 + jnp.exp(-gate))) * br
                st["merged"] = t if st["merged"] is None else st["merged"] + t
            return issue, finish

        def out_issue():
            return _dot(st["merged"].astype(BF16), wo_ref[...])

        def out_finish(out):
            x1_ref[rows, :] = st["x"] + mod[:, 2 * D_MODEL:3 * D_MODEL] * out

        return [branch(n) for n in range(4)], (out_issue, out_finish)

    (a_br, a_out), (b_br, b_out) = half_units(0), half_units(1)
    idle = (lambda: None, lambda _: None)
    _run_units(a_br + b_br[:2] + [a_out] + b_br[2:] + [idle, b_out], depth=2)


def _merge(l, x_ctx, x_lat, o_ctx, o_lat, mods, norm1, w_in_t, w_branch, w_out):
    per_batch = DEC_SEQ // TM_IN
    ctx_blk = lambda i: jnp.minimum(i, IN_CTX_STEPS - 1)
    lat_blk = lambda i: jnp.maximum(i - IN_CTX_STEPS, 0)
    tok = pl.BlockSpec((TM_IN, D_MODEL), lambda i: (i, 0))
    ctx_tok = pl.BlockSpec((TM_IN, D_MODEL), lambda i: (ctx_blk(i), 0))
    lat_tok = lambda w: pl.BlockSpec((TM_IN, w), lambda i: (lat_blk(i), 0))
    return pl.pallas_call(
        functools.partial(_merge_kernel, layer=l),
        grid=(IN_STEPS,),
        in_specs=[
            ctx_tok, lat_tok(D_MODEL), ctx_tok, lat_tok(256), lat_tok(256), lat_tok(256), lat_tok(256),
            pl.BlockSpec((None, None, 1, 6 * D_MODEL),
                         lambda i: (l, jnp.where(i < IN_CTX_STEPS, 0, 1 + lat_blk(i) // per_batch), 0, 0)),
            _resident((None, 1, D_MODEL), lambda i: (l, 0, 0)),
            pl.BlockSpec(memory_space=pl.ANY),
            pl.BlockSpec(memory_space=pl.ANY),
            pl.BlockSpec(memory_space=pl.ANY),
        ],
        out_specs=tok,
        out_shape=jax.ShapeDtypeStruct((N_TOK, D_MODEL), F32),
        scratch_shapes=[pltpu.VMEM((D_MODEL, N_GATE), BF16), pltpu.VMEM((D_MODEL, D_MODEL), BF16),
                        pltpu.VMEM((D_MODEL, D_MODEL), BF16), pltpu.VMEM((2, WT_ROWS, D_MODEL), F32),
                        pltpu.SemaphoreType.DMA((2,))],
        compiler_params=_cparams(("arbitrary",)),
        name=f"merge_l{l}",
    )(x_ctx, x_lat, o_ctx, *o_lat, mods, norm1, w_in_t, w_branch, w_out)


HALO = 8


def _sublane_transpose(tiles):
    sub = lax.broadcasted_iota(jnp.int32, tiles[0].shape, 0)
    cur = list(tiles)
    for d in (4, 2, 1):
        keep = (sub & d) == 0
        nxt = list(cur)
        for a in range(HALO):
            if a & d:
                continue
            nxt[a] = jnp.where(keep, cur[a], pltpu.roll(cur[a + d], d, 0))
            nxt[a + d] = jnp.where(keep, pltpu.roll(cur[a], HALO - d, 0), cur[a + d])
        cur = nxt
    return cur


ROW_TILES = TM // HALO
PERM_GROUPS = ROW_TILES // HALO


def _permute_rows(x, to_permuted):
    tile = lambda j: x[j * HALO:(j + 1) * HALO, :]
    out = [None] * ROW_TILES
    for m in range(PERM_GROUPS):
        nat = [m + PERM_GROUPS * s for s in range(HALO)]
        perm = [HALO * m + i for i in range(HALO)]
        src, dst = (nat, perm) if to_permuted else (perm, nat)
        for j, t in zip(dst, _sublane_transpose([tile(j) for j in src])):
            out[j] = t
    return jnp.concatenate(out, axis=0)


def _ffn_tile(x, x_prev, x_next, has_prev, has_next, mod, n2, wup_ref, cw_ref, cb_ref, wdn_ref, u_ref):
    sh2, sc2 = mod[:, 3 * D_MODEL:4 * D_MODEL], mod[:, 4 * D_MODEL:5 * D_MODEL]
    h_prev = jnp.where(has_prev, _rms_mod(x_prev, n2, sc2, sh2), 0.0)
    h_next = jnp.where(has_next, _rms_mod(x_next, n2, sc2, sh2), 0.0)
    hp = _permute_rows(_rms_mod(x, n2, sc2, sh2), True)
    lhs = jnp.concatenate([hp, h_prev, h_next], axis=0).astype(BF16)
    sub = lax.broadcasted_iota(jnp.int32, (HALO, FF_CHUNK), 0)

    def cols(c):
        return (slice(c * FF_CHUNK, (c + 1) * FF_CHUNK), slice(D_FF + c * FF_CHUNK, D_FF + (c + 1) * FF_CHUNK))

    def up(c):
        for half, cs in enumerate(cols(c)):
            lanes = slice(half * FF_CHUNK, (half + 1) * FF_CHUNK)
            u = _dot(lhs, wup_ref[:, cs])
            u_ref[c, HALO:HALO + TM, lanes] = u[:TM]
            prev_row, next_row = u[TM + HALO - 1:TM + HALO], u[TM + HALO:TM + HALO + 1]
            u_ref[c, 0:HALO, lanes] = jnp.where(sub == 0, prev_row, pltpu.roll(u[TM - HALO:TM], 1, 0))
            u_ref[c, HALO + TM:, lanes] = jnp.where(sub == HALO - 1, next_row, pltpu.roll(u[0:HALO], HALO - 1, 0))

    def both(ref, c):
        va, vg = cols(c)
        return jnp.concatenate([ref[:, va], ref[:, vg]], axis=1)

    state = {"acc": None}

    def chunk(c, between):
        cw = both(cw_ref, c)
        t = (both(cb_ref, c) + u_ref[c, 0:TM, :] * cw[0:1] + u_ref[c, HALO:HALO + TM, :] * cw[1:2]
             + u_ref[c, 2 * HALO:2 * HALO + TM, :] * cw[2:3])
        a, g = t[:, :FF_CHUNK], t[:, FF_CHUNK:]
        act = ((g * (1.0 / (1.0 + jnp.exp(-g)))) * a).astype(BF16)
        between()
        d = _dot(act, wdn_ref[c * FF_CHUNK:(c + 1) * FF_CHUNK, :])
        state["acc"] = d if state["acc"] is None else state["acc"] + d

    def finish():
        return x + mod[:, 5 * D_MODEL:6 * D_MODEL] * _permute_rows(state["acc"], False)

    return up, chunk, finish


W_BLOCKS = N_FF_CHUNKS
WUP_BLOCK_COLS = 2 * D_FF // W_BLOCKS
WDN_BLOCK_ROWS = D_FF // W_BLOCKS


def _load_bf16_weights(layer, wup_hbm, wdn_hbm, wup_ref, wdn_ref, up_stage, dn_stage, sems):
    def up_copy(b):
        return pltpu.make_async_copy(wup_hbm.at[layer, :, pl.ds(b * WUP_BLOCK_COLS, WUP_BLOCK_COLS)],
                                     up_stage.at[b % 2], sems.at[0, b % 2])

    def dn_copy(b):
        return pltpu.make_async_copy(wdn_hbm.at[layer, pl.ds(b * WDN_BLOCK_ROWS, WDN_BLOCK_ROWS), :],
                                     dn_stage.at[b % 2], sems.at[1, b % 2])

    up_copy(0).start()
    dn_copy(0).start()
    for b in range(W_BLOCKS):
        if b + 1 < W_BLOCKS:
            up_copy(b + 1).start()
            dn_copy(b + 1).start()
        up_copy(b).wait()
        wup_ref[:, b * WUP_BLOCK_COLS:(b + 1) * WUP_BLOCK_COLS] = up_stage[b % 2].astype(BF16)
        dn_copy(b).wait()
        wdn_ref[b * WDN_BLOCK_ROWS:(b + 1) * WDN_BLOCK_ROWS, :] = dn_stage[b % 2].astype(BF16)


def _ffn_kernel(x_ref, xp_ref, xn_ref, mod_ref, n2_ref, wup_hbm, cw_ref, cb_ref, wdn_hbm, yc_ref, yl_ref,
                u_ref, wup_ref, wdn_ref, up_stage, dn_stage, sems, *, layer):
    i = pl.program_id(0)

    @pl.when(i == 0)
    def _():
        _load_bf16_weights(layer, wup_hbm, wdn_hbm, wup_ref, wdn_ref, up_stage, dn_stage, sems)

    latent = i >= IN_CTX_STEPS
    first = (i - IN_CTX_STEPS) % (DEC_SEQ // TM_IN) == 0
    last = (i - IN_CTX_STEPS) % (DEC_SEQ // TM_IN) == DEC_SEQ // TM_IN - 1
    mod = mod_ref[...]
    n2 = n2_ref[...]
    make = [
        lambda: _ffn_tile(x_ref[0:TM, :], xp_ref[...], x_ref[TM:TM + HALO, :], latent & ~first, latent, mod, n2,
                          wup_ref, cw_ref, cb_ref, wdn_ref, u_ref.at[0]),
        lambda: _ffn_tile(x_ref[TM:, :], x_ref[TM - HALO:TM, :], xn_ref[...], latent, latent & ~last, mod, n2,
                          wup_ref, cw_ref, cb_ref, wdn_ref, u_ref.at[1]),
    ]
    tiles = {}

    def tile(t):
        if t not in tiles:
            tiles[t] = make[t]()
        return tiles[t]

    tasks = [(t, c) for t in range(len(make)) for c in range(N_FF_CHUNKS)]
    early = N_FF_CHUNKS // 2 + 1
    ups = tasks[:early] + [tasks[N_FF_CHUNKS]] + tasks[early:N_FF_CHUNKS] + tasks[N_FF_CHUNKS + 1:]
    depth = 3
    for t, c in ups[:depth]:
        tile(t)[0](c)
    ys = []
    for n, (t, c) in enumerate(tasks):
        ahead = ups[n + depth] if n + depth < len(ups) else None
        tile(t)[1](c, (lambda a=ahead: tile(a[0])[0](a[1])) if ahead else (lambda: None))
        if c == N_FF_CHUNKS - 1:
            ys.append(tile(t)[2]())
    y = jnp.concatenate(ys, axis=0)

    @pl.when(i < IN_CTX_STEPS)
    def _():
        yc_ref[...] = y

    @pl.when(latent)
    def _():
        yl_ref[...] = y


def _ffn(l, x1, mods, norm2, w_up, conv_w, conv_b, w_down):
    per = TM_IN // HALO
    last = N_TOK // HALO - 1
    per_batch = DEC_SEQ // TM_IN
    lat_blk = lambda i: jnp.maximum(i - IN_CTX_STEPS, 0)
    return pl.pallas_call(
        functools.partial(_ffn_kernel, layer=l),
        grid=(IN_STEPS,),
        in_specs=[
            pl.BlockSpec((TM_IN, D_MODEL), lambda i: (i, 0)),
            pl.BlockSpec((HALO, D_MODEL), lambda i: (jnp.maximum(i * per - 1, 0), 0)),
            pl.BlockSpec((HALO, D_MODEL), lambda i: (jnp.minimum((i + 1) * per, last), 0)),
            pl.BlockSpec((None, None, 1, 6 * D_MODEL),
                         lambda i: (l, jnp.where(i < IN_CTX_STEPS, 0, 1 + lat_blk(i) // per_batch), 0, 0)),
            _resident((None, 1, D_MODEL), lambda i: (l, 0, 0)),
            pl.BlockSpec(memory_space=pl.ANY),
            _resident((None, 3, 2 * D_FF), lambda i: (l, 0, 0)),
            _resident((None, 1, 2 * D_FF), lambda i: (l, 0, 0)),
            pl.BlockSpec(memory_space=pl.ANY),
        ],
        out_specs=[pl.BlockSpec((TM_IN, D_MODEL), lambda i: (jnp.minimum(i, IN_CTX_STEPS - 1), 0)),
                   pl.BlockSpec((TM_IN, D_MODEL), lambda i: (lat_blk(i), 0))],
        out_shape=[jax.ShapeDtypeStruct((N_CTX, D_MODEL), F32), jax.ShapeDtypeStruct((N_LAT, D_MODEL), F32)],
        scratch_shapes=[
            pltpu.VMEM((TM_IN // TM, N_FF_CHUNKS, TM + 2 * HALO, 2 * FF_CHUNK), F32),
            pltpu.VMEM((D_MODEL, 2 * D_FF), BF16),
            pltpu.VMEM((D_FF, D_MODEL), BF16),
            pltpu.VMEM((2, D_MODEL, WUP_BLOCK_COLS), F32),
            pltpu.VMEM((2, WDN_BLOCK_ROWS, D_MODEL), F32),
            pltpu.SemaphoreType.DMA((2, 2)),
        ],
        compiler_params=_cparams(("arbitrary",)),
        name=f"ffn_l{l}",
    )(x1, x1, x1, mods, norm2, w_up, conv_w, conv_b, w_down)


WT_ROWS = 512
_SRC_CKV = C_QD + 4 * MLA_QK
QKV_BLOCK_STARTS = (0, WT_ROWS, 2 * WT_ROWS, C_KC, C_QD + 2 * MLA_QK)
GATE_BLOCK_STARTS = tuple(IN_QKV + WT_ROWS * b for b in range(N_GATE // WT_ROWS))


def _qkv_block_rows(b, w):
    zeros = lambda n: jnp.zeros((n, D_MODEL), F32)
    pad = zeros(LANES - MLA_QK)

    def two_heads(r0):
        return [w[r0:r0 + MLA_QK], pad, w[r0 + MLA_QK:r0 + 2 * MLA_QK], pad]

    if QKV_BLOCK_STARTS[b] == C_KC:
        n_kv = C_QD - C_KC
        return jnp.concatenate([w[0:n_kv]] + two_heads(n_kv), axis=0)
    if QKV_BLOCK_STARTS[b] == C_QD + 2 * MLA_QK:
        r_ckv = _SRC_CKV - QKV_BLOCK_STARTS[b]
        r_kpe = r_ckv + MLA_KV_LORA
        return jnp.concatenate(two_heads(0) + [w[r_ckv:r_kpe], zeros(MLA_NOPE), w[r_kpe:r_kpe + MLA_ROPE], pad],
                               axis=0)
    return w


def _load_rows_bf16(w_hbm, layer, dst_ref, stage, sems):
    n_blocks = dst_ref.shape[0] // WT_ROWS

    def copy(b):
        return pltpu.make_async_copy(w_hbm.at[layer, pl.ds(b * WT_ROWS, WT_ROWS), :], stage.at[b % 2], sems.at[b % 2])

    copy(0).start()
    for b in range(n_blocks):
        if b + 1 < n_blocks:
            copy(b + 1).start()
        copy(b).wait()
        dst_ref[b * WT_ROWS:(b + 1) * WT_ROWS, :] = stage[b % 2].astype(BF16)


def _load_transposed_bf16(wt_hbm, layer, starts, select, dst_ref, stage, sems):
    def copy(b):
        return pltpu.make_async_copy(wt_hbm.at[layer, pl.ds(starts[b], WT_ROWS), :], stage.at[b % 2], sems.at[b % 2])

    copy(0).start()
    for b in range(len(starts)):
        if b + 1 < len(starts):
            copy(b + 1).start()
        copy(b).wait()
        dst_ref[:, b * WT_ROWS:(b + 1) * WT_ROWS] = select(b, stage[b % 2]).T.astype(BF16)


def _rope_tables():
    t = np.arange(DEC_SEQ)

    def angles(dim):
        half = dim // 2
        inv = ROPE_THETA ** (-np.arange(0, half, 2, dtype=np.float32) / half)
        row = (t // GRID_W).astype(np.float32)[:, None] * inv[None, :]
        col = (t % GRID_W).astype(np.float32)[:, None] * inv[None, :]
        return np.concatenate([row, row, col, col], axis=-1).astype(np.float32)

    def signed(ang, quarter):
        cos, sin = np.cos(ang), np.sin(ang)
        first = (np.arange(ang.shape[1]) % (2 * quarter)) < quarter
        return cos, np.where(first, -sin, 0.0), np.where(first, 0.0, sin)

    c64, lo64, hi64 = signed(angles(HEAD_DIM), 16)
    c64, lo64, hi64 = (np.tile(a, (1, 2)) for a in (c64, lo64, hi64))
    cm, lom, him = signed(angles(MLA_ROPE), 8)

    def mla_pad(a, fill):
        out = np.full((DEC_SEQ, LANES), fill, np.float32)
        out[:, MLA_NOPE:MLA_QK] = a
        return out

    tabs = [c64, lo64, hi64, mla_pad(cm, 1.0), mla_pad(lom, 0.0), mla_pad(him, 0.0)]
    ident = [np.ones, np.zeros, np.zeros, np.ones, np.zeros, np.zeros]
    full = [np.concatenate([f((TM_IN, LANES), np.float32), a.astype(np.float32)], axis=0) for f, a in zip(ident, tabs)]
    return jnp.asarray(np.stack(full, axis=0))


def _block_diag_ones():
    idx = np.arange(256)
    mats = [(idx[:, None] // w == idx[None, :] // w).astype(np.float32) for w in (HEAD_DIM, LANES)]
    return jnp.asarray(np.stack(mats, axis=0), dtype=BF16)


def _prep_small(w_ukv, qn_a, kn_a, qn_b, kn_b, qn_c, kn_c, qn_d, kn_d, kvn_d):
    ukv = w_ukv.reshape(DEPTH, MLA_KV_LORA, 4, 2, HEAD_DIM)
    k_part = jnp.pad(ukv[:, :, :, 0], ((0, 0), (0, 0), (0, 0), (0, HEAD_DIM))).reshape(DEPTH, MLA_KV_LORA, 512)
    v_part = ukv[:, :, :, 1].reshape(DEPTH, MLA_KV_LORA, 256)
    w_ukv_p = jnp.concatenate([k_part, v_part], axis=-1).astype(BF16)

    def row(g, reps, width=256):
        t = jnp.tile(g, (1, reps))
        return jnp.pad(t, ((0, 0), (0, width - t.shape[1])))

    pad96 = lambda g: jnp.pad(g, ((0, 0), (0, LANES - MLA_QK)))
    rows = [row(qn_a, 4), row(kn_a, 4), row(qn_b, 4), row(kn_b, 2), row(qn_c, 4), row(kn_c, 2),
            row(pad96(qn_d), 2), row(pad96(kn_d), 2), row(kvn_d, 1)]
    gains = jnp.stack(rows + [jnp.zeros_like(rows[0])] * (16 - len(rows)), axis=1)
    return w_ukv_p, gains


def kernel(x_prompt, x_sample, cache_nat_k, cache_nat_v, cache_gqa_k, cache_gqa_v, cache_win_k, cache_win_v,
           cache_mla_ckv, cache_mla_kpe, c, c_ctx, w_ada, b_ada, norm1, norm2, w_in, qn_a, kn_a, rpb_a,
           qn_b, kn_b, qn_c, kn_c, sink_c, qn_d, kn_d, kvn_d, w_ukv, w_branch, w_out, w_up, conv_w, conv_b,
           w_down):
    w_ukv_p, gains = _prep_small(w_ukv, qn_a, kn_a, qn_b, kn_b, qn_c, kn_c, qn_d, kn_d, kvn_d)
    w_in_t = jnp.swapaxes(w_in, 1, 2)
    w_branch_r = w_branch.reshape(DEPTH, 4 * 256, D_MODEL)
    conv_b_r = conv_b.reshape(DEPTH, 1, 2 * D_FF)
    rope_tab = _rope_tables()
    bd = _block_diag_ones()
    norm1_r = norm1.reshape(DEPTH, 1, D_MODEL)
    norm2_r = norm2.reshape(DEPTH, 1, D_MODEL)

    cond8 = jnp.zeros((8, D_MODEL), F32).at[0].set(c_ctx).at[1:1 + DEC_BATCH].set(c)
    mods = _adaln(cond8, w_ada, b_ada).reshape(DEPTH, 8, 1, 6 * D_MODEL)

    rpb_flat = jnp.pad(rpb_a.reshape(-1), (0, 8192 - rpb_a.size))
    na_bias = _na_bias(rpb_flat)
    kpe_p = jnp.pad(cache_mla_kpe, ((0, 0), (0, 0), (0, 0), (MLA_NOPE, LANES - MLA_QK)))
    kd_ctx, vd_ctx = _mla_ctx(cache_mla_ckv, kpe_p, w_ukv_p, gains, bd)
    c_nat_k = cache_nat_k.reshape(DEC_BATCH, DEPTH, PAST_LEN, 256)
    c_nat_v = cache_nat_v.reshape(DEC_BATCH, DEPTH, PAST_LEN, 256)
    c_gqa_k = cache_gqa_k.reshape(DEC_BATCH, DEPTH, PAST_LEN, LANES)
    c_gqa_v = cache_gqa_v.reshape(DEC_BATCH, DEPTH, PAST_LEN, LANES)
    c_win_k = cache_win_k.reshape(DEC_BATCH, DEPTH, PAST_LEN, LANES)
    c_win_v = cache_win_v.reshape(DEC_BATCH, DEPTH, PAST_LEN, LANES)

    x_ctx, x_lat = x_prompt.reshape(N_CTX, D_MODEL), x_sample.reshape(N_LAT, D_MODEL)
    caches = _zero_caches()
    for l in range(DEPTH):
        qkv, caches = _inproj(l, x_ctx, x_lat, mods, norm1_r, w_in_t, w_ukv_p, gains, bd, rope_tab, caches)
        qa, ka, va, qb, kb, vb, qc, kc, vc, qd, kd, vd = qkv
        sink_l = sink_c[l]
        o_ctx = _ctx_attention(l, sink_l, qkv)
        o_lat = (_lat_na(l, qa, ka, va, c_nat_k, c_nat_v, na_bias),
                 _lat_gqa(l, qb, kb, vb, c_gqa_k, c_gqa_v),
                 _lat_window(l, sink_l, qc, kc, vc, c_win_k, c_win_v),
                 _lat_mla(l, qd, kd, vd, kd_ctx, vd_ctx))
        x1 = _merge(l, x_ctx, x_lat, o_ctx, o_lat, mods, norm1_r, w_in_t, w_branch_r, w_out)
        x_ctx, x_lat = _ffn(l, x1, mods, norm2_r, w_up, conv_w, conv_b_r, w_down)
    y_prompt = x_ctx.reshape(BATCH, SEQ, D_MODEL)
    y_sample = x_lat.reshape(DEC_BATCH, DEC_SEQ, D_MODEL)
    heads = (4, 4, 2, 2, 2, 2)
    outs = [a.reshape(BATCH, DEPTH, h, HEAD_DIM, SEQ).transpose(0, 1, 4, 2, 3) for a, h in zip(caches[:6], heads)]
    return (y_prompt, y_sample, *outs, jnp.swapaxes(caches[6], 2, 3), jnp.swapaxes(caches[7], 2, 3))
```

```python
import functools

import jax
import jax.numpy as jnp
import numpy as np
from jax import lax
from jax.experimental import pallas as pl
from jax.experimental.pallas import tpu as pltpu

F32 = jnp.float32
BF16 = jnp.bfloat16

D_MODEL = 1024
BATCH = 16
SEQ = 256
DEPTH = 4
DEC_BATCH = 2
DEC_SEQ = 2048
PAST_LEN = 256
GRID_W = 64
HEAD_DIM = 64
NA_ROWS = 8
NA_COLS = 16
WINDOW = 128
MLA_NOPE = 64
MLA_ROPE = 32
MLA_QK = MLA_NOPE + MLA_ROPE
MLA_KV_LORA = 128
D_FF = 2816
ROPE_THETA = 10000.0
EPS = 1e-6
NEG_INF = -1e30

LANES = 128
TM = 256
N_CTX = BATCH * SEQ
N_LAT = DEC_BATCH * DEC_SEQ
N_TOK = N_CTX + N_LAT
GRID_ROWS = DEC_SEQ // GRID_W
FF_CHUNK = 256
N_FF_CHUNKS = D_FF // FF_CHUNK
VMEM_LIMIT = 56 * 1024 * 1024

C_QA, C_KA, C_VA = 0, 256, 512
C_QB, C_KB, C_VB = 768, 1024, 1152
C_QC, C_KC, C_VC = 1280, 1536, 1664
C_QD, C_CKV, C_KPE, C_END = 1792, 2304, 2432, 2560
N_GATE = 4 * D_MODEL
IN_QKV = 2336
LOG2E = float(np.log2(np.e))
Q_SCALE = HEAD_DIM ** -0.5 * LOG2E
MLA_Q_SCALE = MLA_QK ** -0.5 * LOG2E


def _cparams(sem):
    return pltpu.CompilerParams(dimension_semantics=sem, vmem_limit_bytes=VMEM_LIMIT)


def _resident(shape, index_map):
    return pl.BlockSpec(shape, index_map, pipeline_mode=pl.Buffered(1))


def _dot(a, b):
    return jnp.dot(a, b, preferred_element_type=F32)


def _dot_t(a, b):
    return lax.dot_general(a, b, (((1,), (1,)), ((), ())), preferred_element_type=F32)


def _rms_mod(x, gain, scale, shift):
    ms = jnp.mean(x * x, axis=-1, keepdims=True)
    return (x * lax.rsqrt(ms + EPS)) * gain * (1.0 + scale) + shift


def _group_norm(t, ones_bd, inv_n, gain):
    ss = _dot((t * t).astype(BF16), ones_bd)
    return t * lax.rsqrt(ss * inv_n + EPS) * gain


def _rope(t, cos, sin_lo, sin_hi, half):
    return (t * cos + pltpu.roll(t, LANES - half, 1) * sin_lo + pltpu.roll(t, half, 1) * sin_hi)


def _lane_iota(shape):
    return lax.broadcasted_iota(jnp.int32, shape, len(shape) - 1)


def _low_half(shape):
    return _lane_iota(shape) < HEAD_DIM


def _run_units(units, depth=2):
    n = len(units)
    pending = {}
    outs = []
    for k in range(min(depth, n)):
        pending[k] = units[k][0]()
    for k in range(n):
        outs.append(units[k][1](pending.pop(k)))
        if k + depth < n:
            pending[k + depth] = units[k + depth][0]()
    return outs


N_COND = 1 + DEC_BATCH


def _adaln_kernel(cond_ref, w_ref, b_ref, out_ref):
    cnd = cond_ref[...]
    s_t = (cnd * (1.0 / (1.0 + jnp.exp(-cnd)))).T
    w = w_ref[...]
    out_ref[...] = jnp.zeros(out_ref.shape, F32)
    for r in range(N_COND):
        out_ref[r:r + 1, :] = jnp.sum(s_t[:, r:r + 1] * w, axis=0, keepdims=True) + b_ref[...]


def _adaln(cond8, w_ada, b_ada):
    n_col = 6 * D_MODEL
    tn = 2048
    return pl.pallas_call(
        _adaln_kernel,
        grid=(DEPTH, n_col // tn),
        in_specs=[
            pl.BlockSpec((8, D_MODEL), lambda l, n: (0, 0)),
            pl.BlockSpec((None, D_MODEL, tn), lambda l, n: (l, 0, n)),
            pl.BlockSpec((None, 1, tn), lambda l, n: (l, 0, n)),
        ],
        out_specs=pl.BlockSpec((None, 8, tn), lambda l, n: (l, 0, n)),
        out_shape=jax.ShapeDtypeStruct((DEPTH, 8, n_col), F32),
        compiler_params=_cparams(("parallel", "parallel")),
        name="adaln",
    )(cond8, w_ada, b_ada.reshape(DEPTH, 1, n_col))


_QKV_WIDTHS = (512, 256, 256, 512, 128, 128, 512, 128, 128, 512, 512, 256)
_CACHE_WIDTHS = (256, 256, 128, 128, 128, 128, MLA_KV_LORA, MLA_ROPE)
_CACHE_BLOCKS = tuple((w, SEQ) for w in _CACHE_WIDTHS)


def _zeros_kernel(*outs):
    for o in outs:
        o[...] = jnp.zeros(o.shape, F32)


def _zero_caches():
    return pl.pallas_call(
        _zeros_kernel,
        grid=(BATCH,),
        in_specs=[],
        out_specs=[pl.BlockSpec((None, DEPTH) + blk, lambda b: (b, 0, 0, 0)) for blk in _CACHE_BLOCKS],
        out_shape=[jax.ShapeDtypeStruct((BATCH, DEPTH) + blk, F32) for blk in _CACHE_BLOCKS],
        compiler_params=_cparams(("parallel",)),
        name="zero_caches",
    )()


TM_IN = 2 * TM
IN_STEPS = N_TOK // TM_IN
IN_CTX_STEPS = N_CTX // TM_IN


def _inproj_kernel(xc_ref, xl_ref, mod_ref, n1_ref, wt_hbm, wukv_ref, g_ref, bd_ref, rope_ref, *rest, layer):
    n_out = len(_QKV_WIDTHS) + len(_CACHE_WIDTHS)
    outs = rest[len(_CACHE_WIDTHS):len(_CACHE_WIDTHS) + n_out]
    w_ref, stage, sems = rest[len(_CACHE_WIDTHS) + n_out:]

    @pl.when(pl.program_id(0) == 0)
    def _():
        _load_transposed_bf16(wt_hbm, layer, QKV_BLOCK_STARTS, _qkv_block_rows, w_ref, stage, sems)

    is_ctx = pl.program_id(0) >= IN_STEPS - IN_CTX_STEPS
    units = []
    for half in range(TM_IN // TM):
        rows = slice(half * TM, (half + 1) * TM)
        views = [o.at[rows] for o in outs[:len(_QKV_WIDTHS)]] + [o.at[half] for o in outs[len(_QKV_WIDTHS):]]
        units += _inproj_half(is_ctx, xc_ref.at[rows], xl_ref.at[rows], mod_ref, n1_ref, w_ref, wukv_ref, g_ref,
                              bd_ref, rope_ref.at[:, rows, :], views)
    _run_units(units, depth=2)


def _inproj_half(is_ctx, xc_ref, xl_ref, mod_ref, n1_ref, w_ref, wukv_ref, g_ref, bd_ref, rope_ref, outs):
    qa_o, ka_o, va_o, qb_o, kb_o, vb_o, qc_o, kc_o, vc_o, qd_o, kd_o, vd_o = outs[:len(_QKV_WIDTHS)]
    natk_o, natv_o, gqak_o, gqav_o, wink_o, winv_o, ckv_o, kpe_o = outs[len(_QKV_WIDTHS):]
    x = jnp.where(is_ctx, xc_ref[...], xl_ref[...])
    mod = mod_ref[...]
    hb = _rms_mod(x, n1_ref[...], mod[:, D_MODEL:2 * D_MODEL], mod[:, 0:D_MODEL]).astype(BF16)
    bd64, bd128 = bd_ref[0], bd_ref[1]
    cos64, slo64, shi64 = rope_ref[0], rope_ref[1], rope_ref[2]
    cosm, slom, shim = rope_ref[3], rope_ref[4], rope_ref[5]
    inv64, inv96 = 1.0 / HEAD_DIM, 1.0 / MLA_QK
    low = _low_half((TM, LANES))
    held = {}

    def chunks(t):
        return [t[:, c:c + LANES] for c in range(0, t.shape[1], LANES)]

    def rope64(t):
        parts = [_rope(p, cos64, slo64, shi64, 16) for p in chunks(t)]
        return parts[0] if len(parts) == 1 else jnp.concatenate(parts, axis=1)

    def ropem(t):
        return jnp.concatenate([_rope(p, cosm, slom, shim, 8) for p in chunks(t)], axis=1)

    def store_q(ref, q, grouped):
        q = q * Q_SCALE
        slots = []
        for hd in range(4):
            part, src = q[:, (hd // 2) * LANES:(hd // 2 + 1) * LANES], hd % 2
            dst = hd // 2 if grouped else src
            if src != dst:
                part = pltpu.roll(part, HEAD_DIM, 1)
            slots.append(jnp.where(low if dst == 0 else ~low, part, 0.0))
        ref[...] = jnp.concatenate(slots, axis=1).astype(BF16)

    def post_latent(t):
        ckv = _group_norm(t[:, :LANES], bd128[:LANES, :LANES], 1.0 / MLA_KV_LORA, g_ref[8:9, :LANES])
        kpe = t[:, LANES:]
        ckv_o[...] = ckv.T
        kpe_o[...] = kpe.T[MLA_NOPE:MLA_QK, :]
        held["kpe"] = kpe
        held["kv"] = _dot(ckv.astype(BF16), wukv_ref[...])

    def post_kd(_):
        kv, kpe = held["kv"], held["kpe"]
        for j in range(2):
            kf = jnp.concatenate([kv[:, (2 * j) * LANES:(2 * j + 1) * LANES] + kpe,
                                  kv[:, (2 * j + 1) * LANES:(2 * j + 2) * LANES] + kpe], axis=1)
            kd_o[:, 2 * j * LANES:(2 * j + 2) * LANES] = ropem(
                _group_norm(kf, bd128, inv96, g_ref[7:8, :])).astype(BF16)
        vd_o[...] = kv[:, 4 * LANES:].astype(BF16)

    def post_qa(t):
        store_q(qa_o, _group_norm(t, bd64, inv64, g_ref[0:1, :]), grouped=False)

    def post_ka(t):
        k = _group_norm(t, bd64, inv64, g_ref[1:2, :])
        natk_o[...] = k.T
        ka_o[...] = k.astype(BF16)

    def post_va(t):
        natv_o[...] = t.T
        va_o[...] = t.astype(BF16)

    def post_q_rope(ref, row):
        def post(t):
            store_q(ref, rope64(_group_norm(t, bd64, inv64, g_ref[row:row + 1, :])), grouped=True)
        return post

    def post_kv(k_ref, v_ref, kc_ref, vc_ref, row):
        def post(t):
            k = _group_norm(t[:, :LANES], bd64[:LANES, :LANES], inv64, g_ref[row:row + 1, :LANES])
            kc_ref[...] = k.T
            k_ref[...] = rope64(k).astype(BF16)
            v = t[:, LANES:]
            vc_ref[...] = v.T
            v_ref[...] = v.astype(BF16)
        return post

    def post_qd(j):
        def post(t):
            qd_o[:, 2 * j * LANES:(2 * j + 2) * LANES] = (ropem(
                _group_norm(t, bd128, inv96, g_ref[6:7, :])) * MLA_Q_SCALE).astype(BF16)
        return post

    stages = [
        ((C_CKV, C_END), post_latent),
        ((C_QA, C_KA), post_qa),
        (None, post_kd),
        ((C_QB, C_KB), post_q_rope(qb_o, 2)),
        ((C_QC, C_KC), post_q_rope(qc_o, 4)),
        ((C_QD, C_QD + 2 * LANES), post_qd(0)),
        ((C_QD + 2 * LANES, C_CKV), post_qd(1)),
        ((C_KB, C_QC), post_kv(kb_o, vb_o, gqak_o, gqav_o, 3)),
        ((C_KC, C_QD), post_kv(kc_o, vc_o, wink_o, winv_o, 5)),
        ((C_KA, C_VA), post_ka),
        ((C_VA, C_QB), post_va),
    ]

    def issue(cols):
        return None if cols is None else _dot(hb, w_ref[:, cols[0]:cols[1]])

    return [(functools.partial(issue, cols), post) for cols, post in stages]


def _inproj(l, x_ctx, x_lat, mods, norm1, w_in_t, w_ukv_p, gains, bd, rope_tab, caches):
    n_lat = IN_STEPS - IN_CTX_STEPS
    blk = lambda s: (s + IN_CTX_STEPS) % IN_STEPS
    lat_blk = lambda s: jnp.maximum(blk(s) - IN_CTX_STEPS, 0)
    per_batch = DEC_SEQ // TM_IN
    tok = lambda w: pl.BlockSpec((TM_IN, w), lambda s: (blk(s), 0))
    in_specs = [
        pl.BlockSpec((TM_IN, D_MODEL), lambda s: (jnp.minimum(blk(s), IN_CTX_STEPS - 1), 0)),
        pl.BlockSpec((TM_IN, D_MODEL), lambda s: (lat_blk(s), 0)),
        pl.BlockSpec((None, None, 1, 6 * D_MODEL),
                     lambda s: (l, jnp.where(s < n_lat, 1 + lat_blk(s) // per_batch, 0), 0, 0)),
        _resident((None, 1, D_MODEL), lambda s: (l, 0, 0)),
        pl.BlockSpec(memory_space=pl.ANY),
        _resident((None, MLA_KV_LORA, 768), lambda s: (l, 0, 0)),
        _resident((None, 16, 256), lambda s: (l, 0, 0)),
        _resident((2, 256, 256), lambda s: (0, 0, 0)),
        pl.BlockSpec((6, TM_IN, LANES), lambda s: (0, jnp.where(s < n_lat, 1 + lat_blk(s) % per_batch, 0), 0)),
    ]
    args = [x_ctx, x_lat, mods, norm1, w_in_t, w_ukv_p, gains, bd, rope_tab] + list(caches)
    aliases = {len(in_specs) + n: len(_QKV_WIDTHS) + n for n in range(len(caches))}
    in_specs = in_specs + [pl.BlockSpec(memory_space=pl.ANY)] * len(caches)
    cache_spec = lambda cb: pl.BlockSpec((TM_IN // SEQ, None) + cb, lambda s: (jnp.maximum(s - n_lat, 0), l, 0, 0))
    outs = pl.pallas_call(
        functools.partial(_inproj_kernel, layer=l),
        grid=(IN_STEPS,),
        in_specs=in_specs,
        out_specs=[tok(w) for w in _QKV_WIDTHS] + [cache_spec(blk) for blk in _CACHE_BLOCKS],
        out_shape=[jax.ShapeDtypeStruct((N_TOK, w), BF16) for w in _QKV_WIDTHS]
        + [jax.ShapeDtypeStruct((BATCH, DEPTH) + blk, F32) for blk in _CACHE_BLOCKS],
        scratch_shapes=[pltpu.VMEM((D_MODEL, C_END), BF16), pltpu.VMEM((2, WT_ROWS, D_MODEL), F32),
                        pltpu.SemaphoreType.DMA((2,))],
        input_output_aliases=aliases,
        compiler_params=_cparams(("arbitrary",)),
        name=f"inproj_l{l}",
    )(*args)
    return outs[:len(_QKV_WIDTHS)], outs[len(_QKV_WIDTHS):]


def _scores(q, keys):
    out = []
    for k, bias in keys:
        s = _dot_t(q, k)
        if bias is not None:
            s = s + bias
        out.append(s)
    return out


def _softmax_pv(scores, values, sink=None):
    m = functools.reduce(jnp.maximum, [jnp.max(s, axis=-1, keepdims=True) for s in scores])
    if sink is not None:
        m = jnp.maximum(m, sink)
    denom = None
    out = None
    for s, v in zip(scores, values):
        e = jnp.exp2(s - m)
        d = jnp.sum(e, axis=-1, keepdims=True)
        o = _dot(e.astype(BF16), v)
        denom = d if denom is None else denom + d
        out = o if out is None else out + o
    if sink is not None:
        denom = denom + jnp.exp2(sink - m)
    return out / denom


def _pair_unit(q_ref, rows, j, keys, values, kv_half, sinks=None):
    m_rows = rows.stop - rows.start

    def issue():
        q2 = jnp.concatenate([q_ref[rows, (2 * j) * LANES:(2 * j + 1) * LANES],
                              q_ref[rows, (2 * j + 1) * LANES:(2 * j + 2) * LANES]], axis=0)
        return _scores(q2, keys())

    def finish(scores):
        sink = None
        if sinks is not None:
            row = lax.broadcasted_iota(jnp.int32, (2 * m_rows, 1), 0)
            sink = jnp.where(row < m_rows, sinks[0], sinks[1])
        o2 = _softmax_pv(scores, values(), sink)
        lo, hi = o2[:m_rows], o2[m_rows:]
        if kv_half == 0:
            hi = pltpu.roll(hi, HEAD_DIM, 1)
        elif kv_half == 1:
            lo = pltpu.roll(lo, HEAD_DIM, 1)
        return jnp.where(_low_half(lo.shape), lo, hi)

    return issue, finish


def _single_unit(q_ref, rows, hd, keys, values):
    def issue():
        return _scores(q_ref[rows, hd * LANES:(hd + 1) * LANES], keys())

    def finish(scores):
        return _softmax_pv(scores, values())

    return issue, finish


def _merge_halves(o_even, o_odd):
    return jnp.where(_low_half(o_even.shape), o_even, o_odd)


CTX_PER_STEP = 2
CTX_UNITS = 10


def _ctx_attn_kernel(sink_ref, qa, ka, va, qb, kb, vb, qc, kc, vc, qd, kd, vd, o_ref):
    units = []
    for e in range(CTX_PER_STEP):
        rows = slice(e * SEQ, (e + 1) * SEQ)
        for j in range(2):
            c = slice(j * LANES, (j + 1) * LANES)
            units.append(_pair_unit(qa, rows, j, lambda r=rows, c=c: [(ka[r, c], None)],
                                    lambda r=rows, c=c: [va[r, c]], None))
        for q, k, v, with_sink in ((qb, kb, vb, False), (qc, kc, vc, True)):
            for j in range(2):
                sinks = (sink_ref[2 * j] * LOG2E, sink_ref[2 * j + 1] * LOG2E) if with_sink else None
                units.append(_pair_unit(q, rows, j, lambda r=rows, k=k: [(k[r, :], None)],
                                        lambda r=rows, v=v: [v[r, :]], j, sinks))
        for hd in range(4):
            c = slice(hd * LANES, (hd + 1) * LANES)
            vs = slice((hd // 2) * LANES, (hd // 2 + 1) * LANES)
            units.append(_single_unit(qd, rows, hd, lambda r=rows, c=c: [(kd[r, c], None)],
                                      lambda r=rows, vs=vs: [vd[r, vs]]))
    outs = _run_units(units)
    for e in range(CTX_PER_STEP):
        rows = slice(e * SEQ, (e + 1) * SEQ)
        seq = outs[e * CTX_UNITS:(e + 1) * CTX_UNITS]
        for n in range(6):
            o_ref[rows, n * LANES:(n + 1) * LANES] = seq[n].astype(BF16)
        for j in range(2):
            o_ref[rows, (6 + j) * LANES:(7 + j) * LANES] = _merge_halves(seq[6 + 2 * j], seq[7 + 2 * j]).astype(BF16)


def _ctx_attention(l, sink_c, qkv):
    blk = lambda w: pl.BlockSpec((CTX_PER_STEP * SEQ, w), lambda b: (b, 0))
    return pl.pallas_call(
        _ctx_attn_kernel,
        grid=(BATCH // CTX_PER_STEP,),
        in_specs=[pl.BlockSpec(memory_space=pltpu.SMEM)] + [blk(w) for w in _QKV_WIDTHS],
        out_specs=pl.BlockSpec((CTX_PER_STEP * SEQ, D_MODEL), lambda b: (b, 0)),
        out_shape=jax.ShapeDtypeStruct((N_CTX, D_MODEL), BF16),
        compiler_params=_cparams(("parallel",)),
        name=f"ctx_attn_l{l}",
    )(sink_c, *qkv)


LAT_TQ = 256
LAT_SUB = 128
LAT_QBLOCKS = DEC_SEQ // LAT_TQ
LAT_ROW0 = N_CTX // LAT_TQ
LAT_KV0 = N_CTX // DEC_SEQ


def _lat_qspec(width):
    return pl.BlockSpec((LAT_TQ, width), lambda b, i: (LAT_ROW0 + b * LAT_QBLOCKS + i, 0))


def _lat_kvspec(width):
    return pl.BlockSpec((DEC_SEQ, width), lambda b, i: (LAT_KV0 + b, 0))


def _lat_call(kernel, l, name, in_specs, args):
    return pl.pallas_call(
        kernel,
        grid=(DEC_BATCH, LAT_QBLOCKS),
        in_specs=in_specs,
        out_specs=pl.BlockSpec((LAT_TQ, 256), lambda b, i: (b * LAT_QBLOCKS + i, 0)),
        out_shape=jax.ShapeDtypeStruct((N_LAT, 256), BF16),
        compiler_params=_cparams(("parallel", "parallel")),
        name=f"{name}_l{l}",
    )(*args)


def _sub_rows(u):
    return slice(u * LAT_SUB, (u + 1) * LAT_SUB)


def _store_units(o_ref, outs):
    for n, o in enumerate(outs):
        u, j = divmod(n, 2)
        o_ref[_sub_rows(u), j * LANES:(j + 1) * LANES] = o.astype(BF16)


def _gqa_kernel(q_ref, k_ref, v_ref, kc_ref, vc_ref, o_ref):
    keys = lambda: [(kc_ref[...].astype(BF16), None), (k_ref[...], None)]
    values = lambda: [vc_ref[...].astype(BF16), v_ref[...]]
    units = [_pair_unit(q_ref, _sub_rows(u), j, keys, values, j) for u in range(LAT_TQ // LAT_SUB) for j in range(2)]
    _store_units(o_ref, _run_units(units))


def _lat_gqa(l, qb, kb, vb, cache_k, cache_v):
    cspec = pl.BlockSpec((None, None, PAST_LEN, LANES), lambda b, i: (b, l, 0, 0))
    return _lat_call(_gqa_kernel, l, "lat_gqa",
                     [_lat_qspec(512), _lat_kvspec(LANES), _lat_kvspec(LANES), cspec, cspec],
                     (qb, kb, vb, cache_k, cache_v))


WIN_KEYS = 3 * LAT_SUB


def _win_kernel(sink_ref, q_ref, k_ref, v_ref, kc_ref, vc_ref, o_ref):
    units = []
    for u in range(LAT_TQ // LAT_SUB):
        blk = pl.program_id(1) * (LAT_TQ // LAT_SUB) + u
        start = pl.multiple_of(jnp.clip((blk - 1) * LAT_SUB, 0, DEC_SEQ - WIN_KEYS), LAT_SUB)

        def keys(blk=blk, start=start):
            row = lax.broadcasted_iota(jnp.int32, (2 * LAT_SUB, WIN_KEYS), 0)
            qpos = blk * LAT_SUB + jnp.where(row < LAT_SUB, row, row - LAT_SUB)
            kpos = start + lax.broadcasted_iota(jnp.int32, (2 * LAT_SUB, WIN_KEYS), 1)
            bias = jnp.where(jnp.abs(qpos - kpos) <= WINDOW, 0.0, NEG_INF)
            return [(kc_ref[...].astype(BF16), None), (k_ref[pl.ds(start, WIN_KEYS), :], bias)]

        def values(start=start):
            return [vc_ref[...].astype(BF16), v_ref[pl.ds(start, WIN_KEYS), :]]

        for j in range(2):
            sinks = (sink_ref[2 * j] * LOG2E, sink_ref[2 * j + 1] * LOG2E)
            units.append(_pair_unit(q_ref, _sub_rows(u), j, keys, values, j, sinks))
    _store_units(o_ref, _run_units(units, depth=4))


def _lat_window(l, sink_c, qc, kc, vc, cache_k, cache_v):
    cspec = pl.BlockSpec((None, None, PAST_LEN, LANES), lambda b, i: (b, l, 0, 0))
    return _lat_call(_win_kernel, l, "lat_win",
                     [pl.BlockSpec(memory_space=pltpu.SMEM), _lat_qspec(512), _lat_kvspec(LANES),
                      _lat_kvspec(LANES), cspec, cspec],
                     (sink_c, qc, kc, vc, cache_k, cache_v))


def _mla_kernel(q_ref, k_ref, v_ref, kc_ref, vc_ref, o_ref):
    units = []
    rows = slice(0, LAT_TQ)
    for hd in range(4):
        c = slice(hd * LANES, (hd + 1) * LANES)
        vs = slice((hd // 2) * LANES, (hd // 2 + 1) * LANES)
        units.append(_single_unit(q_ref, rows, hd,
                                  lambda c=c: [(kc_ref[:, c], None), (k_ref[:, c], None)],
                                  lambda vs=vs: [vc_ref[:, vs], v_ref[:, vs]]))
    outs = _run_units(units, depth=3)
    for j in range(2):
        o_ref[:, j * LANES:(j + 1) * LANES] = _merge_halves(outs[2 * j], outs[2 * j + 1]).astype(BF16)


def _lat_mla(l, qd, kd, vd, kd_ctx, vd_ctx):
    kcs = pl.BlockSpec((None, None, PAST_LEN, 512), lambda b, i: (l, b, 0, 0))
    vcs = pl.BlockSpec((None, None, PAST_LEN, 256), lambda b, i: (l, b, 0, 0))
    return _lat_call(_mla_kernel, l, "lat_mla",
                     [_lat_qspec(512), _lat_kvspec(512), _lat_kvspec(256), kcs, vcs],
                     (qd, kd, vd, kd_ctx, vd_ctx))


NA_KEYS = NA_ROWS * GRID_W
NA_ROWS_PER_STEP = LAT_TQ // GRID_W


def _na_shift(r):
    return jnp.where(r < NA_ROWS // 2, r,
                     jnp.where(r > GRID_ROWS - NA_ROWS // 2, r - (GRID_ROWS - NA_ROWS), NA_ROWS // 2))


def _na_kernel(q_ref, k_ref, v_ref, kc_ref, vc_ref, bias_ref, o_ref):
    units = []
    for rr in range(NA_ROWS_PER_STEP):
        r = pl.program_id(1) * NA_ROWS_PER_STEP + rr
        start = pl.multiple_of(jnp.clip(r - NA_ROWS // 2, 0, GRID_ROWS - NA_ROWS) * GRID_W, GRID_W)
        shift = _na_shift(r)
        rows = slice(rr * GRID_W, (rr + 1) * GRID_W)
        for j in range(2):
            c = slice(j * LANES, (j + 1) * LANES)

            def keys(j=j, c=c, start=start, shift=shift):
                bias = jnp.concatenate([bias_ref[2 * j, shift], bias_ref[2 * j + 1, shift]], axis=0)
                return [(kc_ref[:, c].astype(BF16), None), (k_ref[pl.ds(start, NA_KEYS), c], bias)]

            def values(c=c, start=start):
                return [vc_ref[:, c].astype(BF16), v_ref[pl.ds(start, NA_KEYS), c]]

            units.append(_pair_unit(q_ref, rows, j, keys, values, None))
    for n, o in enumerate(_run_units(units, depth=4)):
        rr, j = divmod(n, 2)
        o_ref[rr * GRID_W:(rr + 1) * GRID_W, j * LANES:(j + 1) * LANES] = o.astype(BF16)


def _lat_na(l, qa, ka, va, cache_k, cache_v, na_bias):
    cspec = pl.BlockSpec((None, None, PAST_LEN, 256), lambda b, i: (b, l, 0, 0))
    bspec = _resident((None, 4, NA_ROWS, GRID_W, NA_KEYS), lambda b, i: (l, 0, 0, 0, 0))
    return _lat_call(_na_kernel, l, "lat_na",
                     [_lat_qspec(512), _lat_kvspec(256), _lat_kvspec(256), cspec, cspec, bspec],
                     (qa, ka, va, cache_k, cache_v, na_bias))


def _na_bias_kernel(rpb_ref, out_ref, t_ref):
    l = pl.program_id(0)
    n_dr, n_dc = 2 * NA_ROWS - 1, 2 * NA_COLS - 1
    lane = _lane_iota((GRID_W, LANES))
    lane8 = _lane_iota((HALO, LANES))
    sub8 = lax.broadcasted_iota(jnp.int32, (HALO, LANES), 0)
    clamp = lambda d: jnp.clip(d, 1 - NA_COLS, NA_COLS - 1) + (NA_COLS - 1)
    idx_lo = clamp(jnp.where(lane8 < GRID_W, lane8, lane8 - LANES))
    idx_hi = clamp(lane8 - GRID_W)
    kc = lane8 & (GRID_W - 1)
    for hd in range(4):
        def fill(a, carry):
            base = ((l * 4 + hd) * n_dr + a) * n_dc
            g_lo = jnp.zeros((HALO, LANES), F32)
            g_hi = jnp.zeros((HALO, LANES), F32)
            for b in range(n_dc):
                g_lo = jnp.where(idx_lo == b, rpb_ref[base + b], g_lo)
                g_hi = jnp.where(idx_hi == b, rpb_ref[base + b], g_hi)
            for v in range(GRID_W // HALO):
                t = jnp.where(lane8 < GRID_W,
                              pltpu.roll(g_lo, HALO * v, 1, stride=1, stride_axis=0),
                              pltpu.roll(g_hi, HALO * v, 1, stride=1, stride_axis=0))
                c0 = jnp.clip(sub8 + (HALO * v - NA_COLS // 2), 0, GRID_W - NA_COLS)
                ok = (kc >= c0) & (kc < c0 + NA_COLS)
                t_ref[a, HALO * v:HALO * (v + 1), :] = jnp.where(ok, t * LOG2E, NEG_INF)
            return carry
        lax.fori_loop(0, n_dr, fill, 0, unroll=True)
        for d in range(NA_ROWS):
            for m in range(NA_ROWS // 2):
                lo = t_ref[2 * m - d + NA_ROWS - 1]
                hi = t_ref[2 * m + 1 - d + NA_ROWS - 1]
                out_ref[hd, d, :, m * LANES:(m + 1) * LANES] = jnp.where(lane < GRID_W, lo, hi)


def _na_bias(rpb_flat):
    return pl.pallas_call(
        _na_bias_kernel,
        grid=(DEPTH,),
        in_specs=[pl.BlockSpec(memory_space=pltpu.SMEM)],
        out_specs=pl.BlockSpec((None, 4, NA_ROWS, GRID_W, NA_KEYS), lambda l: (l, 0, 0, 0, 0)),
        out_shape=jax.ShapeDtypeStruct((DEPTH, 4, NA_ROWS, GRID_W, NA_KEYS), F32),
        scratch_shapes=[pltpu.VMEM((2 * NA_ROWS - 1, GRID_W, LANES), F32)],
        compiler_params=_cparams(("parallel",)),
        name="na_bias",
    )(rpb_flat)


def _mla_ctx_kernel(ckv_ref, kpe_ref, wukv_ref, g_ref, bd_ref, kd_o, vd_o):
    kv = _dot(ckv_ref[...].astype(BF16), wukv_ref[...])
    kpe = kpe_ref[...]
    for j in range(2):
        kf = jnp.concatenate([kv[:, (2 * j) * LANES:(2 * j + 1) * LANES] + kpe,
                              kv[:, (2 * j + 1) * LANES:(2 * j + 2) * LANES] + kpe], axis=1)
        kd_o[:, 2 * j * LANES:(2 * j + 2) * LANES] = _group_norm(
            kf, bd_ref[1], 1.0 / MLA_QK, g_ref[7:8, :]).astype(BF16)
    vd_o[...] = kv[:, 4 * LANES:].astype(BF16)


def _mla_ctx(cache_ckv, cache_kpe_p, w_ukv_p, gains, bd):
    return pl.pallas_call(
        _mla_ctx_kernel,
        grid=(DEPTH, DEC_BATCH),
        in_specs=[
            pl.BlockSpec((None, None, PAST_LEN, LANES), lambda l, b: (b, l, 0, 0)),
            pl.BlockSpec((None, None, PAST_LEN, LANES), lambda l, b: (b, l, 0, 0)),
            pl.BlockSpec((None, MLA_KV_LORA, 768), lambda l, b: (l, 0, 0)),
            pl.BlockSpec((None, 16, 256), lambda l, b: (l, 0, 0)),
            pl.BlockSpec((2, 256, 256), lambda l, b: (0, 0, 0)),
        ],
        out_specs=[pl.BlockSpec((None, None, PAST_LEN, 512), lambda l, b: (l, b, 0, 0)),
                   pl.BlockSpec((None, None, PAST_LEN, 256), lambda l, b: (l, b, 0, 0))],
        out_shape=[jax.ShapeDtypeStruct((DEPTH, DEC_BATCH, PAST_LEN, 512), BF16),
                   jax.ShapeDtypeStruct((DEPTH, DEC_BATCH, PAST_LEN, 256), BF16)],
        compiler_params=_cparams(("parallel", "parallel")),
        name="mla_ctx",
    )(cache_ckv, cache_kpe_p, w_ukv_p, gains, bd)


def _merge_kernel(xc_ref, xl_ref, oc_ref, oa_ref, ob_ref, occ_ref, od_ref, mod_ref, n1_ref, wt_hbm, wb_hbm, wo_hbm,
                  x1_ref, wg_ref, wb_ref, wo_ref, stage, sems, *, layer):
    @pl.when(pl.program_id(0) == 0)
    def _():
        _load_transposed_bf16(wt_hbm, layer, GATE_BLOCK_STARTS, lambda b, w: w, wg_ref, stage, sems)
        _load_rows_bf16(wb_hbm, layer, wb_ref, stage, sems)
        _load_rows_bf16(wo_hbm, layer, wo_ref, stage, sems)

    is_ctx = pl.program_id(0) < IN_CTX_STEPS
    mod = mod_ref[...]
    lat_refs = (oa_ref, ob_ref, occ_ref, od_ref)

    def half_units(half):
        rows = slice(half * TM, (half + 1) * TM)
        st = {"merged": None}
        st["x"] = jnp.where(is_ctx, xc_ref[rows, :], xl_ref[rows, :])
        st["hb"] = _rms_mod(st["x"], n1_ref[...], mod[:, D_MODEL:2 * D_MODEL], mod[:, 0:D_MODEL]).astype(BF16)

        def branch(n):
            def issue():
                o = jnp.where(is_ctx, oc_ref[rows, n * 256:(n + 1) * 256], lat_refs[n][rows, :])
                return (_dot(st["hb"], wg_ref[:, n * D_MODEL:(n + 1) * D_MODEL]),
                        _dot(o, wb_ref[n * 256:(n + 1) * 256, :]))

            def finish(res):
                gate, br = res
                t = (1.0 / (1.0 + jnp.exp(-gate))) * br
                st["merged"] = t if st["merged"] is None else st["merged"] + t
            return issue, finish

        def out_issue():
            return _dot(st["merged"].astype(BF16), wo_ref[...])

        def out_finish(out):
            x1_ref[rows, :] = st["x"] + mod[:, 2 * D_MODEL:3 * D_MODEL] * out

        return [branch(n) for n in range(4)], (out_issue, out_finish)

    (a_br, a_out), (b_br, b_out) = half_units(0), half_units(1)
    idle = (lambda: None, lambda _: None)
    _run_units(a_br + b_br[:2] + [a_out] + b_br[2:] + [idle, b_out], depth=2)


def _merge(l, x_ctx, x_lat, o_ctx, o_lat, mods, norm1, w_in_t, w_branch, w_out):
    per_batch = DEC_SEQ // TM_IN
    ctx_blk = lambda i: jnp.minimum(i, IN_CTX_STEPS - 1)
    lat_blk = lambda i: jnp.maximum(i - IN_CTX_STEPS, 0)
    tok = pl.BlockSpec((TM_IN, D_MODEL), lambda i: (i, 0))
    ctx_tok = pl.BlockSpec((TM_IN, D_MODEL), lambda i: (ctx_blk(i), 0))
    lat_tok = lambda w: pl.BlockSpec((TM_IN, w), lambda i: (lat_blk(i), 0))
    return pl.pallas_call(
        functools.partial(_merge_kernel, layer=l),
        grid=(IN_STEPS,),
        in_specs=[
            ctx_tok, lat_tok(D_MODEL), ctx_tok, lat_tok(256), lat_tok(256), lat_tok(256), lat_tok(256),
            pl.BlockSpec((None, None, 1, 6 * D_MODEL),
                         lambda i: (l, jnp.where(i < IN_CTX_STEPS, 0, 1 + lat_blk(i) // per_batch), 0, 0)),
            _resident((None, 1, D_MODEL), lambda i: (l, 0, 0)),
            pl.BlockSpec(memory_space=pl.ANY),
            pl.BlockSpec(memory_space=pl.ANY),
            pl.BlockSpec(memory_space=pl.ANY),
        ],
        out_specs=tok,
        out_shape=jax.ShapeDtypeStruct((N_TOK, D_MODEL), F32),
        scratch_shapes=[pltpu.VMEM((D_MODEL, N_GATE), BF16), pltpu.VMEM((D_MODEL, D_MODEL), BF16),
                        pltpu.VMEM((D_MODEL, D_MODEL), BF16), pltpu.VMEM((2, WT_ROWS, D_MODEL), F32),
                        pltpu.SemaphoreType.DMA((2,))],
        compiler_params=_cparams(("arbitrary",)),
        name=f"merge_l{l}",
    )(x_ctx, x_lat, o_ctx, *o_lat, mods, norm1, w_in_t, w_branch, w_out)


HALO = 8


def _sublane_transpose(tiles):
    sub = lax.broadcasted_iota(jnp.int32, tiles[0].shape, 0)
    cur = list(tiles)
    for d in (4, 2, 1):
        keep = (sub & d) == 0
        nxt = list(cur)
        for a in range(HALO):
            if a & d:
                continue
            nxt[a] = jnp.where(keep, cur[a], pltpu.roll(cur[a + d], d, 0))
            nxt[a + d] = jnp.where(keep, pltpu.roll(cur[a], HALO - d, 0), cur[a + d])
        cur = nxt
    return cur


ROW_TILES = TM // HALO
PERM_GROUPS = ROW_TILES // HALO


def _permute_rows(x, to_permuted):
    tile = lambda j: x[j * HALO:(j + 1) * HALO, :]
    out = [None] * ROW_TILES
    for m in range(PERM_GROUPS):
        nat = [m + PERM_GROUPS * s for s in range(HALO)]
        perm = [HALO * m + i for i in range(HALO)]
        src, dst = (nat, perm) if to_permuted else (perm, nat)
        for j, t in zip(dst, _sublane_transpose([tile(j) for j in src])):
            out[j] = t
    return jnp.concatenate(out, axis=0)


def _ffn_tile(x, x_prev, x_next, has_prev, has_next, mod, n2, wup_ref, cw_ref, cb_ref, wdn_ref, u_ref):
    sh2, sc2 = mod[:, 3 * D_MODEL:4 * D_MODEL], mod[:, 4 * D_MODEL:5 * D_MODEL]
    h_prev = jnp.where(has_prev, _rms_mod(x_prev, n2, sc2, sh2), 0.0)
    h_next = jnp.where(has_next, _rms_mod(x_next, n2, sc2, sh2), 0.0)
    hp = _permute_rows(_rms_mod(x, n2, sc2, sh2), True)
    lhs = jnp.concatenate([hp, h_prev, h_next], axis=0).astype(BF16)
    sub = lax.broadcasted_iota(jnp.int32, (HALO, FF_CHUNK), 0)

    def cols(c):
        return (slice(c * FF_CHUNK, (c + 1) * FF_CHUNK), slice(D_FF + c * FF_CHUNK, D_FF + (c + 1) * FF_CHUNK))

    def up(c):
        for half, cs in enumerate(cols(c)):
            lanes = slice(half * FF_CHUNK, (half + 1) * FF_CHUNK)
            u = _dot(lhs, wup_ref[:, cs])
            u_ref[c, HALO:HALO + TM, lanes] = u[:TM]
            prev_row, next_row = u[TM + HALO - 1:TM + HALO], u[TM + HALO:TM + HALO + 1]
            u_ref[c, 0:HALO, lanes] = jnp.where(sub == 0, prev_row, pltpu.roll(u[TM - HALO:TM], 1, 0))
            u_ref[c, HALO + TM:, lanes] = jnp.where(sub == HALO - 1, next_row, pltpu.roll(u[0:HALO], HALO - 1, 0))

    def both(ref, c):
        va, vg = cols(c)
        return jnp.concatenate([ref[:, va], ref[:, vg]], axis=1)

    state = {"acc": None}

    def chunk(c, between):
        cw = both(cw_ref, c)
        t = (both(cb_ref, c) + u_ref[c, 0:TM, :] * cw[0:1] + u_ref[c, HALO:HALO + TM, :] * cw[1:2]
             + u_ref[c, 2 * HALO:2 * HALO + TM, :] * cw[2:3])
        a, g = t[:, :FF_CHUNK], t[:, FF_CHUNK:]
        act = ((g * (1.0 / (1.0 + jnp.exp(-g)))) * a).astype(BF16)
        between()
        d = _dot(act, wdn_ref[c * FF_CHUNK:(c + 1) * FF_CHUNK, :])
        state["acc"] = d if state["acc"] is None else state["acc"] + d

    def finish():
        return x + mod[:, 5 * D_MODEL:6 * D_MODEL] * _permute_rows(state["acc"], False)

    return up, chunk, finish


W_BLOCKS = N_FF_CHUNKS
WUP_BLOCK_COLS = 2 * D_FF // W_BLOCKS
WDN_BLOCK_ROWS = D_FF // W_BLOCKS


def _load_bf16_weights(layer, wup_hbm, wdn_hbm, wup_ref, wdn_ref, up_stage, dn_stage, sems):
    def up_copy(b):
        return pltpu.make_async_copy(wup_hbm.at[layer, :, pl.ds(b * WUP_BLOCK_COLS, WUP_BLOCK_COLS)],
                                     up_stage.at[b % 2], sems.at[0, b % 2])

    def dn_copy(b):
        return pltpu.make_async_copy(wdn_hbm.at[layer, pl.ds(b * WDN_BLOCK_ROWS, WDN_BLOCK_ROWS), :],
                                     dn_stage.at[b % 2], sems.at[1, b % 2])

    up_copy(0).start()
    dn_copy(0).start()
    for b in range(W_BLOCKS):
        if b + 1 < W_BLOCKS:
            up_copy(b + 1).start()
            dn_copy(b + 1).start()
        up_copy(b).wait()
        wup_ref[:, b * WUP_BLOCK_COLS:(b + 1) * WUP_BLOCK_COLS] = up_stage[b % 2].astype(BF16)
        dn_copy(b).wait()
        wdn_ref[b * WDN_BLOCK_ROWS:(b + 1) * WDN_BLOCK_ROWS, :] = dn_stage[b % 2].astype(BF16)


def _ffn_kernel(x_ref, xp_ref, xn_ref, mod_ref, n2_ref, wup_hbm, cw_ref, cb_ref, wdn_hbm, yc_ref, yl_ref,
                u_ref, wup_ref, wdn_ref, up_stage, dn_stage, sems, *, layer):
    i = pl.program_id(0)

    @pl.when(i == 0)
    def _():
        _load_bf16_weights(layer, wup_hbm, wdn_hbm, wup_ref, wdn_ref, up_stage, dn_stage, sems)

    latent = i >= IN_CTX_STEPS
    first = (i - IN_CTX_STEPS) % (DEC_SEQ // TM_IN) == 0
    last = (i - IN_CTX_STEPS) % (DEC_SEQ // TM_IN) == DEC_SEQ // TM_IN - 1
    mod = mod_ref[...]
    n2 = n2_ref[...]
    make = [
        lambda: _ffn_tile(x_ref[0:TM, :], xp_ref[...], x_ref[TM:TM + HALO, :], latent & ~first, latent, mod, n2,
                          wup_ref, cw_ref, cb_ref, wdn_ref, u_ref.at[0]),
        lambda: _ffn_tile(x_ref[TM:, :], x_ref[TM - HALO:TM, :], xn_ref[...], latent, latent & ~last, mod, n2,
                          wup_ref, cw_ref, cb_ref, wdn_ref, u_ref.at[1]),
    ]
    tiles = {}

    def tile(t):
        if t not in tiles:
            tiles[t] = make[t]()
        return tiles[t]

    tasks = [(t, c) for t in range(len(make)) for c in range(N_FF_CHUNKS)]
    early = N_FF_CHUNKS // 2 + 1
    ups = tasks[:early] + [tasks[N_FF_CHUNKS]] + tasks[early:N_FF_CHUNKS] + tasks[N_FF_CHUNKS + 1:]
    depth = 3
    for t, c in ups[:depth]:
        tile(t)[0](c)
    ys = []
    for n, (t, c) in enumerate(tasks):
        ahead = ups[n + depth] if n + depth < len(ups) else None
        tile(t)[1](c, (lambda a=ahead: tile(a[0])[0](a[1])) if ahead else (lambda: None))
        if c == N_FF_CHUNKS - 1:
            ys.append(tile(t)[2]())
    y = jnp.concatenate(ys, axis=0)

    @pl.when(i < IN_CTX_STEPS)
    def _():
        yc_ref[...] = y

    @pl.when(latent)
    def _():
        yl_ref[...] = y


def _ffn(l, x1, mods, norm2, w_up, conv_w, conv_b, w_down):
    per = TM_IN // HALO
    last = N_TOK // HALO - 1
    per_batch = DEC_SEQ // TM_IN
    lat_blk = lambda i: jnp.maximum(i - IN_CTX_STEPS, 0)
    return pl.pallas_call(
        functools.partial(_ffn_kernel, layer=l),
        grid=(IN_STEPS,),
        in_specs=[
            pl.BlockSpec((TM_IN, D_MODEL), lambda i: (i, 0)),
            pl.BlockSpec((HALO, D_MODEL), lambda i: (jnp.maximum(i * per - 1, 0), 0)),
            pl.BlockSpec((HALO, D_MODEL), lambda i: (jnp.minimum((i + 1) * per, last), 0)),
            pl.BlockSpec((None, None, 1, 6 * D_MODEL),
                         lambda i: (l, jnp.where(i < IN_CTX_STEPS, 0, 1 + lat_blk(i) // per_batch), 0, 0)),
            _resident((None, 1, D_MODEL), lambda i: (l, 0, 0)),
            pl.BlockSpec(memory_space=pl.ANY),
            _resident((None, 3, 2 * D_FF), lambda i: (l, 0, 0)),
            _resident((None, 1, 2 * D_FF), lambda i: (l, 0, 0)),
            pl.BlockSpec(memory_space=pl.ANY),
        ],
        out_specs=[pl.BlockSpec((TM_IN, D_MODEL), lambda i: (jnp.minimum(i, IN_CTX_STEPS - 1), 0)),
                   pl.BlockSpec((TM_IN, D_MODEL), lambda i: (lat_blk(i), 0))],
        out_shape=[jax.ShapeDtypeStruct((N_CTX, D_MODEL), F32), jax.ShapeDtypeStruct((N_LAT, D_MODEL), F32)],
        scratch_shapes=[
            pltpu.VMEM((TM_IN // TM, N_FF_CHUNKS, TM + 2 * HALO, 2 * FF_CHUNK), F32),
            pltpu.VMEM((D_MODEL, 2 * D_FF), BF16),
            pltpu.VMEM((D_FF, D_MODEL), BF16),
            pltpu.VMEM((2, D_MODEL, WUP_BLOCK_COLS), F32),
            pltpu.VMEM((2, WDN_BLOCK_ROWS, D_MODEL), F32),
            pltpu.SemaphoreType.DMA((2, 2)),
        ],
        compiler_params=_cparams(("arbitrary",)),
        name=f"ffn_l{l}",
    )(x1, x1, x1, mods, norm2, w_up, conv_w, conv_b, w_down)


WT_ROWS = 512
_SRC_CKV = C_QD + 4 * MLA_QK
QKV_BLOCK_STARTS = (0, WT_ROWS, 2 * WT_ROWS, C_KC, C_QD + 2 * MLA_QK)
GATE_BLOCK_STARTS = tuple(IN_QKV + WT_ROWS * b for b in range(N_GATE // WT_ROWS))


def _qkv_block_rows(b, w):
    zeros = lambda n: jnp.zeros((n, D_MODEL), F32)
    pad = zeros(LANES - MLA_QK)

    def two_heads(r0):
        return [w[r0:r0 + MLA_QK], pad, w[r0 + MLA_QK:r0 + 2 * MLA_QK], pad]

    if QKV_BLOCK_STARTS[b] == C_KC:
        n_kv = C_QD - C_KC
        return jnp.concatenate([w[0:n_kv]] + two_heads(n_kv), axis=0)
    if QKV_BLOCK_STARTS[b] == C_QD + 2 * MLA_QK:
        r_ckv = _SRC_CKV - QKV_BLOCK_STARTS[b]
        r_kpe = r_ckv + MLA_KV_LORA
        return jnp.concatenate(two_heads(0) + [w[r_ckv:r_kpe], zeros(MLA_NOPE), w[r_kpe:r_kpe + MLA_ROPE], pad],
                               axis=0)
    return w


def _load_rows_bf16(w_hbm, layer, dst_ref, stage, sems):
    n_blocks = dst_ref.shape[0] // WT_ROWS

    def copy(b):
        return pltpu.make_async_copy(w_hbm.at[layer, pl.ds(b * WT_ROWS, WT_ROWS), :], stage.at[b % 2], sems.at[b % 2])

    copy(0).start()
    for b in range(n_blocks):
        if b + 1 < n_blocks:
            copy(b + 1).start()
        copy(b).wait()
        dst_ref[b * WT_ROWS:(b + 1) * WT_ROWS, :] = stage[b % 2].astype(BF16)


def _load_transposed_bf16(wt_hbm, layer, starts, select, dst_ref, stage, sems):
    def copy(b):
        return pltpu.make_async_copy(wt_hbm.at[layer, pl.ds(starts[b], WT_ROWS), :], stage.at[b % 2], sems.at[b % 2])

    copy(0).start()
    for b in range(len(starts)):
        if b + 1 < len(starts):
            copy(b + 1).start()
        copy(b).wait()
        dst_ref[:, b * WT_ROWS:(b + 1) * WT_ROWS] = select(b, stage[b % 2]).T.astype(BF16)


def _rope_tables():
    t = np.arange(DEC_SEQ)

    def angles(dim):
        half = dim // 2
        inv = ROPE_THETA ** (-np.arange(0, half, 2, dtype=np.float32) / half)
        row = (t // GRID_W).astype(np.float32)[:, None] * inv[None, :]
        col = (t % GRID_W).astype(np.float32)[:, None] * inv[None, :]
        return np.concatenate([row, row, col, col], axis=-1).astype(np.float32)

    def signed(ang, quarter):
        cos, sin = np.cos(ang), np.sin(ang)
        first = (np.arange(ang.shape[1]) % (2 * quarter)) < quarter
        return cos, np.where(first, -sin, 0.0), np.where(first, 0.0, sin)

    c64, lo64, hi64 = signed(angles(HEAD_DIM), 16)
    c64, lo64, hi64 = (np.tile(a, (1, 2)) for a in (c64, lo64, hi64))
    cm, lom, him = signed(angles(MLA_ROPE), 8)

    def mla_pad(a, fill):
        out = np.full((DEC_SEQ, LANES), fill, np.float32)
        out[:, MLA_NOPE:MLA_QK] = a
        return out

    tabs = [c64, lo64, hi64, mla_pad(cm, 1.0), mla_pad(lom, 0.0), mla_pad(him, 0.0)]
    ident = [np.ones, np.zeros, np.zeros, np.ones, np.zeros, np.zeros]
    full = [np.concatenate([f((TM_IN, LANES), np.float32), a.astype(np.float32)], axis=0) for f, a in zip(ident, tabs)]
    return jnp.asarray(np.stack(full, axis=0))


def _block_diag_ones():
    idx = np.arange(256)
    mats = [(idx[:, None] // w == idx[None, :] // w).astype(np.float32) for w in (HEAD_DIM, LANES)]
    return jnp.asarray(np.stack(mats, axis=0), dtype=BF16)


def _prep_small(w_ukv, qn_a, kn_a, qn_b, kn_b, qn_c, kn_c, qn_d, kn_d, kvn_d):
    ukv = w_ukv.reshape(DEPTH, MLA_KV_LORA, 4, 2, HEAD_DIM)
    k_part = jnp.pad(ukv[:, :, :, 0], ((0, 0), (0, 0), (0, 0), (0, HEAD_DIM))).reshape(DEPTH, MLA_KV_LORA, 512)
    v_part = ukv[:, :, :, 1].reshape(DEPTH, MLA_KV_LORA, 256)
    w_ukv_p = jnp.concatenate([k_part, v_part], axis=-1).astype(BF16)

    def row(g, reps, width=256):
        t = jnp.tile(g, (1, reps))
        return jnp.pad(t, ((0, 0), (0, width - t.shape[1])))

    pad96 = lambda g: jnp.pad(g, ((0, 0), (0, LANES - MLA_QK)))
    rows = [row(qn_a, 4), row(kn_a, 4), row(qn_b, 4), row(kn_b, 2), row(qn_c, 4), row(kn_c, 2),
            row(pad96(qn_d), 2), row(pad96(kn_d), 2), row(kvn_d, 1)]
    gains = jnp.stack(rows + [jnp.zeros_like(rows[0])] * (16 - len(rows)), axis=1)
    return w_ukv_p, gains


def kernel(x_prompt, x_sample, cache_nat_k, cache_nat_v, cache_gqa_k, cache_gqa_v, cache_win_k, cache_win_v,
           cache_mla_ckv, cache_mla_kpe, c, c_ctx, w_ada, b_ada, norm1, norm2, w_in, qn_a, kn_a, rpb_a,
           qn_b, kn_b, qn_c, kn_c, sink_c, qn_d, kn_d, kvn_d, w_ukv, w_branch, w_out, w_up, conv_w, conv_b,
           w_down):
    w_ukv_p, gains = _prep_small(w_ukv, qn_a, kn_a, qn_b, kn_b, qn_c, kn_c, qn_d, kn_d, kvn_d)
    w_in_t = jnp.swapaxes(w_in, 1, 2)
    w_branch_r = w_branch.reshape(DEPTH, 4 * 256, D_MODEL)
    conv_b_r = conv_b.reshape(DEPTH, 1, 2 * D_FF)
    rope_tab = _rope_tables()
    bd = _block_diag_ones()
    norm1_r = norm1.reshape(DEPTH, 1, D_MODEL)
    norm2_r = norm2.reshape(DEPTH, 1, D_MODEL)

    cond8 = jnp.zeros((8, D_MODEL), F32).at[0].set(c_ctx).at[1:1 + DEC_BATCH].set(c)
    mods = _adaln(cond8, w_ada, b_ada).reshape(DEPTH, 8, 1, 6 * D_MODEL)

    rpb_flat = jnp.pad(rpb_a.reshape(-1), (0, 8192 - rpb_a.size))
    na_bias = _na_bias(rpb_flat)
    kpe_p = jnp.pad(cache_mla_kpe, ((0, 0), (0, 0), (0, 0), (MLA_NOPE, LANES - MLA_QK)))
    kd_ctx, vd_ctx = _mla_ctx(cache_mla_ckv, kpe_p, w_ukv_p, gains, bd)
    c_nat_k = cache_nat_k.reshape(DEC_BATCH, DEPTH, PAST_LEN, 256)
    c_nat_v = cache_nat_v.reshape(DEC_BATCH, DEPTH, PAST_LEN, 256)
    c_gqa_k = cache_gqa_k.reshape(DEC_BATCH, DEPTH, PAST_LEN, LANES)
    c_gqa_v = cache_gqa_v.reshape(DEC_BATCH, DEPTH, PAST_LEN, LANES)
    c_win_k = cache_win_k.reshape(DEC_BATCH, DEPTH, PAST_LEN, LANES)
    c_win_v = cache_win_v.reshape(DEC_BATCH, DEPTH, PAST_LEN, LANES)

    x_ctx, x_lat = x_prompt.reshape(N_CTX, D_MODEL), x_sample.reshape(N_LAT, D_MODEL)
    caches = _zero_caches()
    for l in range(DEPTH):
        qkv, caches = _inproj(l, x_ctx, x_lat, mods, norm1_r, w_in_t, w_ukv_p, gains, bd, rope_tab, caches)
        qa, ka, va, qb, kb, vb, qc, kc, vc, qd, kd, vd = qkv
        sink_l = sink_c[l]
        o_ctx = _ctx_attention(l, sink_l, qkv)
        o_lat = (_lat_na(l, qa, ka, va, c_nat_k, c_nat_v, na_bias),
                 _lat_gqa(l, qb, kb, vb, c_gqa_k, c_gqa_v),
                 _lat_window(l, sink_l, qc, kc, vc, c_win_k, c_win_v),
                 _lat_mla(l, qd, kd, vd, kd_ctx, vd_ctx))
        x1 = _merge(l, x_ctx, x_lat, o_ctx, o_lat, mods, norm1_r, w_in_t, w_branch_r, w_out)
        x_ctx, x_lat = _ffn(l, x1, mods, norm2_r, w_up, conv_w, conv_b_r, w_down)
    y_prompt = x_ctx.reshape(BATCH, SEQ, D_MODEL)
    y_sample = x_lat.reshape(DEC_BATCH, DEC_SEQ, D_MODEL)
    heads = (4, 4, 2, 2, 2, 2)
    outs = [a.reshape(BATCH, DEPTH, h, HEAD_DIM, SEQ).transpose(0, 1, 4, 2, 3) for a, h in zip(caches[:6], heads)]
    return (y_prompt, y_sample, *outs, jnp.swapaxes(caches[6], 2, 3), jnp.swapaxes(caches[7], 2, 3))
```

```python
import functools

import jax
import jax.numpy as jnp
import numpy as np
from jax import lax
from jax.experimental import pallas as pl
from jax.experimental.pallas import tpu as pltpu

F32 = jnp.float32
BF16 = jnp.bfloat16

D_MODEL = 1024
BATCH = 16
SEQ = 256
DEPTH = 4
DEC_BATCH = 2
DEC_SEQ = 2048
PAST_LEN = 256
GRID_W = 64
HEAD_DIM = 64
NA_ROWS = 8
NA_COLS = 16
WINDOW = 128
MLA_NOPE = 64
MLA_ROPE = 32
MLA_QK = MLA_NOPE + MLA_ROPE
MLA_KV_LORA = 128
D_FF = 2816
ROPE_THETA = 10000.0
EPS = 1e-6
NEG_INF = -1e30

LANES = 128
TM = 256
N_CTX = BATCH * SEQ
N_LAT = DEC_BATCH * DEC_SEQ
N_TOK = N_CTX + N_LAT
GRID_ROWS = DEC_SEQ // GRID_W
FF_CHUNK = 256
N_FF_CHUNKS = D_FF // FF_CHUNK
VMEM_LIMIT = 56 * 1024 * 1024

C_QA, C_KA, C_VA = 0, 256, 512
C_QB, C_KB, C_VB = 768, 1024, 1152
C_QC, C_KC, C_VC = 1280, 1536, 1664
C_QD, C_CKV, C_KPE, C_END = 1792, 2304, 2432, 2560
N_GATE = 4 * D_MODEL
IN_QKV = 2336
LOG2E = float(np.log2(np.e))
Q_SCALE = HEAD_DIM ** -0.5 * LOG2E
MLA_Q_SCALE = MLA_QK ** -0.5 * LOG2E


def _cparams(sem):
    return pltpu.CompilerParams(dimension_semantics=sem, vmem_limit_bytes=VMEM_LIMIT)


def _resident(shape, index_map):
    return pl.BlockSpec(shape, index_map, pipeline_mode=pl.Buffered(1))


def _dot(a, b):
    return jnp.dot(a, b, preferred_element_type=F32)


def _dot_t(a, b):
    return lax.dot_general(a, b, (((1,), (1,)), ((), ())), preferred_element_type=F32)


def _rms_mod(x, gain, scale, shift):
    ms = jnp.mean(x * x, axis=-1, keepdims=True)
    return (x * lax.rsqrt(ms + EPS)) * gain * (1.0 + scale) + shift


def _group_norm(t, ones_bd, inv_n, gain):
    ss = _dot((t * t).astype(BF16), ones_bd)
    return t * lax.rsqrt(ss * inv_n + EPS) * gain


def _rope(t, cos, sin_lo, sin_hi, half):
    return (t * cos + pltpu.roll(t, LANES - half, 1) * sin_lo + pltpu.roll(t, half, 1) * sin_hi)


def _lane_iota(shape):
    return lax.broadcasted_iota(jnp.int32, shape, len(shape) - 1)


def _low_half(shape):
    return _lane_iota(shape) < HEAD_DIM


def _run_units(units, depth=2):
    n = len(units)
    pending = {}
    outs = []
    for k in range(min(depth, n)):
        pending[k] = units[k][0]()
    for k in range(n):
        outs.append(units[k][1](pending.pop(k)))
        if k + depth < n:
            pending[k + depth] = units[k + depth][0]()
    return outs


N_COND = 1 + DEC_BATCH


def _adaln_kernel(cond_ref, w_ref, b_ref, out_ref):
    cnd = cond_ref[...]
    s_t = (cnd * (1.0 / (1.0 + jnp.exp(-cnd)))).T
    w = w_ref[...]
    out_ref[...] = jnp.zeros(out_ref.shape, F32)
    for r in range(N_COND):
        out_ref[r:r + 1, :] = jnp.sum(s_t[:, r:r + 1] * w, axis=0, keepdims=True) + b_ref[...]


def _adaln(cond8, w_ada, b_ada):
    n_col = 6 * D_MODEL
    tn = 2048
    return pl.pallas_call(
        _adaln_kernel,
        grid=(DEPTH, n_col // tn),
        in_specs=[
            pl.BlockSpec((8, D_MODEL), lambda l, n: (0, 0)),
            pl.BlockSpec((None, D_MODEL, tn), lambda l, n: (l, 0, n)),
            pl.BlockSpec((None, 1, tn), lambda l, n: (l, 0, n)),
        ],
        out_specs=pl.BlockSpec((None, 8, tn), lambda l, n: (l, 0, n)),
        out_shape=jax.ShapeDtypeStruct((DEPTH, 8, n_col), F32),
        compiler_params=_cparams(("parallel", "parallel")),
        name="adaln",
    )(cond8, w_ada, b_ada.reshape(DEPTH, 1, n_col))


_QKV_WIDTHS = (512, 256, 256, 512, 128, 128, 512, 128, 128, 512, 512, 256)
_CACHE_WIDTHS = (256, 256, 128, 128, 128, 128, MLA_KV_LORA, MLA_ROPE)
_CACHE_BLOCKS = tuple((w, SEQ) for w in _CACHE_WIDTHS)


def _zeros_kernel(*outs):
    for o in outs:
        o[...] = jnp.zeros(o.shape, F32)


def _zero_caches():
    return pl.pallas_call(
        _zeros_kernel,
        grid=(BATCH,),
        in_specs=[],
        out_specs=[pl.BlockSpec((None, DEPTH) + blk, lambda b: (b, 0, 0, 0)) for blk in _CACHE_BLOCKS],
        out_shape=[jax.ShapeDtypeStruct((BATCH, DEPTH) + blk, F32) for blk in _CACHE_BLOCKS],
        compiler_params=_cparams(("parallel",)),
        name="zero_caches",
    )()


TM_IN = 2 * TM
IN_STEPS = N_TOK // TM_IN
IN_CTX_STEPS = N_CTX // TM_IN


def _inproj_kernel(xc_ref, xl_ref, mod_ref, n1_ref, wt_hbm, wukv_ref, g_ref, bd_ref, rope_ref, *rest, layer):
    n_out = len(_QKV_WIDTHS) + len(_CACHE_WIDTHS)
    outs = rest[len(_CACHE_WIDTHS):len(_CACHE_WIDTHS) + n_out]
    w_ref, stage, sems = rest[len(_CACHE_WIDTHS) + n_out:]

    @pl.when(pl.program_id(0) == 0)
    def _():
        _load_transposed_bf16(wt_hbm, layer, QKV_BLOCK_STARTS, _qkv_block_rows, w_ref, stage, sems)

    is_ctx = pl.program_id(0) >= IN_STEPS - IN_CTX_STEPS
    units = []
    for half in range(TM_IN // TM):
        rows = slice(half * TM, (half + 1) * TM)
        views = [o.at[rows] for o in outs[:len(_QKV_WIDTHS)]] + [o.at[half] for o in outs[len(_QKV_WIDTHS):]]
        units += _inproj_half(is_ctx, xc_ref.at[rows], xl_ref.at[rows], mod_ref, n1_ref, w_ref, wukv_ref, g_ref,
                              bd_ref, rope_ref.at[:, rows, :], views)
    _run_units(units, depth=2)


def _inproj_half(is_ctx, xc_ref, xl_ref, mod_ref, n1_ref, w_ref, wukv_ref, g_ref, bd_ref, rope_ref, outs):
    qa_o, ka_o, va_o, qb_o, kb_o, vb_o, qc_o, kc_o, vc_o, qd_o, kd_o, vd_o = outs[:len(_QKV_WIDTHS)]
    natk_o, natv_o, gqak_o, gqav_o, wink_o, winv_o, ckv_o, kpe_o = outs[len(_QKV_WIDTHS):]
    x = jnp.where(is_ctx, xc_ref[...], xl_ref[...])
    mod = mod_ref[...]
    hb = _rms_mod(x, n1_ref[...], mod[:, D_MODEL:2 * D_MODEL], mod[:, 0:D_MODEL]).astype(BF16)
    bd64, bd128 = bd_ref[0], bd_ref[1]
    cos64, slo64, shi64 = rope_ref[0], rope_ref[1], rope_ref[2]
    cosm, slom, shim = rope_ref[3], rope_ref[4], rope_ref[5]
    inv64, inv96 = 1.0 / HEAD_DIM, 1.0 / MLA_QK
    low = _low_half((TM, LANES))
    held = {}

    def chunks(t):
        return [t[:, c:c + LANES] for c in range(0, t.shape[1], LANES)]

    def rope64(t):
        parts = [_rope(p, cos64, slo64, shi64, 16) for p in chunks(t)]
        return parts[0] if len(parts) == 1 else jnp.concatenate(parts, axis=1)

    def ropem(t):
        return jnp.concatenate([_rope(p, cosm, slom, shim, 8) for p in chunks(t)], axis=1)

    def store_q(ref, q, grouped):
        q = q * Q_SCALE
        slots = []
        for hd in range(4):
            part, src = q[:, (hd // 2) * LANES:(hd // 2 + 1) * LANES], hd % 2
            dst = hd // 2 if grouped else src
            if src != dst:
                part = pltpu.roll(part, HEAD_DIM, 1)
            slots.append(jnp.where(low if dst == 0 else ~low, part, 0.0))
        ref[...] = jnp.concatenate(slots, axis=1).astype(BF16)

    def post_latent(t):
        ckv = _group_norm(t[:, :LANES], bd128[:LANES, :LANES], 1.0 / MLA_KV_LORA, g_ref[8:9, :LANES])
        kpe = t[:, LANES:]
        ckv_o[...] = ckv.T
        kpe_o[...] = kpe.T[MLA_NOPE:MLA_QK, :]
        held["kpe"] = kpe
        held["kv"] = _dot(ckv.astype(BF16), wukv_ref[...])

    def post_kd(_):
        kv, kpe = held["kv"], held["kpe"]
        for j in range(2):
            kf = jnp.concatenate([kv[:, (2 * j) * LANES:(2 * j + 1) * LANES] + kpe,
                                  kv[:, (2 * j + 1) * LANES:(2 * j + 2) * LANES] + kpe], axis=1)
            kd_o[:, 2 * j * LANES:(2 * j + 2) * LANES] = ropem(
                _group_norm(kf, bd128, inv96, g_ref[7:8, :])).astype(BF16)
        vd_o[...] = kv[:, 4 * LANES:].astype(BF16)

    def post_qa(t):
        store_q(qa_o, _group_norm(t, bd64, inv64, g_ref[0:1, :]), grouped=False)

    def post_ka(t):
        k = _group_norm(t, bd64, inv64, g_ref[1:2, :])
        natk_o[...] = k.T
        ka_o[...] = k.astype(BF16)

    def post_va(t):
        natv_o[...] = t.T
        va_o[...] = t.astype(BF16)

    def post_q_rope(ref, row):
        def post(t):
            store_q(ref, rope64(_group_norm(t, bd64, inv64, g_ref[row:row + 1, :])), grouped=True)
        return post

    def post_kv(k_ref, v_ref, kc_ref, vc_ref, row):
        def post(t):
            k = _group_norm(t[:, :LANES], bd64[:LANES, :LANES], inv64, g_ref[row:row + 1, :LANES])
            kc_ref[...] = k.T
            k_ref[...] = rope64(k).astype(BF16)
            v = t[:, LANES:]
            vc_ref[...] = v.T
            v_ref[...] = v.astype(BF16)
        return post

    def post_qd(j):
        def post(t):
            qd_o[:, 2 * j * LANES:(2 * j + 2) * LANES] = (ropem(
                _group_norm(t, bd128, inv96, g_ref[6:7, :])) * MLA_Q_SCALE).astype(BF16)
        return post

    stages = [
        ((C_CKV, C_END), post_latent),
        ((C_QA, C_KA), post_qa),
        (None, post_kd),
        ((C_QB, C_KB), post_q_rope(qb_o, 2)),
        ((C_QC, C_KC), post_q_rope(qc_o, 4)),
        ((C_QD, C_QD + 2 * LANES), post_qd(0)),
        ((C_QD + 2 * LANES, C_CKV), post_qd(1)),
        ((C_KB, C_QC), post_kv(kb_o, vb_o, gqak_o, gqav_o, 3)),
        ((C_KC, C_QD), post_kv(kc_o, vc_o, wink_o, winv_o, 5)),
        ((C_KA, C_VA), post_ka),
        ((C_VA, C_QB), post_va),
    ]

    def issue(cols):
        return None if cols is None else _dot(hb, w_ref[:, cols[0]:cols[1]])

    return [(functools.partial(issue, cols), post) for cols, post in stages]


def _inproj(l, x_ctx, x_lat, mods, norm1, w_in_t, w_ukv_p, gains, bd, rope_tab, caches):
    n_lat = IN_STEPS - IN_CTX_STEPS
    blk = lambda s: (s + IN_CTX_STEPS) % IN_STEPS
    lat_blk = lambda s: jnp.maximum(blk(s) - IN_CTX_STEPS, 0)
    per_batch = DEC_SEQ // TM_IN
    tok = lambda w: pl.BlockSpec((TM_IN, w), lambda s: (blk(s), 0))
    in_specs = [
        pl.BlockSpec((TM_IN, D_MODEL), lambda s: (jnp.minimum(blk(s), IN_CTX_STEPS - 1), 0)),
        pl.BlockSpec((TM_IN, D_MODEL), lambda s: (lat_blk(s), 0)),
        pl.BlockSpec((None, None, 1, 6 * D_MODEL),
                     lambda s: (l, jnp.where(s < n_lat, 1 + lat_blk(s) // per_batch, 0), 0, 0)),
        _resident((None, 1, D_MODEL), lambda s: (l, 0, 0)),
        pl.BlockSpec(memory_space=pl.ANY),
        _resident((None, MLA_KV_LORA, 768), lambda s: (l, 0, 0)),
        _resident((None, 16, 256), lambda s: (l, 0, 0)),
        _resident((2, 256, 256), lambda s: (0, 0, 0)),
        pl.BlockSpec((6, TM_IN, LANES), lambda s: (0, jnp.where(s < n_lat, 1 + lat_blk(s) % per_batch, 0), 0)),
    ]
    args = [x_ctx, x_lat, mods, norm1, w_in_t, w_ukv_p, gains, bd, rope_tab] + list(caches)
    aliases = {len(in_specs) + n: len(_QKV_WIDTHS) + n for n in range(len(caches))}
    in_specs = in_specs + [pl.BlockSpec(memory_space=pl.ANY)] * len(caches)
    cache_spec = lambda cb: pl.BlockSpec((TM_IN // SEQ, None) + cb, lambda s: (jnp.maximum(s - n_lat, 0), l, 0, 0))
    outs = pl.pallas_call(
        functools.partial(_inproj_kernel, layer=l),
        grid=(IN_STEPS,),
        in_specs=in_specs,
        out_specs=[tok(w) for w in _QKV_WIDTHS] + [cache_spec(blk) for blk in _CACHE_BLOCKS],
        out_shape=[jax.ShapeDtypeStruct((N_TOK, w), BF16) for w in _QKV_WIDTHS]
        + [jax.ShapeDtypeStruct((BATCH, DEPTH) + blk, F32) for blk in _CACHE_BLOCKS],
        scratch_shapes=[pltpu.VMEM((D_MODEL, C_END), BF16), pltpu.VMEM((2, WT_ROWS, D_MODEL), F32),
                        pltpu.SemaphoreType.DMA((2,))],
        input_output_aliases=aliases,
        compiler_params=_cparams(("arbitrary",)),
        name=f"inproj_l{l}",
    )(*args)
    return outs[:len(_QKV_WIDTHS)], outs[len(_QKV_WIDTHS):]


def _scores(q, keys):
    out = []
    for k, bias in keys:
        s = _dot_t(q, k)
        if bias is not None:
            s = s + bias
        out.append(s)
    return out


def _softmax_pv(scores, values, sink=None):
    m = functools.reduce(jnp.maximum, [jnp.max(s, axis=-1, keepdims=True) for s in scores])
    if sink is not None:
        m = jnp.maximum(m, sink)
    denom = None
    out = None
    for s, v in zip(scores, values):
        e = jnp.exp2(s - m)
        d = jnp.sum(e, axis=-1, keepdims=True)
        o = _dot(e.astype(BF16), v)
        denom = d if denom is None else denom + d
        out = o if out is None else out + o
    if sink is not None:
        denom = denom + jnp.exp2(sink - m)
    return out / denom


def _pair_unit(q_ref, rows, j, keys, values, kv_half, sinks=None):
    m_rows = rows.stop - rows.start

    def issue():
        q2 = jnp.concatenate([q_ref[rows, (2 * j) * LANES:(2 * j + 1) * LANES],
                              q_ref[rows, (2 * j + 1) * LANES:(2 * j + 2) * LANES]], axis=0)
        return _scores(q2, keys())

    def finish(scores):
        sink = None
        if sinks is not None:
            row = lax.broadcasted_iota(jnp.int32, (2 * m_rows, 1), 0)
            sink = jnp.where(row < m_rows, sinks[0], sinks[1])
        o2 = _softmax_pv(scores, values(), sink)
        lo, hi = o2[:m_rows], o2[m_rows:]
        if kv_half == 0:
            hi = pltpu.roll(hi, HEAD_DIM, 1)
        elif kv_half == 1:
            lo = pltpu.roll(lo, HEAD_DIM, 1)
        return jnp.where(_low_half(lo.shape), lo, hi)

    return issue, finish


def _single_unit(q_ref, rows, hd, keys, values):
    def issue():
        return _scores(q_ref[rows, hd * LANES:(hd + 1) * LANES], keys())

    def finish(scores):
        return _softmax_pv(scores, values())

    return issue, finish


def _merge_halves(o_even, o_odd):
    return jnp.where(_low_half(o_even.shape), o_even, o_odd)


CTX_PER_STEP = 2
CTX_UNITS = 10


def _ctx_attn_kernel(sink_ref, qa, ka, va, qb, kb, vb, qc, kc, vc, qd, kd, vd, o_ref):
    units = []
    for e in range(CTX_PER_STEP):
        rows = slice(e * SEQ, (e + 1) * SEQ)
        for j in range(2):
            c = slice(j * LANES, (j + 1) * LANES)
            units.append(_pair_unit(qa, rows, j, lambda r=rows, c=c: [(ka[r, c], None)],
                                    lambda r=rows, c=c: [va[r, c]], None))
        for q, k, v, with_sink in ((qb, kb, vb, False), (qc, kc, vc, True)):
            for j in range(2):
                sinks = (sink_ref[2 * j] * LOG2E, sink_ref[2 * j + 1] * LOG2E) if with_sink else None
                units.append(_pair_unit(q, rows, j, lambda r=rows, k=k: [(k[r, :], None)],
                                        lambda r=rows, v=v: [v[r, :]], j, sinks))
        for hd in range(4):
            c = slice(hd * LANES, (hd + 1) * LANES)
            vs = slice((hd // 2) * LANES, (hd // 2 + 1) * LANES)
            units.append(_single_unit(qd, rows, hd, lambda r=rows, c=c: [(kd[r, c], None)],
                                      lambda r=rows, vs=vs: [vd[r, vs]]))
    outs = _run_units(units)
    for e in range(CTX_PER_STEP):
        rows = slice(e * SEQ, (e + 1) * SEQ)
        seq = outs[e * CTX_UNITS:(e + 1) * CTX_UNITS]
        for n in range(6):
            o_ref[rows, n * LANES:(n + 1) * LANES] = seq[n].astype(BF16)
        for j in range(2):
            o_ref[rows, (6 + j) * LANES:(7 + j) * LANES] = _merge_halves(seq[6 + 2 * j], seq[7 + 2 * j]).astype(BF16)


def _ctx_attention(l, sink_c, qkv):
    blk = lambda w: pl.BlockSpec((CTX_PER_STEP * SEQ, w), lambda b: (b, 0))
    return pl.pallas_call(
        _ctx_attn_kernel,
        grid=(BATCH // CTX_PER_STEP,),
        in_specs=[pl.BlockSpec(memory_space=pltpu.SMEM)] + [blk(w) for w in _QKV_WIDTHS],
        out_specs=pl.BlockSpec((CTX_PER_STEP * SEQ, D_MODEL), lambda b: (b, 0)),
        out_shape=jax.ShapeDtypeStruct((N_CTX, D_MODEL), BF16),
        compiler_params=_cparams(("parallel",)),
        name=f"ctx_attn_l{l}",
    )(sink_c, *qkv)


LAT_TQ = 256
LAT_SUB = 128
LAT_QBLOCKS = DEC_SEQ // LAT_TQ
LAT_ROW0 = N_CTX // LAT_TQ
LAT_KV0 = N_CTX // DEC_SEQ


def _lat_qspec(width):
    return pl.BlockSpec((LAT_TQ, width), lambda b, i: (LAT_ROW0 + b * LAT_QBLOCKS + i, 0))


def _lat_kvspec(width):
    return pl.BlockSpec((DEC_SEQ, width), lambda b, i: (LAT_KV0 + b, 0))


def _lat_call(kernel, l, name, in_specs, args):
    return pl.pallas_call(
        kernel,
        grid=(DEC_BATCH, LAT_QBLOCKS),
        in_specs=in_specs,
        out_specs=pl.BlockSpec((LAT_TQ, 256), lambda b, i: (b * LAT_QBLOCKS + i, 0)),
        out_shape=jax.ShapeDtypeStruct((N_LAT, 256), BF16),
        compiler_params=_cparams(("parallel", "parallel")),
        name=f"{name}_l{l}",
    )(*args)


def _sub_rows(u):
    return slice(u * LAT_SUB, (u + 1) * LAT_SUB)


def _store_units(o_ref, outs):
    for n, o in enumerate(outs):
        u, j = divmod(n, 2)
        o_ref[_sub_rows(u), j * LANES:(j + 1) * LANES] = o.astype(BF16)


def _gqa_kernel(q_ref, k_ref, v_ref, kc_ref, vc_ref, o_ref):
    keys = lambda: [(kc_ref[...].astype(BF16), None), (k_ref[...], None)]
    values = lambda: [vc_ref[...].astype(BF16), v_ref[...]]
    units = [_pair_unit(q_ref, _sub_rows(u), j, keys, values, j) for u in range(LAT_TQ // LAT_SUB) for j in range(2)]
    _store_units(o_ref, _run_units(units))


def _lat_gqa(l, qb, kb, vb, cache_k, cache_v):
    cspec = pl.BlockSpec((None, None, PAST_LEN, LANES), lambda b, i: (b, l, 0, 0))
    return _lat_call(_gqa_kernel, l, "lat_gqa",
                     [_lat_qspec(512), _lat_kvspec(LANES), _lat_kvspec(LANES), cspec, cspec],
                     (qb, kb, vb, cache_k, cache_v))


WIN_KEYS = 3 * LAT_SUB


def _win_kernel(sink_ref, q_ref, k_ref, v_ref, kc_ref, vc_ref, o_ref):
    units = []
    for u in range(LAT_TQ // LAT_SUB):
        blk = pl.program_id(1) * (LAT_TQ // LAT_SUB) + u
        start = pl.multiple_of(jnp.clip((blk - 1) * LAT_SUB, 0, DEC_SEQ - WIN_KEYS), LAT_SUB)

        def keys(blk=blk, start=start):
            row = lax.broadcasted_iota(jnp.int32, (2 * LAT_SUB, WIN_KEYS), 0)
            qpos = blk * LAT_SUB + jnp.where(row < LAT_SUB, row, row - LAT_SUB)
            kpos = start + lax.broadcasted_iota(jnp.int32, (2 * LAT_SUB, WIN_KEYS), 1)
            bias = jnp.where(jnp.abs(qpos - kpos) <= WINDOW, 0.0, NEG_INF)
            return [(kc_ref[...].astype(BF16), None), (k_ref[pl.ds(start, WIN_KEYS), :], bias)]

        def values(start=start):
            return [vc_ref[...].astype(BF16), v_ref[pl.ds(start, WIN_KEYS), :]]

        for j in range(2):
            sinks = (sink_ref[2 * j] * LOG2E, sink_ref[2 * j + 1] * LOG2E)
            units.append(_pair_unit(q_ref, _sub_rows(u), j, keys, values, j, sinks))
    _store_units(o_ref, _run_units(units, depth=4))


def _lat_window(l, sink_c, qc, kc, vc, cache_k, cache_v):
    cspec = pl.BlockSpec((None, None, PAST_LEN, LANES), lambda b, i: (b, l, 0, 0))
    return _lat_call(_win_kernel, l, "lat_win",
                     [pl.BlockSpec(memory_space=pltpu.SMEM), _lat_qspec(512), _lat_kvspec(LANES),
                      _lat_kvspec(LANES), cspec, cspec],
                     (sink_c, qc, kc, vc, cache_k, cache_v))


def _mla_kernel(q_ref, k_ref, v_ref, kc_ref, vc_ref, o_ref):
    units = []
    rows = slice(0, LAT_TQ)
    for hd in range(4):
        c = slice(hd * LANES, (hd + 1) * LANES)
        vs = slice((hd // 2) * LANES, (hd // 2 + 1) * LANES)
        units.append(_single_unit(q_ref, rows, hd,
                                  lambda c=c: [(kc_ref[:, c], None), (k_ref[:, c], None)],
                                  lambda vs=vs: [vc_ref[:, vs], v_ref[:, vs]]))
    outs = _run_units(units, depth=3)
    for j in range(2):
        o_ref[:, j * LANES:(j + 1) * LANES] = _merge_halves(outs[2 * j], outs[2 * j + 1]).astype(BF16)


def _lat_mla(l, qd, kd, vd, kd_ctx, vd_ctx):
    kcs = pl.BlockSpec((None, None, PAST_LEN, 512), lambda b, i: (l, b, 0, 0))
    vcs = pl.BlockSpec((None, None, PAST_LEN, 256), lambda b, i: (l, b, 0, 0))
    return _lat_call(_mla_kernel, l, "lat_mla",
                     [_lat_qspec(512), _lat_kvspec(512), _lat_kvspec(256), kcs, vcs],
                     (qd, kd, vd, kd_ctx, vd_ctx))


NA_KEYS = NA_ROWS * GRID_W
NA_ROWS_PER_STEP = LAT_TQ // GRID_W


def _na_shift(r):
    return jnp.where(r < NA_ROWS // 2, r,
                     jnp.where(r > GRID_ROWS - NA_ROWS // 2, r - (GRID_ROWS - NA_ROWS), NA_ROWS // 2))


def _na_kernel(q_ref, k_ref, v_ref, kc_ref, vc_ref, bias_ref, o_ref):
    units = []
    for rr in range(NA_ROWS_PER_STEP):
        r = pl.program_id(1) * NA_ROWS_PER_STEP + rr
        start = pl.multiple_of(jnp.clip(r - NA_ROWS // 2, 0, GRID_ROWS - NA_ROWS) * GRID_W, GRID_W)
        shift = _na_shift(r)
        rows = slice(rr * GRID_W, (rr + 1) * GRID_W)
        for j in range(2):
            c = slice(j * LANES, (j + 1) * LANES)

            def keys(j=j, c=c, start=start, shift=shift):
                bias = jnp.concatenate([bias_ref[2 * j, shift], bias_ref[2 * j + 1, shift]], axis=0)
                return [(kc_ref[:, c].astype(BF16), None), (k_ref[pl.ds(start, NA_KEYS), c], bias)]

            def values(c=c, start=start):
                return [vc_ref[:, c].astype(BF16), v_ref[pl.ds(start, NA_KEYS), c]]

            units.append(_pair_unit(q_ref, rows, j, keys, values, None))
    for n, o in enumerate(_run_units(units, depth=4)):
        rr, j = divmod(n, 2)
        o_ref[rr * GRID_W:(rr + 1) * GRID_W, j * LANES:(j + 1) * LANES] = o.astype(BF16)


def _lat_na(l, qa, ka, va, cache_k, cache_v, na_bias):
    cspec = pl.BlockSpec((None, None, PAST_LEN, 256), lambda b, i: (b, l, 0, 0))
    bspec = _resident((None, 4, NA_ROWS, GRID_W, NA_KEYS), lambda b, i: (l, 0, 0, 0, 0))
    return _lat_call(_na_kernel, l, "lat_na",
                     [_lat_qspec(512), _lat_kvspec(256), _lat_kvspec(256), cspec, cspec, bspec],
                     (qa, ka, va, cache_k, cache_v, na_bias))


def _na_bias_kernel(rpb_ref, out_ref, t_ref):
    l = pl.program_id(0)
    n_dr, n_dc = 2 * NA_ROWS - 1, 2 * NA_COLS - 1
    lane = _lane_iota((GRID_W, LANES))
    lane8 = _lane_iota((HALO, LANES))
    sub8 = lax.broadcasted_iota(jnp.int32, (HALO, LANES), 0)
    clamp = lambda d: jnp.clip(d, 1 - NA_COLS, NA_COLS - 1) + (NA_COLS - 1)
    idx_lo = clamp(jnp.where(lane8 < GRID_W, lane8, lane8 - LANES))
    idx_hi = clamp(lane8 - GRID_W)
    kc = lane8 & (GRID_W - 1)
    for hd in range(4):
        def fill(a, carry):
            base = ((l * 4 + hd) * n_dr + a) * n_dc
            g_lo = jnp.zeros((HALO, LANES), F32)
            g_hi = jnp.zeros((HALO, LANES), F32)
            for b in range(n_dc):
                g_lo = jnp.where(idx_lo == b, rpb_ref[base + b], g_lo)
                g_hi = jnp.where(idx_hi == b, rpb_ref[base + b], g_hi)
            for v in range(GRID_W // HALO):
                t = jnp.where(lane8 < GRID_W,
                              pltpu.roll(g_lo, HALO * v, 1, stride=1, stride_axis=0),
                              pltpu.roll(g_hi, HALO * v, 1, stride=1, stride_axis=0))
                c0 = jnp.clip(sub8 + (HALO * v - NA_COLS // 2), 0, GRID_W - NA_COLS)
                ok = (kc >= c0) & (kc < c0 + NA_COLS)
                t_ref[a, HALO * v:HALO * (v + 1), :] = jnp.where(ok, t * LOG2E, NEG_INF)
            return carry
        lax.fori_loop(0, n_dr, fill, 0, unroll=True)
        for d in range(NA_ROWS):
            for m in range(NA_ROWS // 2):
                lo = t_ref[2 * m - d + NA_ROWS - 1]
                hi = t_ref[2 * m + 1 - d + NA_ROWS - 1]
                out_ref[hd, d, :, m * LANES:(m + 1) * LANES] = jnp.where(lane < GRID_W, lo, hi)


def _na_bias(rpb_flat):
    return pl.pallas_call(
        _na_bias_kernel,
        grid=(DEPTH,),
        in_specs=[pl.BlockSpec(memory_space=pltpu.SMEM)],
        out_specs=pl.BlockSpec((None, 4, NA_ROWS, GRID_W, NA_KEYS), lambda l: (l, 0, 0, 0, 0)),
        out_shape=jax.ShapeDtypeStruct((DEPTH, 4, NA_ROWS, GRID_W, NA_KEYS), F32),
        scratch_shapes=[pltpu.VMEM((2 * NA_ROWS - 1, GRID_W, LANES), F32)],
        compiler_params=_cparams(("parallel",)),
        name="na_bias",
    )(rpb_flat)


def _mla_ctx_kernel(ckv_ref, kpe_ref, wukv_ref, g_ref, bd_ref, kd_o, vd_o):
    kv = _dot(ckv_ref[...].astype(BF16), wukv_ref[...])
    kpe = kpe_ref[...]
    for j in range(2):
        kf = jnp.concatenate([kv[:, (2 * j) * LANES:(2 * j + 1) * LANES] + kpe,
                              kv[:, (2 * j + 1) * LANES:(2 * j + 2) * LANES] + kpe], axis=1)
        kd_o[:, 2 * j * LANES:(2 * j + 2) * LANES] = _group_norm(
            kf, bd_ref[1], 1.0 / MLA_QK, g_ref[7:8, :]).astype(BF16)
    vd_o[...] = kv[:, 4 * LANES:].astype(BF16)


def _mla_ctx(cache_ckv, cache_kpe_p, w_ukv_p, gains, bd):
    return pl.pallas_call(
        _mla_ctx_kernel,
        grid=(DEPTH, DEC_BATCH),
        in_specs=[
            pl.BlockSpec((None, None, PAST_LEN, LANES), lambda l, b: (b, l, 0, 0)),
            pl.BlockSpec((None, None, PAST_LEN, LANES), lambda l, b: (b, l, 0, 0)),
            pl.BlockSpec((None, MLA_KV_LORA, 768), lambda l, b: (l, 0, 0)),
            pl.BlockSpec((None, 16, 256), lambda l, b: (l, 0, 0)),
            pl.BlockSpec((2, 256, 256), lambda l, b: (0, 0, 0)),
        ],
        out_specs=[pl.BlockSpec((None, None, PAST_LEN, 512), lambda l, b: (l, b, 0, 0)),
                   pl.BlockSpec((None, None, PAST_LEN, 256), lambda l, b: (l, b, 0, 0))],
        out_shape=[jax.ShapeDtypeStruct((DEPTH, DEC_BATCH, PAST_LEN, 512), BF16),
                   jax.ShapeDtypeStruct((DEPTH, DEC_BATCH, PAST_LEN, 256), BF16)],
        compiler_params=_cparams(("parallel", "parallel")),
        name="mla_ctx",
    )(cache_ckv, cache_kpe_p, w_ukv_p, gains, bd)


def _merge_kernel(xc_ref, xl_ref, oc_ref, oa_ref, ob_ref, occ_ref, od_ref, mod_ref, n1_ref, wt_hbm, wb_hbm, wo_hbm,
                  x1_ref, wg_ref, wb_ref, wo_ref, stage, sems, *, layer):
    @pl.when(pl.program_id(0) == 0)
    def _():
        _load_transposed_bf16(wt_hbm, layer, GATE_BLOCK_STARTS, lambda b, w: w, wg_ref, stage, sems)
        _load_rows_bf16(wb_hbm, layer, wb_ref, stage, sems)
        _load_rows_bf16(wo_hbm, layer, wo_ref, stage, sems)

    is_ctx = pl.program_id(0) < IN_CTX_STEPS
    mod = mod_ref[...]
    lat_refs = (oa_ref, ob_ref, occ_ref, od_ref)

    def half_units(half):
        rows = slice(half * TM, (half + 1) * TM)
        st = {"merged": None}
        st["x"] = jnp.where(is_ctx, xc_ref[rows, :], xl_ref[rows, :])
        st["hb"] = _rms_mod(st["x"], n1_ref[...], mod[:, D_MODEL:2 * D_MODEL], mod[:, 0:D_MODEL]).astype(BF16)

        def branch(n):
            def issue():
                o = jnp.where(is_ctx, oc_ref[rows, n * 256:(n + 1) * 256], lat_refs[n][rows, :])
                return (_dot(st["hb"], wg_ref[:, n * D_MODEL:(n + 1) * D_MODEL]),
                        _dot(o, wb_ref[n * 256:(n + 1) * 256, :]))

            def finish(res):
                gate, br = res
                t = (1.0 / (1.0 + jnp.exp(-gate))) * br
                st["merged"] = t if st["merged"] is None else st["merged"] + t
            return issue, finish

        def out_issue():
            return _dot(st["merged"].astype(BF16), wo_ref[...])

        def out_finish(out):
            x1_ref[rows, :] = st["x"] + mod[:, 2 * D_MODEL:3 * D_MODEL] * out

        return [branch(n) for n in range(4)], (out_issue, out_finish)

    (a_br, a_out), (b_br, b_out) = half_units(0), half_units(1)
    idle = (lambda: None, lambda _: None)
    _run_units(a_br + b_br[:2] + [a_out] + b_br[2:] + [idle, b_out], depth=2)


def _merge(l, x_ctx, x_lat, o_ctx, o_lat, mods, norm1, w_in_t, w_branch, w_out):
    per_batch = DEC_SEQ // TM_IN
    ctx_blk = lambda i: jnp.minimum(i, IN_CTX_STEPS - 1)
    lat_blk = lambda i: jnp.maximum(i - IN_CTX_STEPS, 0)
    tok = pl.BlockSpec((TM_IN, D_MODEL), lambda i: (i, 0))
    ctx_tok = pl.BlockSpec((TM_IN, D_MODEL), lambda i: (ctx_blk(i), 0))
    lat_tok = lambda w: pl.BlockSpec((TM_IN, w), lambda i: (lat_blk(i), 0))
    return pl.pallas_call(
        functools.partial(_merge_kernel, layer=l),
        grid=(IN_STEPS,),
        in_specs=[
            ctx_tok, lat_tok(D_MODEL), ctx_tok, lat_tok(256), lat_tok(256), lat_tok(256), lat_tok(256),
            pl.BlockSpec((None, None, 1, 6 * D_MODEL),
                         lambda i: (l, jnp.where(i < IN_CTX_STEPS, 0, 1 + lat_blk(i) // per_batch), 0, 0)),
            _resident((None, 1, D_MODEL), lambda i: (l, 0, 0)),
            pl.BlockSpec(memory_space=pl.ANY),
            pl.BlockSpec(memory_space=pl.ANY),
            pl.BlockSpec(memory_space=pl.ANY),
        ],
        out_specs=tok,
        out_shape=jax.ShapeDtypeStruct((N_TOK, D_MODEL), F32),
        scratch_shapes=[pltpu.VMEM((D_MODEL, N_GATE), BF16), pltpu.VMEM((D_MODEL, D_MODEL), BF16),
                        pltpu.VMEM((D_MODEL, D_MODEL), BF16), pltpu.VMEM((2, WT_ROWS, D_MODEL), F32),
                        pltpu.SemaphoreType.DMA((2,))],
        compiler_params=_cparams(("arbitrary",)),
        name=f"merge_l{l}",
    )(x_ctx, x_lat, o_ctx, *o_lat, mods, norm1, w_in_t, w_branch, w_out)


HALO = 8


def _sublane_transpose(tiles):
    sub = lax.broadcasted_iota(jnp.int32, tiles[0].shape, 0)
    cur = list(tiles)
    for d in (4, 2, 1):
        keep = (sub & d) == 0
        nxt = list(cur)
        for a in range(HALO):
            if a & d:
                continue
            nxt[a] = jnp.where(keep, cur[a], pltpu.roll(cur[a + d], d, 0))
            nxt[a + d] = jnp.where(keep, pltpu.roll(cur[a], HALO - d, 0), cur[a + d])
        cur = nxt
    return cur


ROW_TILES = TM // HALO
PERM_GROUPS = ROW_TILES // HALO


def _permute_rows(x, to_permuted):
    tile = lambda j: x[j * HALO:(j + 1) * HALO, :]
    out = [None] * ROW_TILES
    for m in range(PERM_GROUPS):
        nat = [m + PERM_GROUPS * s for s in range(HALO)]
        perm = [HALO * m + i for i in range(HALO)]
        src, dst = (nat, perm) if to_permuted else (perm, nat)
        for j, t in zip(dst, _sublane_transpose([tile(j) for j in src])):
            out[j] = t
    return jnp.concatenate(out, axis=0)


def _ffn_tile(x, x_prev, x_next, has_prev, has_next, mod, n2, wup_ref, cw_ref, cb_ref, wdn_ref, u_ref):
    sh2, sc2 = mod[:, 3 * D_MODEL:4 * D_MODEL], mod[:, 4 * D_MODEL:5 * D_MODEL]
    h_prev = jnp.where(has_prev, _rms_mod(x_prev, n2, sc2, sh2), 0.0)
    h_next = jnp.where(has_next, _rms_mod(x_next, n2, sc2, sh2), 0.0)
    hp = _permute_rows(_rms_mod(x, n2, sc2, sh2), True)
    lhs = jnp.concatenate([hp, h_prev, h_next], axis=0).astype(BF16)
    sub = lax.broadcasted_iota(jnp.int32, (HALO, FF_CHUNK), 0)

    def cols(c):
        return (slice(c * FF_CHUNK, (c + 1) * FF_CHUNK), slice(D_FF + c * FF_CHUNK, D_FF + (c + 1) * FF_CHUNK))

    def up(c):
        for half, cs in enumerate(cols(c)):
            lanes = slice(half * FF_CHUNK, (half + 1) * FF_CHUNK)
            u = _dot(lhs, wup_ref[:, cs])
            u_ref[c, HALO:HALO + TM, lanes] = u[:TM]
            prev_row, next_row = u[TM + HALO - 1:TM + HALO], u[TM + HALO:TM + HALO + 1]
            u_ref[c, 0:HALO, lanes] = jnp.where(sub == 0, prev_row, pltpu.roll(u[TM - HALO:TM], 1, 0))
            u_ref[c, HALO + TM:, lanes] = jnp.where(sub == HALO - 1, next_row, pltpu.roll(u[0:HALO], HALO - 1, 0))

    def both(ref, c):
        va, vg = cols(c)
        return jnp.concatenate([ref[:, va], ref[:, vg]], axis=1)

    state = {"acc": None}

    def chunk(c, between):
        cw = both(cw_ref, c)
        t = (both(cb_ref, c) + u_ref[c, 0:TM, :] * cw[0:1] + u_ref[c, HALO:HALO + TM, :] * cw[1:2]
             + u_ref[c, 2 * HALO:2 * HALO + TM, :] * cw[2:3])
        a, g = t[:, :FF_CHUNK], t[:, FF_CHUNK:]
        act = ((g * (1.0 / (1.0 + jnp.exp(-g)))) * a).astype(BF16)
        between()
        d = _dot(act, wdn_ref[c * FF_CHUNK:(c + 1) * FF_CHUNK, :])
        state["acc"] = d if state["acc"] is None else state["acc"] + d

    def finish():
        return x + mod[:, 5 * D_MODEL:6 * D_MODEL] * _permute_rows(state["acc"], False)

    return up, chunk, finish


W_BLOCKS = N_FF_CHUNKS
WUP_BLOCK_COLS = 2 * D_FF // W_BLOCKS
WDN_BLOCK_ROWS = D_FF // W_BLOCKS


def _load_bf16_weights(layer, wup_hbm, wdn_hbm, wup_ref, wdn_ref, up_stage, dn_stage, sems):
    def up_copy(b):
        return pltpu.make_async_copy(wup_hbm.at[layer, :, pl.ds(b * WUP_BLOCK_COLS, WUP_BLOCK_COLS)],
                                     up_stage.at[b % 2], sems.at[0, b % 2])

    def dn_copy(b):
        return pltpu.make_async_copy(wdn_hbm.at[layer, pl.ds(b * WDN_BLOCK_ROWS, WDN_BLOCK_ROWS), :],
                                     dn_stage.at[b % 2], sems.at[1, b % 2])

    up_copy(0).start()
    dn_copy(0).start()
    for b in range(W_BLOCKS):
        if b + 1 < W_BLOCKS:
            up_copy(b + 1).start()
            dn_copy(b + 1).start()
        up_copy(b).wait()
        wup_ref[:, b * WUP_BLOCK_COLS:(b + 1) * WUP_BLOCK_COLS] = up_stage[b % 2].astype(BF16)
        dn_copy(b).wait()
        wdn_ref[b * WDN_BLOCK_ROWS:(b + 1) * WDN_BLOCK_ROWS, :] = dn_stage[b % 2].astype(BF16)


def _ffn_kernel(x_ref, xp_ref, xn_ref, mod_ref, n2_ref, wup_hbm, cw_ref, cb_ref, wdn_hbm, yc_ref, yl_ref,
                u_ref, wup_ref, wdn_ref, up_stage, dn_stage, sems, *, layer):
    i = pl.program_id(0)

    @pl.when(i == 0)
    def _():
        _load_bf16_weights(layer, wup_hbm, wdn_hbm, wup_ref, wdn_ref, up_stage, dn_stage, sems)

    latent = i >= IN_CTX_STEPS
    first = (i - IN_CTX_STEPS) % (DEC_SEQ // TM_IN) == 0
    last = (i - IN_CTX_STEPS) % (DEC_SEQ // TM_IN) == DEC_SEQ // TM_IN - 1
    mod = mod_ref[...]
    n2 = n2_ref[...]
    make = [
        lambda: _ffn_tile(x_ref[0:TM, :], xp_ref[...], x_ref[TM:TM + HALO, :], latent & ~first, latent, mod, n2,
                          wup_ref, cw_ref, cb_ref, wdn_ref, u_ref.at[0]),
        lambda: _ffn_tile(x_ref[TM:, :], x_ref[TM - HALO:TM, :], xn_ref[...], latent, latent & ~last, mod, n2,
                          wup_ref, cw_ref, cb_ref, wdn_ref, u_ref.at[1]),
    ]
    tiles = {}

    def tile(t):
        if t not in tiles:
            tiles[t] = make[t]()
        return tiles[t]

    tasks = [(t, c) for t in range(len(make)) for c in range(N_FF_CHUNKS)]
    depth = 3
    for t, c in tasks[:depth]:
        tile(t)[0](c)
    ys = []
    for n, (t, c) in enumerate(tasks):
        ahead = tasks[n + depth] if n + depth < len(tasks) else None
        tile(t)[1](c, (lambda a=ahead: tile(a[0])[0](a[1])) if ahead else (lambda: None))
        if c == N_FF_CHUNKS - 1:
            ys.append(tile(t)[2]())
    y = jnp.concatenate(ys, axis=0)

    @pl.when(i < IN_CTX_STEPS)
    def _():
        yc_ref[...] = y

    @pl.when(latent)
    def _():
        yl_ref[...] = y


def _ffn(l, x1, mods, norm2, w_up, conv_w, conv_b, w_down):
    per = TM_IN // HALO
    last = N_TOK // HALO - 1
    per_batch = DEC_SEQ // TM_IN
    lat_blk = lambda i: jnp.maximum(i - IN_CTX_STEPS, 0)
    return pl.pallas_call(
        functools.partial(_ffn_kernel, layer=l),
        grid=(IN_STEPS,),
        in_specs=[
            pl.BlockSpec((TM_IN, D_MODEL), lambda i: (i, 0)),
            pl.BlockSpec((HALO, D_MODEL), lambda i: (jnp.maximum(i * per - 1, 0), 0)),
            pl.BlockSpec((HALO, D_MODEL), lambda i: (jnp.minimum((i + 1) * per, last), 0)),
            pl.BlockSpec((None, None, 1, 6 * D_MODEL),
                         lambda i: (l, jnp.where(i < IN_CTX_STEPS, 0, 1 + lat_blk(i) // per_batch), 0, 0)),
            _resident((None, 1, D_MODEL), lambda i: (l, 0, 0)),
            pl.BlockSpec(memory_space=pl.ANY),
            _resident((None, 3, 2 * D_FF), lambda i: (l, 0, 0)),
            _resident((None, 1, 2 * D_FF), lambda i: (l, 0, 0)),
            pl.BlockSpec(memory_space=pl.ANY),
        ],
        out_specs=[pl.BlockSpec((TM_IN, D_MODEL), lambda i: (jnp.minimum(i, IN_CTX_STEPS - 1), 0)),
                   pl.BlockSpec((TM_IN, D_MODEL), lambda i: (lat_blk(i), 0))],
        out_shape=[jax.ShapeDtypeStruct((N_CTX, D_MODEL), F32), jax.ShapeDtypeStruct((N_LAT, D_MODEL), F32)],
        scratch_shapes=[
            pltpu.VMEM((TM_IN // TM, N_FF_CHUNKS, TM + 2 * HALO, 2 * FF_CHUNK), F32),
            pltpu.VMEM((D_MODEL, 2 * D_FF), BF16),
            pltpu.VMEM((D_FF, D_MODEL), BF16),
            pltpu.VMEM((2, D_MODEL, WUP_BLOCK_COLS), F32),
            pltpu.VMEM((2, WDN_BLOCK_ROWS, D_MODEL), F32),
            pltpu.SemaphoreType.DMA((2, 2)),
        ],
        compiler_params=_cparams(("arbitrary",)),
        name=f"ffn_l{l}",
    )(x1, x1, x1, mods, norm2, w_up, conv_w, conv_b, w_down)


WT_ROWS = 512
_SRC_CKV = C_QD + 4 * MLA_QK
QKV_BLOCK_STARTS = (0, WT_ROWS, 2 * WT_ROWS, C_KC, C_QD + 2 * MLA_QK)
GATE_BLOCK_STARTS = tuple(IN_QKV + WT_ROWS * b for b in range(N_GATE // WT_ROWS))


def _qkv_block_rows(b, w):
    zeros = lambda n: jnp.zeros((n, D_MODEL), F32)
    pad = zeros(LANES - MLA_QK)

    def two_heads(r0):
        return [w[r0:r0 + MLA_QK], pad, w[r0 + MLA_QK:r0 + 2 * MLA_QK], pad]

    if QKV_BLOCK_STARTS[b] == C_KC:
        n_kv = C_QD - C_KC
        return jnp.concatenate([w[0:n_kv]] + two_heads(n_kv), axis=0)
    if QKV_BLOCK_STARTS[b] == C_QD + 2 * MLA_QK:
        r_ckv = _SRC_CKV - QKV_BLOCK_STARTS[b]
        r_kpe = r_ckv + MLA_KV_LORA
        return jnp.concatenate(two_heads(0) + [w[r_ckv:r_kpe], zeros(MLA_NOPE), w[r_kpe:r_kpe + MLA_ROPE], pad],
                               axis=0)
    return w


def _load_rows_bf16(w_hbm, layer, dst_ref, stage, sems):
    n_blocks = dst_ref.shape[0] // WT_ROWS

    def copy(b):
        return pltpu.make_async_copy(w_hbm.at[layer, pl.ds(b * WT_ROWS, WT_ROWS), :], stage.at[b % 2], sems.at[b % 2])

    copy(0).start()
    for b in range(n_blocks):
        if b + 1 < n_blocks:
            copy(b + 1).start()
        copy(b).wait()
        dst_ref[b * WT_ROWS:(b + 1) * WT_ROWS, :] = stage[b % 2].astype(BF16)


def _load_transposed_bf16(wt_hbm, layer, starts, select, dst_ref, stage, sems):
    def copy(b):
        return pltpu.make_async_copy(wt_hbm.at[layer, pl.ds(starts[b], WT_ROWS), :], stage.at[b % 2], sems.at[b % 2])

    copy(0).start()
    for b in range(len(starts)):
        if b + 1 < len(starts):
            copy(b + 1).start()
        copy(b).wait()
        dst_ref[:, b * WT_ROWS:(b + 1) * WT_ROWS] = select(b, stage[b % 2]).T.astype(BF16)


def _rope_tables():
    t = np.arange(DEC_SEQ)

    def angles(dim):
        half = dim // 2
        inv = ROPE_THETA ** (-np.arange(0, half, 2, dtype=np.float32) / half)
        row = (t // GRID_W).astype(np.float32)[:, None] * inv[None, :]
        col = (t % GRID_W).astype(np.float32)[:, None] * inv[None, :]
        return np.concatenate([row, row, col, col], axis=-1).astype(np.float32)

    def signed(ang, quarter):
        cos, sin = np.cos(ang), np.sin(ang)
        first = (np.arange(ang.shape[1]) % (2 * quarter)) < quarter
        return cos, np.where(first, -sin, 0.0), np.where(first, 0.0, sin)

    c64, lo64, hi64 = signed(angles(HEAD_DIM), 16)
    c64, lo64, hi64 = (np.tile(a, (1, 2)) for a in (c64, lo64, hi64))
    cm, lom, him = signed(angles(MLA_ROPE), 8)

    def mla_pad(a, fill):
        out = np.full((DEC_SEQ, LANES), fill, np.float32)
        out[:, MLA_NOPE:MLA_QK] = a
        return out

    tabs = [c64, lo64, hi64, mla_pad(cm, 1.0), mla_pad(lom, 0.0), mla_pad(him, 0.0)]
    ident = [np.ones, np.zeros, np.zeros, np.ones, np.zeros, np.zeros]
    full = [np.concatenate([f((TM_IN, LANES), np.float32), a.astype(np.float32)], axis=0) for f, a in zip(ident, tabs)]
    return jnp.asarray(np.stack(full, axis=0))


def _block_diag_ones():
    idx = np.arange(256)
    mats = [(idx[:, None] // w == idx[None, :] // w).astype(np.float32) for w in (HEAD_DIM, LANES)]
    return jnp.asarray(np.stack(mats, axis=0), dtype=BF16)


def _prep_small(w_ukv, qn_a, kn_a, qn_b, kn_b, qn_c, kn_c, qn_d, kn_d, kvn_d):
    ukv = w_ukv.reshape(DEPTH, MLA_KV_LORA, 4, 2, HEAD_DIM)
    k_part = jnp.pad(ukv[:, :, :, 0], ((0, 0), (0, 0), (0, 0), (0, HEAD_DIM))).reshape(DEPTH, MLA_KV_LORA, 512)
    v_part = ukv[:, :, :, 1].reshape(DEPTH, MLA_KV_LORA, 256)
    w_ukv_p = jnp.concatenate([k_part, v_part], axis=-1).astype(BF16)

    def row(g, reps, width=256):
        t = jnp.tile(g, (1, reps))
        return jnp.pad(t, ((0, 0), (0, width - t.shape[1])))

    pad96 = lambda g: jnp.pad(g, ((0, 0), (0, LANES - MLA_QK)))
    rows = [row(qn_a, 4), row(kn_a, 4), row(qn_b, 4), row(kn_b, 2), row(qn_c, 4), row(kn_c, 2),
            row(pad96(qn_d), 2), row(pad96(kn_d), 2), row(kvn_d, 1)]
    gains = jnp.stack(rows + [jnp.zeros_like(rows[0])] * (16 - len(rows)), axis=1)
    return w_ukv_p, gains


def kernel(x_prompt, x_sample, cache_nat_k, cache_nat_v, cache_gqa_k, cache_gqa_v, cache_win_k, cache_win_v,
           cache_mla_ckv, cache_mla_kpe, c, c_ctx, w_ada, b_ada, norm1, norm2, w_in, qn_a, kn_a, rpb_a,
           qn_b, kn_b, qn_c, kn_c, sink_c, qn_d, kn_d, kvn_d, w_ukv, w_branch, w_out, w_up, conv_w, conv_b,
           w_down):
    w_ukv_p, gains = _prep_small(w_ukv, qn_a, kn_a, qn_b, kn_b, qn_c, kn_c, qn_d, kn_d, kvn_d)
    w_in_t = jnp.swapaxes(w_in, 1, 2)
    w_branch_r = w_branch.reshape(DEPTH, 4 * 256, D_MODEL)
    conv_b_r = conv_b.reshape(DEPTH, 1, 2 * D_FF)
    rope_tab = _rope_tables()
    bd = _block_diag_ones()
    norm1_r = norm1.reshape(DEPTH, 1, D_MODEL)
    norm2_r = norm2.reshape(DEPTH, 1, D_MODEL)

    cond8 = jnp.zeros((8, D_MODEL), F32).at[0].set(c_ctx).at[1:1 + DEC_BATCH].set(c)
    mods = _adaln(cond8, w_ada, b_ada).reshape(DEPTH, 8, 1, 6 * D_MODEL)

    rpb_flat = jnp.pad(rpb_a.reshape(-1), (0, 8192 - rpb_a.size))
    na_bias = _na_bias(rpb_flat)
    kpe_p = jnp.pad(cache_mla_kpe, ((0, 0), (0, 0), (0, 0), (MLA_NOPE, LANES - MLA_QK)))
    kd_ctx, vd_ctx = _mla_ctx(cache_mla_ckv, kpe_p, w_ukv_p, gains, bd)
    c_nat_k = cache_nat_k.reshape(DEC_BATCH, DEPTH, PAST_LEN, 256)
    c_nat_v = cache_nat_v.reshape(DEC_BATCH, DEPTH, PAST_LEN, 256)
    c_gqa_k = cache_gqa_k.reshape(DEC_BATCH, DEPTH, PAST_LEN, LANES)
    c_gqa_v = cache_gqa_v.reshape(DEC_BATCH, DEPTH, PAST_LEN, LANES)
    c_win_k = cache_win_k.reshape(DEC_BATCH, DEPTH, PAST_LEN, LANES)
    c_win_v = cache_win_v.reshape(DEC_BATCH, DEPTH, PAST_LEN, LANES)

    x_ctx, x_lat = x_prompt.reshape(N_CTX, D_MODEL), x_sample.reshape(N_LAT, D_MODEL)
    caches = _zero_caches()
    for l in range(DEPTH):
        qkv, caches = _inproj(l, x_ctx, x_lat, mods, norm1_r, w_in_t, w_ukv_p, gains, bd, rope_tab, caches)
        qa, ka, va, qb, kb, vb, qc, kc, vc, qd, kd, vd = qkv
        sink_l = sink_c[l]
        o_ctx = _ctx_attention(l, sink_l, qkv)
        o_lat = (_lat_na(l, qa, ka, va, c_nat_k, c_nat_v, na_bias),
                 _lat_gqa(l, qb, kb, vb, c_gqa_k, c_gqa_v),
                 _lat_window(l, sink_l, qc, kc, vc, c_win_k, c_win_v),
                 _lat_mla(l, qd, kd, vd, kd_ctx, vd_ctx))
        x1 = _merge(l, x_ctx, x_lat, o_ctx, o_lat, mods, norm1_r, w_in_t, w_branch_r, w_out)
        x_ctx, x_lat = _ffn(l, x1, mods, norm2_r, w_up, conv_w, conv_b_r, w_down)
    y_prompt = x_ctx.reshape(BATCH, SEQ, D_MODEL)
    y_sample = x_lat.reshape(DEC_BATCH, DEC_SEQ, D_MODEL)
    heads = (4, 4, 2, 2, 2, 2)
    outs = [a.reshape(BATCH, DEPTH, h, HEAD_DIM, SEQ).transpose(0, 1, 4, 2, 3) for a, h in zip(caches[:6], heads)]
    return (y_prompt, y_sample, *outs, jnp.swapaxes(caches[6], 2, 3), jnp.swapaxes(caches[7], 2, 3))
```
